```python
import math
import jax, jax.numpy as jnp
from jax import lax
import numpy as np

D_MODEL = 1024
BATCH = 8
SEQ = 4096
DEPTH = 2

N_HEADS = 16
HEAD_DIM = D_MODEL // N_HEADS
D_FF = 4 * D_MODEL
N_MIXERS = 2
N_A_LAYERS = (DEPTH + 1) // 2
N_B_LAYERS = DEPTH // 2
NUM_BUCKETS = 32
MAX_DISTANCE = 128
MOBA_BLOCK = 256
MOBA_TOPK = 3
MOBA_Q_CHUNK = 64
DSA_Q_LORA = 256
DSA_KV_LORA = 256
IDX_HEADS = 8
IDX_DIM = 64
DSA_TOPK_MAX = 256
DSA_Q_CHUNK = 128
DSA_IN_DIM = DSA_Q_LORA + DSA_KV_LORA + IDX_DIM + IDX_HEADS
EPS = 1e-6
NEG = -1e30

kernel_name = "hybrid_moba_dsa_decoder"


def rms_norm(x, g):
    x32 = x.astype(jnp.float32)
    y = x32 * lax.rsqrt(jnp.mean(x32 * x32, axis=-1, keepdims=True) + EPS)
    return y.astype(x.dtype) * g


def rel_bucket(dist):
    n = jnp.maximum(dist, 0)
    exact = NUM_BUCKETS // 2
    nf = jnp.maximum(n, 1).astype(jnp.float32)
    large = exact + (jnp.log(nf / exact) / math.log(MAX_DISTANCE / exact) * (NUM_BUCKETS - exact)).astype(jnp.int32)
    large = jnp.minimum(large, NUM_BUCKETS - 1)
    return jnp.where(n < exact, n, large)


def moba_attention(h, w_qkv, w_o, bias_table):
    B, S, _ = h.shape
    H, Dh, BLK, QC = N_HEADS, HEAD_DIM, MOBA_BLOCK, MOBA_Q_CHUNK
    q, k, v = jnp.split(h @ w_qkv, 3, axis=-1)
    q = q.reshape(B, S, H, Dh).transpose(0, 2, 1, 3)
    k = k.reshape(B, S, H, Dh).transpose(0, 2, 1, 3)
    v = v.reshape(B, S, H, Dh).transpose(0, 2, 1, 3)
    n_blk = -(-S // BLK)
    pad = ((0, 0), (0, 0), (0, n_blk * BLK - S), (0, 0))
    kb = jnp.pad(k, pad).reshape(B, H, n_blk, BLK, Dh)
    vb = jnp.pad(v, pad).reshape(B, H, n_blk, BLK, Dh)
    k_mean = jnp.mean(kb.astype(jnp.float32), axis=3)
    gate = jnp.einsum('bhsd,bhnd->bhsn', q.astype(jnp.float32), k_mean)
    n_past = jnp.arange(S) // BLK
    past = jnp.arange(n_blk)[None, :] < n_past[:, None]
    gate = jnp.where(past[None, None], gate, NEG)
    k_sel = max(1, min(MOBA_TOPK, n_blk - 1))
    _, sel = lax.top_k(gate, k_sel)
    sel = sel.astype(jnp.int32)

    n_ch = S // QC

    def to_chunks(a):
        a = a.reshape(B, H, n_ch, QC, *a.shape[3:])
        a = jnp.moveaxis(a, 2, 1)
        return a.reshape(B * n_ch, H, QC, *a.shape[4:])

    q_c, sel_c = to_chunks(q), to_chunks(sel)
    b_idx = jnp.repeat(jnp.arange(B, dtype=jnp.int32), n_ch)
    c_idx = jnp.tile(jnp.arange(n_ch, dtype=jnp.int32), B)
    head_ids = jnp.arange(H)
    scale = HEAD_DIM ** -0.5

    def step(args):
        qc, selc, b, c = args
        t = c * QC + jnp.arange(QC, dtype=jnp.int32)
        j_own = (c * QC) // BLK
        kbb = lax.dynamic_index_in_dim(kb, b, 0, keepdims=False)
        vbb = lax.dynamic_index_in_dim(vb, b, 0, keepdims=False)
        k_g = kbb[head_ids[:, None, None], selc]
        v_g = vbb[head_ids[:, None, None], selc]
        k_own = lax.dynamic_index_in_dim(kbb, j_own, 1, keepdims=False)
        v_own = lax.dynamic_index_in_dim(vbb, j_own, 1, keepdims=False)
        qf = qc.astype(jnp.float32)
        l_sel = jnp.einsum('hqd,hqnjd->hqnj', qf, k_g.astype(jnp.float32)) * scale
        pos_sel = selc[..., None] * BLK + jnp.arange(BLK, dtype=jnp.int32)
        b_sel = bias_table[rel_bucket(t[None, :, None, None] - pos_sel), head_ids[:, None, None, None]].astype(jnp.float32)
        slot_ok = jnp.arange(k_sel)[None, :] < (t // BLK)[:, None]
        l_sel = jnp.where(slot_ok[None, :, :, None], l_sel + b_sel, NEG).reshape(H, QC, k_sel * BLK)
        l_own = jnp.einsum('hqd,hkd->hqk', qf, k_own.astype(jnp.float32)) * scale
        d_own = t[:, None] - (j_own * BLK + jnp.arange(BLK, dtype=jnp.int32))[None, :]
        b_own = jnp.moveaxis(bias_table[rel_bucket(d_own)].astype(jnp.float32), -1, 0)
        l_own = jnp.where((d_own >= 0)[None], l_own + b_own, NEG)
        p = jax.nn.softmax(jnp.concatenate([l_sel, l_own], axis=-1), axis=-1)
        p_sel = p[..., :k_sel * BLK].reshape(H, QC, k_sel, BLK)
        p_own = p[..., k_sel * BLK:]
        o = (jnp.einsum('hqnj,hqnjd->hqd', p_sel, v_g.astype(jnp.float32))
             + jnp.einsum('hqk,hkd->hqd', p_own, v_own.astype(jnp.float32)))
        return o.astype(qc.dtype)

    o = lax.map(step, (q_c, sel_c, b_idx, c_idx))
    o = o.reshape(B, n_ch, H, QC, Dh).transpose(0, 1, 3, 2, 4).reshape(B, S, H * Dh)
    return o @ w_o


def dsa_attention(h, w_in, g_q, g_kv, w_uq, w_qi, w_uk, w_uv, w_o, bias_table):
    B, S, _ = h.shape
    H, Dh, QC = N_HEADS, HEAD_DIM, DSA_Q_CHUNK
    proj = h @ w_in
    c_q, c_kv, k_idx, w_idx = jnp.split(proj, [DSA_Q_LORA, DSA_Q_LORA + DSA_KV_LORA, DSA_Q_LORA + DSA_KV_LORA + IDX_DIM], axis=-1)
    c_q = rms_norm(c_q, g_q)
    c_kv = rms_norm(c_kv, g_kv)
    q_nope = (c_q @ w_uq).reshape(B, S, H, Dh)
    q_idx = (c_q @ w_qi).reshape(B, S, IDX_HEADS, IDX_DIM)
    w_idx = w_idx.astype(jnp.float32) * (IDX_HEADS ** -0.5 * IDX_DIM ** -0.5)
    top_k = min(DSA_TOPK_MAX, S // 4)
    n_ch = S // QC

    def to_chunks(a):
        return jnp.moveaxis(a.reshape(B, n_ch, QC, *a.shape[2:]), 1, 0)

    k_idx32 = k_idx.astype(jnp.float32)
    key_pos = jnp.arange(S, dtype=jnp.int32)
    b_ids = jnp.arange(B)[:, None, None]
    scale = HEAD_DIM ** -0.5

    def step(args):
        qn, qi, wi, c = args
        t = c * QC + jnp.arange(QC, dtype=jnp.int32)
        rel = jax.nn.relu(jnp.einsum('bqhd,bsd->bqhs', qi.astype(jnp.float32), k_idx32))
        score = jnp.einsum('bqh,bqhs->bqs', wi, rel)
        score = jnp.where((key_pos[None, :] <= t[:, None])[None], score, NEG)
        _, idx = lax.top_k(score, top_k)
        valid = idx <= t[None, :, None]
        c_sel = c_kv[b_ids, idx].astype(jnp.float32)
        q_lat = jnp.einsum('bqhd,hdc->bqhc', qn.astype(jnp.float32), w_uk.astype(jnp.float32))
        logits = jnp.einsum('bqhc,bqkc->bhqk', q_lat, c_sel) * scale
        bias = jnp.moveaxis(bias_table[rel_bucket(t[None, :, None] - idx)].astype(jnp.float32), -1, 1)
        logits = jnp.where(valid[:, None], logits + bias, NEG)
        p = jax.nn.softmax(logits, axis=-1)
        o_lat = jnp.einsum('bhqk,bqkc->bqhc', p, c_sel)
        o = jnp.einsum('bqhc,hcd->bqhd', o_lat, w_uv.astype(jnp.float32))
        return o.astype(qn.dtype)

    o = lax.map(step, (to_chunks(q_nope), to_chunks(q_idx), to_chunks(w_idx), jnp.arange(n_ch, dtype=jnp.int32)))
    o = jnp.moveaxis(o, 0, 1).reshape(B, S, H * Dh)
    return o @ w_o


def sq_relu_mlp(h, w_up, w_down):
    return jnp.square(jax.nn.relu(h @ w_up)) @ w_down


def setup_inputs(seed: int = 0) -> dict:
    key = jax.random.key(seed)
    ks = jax.random.split(key, 20)
    D, H, Dh = D_MODEL, N_HEADS, HEAD_DIM
    nrm = lambda k, shape, s: jax.random.normal(k, shape, jnp.float32) * s
    gain = lambda k, shape: 1.0 + 0.05 * jax.random.normal(k, shape, jnp.float32)
    return {
        "x": nrm(ks[0], (BATCH, SEQ, D), 1.0),
        "rel_bias": nrm(ks[1], (NUM_BUCKETS, H), 0.2),
        "ln_attn": gain(ks[2], (DEPTH, D)),
        "ln_mlp": gain(ks[3], (DEPTH, D)),
        "moba_w_qkv": nrm(ks[4], (N_A_LAYERS, D, 3 * D), D ** -0.5),
        "moba_w_o": nrm(ks[5], (N_A_LAYERS, D, D), D ** -0.5),
        "dsa_w_in": nrm(ks[6], (N_B_LAYERS, D, DSA_IN_DIM), D ** -0.5),
        "dsa_g_q": gain(ks[7], (N_B_LAYERS, DSA_Q_LORA)),
        "dsa_g_kv": gain(ks[8], (N_B_LAYERS, DSA_KV_LORA)),
        "dsa_w_uq": nrm(ks[9], (N_B_LAYERS, DSA_Q_LORA, H * Dh), DSA_Q_LORA ** -0.5),
        "dsa_w_qi": nrm(ks[10], (N_B_LAYERS, DSA_Q_LORA, IDX_HEADS * IDX_DIM), DSA_Q_LORA ** -0.5),
        "dsa_w_uk": nrm(ks[11], (N_B_LAYERS, H, Dh, DSA_KV_LORA), DSA_KV_LORA ** -0.5),
        "dsa_w_uv": nrm(ks[12], (N_B_LAYERS, H, DSA_KV_LORA, Dh), DSA_KV_LORA ** -0.5),
        "dsa_w_o": nrm(ks[13], (N_B_LAYERS, H * Dh, D), (H * Dh) ** -0.5),
        "mlp_w_up": nrm(ks[14], (DEPTH, D, D_FF), D ** -0.5),
        "mlp_w_down": nrm(ks[15], (DEPTH, D_FF, D), 0.5 * D_FF ** -0.5),
        "final_norm": gain(ks[16], (D,)),
    }


def reference(x, rel_bias, ln_attn, ln_mlp, moba_w_qkv, moba_w_o, dsa_w_in, dsa_g_q, dsa_g_kv,
              dsa_w_uq, dsa_w_qi, dsa_w_uk, dsa_w_uv, dsa_w_o, mlp_w_up, mlp_w_down, final_norm):
    for i in range(DEPTH):
        hn = rms_norm(x, ln_attn[i])
        j = i // N_MIXERS
        if i % N_MIXERS == 0:
            x = x + moba_attention(hn, moba_w_qkv[j], moba_w_o[j], rel_bias)
        else:
            x = x + dsa_attention(hn, dsa_w_in[j], dsa_g_q[j], dsa_g_kv[j], dsa_w_uq[j], dsa_w_qi[j],
                                  dsa_w_uk[j], dsa_w_uv[j], dsa_w_o[j], rel_bias)
        x = x + sq_relu_mlp(rms_norm(x, ln_mlp[i]), mlp_w_up[i], mlp_w_down[i])
    return rms_norm(x, final_norm)
```

```python
import functools
import math

import numpy as np
import jax
import jax.numpy as jnp
from jax import lax
from jax.experimental import pallas as pl
from jax.experimental.pallas import tpu as pltpu

F32 = jnp.float32
BF16 = jnp.bfloat16
I32 = jnp.int32

EPS = 1e-6
NEG = -1e30
INT_MIN = -(2 ** 31)

NUM_BUCKETS = 32
MAX_DISTANCE = 128
HEAD_DIM = 64
LANES = 128
BLK = 256
MOBA_TOPK = 3
DSA_TOPK_MAX = 256
DSA_LORA = 256
IDX_HEADS = 8
IDX_DIM = 64
SEL_TQ = 128
VMEM_LIMIT = 56 * 1024 * 1024


def _dot(a, b):
    return jnp.dot(a, b, preferred_element_type=F32)


def _dot_nt(a, b):
    return lax.dot_general(a, b, (((1,), (1,)), ((), ())), preferred_element_type=F32)


def _rms(x, g):
    return x * lax.rsqrt(jnp.mean(x * x, axis=-1, keepdims=True) + EPS) * g


def _head_mask(lane, hh):
    return (lane < HEAD_DIM) if hh == 0 else (lane >= HEAD_DIM)


def _bucket_map():
    i = np.arange(BLK)[:, None]
    jj = np.arange(2 * BLK)[None, :]
    d = i - jj + BLK
    n = np.maximum(d, 0)
    exact = NUM_BUCKETS // 2
    nf = np.maximum(n, 1).astype(np.float32)
    large = exact + (np.log(nf / exact) / math.log(MAX_DISTANCE / exact) * (NUM_BUCKETS - exact)).astype(np.int32)
    large = np.minimum(large, NUM_BUCKETS - 1)
    bucket = np.where(n < exact, n, large)
    return np.where(d >= 0, bucket, -1).astype(np.int32)


def _bias_tile_kernel(tab_ref, bmap_ref, o_ref):
    h = pl.program_id(0)
    bm = bmap_ref[...]
    base = tab_ref[NUM_BUCKETS - 1, h]
    acc = jnp.full(bm.shape, NEG, F32)
    for b in range(NUM_BUCKETS):
        acc = jnp.where(bm == b, tab_ref[b, h] - base, acc)
    o_ref[...] = acc


def _bias_tiles(rel_bias):
    n_heads = rel_bias.shape[1]
    bmap = jnp.asarray(_bucket_map())
    return pl.pallas_call(
        _bias_tile_kernel,
        grid=(n_heads,),
        in_specs=[pl.BlockSpec(memory_space=pltpu.SMEM),
                  pl.BlockSpec((BLK, 2 * BLK), lambda h: (0, 0))],
        out_specs=pl.BlockSpec((None, BLK, 2 * BLK), lambda h: (h, 0, 0)),
        out_shape=jax.ShapeDtypeStruct((n_heads, BLK, 2 * BLK), F32),
        name="bias_tiles",
    )(rel_bias, bmap)


def _norm_matmul_kernel(x_ref, g_ref, w_ref, o_ref):
    xn = _rms(x_ref[...], g_ref[...]).astype(BF16)
    o_ref[...] = _dot(xn, w_ref[...]).astype(o_ref.dtype)


def _norm_matmul(x2, g, w, tm=512):
    n, d = x2.shape
    m = w.shape[1]
    return pl.pallas_call(
        _norm_matmul_kernel,
        grid=(n // tm,),
        in_specs=[pl.BlockSpec((tm, d), lambda i: (i, 0)),
                  pl.BlockSpec((1, d), lambda i: (0, 0)),
                  pl.BlockSpec((d, m), lambda i: (0, 0))],
        out_specs=pl.BlockSpec((tm, m), lambda i: (i, 0)),
        out_shape=jax.ShapeDtypeStruct((n, m), BF16),
        compiler_params=pltpu.CompilerParams(dimension_semantics=("parallel",), vmem_limit_bytes=VMEM_LIMIT),
        name="norm_matmul",
    )(x2, g.reshape(1, d), w)


def _matmul_res_kernel(a_ref, w_ref, r_ref, o_ref):
    o_ref[...] = r_ref[...] + _dot(a_ref[...], w_ref[...])


def _matmul_res(a, w, res, tm=512):
    n, k = a.shape
    m = w.shape[1]
    return pl.pallas_call(
        _matmul_res_kernel,
        grid=(n // tm,),
        in_specs=[pl.BlockSpec((tm, k), lambda i: (i, 0)),
                  pl.BlockSpec((k, m), lambda i: (0, 0)),
                  pl.BlockSpec((tm, m), lambda i: (i, 0))],
        out_specs=pl.BlockSpec((tm, m), lambda i: (i, 0)),
        out_shape=jax.ShapeDtypeStruct((n, m), F32),
        compiler_params=pltpu.CompilerParams(dimension_semantics=("parallel",), vmem_limit_bytes=VMEM_LIMIT),
        name="matmul_res",
    )(a, w, res)


def _mlp_kernel(x_ref, g_ref, wu_ref, wd_ref, o_ref, xn_ref, acc_ref):
    f = pl.program_id(1)

    @pl.when(f == 0)
    def _():
        xn_ref[...] = _rms(x_ref[...], g_ref[...]).astype(BF16)
        acc_ref[...] = jnp.zeros_like(acc_ref)

    h = jnp.square(jnp.maximum(_dot(xn_ref[...], wu_ref[...]), 0.0)).astype(BF16)
    acc_ref[...] += _dot(h, wd_ref[...])

    @pl.when(f == pl.num_programs(1) - 1)
    def _():
        o_ref[...] = x_ref[...] + acc_ref[...]


def _mlp(x2, g, w_up, w_down, tm=512, tf=1024):
    n, d = x2.shape
    d_ff = w_up.shape[1]
    return pl.pallas_call(
        _mlp_kernel,
        grid=(n // tm, d_ff // tf),
        in_specs=[pl.BlockSpec((tm, d), lambda i, f: (i, 0)),
                  pl.BlockSpec((1, d), lambda i, f: (0, 0)),
                  pl.BlockSpec((d, tf), lambda i, f: (0, f)),
                  pl.BlockSpec((tf, d), lambda i, f: (f, 0))],
        out_specs=pl.BlockSpec((tm, d), lambda i, f: (i, 0)),
        out_shape=jax.ShapeDtypeStruct((n, d), F32),
        scratch_shapes=[pltpu.VMEM((tm, d), BF16), pltpu.VMEM((tm, d), F32)],
        compiler_params=pltpu.CompilerParams(dimension_semantics=("parallel", "arbitrary"),
                                             vmem_limit_bytes=VMEM_LIMIT),
        name="mlp",
    )(x2, g.reshape(1, d), w_up, w_down)


def _final_norm_kernel(x_ref, g_ref, o_ref):
    o_ref[...] = _rms(x_ref[...], g_ref[...])


def _final_norm(x2, g, tm=512):
    n, d = x2.shape
    return pl.pallas_call(
        _final_norm_kernel,
        grid=(n // tm,),
        in_specs=[pl.BlockSpec((tm, d), lambda i: (i, 0)),
                  pl.BlockSpec((1, d), lambda i: (0, 0))],
        out_specs=pl.BlockSpec((tm, d), lambda i: (i, 0)),
        out_shape=jax.ShapeDtypeStruct((n, d), F32),
        compiler_params=pltpu.CompilerParams(dimension_semantics=("parallel",)),
        name="final_norm",
    )(x2, g.reshape(1, d))


def _softmax_update(carry, s, v_blk):
    m, l, acc = carry
    m_new = jnp.maximum(m, jnp.max(s, axis=1, keepdims=True))
    alpha = jnp.exp(m - m_new)
    p = jnp.exp(s - m_new)
    l = alpha * l + jnp.sum(p, axis=1, keepdims=True)
    acc = alpha * acc + _dot(p.astype(BF16), v_blk)
    return m_new, l, acc


def _moba_kernel(q_ref, k_ref, v_ref, bt_ref, o_ref, kaug_ref, kmean_ref, *, n_blk, k_sel):
    j = pl.program_id(2)
    seq = k_ref.shape[0]
    lane = lax.broadcasted_iota(I32, (BLK, LANES), 1)

    @pl.when(j == 0)
    def _():
        kaug_ref[:, 0:LANES] = k_ref[...]
        row = lax.broadcasted_iota(I32, (seq, LANES), 0)
        col = lax.broadcasted_iota(I32, (seq, LANES), 1)
        kaug_ref[:, LANES:2 * LANES] = jnp.where(row // BLK == col, 1.0, 0.0).astype(BF16)
        kmean_ref[...] = jnp.zeros_like(kmean_ref)
        for n in range(n_blk):
            kmean_ref[n:n + 1, :] = jnp.mean(k_ref[n * BLK:(n + 1) * BLK, :].astype(F32), axis=0, keepdims=True)

    q = q_ref[...]
    own0 = pl.multiple_of(j * BLK, BLK)
    adj0 = pl.multiple_of(jnp.maximum(j - 1, 0) * BLK, BLK)
    k_own = k_ref[pl.ds(own0, BLK), :]
    v_own = v_ref[pl.ds(own0, BLK), :]
    k_adj = kaug_ref[pl.ds(adj0, BLK), :]
    v_adj = v_ref[pl.ds(adj0, BLK), :]

    outs = []
    for hh in range(2):
        qh = jnp.where(_head_mask(lane, hh), q, jnp.zeros_like(q))
        gate = _dot_nt(qh.astype(F32), kmean_ref[...])
        g = jnp.where(lane < j, gate, NEG)
        g = jnp.where(lane < n_blk, g, -jnp.inf)
        sel_bias = jnp.full((BLK, LANES), NEG, F32)
        for it in range(k_sel):
            mx = jnp.max(g, axis=1, keepdims=True)
            first = jnp.min(jnp.where(g == mx, lane, LANES), axis=1, keepdims=True)
            pick = lane == first
            sel_bias = jnp.where(pick, jnp.where(it < j, 0.0, NEG), sel_bias)
            g = jnp.where(pick, -jnp.inf, g)
        q_aug = jnp.concatenate([qh, sel_bias.astype(BF16)], axis=1)

        s = _dot_nt(qh, k_own) + bt_ref[hh, :, BLK:2 * BLK]
        m = jnp.max(s, axis=1, keepdims=True)
        p = jnp.exp(s - m)
        carry = (m, jnp.sum(p, axis=1, keepdims=True), _dot(p.astype(BF16), v_own))
        s = _dot_nt(q_aug, k_adj) + bt_ref[hh, :, 0:BLK]
        carry = _softmax_update(carry, s, v_adj)

        def far_block(n, carry):
            n0 = pl.multiple_of(n * BLK, BLK)
            s = _dot_nt(q_aug, kaug_ref[pl.ds(n0, BLK), :])
            return _softmax_update(carry, s, v_ref[pl.ds(n0, BLK), :])

        m, l, acc = lax.fori_loop(0, jnp.maximum(j - 1, 0), far_block, carry)
        outs.append(acc / l)

    o_ref[...] = jnp.where(lane < HEAD_DIM, outs[0], outs[1]).astype(o_ref.dtype)


def _moba_attention(qkv, bt, n_heads):
    b, seq, _ = qkv.shape
    n_blk = seq // BLK
    n_pairs = n_heads // 2
    k_sel = max(1, min(MOBA_TOPK, n_blk - 1))
    assert n_blk <= LANES
    return pl.pallas_call(
        functools.partial(_moba_kernel, n_blk=n_blk, k_sel=k_sel),
        grid=(b, n_pairs, n_blk),
        in_specs=[pl.BlockSpec((None, BLK, LANES), lambda bi, hp, j: (bi, j, hp)),
                  pl.BlockSpec((None, seq, LANES), lambda bi, hp, j: (bi, 0, n_pairs + hp)),
                  pl.BlockSpec((None, seq, LANES), lambda bi, hp, j: (bi, 0, 2 * n_pairs + hp)),
                  pl.BlockSpec((2, BLK, 2 * BLK), lambda bi, hp, j: (hp, 0, 0))],
        out_specs=pl.BlockSpec((None, BLK, LANES), lambda bi, hp, j: (bi, j, hp)),
        out_shape=jax.ShapeDtypeStruct((b, seq, n_heads * HEAD_DIM), BF16),
        scratch_shapes=[pltpu.VMEM((seq, 2 * LANES), BF16), pltpu.VMEM((LANES, LANES), F32)],
        compiler_params=pltpu.CompilerParams(dimension_semantics=("parallel", "parallel", "arbitrary"),
                                             vmem_limit_bytes=VMEM_LIMIT),
        name="moba_attn",
    )(qkv, qkv, qkv, bt)


def _dsa_proj_kernel(x_ref, g_ref, win_ref, gq_ref, gkv_ref, wuq_ref, wqi_ref,
                     ckv_ref, kk_ref, widx_ref, qn_ref, qi_ref, *, idx_scale):
    hn = _rms(x_ref[...], g_ref[...]).astype(BF16)
    proj = _dot(hn, win_ref[...])
    c_q = _rms(proj[:, 0:DSA_LORA], gq_ref[...]).astype(BF16)
    ckv_ref[...] = _rms(proj[:, DSA_LORA:2 * DSA_LORA], gkv_ref[...]).astype(BF16)
    kk_ref[...] = proj[:, 2 * DSA_LORA:2 * DSA_LORA + LANES].astype(BF16)
    widx_ref[...] = proj[:, 2 * DSA_LORA + LANES:2 * DSA_LORA + 2 * LANES] * idx_scale
    qn_ref[...] = _dot(c_q, wuq_ref[...]).astype(BF16)
    qi_ref[...] = _dot(c_q, wqi_ref[...]).astype(BF16)


def _dsa_proj(x2, g, w_in_pad, g_q, g_kv, w_uq, w_qi, tm=512):
    n, d = x2.shape
    full = lambda a: pl.BlockSpec(a.shape, lambda i: (0,) * a.ndim)
    rows = lambda w: pl.BlockSpec((tm, w), lambda i: (i, 0))
    g2, gq2, gkv2 = g.reshape(1, d), g_q.reshape(1, -1), g_kv.reshape(1, -1)
    idx_scale = IDX_HEADS ** -0.5 * IDX_DIM ** -0.5
    return pl.pallas_call(
        functools.partial(_dsa_proj_kernel, idx_scale=idx_scale),
        grid=(n // tm,),
        in_specs=[rows(d), full(g2), full(w_in_pad), full(gq2), full(gkv2), full(w_uq), full(w_qi)],
        out_specs=[rows(DSA_LORA), rows(LANES), rows(LANES), rows(w_uq.shape[1]), rows(w_qi.shape[1])],
        out_shape=[jax.ShapeDtypeStruct((n, DSA_LORA), BF16),
                   jax.ShapeDtypeStruct((n, LANES), BF16),
                   jax.ShapeDtypeStruct((n, LANES), F32),
                   jax.ShapeDtypeStruct((n, w_uq.shape[1]), BF16),
                   jax.ShapeDtypeStruct((n, w_qi.shape[1]), BF16)],
        compiler_params=pltpu.CompilerParams(dimension_semantics=("parallel",), vmem_limit_bytes=VMEM_LIMIT),
        name="dsa_proj",
    )(x2, g2, w_in_pad, gq2, gkv2, w_uq, w_qi)


def _dsa_select_kernel(qi_ref, w_ref, kk_ref, o_ref, keys_ref, wb_ref, *, top_k, idx_bits):
    c = pl.program_id(1)
    n_kc, tq, tk = o_ref.shape
    n_proc = ((c + 1) * tq + tk - 1) // tk
    lane = lax.broadcasted_iota(I32, (tq, tk), 1)
    row_pos = c * tq + lax.broadcasted_iota(I32, (tq, tk), 0)
    lane_q = lax.broadcasted_iota(I32, (tq, LANES), 1)

    w = w_ref[...]
    for h in range(IDX_HEADS):
        wb_ref[h] = jnp.broadcast_to(w[:, h:h + 1], (tq, tk))
    qi = qi_ref[...]
    q_heads = []
    for h in range(IDX_HEADS):
        qp = qi[:, (h // 2) * LANES:(h // 2 + 1) * LANES]
        q_heads.append(jnp.where(_head_mask(lane_q, h % 2), qp, jnp.zeros_like(qp)))

    def score_chunk(kc, _):
        kt = kk_ref[pl.ds(pl.multiple_of(kc * tk, tk), tk), :]
        acc = jnp.zeros((tq, tk), F32)
        for h in range(IDX_HEADS):
            acc = acc + wb_ref[h] * jnp.maximum(_dot_nt(q_heads[h], kt), 0.0)
        sc = jnp.where(kc * tk + lane <= row_pos, acc, NEG)
        bits = pltpu.bitcast(sc, I32)
        keys_ref[kc] = jnp.where(bits < 0, bits ^ 0x7FFFFFFF, bits)
        return 0

    lax.fori_loop(0, n_proc, score_chunk, 0)

    def count(pred):
        def body(kc, part):
            return part + jnp.where(pred(keys_ref[kc], kc * tk + lane), 1, 0)
        part = lax.fori_loop(0, n_proc, body, jnp.zeros((tq, tk), I32))
        return jnp.sum(part, axis=1, keepdims=True)

    def value_bit(i, t):
        cand = t ^ jnp.left_shift(jnp.int32(1), 31 - i)
        cnt = count(lambda key, pos: key >= cand)
        return jnp.where(cnt >= top_k, cand, t)

    thr = lax.fori_loop(0, 32, value_bit, jnp.full((tq, 1), INT_MIN, I32))
    cnt_ge = count(lambda key, pos: key >= thr)
    cnt_gt = count(lambda key, pos: key > thr)
    need = top_k - cnt_gt

    def tie_cut():
        def index_bit(i, x):
            cand = x | jnp.left_shift(jnp.int32(1), idx_bits - 1 - i)
            cnt = count(lambda key, pos: (key == thr) & (pos < cand))
            return jnp.where(cnt < need, cand, x)
        return lax.fori_loop(0, idx_bits, index_bit, jnp.zeros((tq, 1), I32))

    last_tie = lax.cond(jnp.max(cnt_ge) > top_k, tie_cut, lambda: jnp.full((tq, 1), 2 ** idx_bits, I32))

    def write_chunk(kc, _):
        key = keys_ref[kc]
        pos = kc * tk + lane
        chosen = (key > thr) | ((key == thr) & (pos <= last_tie))
        o_ref[kc] = jnp.where(chosen & (pos <= row_pos), 0.0, NEG).astype(o_ref.dtype)
        return 0

    lax.fori_loop(0, n_proc, write_chunk, 0)

    def fill_chunk(kc, _):
        o_ref[kc] = jnp.full((tq, tk), NEG, o_ref.dtype)
        return 0

    lax.fori_loop(n_proc, n_kc, fill_chunk, 0)


def _dsa_select(qi, widx, kk, batch, seq):
    tq, tk = SEL_TQ, BLK
    n_kc = seq // tk
    top_k = min(DSA_TOPK_MAX, seq // 4)
    idx_bits = max(1, (seq - 1).bit_length())
    qi3 = qi.reshape(batch, seq, -1)
    return pl.pallas_call(
        functools.partial(_dsa_select_kernel, top_k=top_k, idx_bits=idx_bits),
        grid=(batch, seq // tq),
        in_specs=[pl.BlockSpec((None, tq, qi3.shape[2]), lambda b, c: (b, c, 0)),
                  pl.BlockSpec((None, tq, LANES), lambda b, c: (b, c, 0)),
                  pl.BlockSpec((None, seq, LANES), lambda b, c: (b, 0, 0))],
        out_specs=pl.BlockSpec((None, n_kc, tq, tk), lambda b, c: (b, 0, c, 0)),
        out_shape=jax.ShapeDtypeStruct((batch, n_kc, seq, tk), BF16),
        scratch_shapes=[pltpu.VMEM((n_kc, tq, tk), I32), pltpu.VMEM((IDX_HEADS, tq, tk), F32)],
        compiler_params=pltpu.CompilerParams(dimension_semantics=("parallel", "arbitrary"),
                                             vmem_limit_bytes=VMEM_LIMIT),
        name="dsa_select",
    )(qi3, widx.reshape(batch, seq, LANES), kk.reshape(batch, seq, LANES))


def _dsa_attn_kernel(qn_ref, ckv_ref, mb_ref, bt_ref, wuk_ref, wuv_ref, o_ref,
                     qlat_ref, s_ref, p_ref, acc_ref, m_ref, l_ref, alpha_ref, *, n_heads):
    qt = pl.program_id(1)
    t = BLK
    lane = lax.broadcasted_iota(I32, (t, LANES), 1)

    for hp in range(n_heads // 2):
        qp = qn_ref[:, hp * LANES:(hp + 1) * LANES]
        for hh in range(2):
            qh = jnp.where(_head_mask(lane, hh), qp, jnp.zeros_like(qp))
            h = 2 * hp + hh
            qlat_ref[h * t:(h + 1) * t, :] = _dot(qh, wuk_ref[hp]).astype(BF16)

    m_ref[...] = jnp.full(m_ref.shape, NEG, F32)
    l_ref[...] = jnp.zeros_like(l_ref)
    acc_ref[...] = jnp.zeros_like(acc_ref)

    def process_tile(kk, bias_off):
        kt = ckv_ref[pl.ds(pl.multiple_of(kk * t, t), t), :]
        s_ref[...] = _dot_nt(qlat_ref[...], kt)
        mb = mb_ref[kk].astype(F32)

        def head(h, _):
            rows = pl.ds(pl.multiple_of(h * t, t), t)
            s = s_ref[rows, :] + mb
            if bias_off is not None:
                s = s + bt_ref[h, :, bias_off:bias_off + t]
            m_old = m_ref[rows, :]
            m_new = jnp.maximum(m_old, jnp.max(s, axis=1, keepdims=True))
            alpha = jnp.exp(m_old - m_new)
            p = jnp.exp(s - m_new)
            l_ref[rows, :] = alpha * l_ref[rows, :] + jnp.sum(p, axis=1, keepdims=True)
            m_ref[rows, :] = m_new
            alpha_ref[rows, :] = alpha
            p_ref[rows, :] = p.astype(BF16)
            return 0

        lax.fori_loop(0, n_heads, head, 0)
        acc_ref[...] = alpha_ref[...] * acc_ref[...] + _dot(p_ref[...], kt)

    process_tile(qt, BLK)

    @pl.when(qt >= 1)
    def _():
        process_tile(qt - 1, 0)

    def far_tile(kk, _):
        process_tile(kk, None)
        return 0

    lax.fori_loop(0, jnp.maximum(qt - 1, 0), far_tile, 0)

    for hp in range(n_heads // 2):
        out = jnp.zeros((t, LANES), F32)
        for hh in range(2):
            h = 2 * hp + hh
            o_lat = acc_ref[h * t:(h + 1) * t, :] / l_ref[h * t:(h + 1) * t, :]
            out = out + _dot(o_lat.astype(BF16), wuv_ref[h])
        o_ref[:, hp * LANES:(hp + 1) * LANES] = out.astype(o_ref.dtype)


def _dsa_attention(qn, ckv, mask_bias, bt, w_uk_pairs, w_uv_pad, batch, seq, n_heads):
    t = BLK
    n_kc = seq // t
    d_q = qn.shape[-1]
    full = lambda a: pl.BlockSpec(a.shape, lambda b, q: (0,) * a.ndim)
    return pl.pallas_call(
        functools.partial(_dsa_attn_kernel, n_heads=n_heads),
        grid=(batch, seq // t),
        in_specs=[pl.BlockSpec((None, t, d_q), lambda b, q: (b, q, 0)),
                  pl.BlockSpec((None, seq, DSA_LORA), lambda b, q: (b, 0, 0)),
                  pl.BlockSpec((None, n_kc, t, t), lambda b, q: (b, 0, q, 0)),
                  full(bt), full(w_uk_pairs), full(w_uv_pad)],
        out_specs=pl.BlockSpec((None, t, d_q), lambda b, q: (b, q, 0)),
        out_shape=jax.ShapeDtypeStruct((batch, seq, d_q), BF16),
        scratch_shapes=[pltpu.VMEM((n_heads * t, DSA_LORA), BF16),
                        pltpu.VMEM((n_heads * t, t), F32),
                        pltpu.VMEM((n_heads * t, t), BF16),
                        pltpu.VMEM((n_heads * t, DSA_LORA), F32),
                        pltpu.VMEM((n_heads * t, 1), F32),
                        pltpu.VMEM((n_heads * t, 1), F32),
                        pltpu.VMEM((n_heads * t, 1), F32)],
        compiler_params=pltpu.CompilerParams(dimension_semantics=("parallel", "arbitrary"),
                                             vmem_limit_bytes=VMEM_LIMIT),
        name="dsa_attn",
    )(qn.reshape(batch, seq, d_q), ckv.reshape(batch, seq, DSA_LORA), mask_bias, bt, w_uk_pairs, w_uv_pad)


def _moba_layer(x2, batch, seq, g, w_qkv, w_o, bt, n_heads):
    d = x2.shape[1]
    col_scale = jnp.concatenate([jnp.full((d,), HEAD_DIM ** -0.5, F32), jnp.ones((2 * d,), F32)])
    qkv = _norm_matmul(x2, g, (w_qkv * col_scale).astype(BF16))
    attn = _moba_attention(qkv.reshape(batch, seq, 3 * d), bt, n_heads)
    return _matmul_res(attn.reshape(batch * seq, d), w_o.astype(BF16), x2)


def _dsa_layer(x2, batch, seq, g, w_in, g_q, g_kv, w_uq, w_qi, w_uk, w_uv, w_o, bt, n_heads):
    d = x2.shape[1]
    k_lo = 2 * DSA_LORA
    w_k = w_in[:, k_lo:k_lo + IDX_DIM]
    w_w = w_in[:, k_lo + IDX_DIM:]
    w_in_pad = jnp.concatenate(
        [w_in[:, :k_lo], w_k, w_k, w_w, jnp.zeros((d, LANES - IDX_HEADS), F32)], axis=1).astype(BF16)
    ckv, kk, widx, qn, qi = _dsa_proj(x2, g, w_in_pad, g_q, g_kv,
                                      (w_uq * HEAD_DIM ** -0.5).astype(BF16), w_qi.astype(BF16))
    mask_bias = _dsa_select(qi, widx, kk, batch, seq)
    w_uk_pairs = w_uk.reshape(n_heads // 2, 2 * HEAD_DIM, DSA_LORA).astype(BF16)
    zeros = jnp.zeros_like(w_uv)
    even = jnp.concatenate([w_uv, zeros], axis=-1)
    odd = jnp.concatenate([zeros, w_uv], axis=-1)
    is_even = (jnp.arange(n_heads) % 2 == 0)[:, None, None]
    w_uv_pad = jnp.where(is_even, even, odd).astype(BF16)
    attn = _dsa_attention(qn, ckv, mask_bias, bt, w_uk_pairs, w_uv_pad, batch, seq, n_heads)
    return _matmul_res(attn.reshape(batch * seq, -1), w_o.astype(BF16), x2)


def kernel(x, rel_bias, ln_attn, ln_mlp, moba_w_qkv, moba_w_o, dsa_w_in, dsa_g_q, dsa_g_kv, dsa_w_uq, dsa_w_qi,
           dsa_w_uk, dsa_w_uv, dsa_w_o, mlp_w_up, mlp_w_down, final_norm):
    batch, seq, d = x.shape
    n_heads = rel_bias.shape[1]
    depth = ln_attn.shape[0]
    bt = _bias_tiles(rel_bias)
    x2 = x.reshape(batch * seq, d)
    for i in range(depth):
        j = i // 2
        if i % 2 == 0:
            x2 = _moba_layer(x2, batch, seq, ln_attn[i], moba_w_qkv[j], moba_w_o[j], bt, n_heads)
        else:
            x2 = _dsa_layer(x2, batch, seq, ln_attn[i], dsa_w_in[j], dsa_g_q[j], dsa_g_kv[j], dsa_w_uq[j],
                            dsa_w_qi[j], dsa_w_uk[j], dsa_w_uv[j], dsa_w_o[j], bt, n_heads)
        x2 = _mlp(x2, ln_mlp[i], mlp_w_up[i].astype(BF16), mlp_w_down[i].astype(BF16))
    return _final_norm(x2, final_norm).reshape(batch, seq, d)
```

```python
import functools
import math

import numpy as np
import jax
import jax.numpy as jnp
from jax import lax
from jax.experimental import pallas as pl
from jax.experimental.pallas import tpu as pltpu

F32 = jnp.float32
BF16 = jnp.bfloat16
I32 = jnp.int32

EPS = 1e-6
NEG = -1e30
INT_MIN = -(2 ** 31)

NUM_BUCKETS = 32
MAX_DISTANCE = 128
HEAD_DIM = 64
LANES = 128
BLK = 256
MOBA_TOPK = 3
DSA_TOPK_MAX = 256
DSA_LORA = 256
IDX_HEADS = 8
IDX_DIM = 64
SEL_TQ = 128
VMEM_LIMIT = 56 * 1024 * 1024


def _dot(a, b):
    return jnp.dot(a, b, preferred_element_type=F32)


def _dot_nt(a, b):
    return lax.dot_general(a, b, (((1,), (1,)), ((), ())), preferred_element_type=F32)


def _rms(x, g):
    return x * lax.rsqrt(jnp.mean(x * x, axis=-1, keepdims=True) + EPS) * g


def _head_mask(lane, hh):
    return (lane < HEAD_DIM) if hh == 0 else (lane >= HEAD_DIM)


def _bucket_map():
    i = np.arange(BLK)[:, None]
    jj = np.arange(2 * BLK)[None, :]
    d = i - jj + BLK
    n = np.maximum(d, 0)
    exact = NUM_BUCKETS // 2
    nf = np.maximum(n, 1).astype(np.float32)
    large = exact + (np.log(nf / exact) / math.log(MAX_DISTANCE / exact) * (NUM_BUCKETS - exact)).astype(np.int32)
    large = np.minimum(large, NUM_BUCKETS - 1)
    bucket = np.where(n < exact, n, large)
    return np.where(d >= 0, bucket, -1).astype(np.int32)


def _bias_tile_kernel(tab_ref, bmap_ref, o_ref):
    h = pl.program_id(0)
    bm = bmap_ref[...]
    base = tab_ref[NUM_BUCKETS - 1, h]
    acc = jnp.full(bm.shape, NEG, F32)
    for b in range(NUM_BUCKETS):
        acc = jnp.where(bm == b, tab_ref[b, h] - base, acc)
    o_ref[...] = acc


def _bias_tiles(rel_bias):
    n_heads = rel_bias.shape[1]
    bmap = jnp.asarray(_bucket_map())
    return pl.pallas_call(
        _bias_tile_kernel,
        grid=(n_heads,),
        in_specs=[pl.BlockSpec(memory_space=pltpu.SMEM),
                  pl.BlockSpec((BLK, 2 * BLK), lambda h: (0, 0))],
        out_specs=pl.BlockSpec((None, BLK, 2 * BLK), lambda h: (h, 0, 0)),
        out_shape=jax.ShapeDtypeStruct((n_heads, BLK, 2 * BLK), F32),
        name="bias_tiles",
    )(rel_bias, bmap)


def _norm_matmul_kernel(x_ref, g_ref, w_ref, o_ref):
    xn = _rms(x_ref[...], g_ref[...]).astype(BF16)
    o_ref[...] = _dot(xn, w_ref[...]).astype(o_ref.dtype)


def _norm_matmul(x2, g, w, tm=512):
    n, d = x2.shape
    m = w.shape[1]
    return pl.pallas_call(
        _norm_matmul_kernel,
        grid=(n // tm,),
        in_specs=[pl.BlockSpec((tm, d), lambda i: (i, 0)),
                  pl.BlockSpec((1, d), lambda i: (0, 0)),
                  pl.BlockSpec((d, m), lambda i: (0, 0))],
        out_specs=pl.BlockSpec((tm, m), lambda i: (i, 0)),
        out_shape=jax.ShapeDtypeStruct((n, m), BF16),
        compiler_params=pltpu.CompilerParams(dimension_semantics=("parallel",), vmem_limit_bytes=VMEM_LIMIT),
        name="norm_matmul",
    )(x2, g.reshape(1, d), w)


def _matmul_res_kernel(a_ref, w_ref, r_ref, o_ref):
    o_ref[...] = r_ref[...] + _dot(a_ref[...], w_ref[...])


def _matmul_res(a, w, res, tm=512):
    n, k = a.shape
    m = w.shape[1]
    return pl.pallas_call(
        _matmul_res_kernel,
        grid=(n // tm,),
        in_specs=[pl.BlockSpec((tm, k), lambda i: (i, 0)),
                  pl.BlockSpec((k, m), lambda i: (0, 0)),
                  pl.BlockSpec((tm, m), lambda i: (i, 0))],
        out_specs=pl.BlockSpec((tm, m), lambda i: (i, 0)),
        out_shape=jax.ShapeDtypeStruct((n, m), F32),
        compiler_params=pltpu.CompilerParams(dimension_semantics=("parallel",), vmem_limit_bytes=VMEM_LIMIT),
        name="matmul_res",
    )(a, w, res)


def _mlp_kernel(x_ref, g_ref, wu_ref, wd_ref, o_ref, xn_ref, acc_ref):
    f = pl.program_id(1)

    @pl.when(f == 0)
    def _():
        xn_ref[...] = _rms(x_ref[...], g_ref[...]).astype(BF16)
        acc_ref[...] = jnp.zeros_like(acc_ref)

    h = jnp.square(jnp.maximum(_dot(xn_ref[...], wu_ref[...]), 0.0)).astype(BF16)
    acc_ref[...] += _dot(h, wd_ref[...])

    @pl.when(f == pl.num_programs(1) - 1)
    def _():
        o_ref[...] = x_ref[...] + acc_ref[...]


def _mlp(x2, g, w_up, w_down, tm=512, tf=1024):
    n, d = x2.shape
    d_ff = w_up.shape[1]
    return pl.pallas_call(
        _mlp_kernel,
        grid=(n // tm, d_ff // tf),
        in_specs=[pl.BlockSpec((tm, d), lambda i, f: (i, 0)),
                  pl.BlockSpec((1, d), lambda i, f: (0, 0)),
                  pl.BlockSpec((d, tf), lambda i, f: (0, f)),
                  pl.BlockSpec((tf, d), lambda i, f: (f, 0))],
        out_specs=pl.BlockSpec((tm, d), lambda i, f: (i, 0)),
        out_shape=jax.ShapeDtypeStruct((n, d), F32),
        scratch_shapes=[pltpu.VMEM((tm, d), BF16), pltpu.VMEM((tm, d), F32)],
        compiler_params=pltpu.CompilerParams(dimension_semantics=("parallel", "arbitrary"),
                                             vmem_limit_bytes=VMEM_LIMIT),
        name="mlp",
    )(x2, g.reshape(1, d), w_up, w_down)


def _final_norm_kernel(x_ref, g_ref, o_ref):
    o_ref[...] = _rms(x_ref[...], g_ref[...])


def _final_norm(x2, g, tm=512):
    n, d = x2.shape
    return pl.pallas_call(
        _final_norm_kernel,
        grid=(n // tm,),
        in_specs=[pl.BlockSpec((tm, d), lambda i: (i, 0)),
                  pl.BlockSpec((1, d), lambda i: (0, 0))],
        out_specs=pl.BlockSpec((tm, d), lambda i: (i, 0)),
        out_shape=jax.ShapeDtypeStruct((n, d), F32),
        compiler_params=pltpu.CompilerParams(dimension_semantics=("parallel",)),
        name="final_norm",
    )(x2, g.reshape(1, d))


def _softmax_update(carry, s, v_blk):
    m, l, acc = carry
    m_new = jnp.maximum(m, jnp.max(s, axis=1, keepdims=True))
    alpha = jnp.exp(m - m_new)
    p = jnp.exp(s - m_new)
    l = alpha * l + jnp.sum(p, axis=1, keepdims=True)
    acc = alpha * acc + _dot(p.astype(BF16), v_blk)
    return m_new, l, acc


def _moba_kernel(q_ref, k_ref, v_ref, bt_ref, o_ref, kaug_ref, kmean_ref, qfar_ref, m_ref, l_ref, acc_ref,
                 *, n_blk, k_sel):
    j = pl.program_id(2)
    seq = k_ref.shape[0]
    n_grp_heads = bt_ref.shape[0]
    lane = lax.broadcasted_iota(I32, (BLK, LANES), 1)

    @pl.when(j == 0)
    def _():
        row = lax.broadcasted_iota(I32, (seq, LANES), 0)
        col = lax.broadcasted_iota(I32, (seq, LANES), 1)
        onehot = jnp.where(row // BLK == col, 1.0, 0.0).astype(BF16)
        for p in range(n_grp_heads // 2):
            kaug_ref[p, :, 0:LANES] = k_ref[:, p * LANES:(p + 1) * LANES]
            kaug_ref[p, :, LANES:2 * LANES] = onehot
            kmean_ref[p] = jnp.zeros((LANES, LANES), F32)
            for n in range(n_blk):
                kmean_ref[p, n:n + 1, :] = jnp.mean(
                    k_ref[n * BLK:(n + 1) * BLK, p * LANES:(p + 1) * LANES].astype(F32), axis=0, keepdims=True)

    own0 = pl.multiple_of(j * BLK, BLK)
    adj0 = pl.multiple_of(jnp.maximum(j - 1, 0) * BLK, BLK)

    for h in range(n_grp_heads):
        p, hh = divmod(h, 2)
        lanes = slice(p * LANES, (p + 1) * LANES)
        q = q_ref[:, lanes]
        qh = jnp.where(_head_mask(lane, hh), q, jnp.zeros_like(q))
        gate = _dot_nt(qh.astype(F32), kmean_ref[p])
        g = jnp.where(lane < j, gate, NEG)
        g = jnp.where(lane < n_blk, g, -jnp.inf)
        sel_bias = jnp.full((BLK, LANES), NEG, F32)
        for it in range(k_sel):
            mx = jnp.max(g, axis=1, keepdims=True)
            first = jnp.min(jnp.where(g == mx, lane, LANES), axis=1, keepdims=True)
            pick = lane == first
            sel_bias = jnp.where(pick, jnp.where(it < j, 0.0, NEG), sel_bias)
            g = jnp.where(pick, -jnp.inf, g)
        q_aug = jnp.concatenate([qh, sel_bias.astype(BF16)], axis=1)
        qfar_ref[h] = jnp.concatenate([qh, jnp.where(lane < j - 1, sel_bias, NEG).astype(BF16)], axis=1)

        s = _dot_nt(qh, k_ref[pl.ds(own0, BLK), lanes]) + bt_ref[h, :, BLK:2 * BLK]
        m = jnp.max(s, axis=1, keepdims=True)
        pr = jnp.exp(s - m)
        carry = (m, jnp.sum(pr, axis=1, keepdims=True), _dot(pr.astype(BF16), v_ref[pl.ds(own0, BLK), lanes]))
        s = _dot_nt(q_aug, kaug_ref[p, pl.ds(adj0, BLK), :]) + bt_ref[h, :, 0:BLK]
        m, l, acc = _softmax_update(carry, s, v_ref[pl.ds(adj0, BLK), lanes])
        m_ref[h] = m
        l_ref[h] = l
        acc_ref[h] = acc

    def far_pair(pi, _):
        n0 = pl.multiple_of(pi * 2 * BLK, 2 * BLK)
        for h in range(n_grp_heads):
            p = h // 2
            s = _dot_nt(qfar_ref[h], kaug_ref[p, pl.ds(n0, 2 * BLK), :])
            m, l, acc = _softmax_update((m_ref[h], l_ref[h], acc_ref[h]), s,
                                        v_ref[pl.ds(n0, 2 * BLK), p * LANES:(p + 1) * LANES])
            m_ref[h] = m
            l_ref[h] = l
            acc_ref[h] = acc
        return 0

    lax.fori_loop(0, jnp.maximum(j, 1) // 2, far_pair, 0)

    for p in range(n_grp_heads // 2):
        out_a = acc_ref[2 * p] / l_ref[2 * p]
        out_b = acc_ref[2 * p + 1] / l_ref[2 * p + 1]
        o_ref[:, p * LANES:(p + 1) * LANES] = jnp.where(lane < HEAD_DIM, out_a, out_b).astype(o_ref.dtype)


def _moba_attention(qkv, bt, n_heads, grp_heads=4):
    b, seq, _ = qkv.shape
    n_blk = seq // BLK
    n_grp = n_heads // grp_heads
    w = grp_heads // 2 * LANES
    k_sel = max(1, min(MOBA_TOPK, n_blk - 1))
    assert n_blk <= LANES and n_blk % 2 == 0
    return pl.pallas_call(
        functools.partial(_moba_kernel, n_blk=n_blk, k_sel=k_sel),
        grid=(b, n_grp, n_blk),
        in_specs=[pl.BlockSpec((None, BLK, w), lambda bi, g, j: (bi, j, g)),
                  pl.BlockSpec((None, seq, w), lambda bi, g, j: (bi, 0, n_grp + g)),
                  pl.BlockSpec((None, seq, w), lambda bi, g, j: (bi, 0, 2 * n_grp + g)),
                  pl.BlockSpec((grp_heads, BLK, 2 * BLK), lambda bi, g, j: (g, 0, 0))],
        out_specs=pl.BlockSpec((None, BLK, w), lambda bi, g, j: (bi, j, g)),
        out_shape=jax.ShapeDtypeStruct((b, seq, n_heads * HEAD_DIM), BF16),
        scratch_shapes=[pltpu.VMEM((grp_heads // 2, seq, 2 * LANES), BF16),
                        pltpu.VMEM((grp_heads // 2, LANES, LANES), F32),
                        pltpu.VMEM((grp_heads, BLK, 2 * LANES), BF16),
                        pltpu.VMEM((grp_heads, BLK, 1), F32),
                        pltpu.VMEM((grp_heads, BLK, 1), F32),
                        pltpu.VMEM((grp_heads, BLK, LANES), F32)],
        compiler_params=pltpu.CompilerParams(dimension_semantics=("parallel", "parallel", "arbitrary"),
                                             vmem_limit_bytes=VMEM_LIMIT),
        name="moba_attn",
    )(qkv, qkv, qkv, bt)


def _dsa_proj_kernel(x_ref, g_ref, win_ref, gq_ref, gkv_ref, wuq_ref, wqi_ref,
                     ckv_ref, kk_ref, widx_ref, qn_ref, qi_ref, *, idx_scale):
    hn = _rms(x_ref[...], g_ref[...]).astype(BF16)
    proj = _dot(hn, win_ref[...])
    c_q = _rms(proj[:, 0:DSA_LORA], gq_ref[...]).astype(BF16)
    ckv_ref[...] = _rms(proj[:, DSA_LORA:2 * DSA_LORA], gkv_ref[...]).astype(BF16)
    kk_ref[...] = proj[:, 2 * DSA_LORA:2 * DSA_LORA + LANES].astype(BF16)
    widx_ref[...] = proj[:, 2 * DSA_LORA + LANES:2 * DSA_LORA + 2 * LANES] * idx_scale
    qn_ref[...] = _dot(c_q, wuq_ref[...]).astype(BF16)
    qi_ref[...] = _dot(c_q, wqi_ref[...]).astype(BF16)


def _dsa_proj(x2, g, w_in_pad, g_q, g_kv, w_uq, w_qi, tm=512):
    n, d = x2.shape
    full = lambda a: pl.BlockSpec(a.shape, lambda i: (0,) * a.ndim)
    rows = lambda w: pl.BlockSpec((tm, w), lambda i: (i, 0))
    g2, gq2, gkv2 = g.reshape(1, d), g_q.reshape(1, -1), g_kv.reshape(1, -1)
    idx_scale = IDX_HEADS ** -0.5 * IDX_DIM ** -0.5
    return pl.pallas_call(
        functools.partial(_dsa_proj_kernel, idx_scale=idx_scale),
        grid=(n // tm,),
        in_specs=[rows(d), full(g2), full(w_in_pad), full(gq2), full(gkv2), full(w_uq), full(w_qi)],
        out_specs=[rows(DSA_LORA), rows(LANES), rows(LANES), rows(w_uq.shape[1]), rows(w_qi.shape[1])],
        out_shape=[jax.ShapeDtypeStruct((n, DSA_LORA), BF16),
                   jax.ShapeDtypeStruct((n, LANES), BF16),
                   jax.ShapeDtypeStruct((n, LANES), F32),
                   jax.ShapeDtypeStruct((n, w_uq.shape[1]), BF16),
                   jax.ShapeDtypeStruct((n, w_qi.shape[1]), BF16)],
        compiler_params=pltpu.CompilerParams(dimension_semantics=("parallel",), vmem_limit_bytes=VMEM_LIMIT),
        name="dsa_proj",
    )(x2, g2, w_in_pad, gq2, gkv2, w_uq, w_qi)


def _dsa_select_kernel(qi_ref, w_ref, kk_ref, o_ref, keys_ref, wb_ref, *, top_k, idx_bits):
    c = pl.program_id(1)
    n_kc, tq, tk = o_ref.shape
    n_proc = ((c + 1) * tq + tk - 1) // tk
    lane = lax.broadcasted_iota(I32, (tq, tk), 1)
    row_pos = c * tq + lax.broadcasted_iota(I32, (tq, tk), 0)
    lane_q = lax.broadcasted_iota(I32, (tq, LANES), 1)

    w = w_ref[...]
    for h in range(IDX_HEADS):
        wb_ref[h] = jnp.broadcast_to(w[:, h:h + 1], (tq, tk))
    qi = qi_ref[...]
    q_heads = []
    for h in range(IDX_HEADS):
        qp = qi[:, (h // 2) * LANES:(h // 2 + 1) * LANES]
        q_heads.append(jnp.where(_head_mask(lane_q, h % 2), qp, jnp.zeros_like(qp)))

    def score_chunk(kc, _):
        kt = kk_ref[pl.ds(pl.multiple_of(kc * tk, tk), tk), :]
        acc = jnp.zeros((tq, tk), F32)
        for h in range(IDX_HEADS):
            acc = acc + wb_ref[h] * jnp.maximum(_dot_nt(q_heads[h], kt), 0.0)
        sc = jnp.where(kc * tk + lane <= row_pos, acc, NEG)
        bits = pltpu.bitcast(sc, I32)
        keys_ref[kc] = jnp.where(bits < 0, bits ^ 0x7FFFFFFF, bits)
        return 0

    lax.fori_loop(0, n_proc, score_chunk, 0)

    def count(pred):
        def body(kc, part):
            return part + jnp.where(pred(keys_ref[kc], kc * tk + lane), 1, 0)
        part = lax.fori_loop(0, n_proc, body, jnp.zeros((tq, tk), I32))
        return jnp.sum(part, axis=1, keepdims=True)

    def value_bit(i, t):
        cand = t ^ jnp.left_shift(jnp.int32(1), 31 - i)
        cnt = count(lambda key, pos: key >= cand)
        return jnp.where(cnt >= top_k, cand, t)

    thr = lax.fori_loop(0, 32, value_bit, jnp.full((tq, 1), INT_MIN, I32))
    cnt_ge = count(lambda key, pos: key >= thr)
    cnt_gt = count(lambda key, pos: key > thr)
    need = top_k - cnt_gt

    def tie_cut():
        def index_bit(i, x):
            cand = x | jnp.left_shift(jnp.int32(1), idx_bits - 1 - i)
            cnt = count(lambda key, pos: (key == thr) & (pos < cand))
            return jnp.where(cnt < need, cand, x)
        return lax.fori_loop(0, idx_bits, index_bit, jnp.zeros((tq, 1), I32))

    last_tie = lax.cond(jnp.max(cnt_ge) > top_k, tie_cut, lambda: jnp.full((tq, 1), 2 ** idx_bits, I32))

    def write_chunk(kc, _):
        key = keys_ref[kc]
        pos = kc * tk + lane
        chosen = (key > thr) | ((key == thr) & (pos <= last_tie))
        o_ref[kc] = jnp.where(chosen & (pos <= row_pos), 0.0, NEG).astype(o_ref.dtype)
        return 0

    lax.fori_loop(0, n_proc, write_chunk, 0)

    def fill_chunk(kc, _):
        o_ref[kc] = jnp.full((tq, tk), NEG, o_ref.dtype)
        return 0

    lax.fori_loop(n_proc, n_kc, fill_chunk, 0)


def _dsa_select(qi, widx, kk, batch, seq):
    tq, tk = SEL_TQ, BLK
    n_kc = seq // tk
    top_k = min(DSA_TOPK_MAX, seq // 4)
    idx_bits = max(1, (seq - 1).bit_length())
    qi3 = qi.reshape(batch, seq, -1)
    return pl.pallas_call(
        functools.partial(_dsa_select_kernel, top_k=top_k, idx_bits=idx_bits),
        grid=(batch, seq // tq),
        in_specs=[pl.BlockSpec((None, tq, qi3.shape[2]), lambda b, c: (b, c, 0)),
                  pl.BlockSpec((None, tq, LANES), lambda b, c: (b, c, 0)),
                  pl.BlockSpec((None, seq, LANES), lambda b, c: (b, 0, 0))],
        out_specs=pl.BlockSpec((None, n_kc, tq, tk), lambda b, c: (b, 0, c, 0)),
        out_shape=jax.ShapeDtypeStruct((batch, n_kc, seq, tk), BF16),
        scratch_shapes=[pltpu.VMEM((n_kc, tq, tk), I32), pltpu.VMEM((IDX_HEADS, tq, tk), F32)],
        compiler_params=pltpu.CompilerParams(dimension_semantics=("parallel", "arbitrary"),
                                             vmem_limit_bytes=VMEM_LIMIT),
        name="dsa_select",
    )(qi3, widx.reshape(batch, seq, LANES), kk.reshape(batch, seq, LANES))


def _dsa_attn_kernel(qn_ref, ckv_ref, mb_ref, bt_ref, wuk_ref, wuv_ref, o_ref,
                     qlat_ref, acc_ref, m_ref, l_ref, *, n_heads, grp):
    qt = pl.program_id(1)
    t = BLK
    lane = lax.broadcasted_iota(I32, (t, LANES), 1)

    for hp in range(n_heads // 2):
        qp = qn_ref[:, hp * LANES:(hp + 1) * LANES]
        for hh in range(2):
            qh = jnp.where(_head_mask(lane, hh), qp, jnp.zeros_like(qp))
            h = 2 * hp + hh
            qlat_ref[h * t:(h + 1) * t, :] = _dot(qh, wuk_ref[hp]).astype(BF16)

    m_ref[...] = jnp.full(m_ref.shape, NEG, F32)
    l_ref[...] = jnp.zeros_like(l_ref)
    acc_ref[...] = jnp.zeros_like(acc_ref)

    def process_tile(kk, bias_off):
        kt = ckv_ref[pl.ds(pl.multiple_of(kk * t, t), t), :]
        mb = mb_ref[kk].astype(F32)
        for g in range(n_heads // grp):
            s_all = _dot_nt(qlat_ref[g * grp * t:(g + 1) * grp * t, :], kt)
            probs, alphas = [], []
            for i in range(grp):
                h = g * grp + i
                s = s_all[i * t:(i + 1) * t, :] + mb
                if bias_off is not None:
                    s = s + bt_ref[h, :, bias_off:bias_off + t]
                m_old = m_ref[h]
                m_new = jnp.maximum(m_old, jnp.max(s, axis=1, keepdims=True))
                alpha = jnp.exp(m_old - m_new)
                p = jnp.exp(s - m_new)
                l_ref[h] = alpha * l_ref[h] + jnp.sum(p, axis=1, keepdims=True)
                m_ref[h] = m_new
                probs.append(p.astype(BF16))
                alphas.append(alpha)
            pv = _dot(jnp.concatenate(probs, axis=0), kt)
            for i in range(grp):
                h = g * grp + i
                acc_ref[h] = alphas[i] * acc_ref[h] + pv[i * t:(i + 1) * t, :]

    process_tile(qt, BLK)

    @pl.when(qt >= 1)
    def _():
        process_tile(qt - 1, 0)

    def far_tile(kk, _):
        process_tile(kk, None)
        return 0

    lax.fori_loop(0, jnp.maximum(qt - 1, 0), far_tile, 0)

    for hp in range(n_heads // 2):
        out = jnp.zeros((t, LANES), F32)
        for hh in range(2):
            h = 2 * hp + hh
            out = out + _dot((acc_ref[h] / l_ref[h]).astype(BF16), wuv_ref[h])
        o_ref[:, hp * LANES:(hp + 1) * LANES] = out.astype(o_ref.dtype)


def _dsa_attention(qn, ckv, mask_bias, bt, w_uk_pairs, w_uv_pad, batch, seq, n_heads):
    t = BLK
    n_kc = seq // t
    d_q = qn.shape[-1]
    full = lambda a: pl.BlockSpec(a.shape, lambda b, q: (0,) * a.ndim)
    return pl.pallas_call(
        functools.partial(_dsa_attn_kernel, n_heads=n_heads, grp=4),
        grid=(batch, seq // t),
        in_specs=[pl.BlockSpec((None, t, d_q), lambda b, q: (b, q, 0)),
                  pl.BlockSpec((None, seq, DSA_LORA), lambda b, q: (b, 0, 0)),
                  pl.BlockSpec((None, n_kc, t, t), lambda b, q: (b, 0, q, 0)),
                  full(bt), full(w_uk_pairs), full(w_uv_pad)],
        out_specs=pl.BlockSpec((None, t, d_q), lambda b, q: (b, q, 0)),
        out_shape=jax.ShapeDtypeStruct((batch, seq, d_q), BF16),
        scratch_shapes=[pltpu.VMEM((n_heads * t, DSA_LORA), BF16),
                        pltpu.VMEM((n_heads, t, DSA_LORA), F32),
                        pltpu.VMEM((n_heads, t, 1), F32),
                        pltpu.VMEM((n_heads, t, 1), F32)],
        compiler_params=pltpu.CompilerParams(dimension_semantics=("parallel", "arbitrary"),
                                             vmem_limit_bytes=VMEM_LIMIT),
        name="dsa_attn",
    )(qn.reshape(batch, seq, d_q), ckv.reshape(batch, seq, DSA_LORA), mask_bias, bt, w_uk_pairs, w_uv_pad)


def _moba_layer(x2, batch, seq, g, w_qkv, w_o, bt, n_heads):
    d = x2.shape[1]
    col_scale = jnp.concatenate([jnp.full((d,), HEAD_DIM ** -0.5, F32), jnp.ones((2 * d,), F32)])
    qkv = _norm_matmul(x2, g, (w_qkv * col_scale).astype(BF16))
    attn = _moba_attention(qkv.reshape(batch, seq, 3 * d), bt, n_heads)
    return _matmul_res(attn.reshape(batch * seq, d), w_o.astype(BF16), x2)


def _dsa_layer(x2, batch, seq, g, w_in, g_q, g_kv, w_uq, w_qi, w_uk, w_uv, w_o, bt, n_heads):
    d = x2.shape[1]
    k_lo = 2 * DSA_LORA
    w_k = w_in[:, k_lo:k_lo + IDX_DIM]
    w_w = w_in[:, k_lo + IDX_DIM:]
    w_in_pad = jnp.concatenate(
        [w_in[:, :k_lo], w_k, w_k, w_w, jnp.zeros((d, LANES - IDX_HEADS), F32)], axis=1).astype(BF16)
    ckv, kk, widx, qn, qi = _dsa_proj(x2, g, w_in_pad, g_q, g_kv,
                                      (w_uq * HEAD_DIM ** -0.5).astype(BF16), w_qi.astype(BF16))
    mask_bias = _dsa_select(qi, widx, kk, batch, seq)
    w_uk_pairs = w_uk.reshape(n_heads // 2, 2 * HEAD_DIM, DSA_LORA).astype(BF16)
    zeros = jnp.zeros_like(w_uv)
    even = jnp.concatenate([w_uv, zeros], axis=-1)
    odd = jnp.concatenate([zeros, w_uv], axis=-1)
    is_even = (jnp.arange(n_heads) % 2 == 0)[:, None, None]
    w_uv_pad = jnp.where(is_even, even, odd).astype(BF16)
    attn = _dsa_attention(qn, ckv, mask_bias, bt, w_uk_pairs, w_uv_pad, batch, seq, n_heads)
    return _matmul_res(attn.reshape(batch * seq, -1), w_o.astype(BF16), x2)


def kernel(x, rel_bias, ln_attn, ln_mlp, moba_w_qkv, moba_w_o, dsa_w_in, dsa_g_q, dsa_g_kv, dsa_w_uq, dsa_w_qi,
           dsa_w_uk, dsa_w_uv, dsa_w_o, mlp_w_up, mlp_w_down, final_norm):
    batch, seq, d = x.shape
    n_heads = rel_bias.shape[1]
    depth = ln_attn.shape[0]
    bt = _bias_tiles(rel_bias)
    x2 = x.reshape(batch * seq, d)
    for i in range(depth):
        j = i // 2
        if i % 2 == 0:
            x2 = _moba_layer(x2, batch, seq, ln_attn[i], moba_w_qkv[j], moba_w_o[j], bt, n_heads)
        else:
            x2 = _dsa_layer(x2, batch, seq, ln_attn[i], dsa_w_in[j], dsa_g_q[j], dsa_g_kv[j], dsa_w_uq[j],
                            dsa_w_qi[j], dsa_w_uk[j], dsa_w_uv[j], dsa_w_o[j], bt, n_heads)
        x2 = _mlp(x2, ln_mlp[i], mlp_w_up[i].astype(BF16), mlp_w_down[i].astype(BF16))
    return _final_norm(x2, final_norm).reshape(batch, seq, d)
```

```python
import functools
import math

import numpy as np
import jax
import jax.numpy as jnp
from jax import lax
from jax.experimental import pallas as pl
from jax.experimental.pallas import tpu as pltpu

F32 = jnp.float32
BF16 = jnp.bfloat16
I32 = jnp.int32

EPS = 1e-6
NEG = -1e30
INT_MIN = -(2 ** 31)

NUM_BUCKETS = 32
MAX_DISTANCE = 128
HEAD_DIM = 64
LANES = 128
BF16_ROWS = 16
BLK = 256
MOBA_TOPK = 3
MOBA_GRP = 4
DSA_TOPK_MAX = 256
DSA_LORA = 256
DSA_GRP = 4
IDX_HEADS = 8
IDX_DIM = 64
VMEM_LIMIT = 56 * 1024 * 1024


def _dot(a, b):
    return jnp.dot(a, b, preferred_element_type=F32)


def _dot_nt(a, b):
    return lax.dot_general(a, b, (((1,), (1,)), ((), ())), preferred_element_type=F32)


def _rms(x, g):
    return x * lax.rsqrt(jnp.mean(x * x, axis=-1, keepdims=True) + EPS) * g


def _cparams(*sem):
    return pltpu.CompilerParams(dimension_semantics=sem, vmem_limit_bytes=VMEM_LIMIT)


def _full(a, n_grid):
    return pl.BlockSpec(a.shape, lambda *_: (0,) * a.ndim)


def _bucket_map_t():
    jj = np.arange(2 * BLK)[:, None]
    i = np.arange(BLK)[None, :]
    d = i - jj + BLK
    n = np.maximum(d, 0)
    exact = NUM_BUCKETS // 2
    nf = np.maximum(n, 1).astype(np.float32)
    large = exact + (np.log(nf / exact) / math.log(MAX_DISTANCE / exact) * (NUM_BUCKETS - exact)).astype(np.int32)
    large = np.minimum(large, NUM_BUCKETS - 1)
    bucket = np.where(n < exact, n, large)
    return np.where(d >= 0, bucket, -1).astype(np.int32)


def _bias_tile_kernel(tab_ref, bmap_ref, o_ref):
    h = pl.program_id(0)
    bm = bmap_ref[...]
    base = tab_ref[NUM_BUCKETS - 1, h]
    acc = jnp.full(bm.shape, NEG, F32)
    for b in range(NUM_BUCKETS):
        acc = jnp.where(bm == b, tab_ref[b, h] - base, acc)
    o_ref[...] = acc


def _bias_tiles(rel_bias):
    n_heads = rel_bias.shape[1]
    bmap = jnp.asarray(_bucket_map_t())
    return pl.pallas_call(
        _bias_tile_kernel,
        grid=(n_heads,),
        in_specs=[pl.BlockSpec(memory_space=pltpu.SMEM),
                  pl.BlockSpec((2 * BLK, BLK), lambda h: (0, 0))],
        out_specs=pl.BlockSpec((None, 2 * BLK, BLK), lambda h: (h, 0, 0)),
        out_shape=jax.ShapeDtypeStruct((n_heads, 2 * BLK, BLK), F32),
        name="bias_tiles",
    )(rel_bias, bmap)


def _matmul_res_kernel(a_ref, w_ref, r_ref, o_ref):
    o_ref[...] = r_ref[...] + _dot(a_ref[...], w_ref[...])


def _matmul_res(a, w, res, tm=512):
    n, k = a.shape
    m = w.shape[1]
    return pl.pallas_call(
        _matmul_res_kernel,
        grid=(n // tm,),
        in_specs=[pl.BlockSpec((tm, k), lambda i: (i, 0)),
                  pl.BlockSpec((k, m), lambda i: (0, 0)),
                  pl.BlockSpec((tm, m), lambda i: (i, 0))],
        out_specs=pl.BlockSpec((tm, m), lambda i: (i, 0)),
        out_shape=jax.ShapeDtypeStruct((n, m), F32),
        compiler_params=_cparams("parallel"),
        name="matmul_res",
    )(a, w, res)


def _mlp_kernel(x_ref, g_ref, wu_ref, wd_ref, o_ref, xn_ref, acc_ref):
    f = pl.program_id(1)

    @pl.when(f == 0)
    def _():
        xn_ref[...] = _rms(x_ref[...], g_ref[...]).astype(BF16)
        acc_ref[...] = jnp.zeros_like(acc_ref)

    h = jnp.square(jnp.maximum(_dot(xn_ref[...], wu_ref[...]), 0.0)).astype(BF16)
    acc_ref[...] += _dot(h, wd_ref[...])

    @pl.when(f == pl.num_programs(1) - 1)
    def _():
        o_ref[...] = x_ref[...] + acc_ref[...]


def _mlp(x2, g, w_up, w_down, tm=512, tf=1024):
    n, d = x2.shape
    d_ff = w_up.shape[1]
    return pl.pallas_call(
        _mlp_kernel,
        grid=(n // tm, d_ff // tf),
        in_specs=[pl.BlockSpec((tm, d), lambda i, f: (i, 0)),
                  pl.BlockSpec((1, d), lambda i, f: (0, 0)),
                  pl.BlockSpec((d, tf), lambda i, f: (0, f)),
                  pl.BlockSpec((tf, d), lambda i, f: (f, 0))],
        out_specs=pl.BlockSpec((tm, d), lambda i, f: (i, 0)),
        out_shape=jax.ShapeDtypeStruct((n, d), F32),
        scratch_shapes=[pltpu.VMEM((tm, d), BF16), pltpu.VMEM((tm, d), F32)],
        compiler_params=_cparams("parallel", "arbitrary"),
        name="mlp",
    )(x2, g.reshape(1, d), w_up, w_down)


def _final_norm_kernel(x_ref, g_ref, o_ref):
    o_ref[...] = _rms(x_ref[...], g_ref[...])


def _final_norm(x2, g, tm=512):
    n, d = x2.shape
    return pl.pallas_call(
        _final_norm_kernel,
        grid=(n // tm,),
        in_specs=[pl.BlockSpec((tm, d), lambda i: (i, 0)),
                  pl.BlockSpec((1, d), lambda i: (0, 0))],
        out_specs=pl.BlockSpec((tm, d), lambda i: (i, 0)),
        out_shape=jax.ShapeDtypeStruct((n, d), F32),
        compiler_params=_cparams("parallel"),
        name="final_norm",
    )(x2, g.reshape(1, d))


def _softmax_step(m_old, l_old, s):
    m_new = jnp.maximum(m_old, jnp.max(s, axis=0, keepdims=True))
    alpha = jnp.exp(m_old - m_new)
    p = jnp.exp(s - m_new)
    l_new = alpha * l_old + jnp.sum(p, axis=0, keepdims=True)
    return m_new, l_new, alpha, p.astype(BF16)


def _moba_proj_kernel(x_ref, g_ref, wqt_ref, wk_ref, wvt_ref, qt_ref, k_ref, vt_ref):
    xn = _rms(x_ref[...], g_ref[...]).astype(BF16)
    qt_ref[...] = _dot_nt(wqt_ref[...], xn).astype(BF16)
    k_ref[...] = _dot(xn, wk_ref[...]).astype(BF16)
    vt_ref[...] = _dot_nt(wvt_ref[...], xn).astype(BF16)


def _moba_proj(x3, g, wq_t, wk, wv_t):
    b, seq, d = x3.shape
    g2 = g.reshape(1, d)
    return pl.pallas_call(
        _moba_proj_kernel,
        grid=(b, seq // BLK),
        in_specs=[pl.BlockSpec((None, BLK, d), lambda bi, i: (bi, i, 0)),
                  _full(g2, 2), _full(wq_t, 2), _full(wk, 2), _full(wv_t, 2)],
        out_specs=[pl.BlockSpec((None, d, BLK), lambda bi, i: (bi, 0, i)),
                   pl.BlockSpec((None, BLK, d), lambda bi, i: (bi, i, 0)),
                   pl.BlockSpec((None, None, d, BLK), lambda bi, i: (bi, i, 0, 0))],
        out_shape=[jax.ShapeDtypeStruct((b, d, seq), BF16),
                   jax.ShapeDtypeStruct((b, seq, d), BF16),
                   jax.ShapeDtypeStruct((b, seq // BLK, d, BLK), BF16)],
        compiler_params=_cparams("parallel", "parallel"),
        name="moba_proj",
    )(x3, g2, wq_t, wk, wv_t)


def _moba_kernel(qt_ref, k_ref, vt_ref, bt_ref, o_ref, kaug_ref, kmean_ref, qfar_ref, m_ref, l_ref, acc_ref,
                 *, n_blk, k_sel, sel_rows):
    j = pl.program_id(2)
    seq = k_ref.shape[0]
    n_pairs = MOBA_GRP // 2
    row = lax.broadcasted_iota(I32, (LANES, BLK), 0)
    blk_id = lax.broadcasted_iota(I32, (sel_rows, BLK), 0)

    @pl.when(j == 0)
    def _():
        krow = lax.broadcasted_iota(I32, (seq, LANES), 0)
        kcol = lax.broadcasted_iota(I32, (seq, LANES), 1)
        onehot = jnp.where(krow // BLK == kcol, 1.0, 0.0).astype(BF16)
        for p in range(n_pairs):
            kaug_ref[p, :, 0:LANES] = k_ref[:, p * LANES:(p + 1) * LANES]
            kaug_ref[p, :, LANES:2 * LANES] = onehot
            kmean_ref[p] = jnp.zeros((LANES, LANES), F32)
            for n in range(n_blk):
                kmean_ref[p, n:n + 1, :] = jnp.mean(
                    k_ref[n * BLK:(n + 1) * BLK, p * LANES:(p + 1) * LANES].astype(F32), axis=0, keepdims=True)

    own0 = pl.multiple_of(j * BLK, BLK)
    adj = jnp.maximum(j - 1, 0)
    adj0 = pl.multiple_of(adj * BLK, BLK)
    pad_rows = jnp.zeros((LANES - sel_rows, BLK), BF16)

    for h in range(MOBA_GRP):
        p, hh = divmod(h, 2)
        feats = slice(p * LANES, (p + 1) * LANES)
        qp = qt_ref[feats, :]
        qh = jnp.where((row < HEAD_DIM) if hh == 0 else (row >= HEAD_DIM), qp, jnp.zeros_like(qp))
        gate = _dot(kmean_ref[p], qh.astype(F32))[0:sel_rows, :]
        g = jnp.where(blk_id < j, gate, NEG)
        g = jnp.where(blk_id < n_blk, g, -jnp.inf)
        sel_bias = jnp.full((sel_rows, BLK), NEG, F32)
        for it in range(k_sel):
            mx = jnp.max(g, axis=0, keepdims=True)
            first = jnp.min(jnp.where(g == mx, blk_id, LANES), axis=0, keepdims=True)
            pick = blk_id == first
            sel_bias = jnp.where(pick, jnp.where(it < j, 0.0, NEG), sel_bias)
            g = jnp.where(pick, -jnp.inf, g)
        q_aug = jnp.concatenate([qh, sel_bias.astype(BF16), pad_rows], axis=0)
        far_bias = jnp.where(blk_id < j - 1, sel_bias, NEG).astype(BF16)
        qfar_ref[h] = jnp.concatenate([qh, far_bias, pad_rows], axis=0)

        s = _dot(k_ref[pl.ds(own0, BLK), feats], qh) + bt_ref[h, BLK:2 * BLK, :]
        m = jnp.max(s, axis=0, keepdims=True)
        pr = jnp.exp(s - m)
        l = jnp.sum(pr, axis=0, keepdims=True)
        acc = _dot(vt_ref[j, feats, :], pr.astype(BF16))
        s = _dot(kaug_ref[p, pl.ds(adj0, BLK), :], q_aug) + bt_ref[h, 0:BLK, :]
        m, l, alpha, pr = _softmax_step(m, l, s)
        m_ref[h] = m
        l_ref[h] = l
        acc_ref[h] = alpha * acc + _dot(vt_ref[adj, feats, :], pr)

    def far_pair(pi, _):
        n0 = pl.multiple_of(pi * 2 * BLK, 2 * BLK)
        for h in range(MOBA_GRP):
            p = h // 2
            feats = slice(p * LANES, (p + 1) * LANES)
            s = _dot(kaug_ref[p, pl.ds(n0, 2 * BLK), :], qfar_ref[h])
            m, l, alpha, pr = _softmax_step(m_ref[h], l_ref[h], s)
            m_ref[h] = m
            l_ref[h] = l
            pv = _dot(vt_ref[2 * pi, feats, :], pr[0:BLK, :]) + _dot(vt_ref[2 * pi + 1, feats, :], pr[BLK:2 * BLK, :])
            acc_ref[h] = alpha * acc_ref[h] + pv
        return 0

    lax.fori_loop(0, jnp.maximum(j, 1) // 2, far_pair, 0)

    for p in range(n_pairs):
        out_a = acc_ref[2 * p] / l_ref[2 * p]
        out_b = acc_ref[2 * p + 1] / l_ref[2 * p + 1]
        out_t = jnp.where(row < HEAD_DIM, out_a, out_b)
        o_ref[:, p * LANES:(p + 1) * LANES] = out_t.T.astype(o_ref.dtype)


def _moba_attention(q_t, k, v_t, bt, n_heads):
    b, seq, d = k.shape
    n_blk = seq // BLK
    n_grp = n_heads // MOBA_GRP
    w = MOBA_GRP * HEAD_DIM
    k_sel = max(1, min(MOBA_TOPK, n_blk - 1))
    sel_rows = BF16_ROWS * pl.cdiv(n_blk, BF16_ROWS)
    assert sel_rows <= LANES and n_blk % 2 == 0
    return pl.pallas_call(
        functools.partial(_moba_kernel, n_blk=n_blk, k_sel=k_sel, sel_rows=sel_rows),
        grid=(b, n_grp, n_blk),
        in_specs=[pl.BlockSpec((None, w, BLK), lambda bi, g, j: (bi, g, j)),
                  pl.BlockSpec((None, seq, w), lambda bi, g, j: (bi, 0, g)),
                  pl.BlockSpec((None, n_blk, w, BLK), lambda bi, g, j: (bi, 0, g, 0)),
                  pl.BlockSpec((MOBA_GRP, 2 * BLK, BLK), lambda bi, g, j: (g, 0, 0))],
        out_specs=pl.BlockSpec((None, BLK, w), lambda bi, g, j: (bi, j, g)),
        out_shape=jax.ShapeDtypeStruct((b, seq, d), BF16),
        scratch_shapes=[pltpu.VMEM((MOBA_GRP // 2, seq, 2 * LANES), BF16),
                        pltpu.VMEM((MOBA_GRP // 2, LANES, LANES), F32),
                        pltpu.VMEM((MOBA_GRP, 2 * LANES, BLK), BF16),
                        pltpu.VMEM((MOBA_GRP, 1, BLK), F32),
                        pltpu.VMEM((MOBA_GRP, 1, BLK), F32),
                        pltpu.VMEM((MOBA_GRP, LANES, BLK), F32)],
        compiler_params=_cparams("parallel", "parallel", "arbitrary"),
        name="moba_attn",
    )(q_t, k, v_t, bt)


def _dsa_proj_kernel(x_ref, g_ref, win_ref, wkvt_ref, wwt_ref, gq_ref, gkv_ref, gkvt_ref, wuqt_ref, wqit_ref,
                     ckv_ref, ckvt_ref, kk_ref, widx_ref, qnt_ref, qit_ref, *, idx_scale):
    hn = _rms(x_ref[...], g_ref[...]).astype(BF16)
    proj = _dot(hn, win_ref[...])
    c_q = _rms(proj[:, 0:DSA_LORA], gq_ref[...]).astype(BF16)
    ckv_ref[...] = _rms(proj[:, DSA_LORA:2 * DSA_LORA], gkv_ref[...]).astype(BF16)
    kk_ref[...] = proj[:, 2 * DSA_LORA:2 * DSA_LORA + IDX_DIM].astype(BF16)
    ct = _dot_nt(wkvt_ref[...], hn)
    scale = lax.rsqrt(jnp.mean(ct * ct, axis=0, keepdims=True) + EPS)
    gt = jnp.concatenate([gkvt_ref[...]] * (ct.shape[1] // LANES), axis=1)
    ckvt_ref[...] = (ct * scale * gt).astype(BF16)
    widx_ref[...] = _dot_nt(wwt_ref[...], hn)[0:IDX_HEADS, :] * idx_scale
    qnt_ref[...] = _dot_nt(wuqt_ref[...], c_q).astype(BF16)
    qit_ref[...] = _dot_nt(wqit_ref[...], c_q).astype(BF16)


def _dsa_proj(x3, g, w_in_row, w_kv_t, w_w_t, g_q, g_kv, w_uq_t, w_qi_t):
    b, seq, d = x3.shape
    tm = BLK
    g2, gq2, gkv2 = g.reshape(1, d), g_q.reshape(1, -1), g_kv.reshape(1, -1)
    gkv_t = jnp.broadcast_to(g_kv.reshape(-1, 1), (DSA_LORA, LANES))
    n_q, n_i = w_uq_t.shape[0], w_qi_t.shape[0]
    consts = (g2, w_in_row, w_kv_t, w_w_t, gq2, gkv2, gkv_t, w_uq_t, w_qi_t)
    idx_scale = IDX_HEADS ** -0.5 * IDX_DIM ** -0.5
    return pl.pallas_call(
        functools.partial(_dsa_proj_kernel, idx_scale=idx_scale),
        grid=(b, seq // tm),
        in_specs=[pl.BlockSpec((None, tm, d), lambda bi, i: (bi, i, 0))] + [_full(a, 2) for a in consts],
        out_specs=[pl.BlockSpec((None, tm, DSA_LORA), lambda bi, i: (bi, i, 0)),
                   pl.BlockSpec((None, None, DSA_LORA, tm), lambda bi, i: (bi, i, 0, 0)),
                   pl.BlockSpec((None, tm, IDX_DIM), lambda bi, i: (bi, i, 0)),
                   pl.BlockSpec((None, IDX_HEADS, tm), lambda bi, i: (bi, 0, i)),
                   pl.BlockSpec((None, n_q, tm), lambda bi, i: (bi, 0, i)),
                   pl.BlockSpec((None, n_i, tm), lambda bi, i: (bi, 0, i))],
        out_shape=[jax.ShapeDtypeStruct((b, seq, DSA_LORA), BF16),
                   jax.ShapeDtypeStruct((b, seq // tm, DSA_LORA, tm), BF16),
                   jax.ShapeDtypeStruct((b, seq, IDX_DIM), BF16),
                   jax.ShapeDtypeStruct((b, IDX_HEADS, seq), F32),
                   jax.ShapeDtypeStruct((b, n_q, seq), BF16),
                   jax.ShapeDtypeStruct((b, n_i, seq), BF16)],
        compiler_params=_cparams("parallel", "parallel"),
        name="dsa_proj",
    )(x3, *consts)


def _dsa_select_kernel(qit_ref, w_ref, kk_ref, o_ref, keys_ref, *, top_k, idx_bits):
    c = pl.program_id(1)
    n_kc, tk, tq = keys_ref.shape
    n_proc = c + 1
    key_row = lax.broadcasted_iota(I32, (tk, tq), 0)
    q_pos = c * tq + lax.broadcasted_iota(I32, (tk, tq), 1)
    w = w_ref[...]

    def score_chunk(kc, _):
        kt = kk_ref[pl.ds(pl.multiple_of(kc * tk, tk), tk), :]
        acc = jnp.zeros((tk, tq), F32)
        for h in range(IDX_HEADS):
            x = _dot(kt, qit_ref[h * IDX_DIM:(h + 1) * IDX_DIM, :])
            acc = acc + w[h:h + 1, :] * jnp.maximum(x, 0.0)
        sc = jnp.where(kc * tk + key_row <= q_pos, acc, NEG)
        bits = pltpu.bitcast(sc, I32)
        keys_ref[kc] = jnp.where(bits < 0, bits ^ 0x7FFFFFFF, bits)
        return 0

    lax.fori_loop(0, n_proc, score_chunk, 0)

    def count(pred):
        def body(kc, tot):
            hit = jnp.where(pred(keys_ref[kc], kc * tk + key_row), 1, 0)
            return tot + jnp.sum(hit, axis=0, keepdims=True)
        return lax.fori_loop(0, n_proc, body, jnp.zeros((1, tq), I32))

    def value_bit(i, t):
        cand = t ^ jnp.left_shift(jnp.int32(1), 31 - i)
        cnt = count(lambda key, pos: key >= cand)
        return jnp.where(cnt >= top_k, cand, t)

    thr = lax.fori_loop(0, 32, value_bit, jnp.full((1, tq), INT_MIN, I32))
    cnt_ge = count(lambda key, pos: key >= thr)
    cnt_gt = count(lambda key, pos: key > thr)
    need = top_k - cnt_gt

    def tie_cut():
        def index_bit(i, x):
            cand = x | jnp.left_shift(jnp.int32(1), idx_bits - 1 - i)
            cnt = count(lambda key, pos: (key == thr) & (pos < cand))
            return jnp.where(cnt < need, cand, x)
        return lax.fori_loop(0, idx_bits, index_bit, jnp.zeros((1, tq), I32))

    last_tie = lax.cond(jnp.max(cnt_ge) > top_k, tie_cut, lambda: jnp.full((1, tq), 2 ** idx_bits, I32))

    def write_chunk(kc, _):
        key = keys_ref[kc]
        pos = kc * tk + key_row
        chosen = (key > thr) | ((key == thr) & (pos <= last_tie))
        rows = pl.ds(pl.multiple_of(kc * tk, tk), tk)
        o_ref[rows, :] = jnp.where(chosen & (pos <= q_pos), 0.0, NEG).astype(o_ref.dtype)
        return 0

    lax.fori_loop(0, n_proc, write_chunk, 0)

    def fill_chunk(kc, _):
        o_ref[pl.ds(pl.multiple_of(kc * tk, tk), tk), :] = jnp.full((tk, tq), NEG, o_ref.dtype)
        return 0

    lax.fori_loop(n_proc, n_kc, fill_chunk, 0)


def _dsa_select(qi_t, widx_t, kk):
    b, seq, _ = kk.shape
    t = BLK
    n_kc = seq // t
    top_k = min(DSA_TOPK_MAX, seq // 4)
    idx_bits = max(1, (seq - 1).bit_length())
    return pl.pallas_call(
        functools.partial(_dsa_select_kernel, top_k=top_k, idx_bits=idx_bits),
        grid=(b, seq // t),
        in_specs=[pl.BlockSpec((None, qi_t.shape[1], t), lambda bi, c: (bi, 0, c)),
                  pl.BlockSpec((None, IDX_HEADS, t), lambda bi, c: (bi, 0, c)),
                  pl.BlockSpec((None, seq, IDX_DIM), lambda bi, c: (bi, 0, 0))],
        out_specs=pl.BlockSpec((None, None, seq, t), lambda bi, c: (bi, c, 0, 0)),
        out_shape=jax.ShapeDtypeStruct((b, seq // t, seq, t), BF16),
        scratch_shapes=[pltpu.VMEM((n_kc, t, t), I32)],
        compiler_params=_cparams("parallel", "arbitrary"),
        name="dsa_select",
    )(qi_t, widx_t, kk)


def _dsa_attn_kernel(qnt_ref, ckv_ref, ckvt_ref, mb_ref, bt_ref, wukt_ref, wuvt_ref, o_ref,
                     qlat_ref, acc_ref, m_ref, l_ref, *, n_heads):
    qt = pl.program_id(1)
    t = BLK
    row = lax.broadcasted_iota(I32, (LANES, t), 0)

    for hp in range(n_heads // 2):
        qp = qnt_ref[hp * LANES:(hp + 1) * LANES, :]
        for hh in range(2):
            qh = jnp.where((row < HEAD_DIM) if hh == 0 else (row >= HEAD_DIM), qp, jnp.zeros_like(qp))
            h = 2 * hp + hh
            qlat_ref[:, h * t:(h + 1) * t] = _dot(wukt_ref[hp], qh).astype(BF16)

    m_ref[...] = jnp.full(m_ref.shape, NEG, F32)
    l_ref[...] = jnp.zeros_like(l_ref)
    acc_ref[...] = jnp.zeros_like(acc_ref)

    def process_tile(kk, bias_row0):
        rows = pl.ds(pl.multiple_of(kk * t, t), t)
        kt = ckv_ref[rows, :]
        kt_t = ckvt_ref[kk]
        mb = mb_ref[rows, :].astype(F32)
        for g in range(n_heads // DSA_GRP):
            cols = slice(g * DSA_GRP * t, (g + 1) * DSA_GRP * t)
            s_all = _dot(kt, qlat_ref[:, cols])
            parts = []
            for i in range(DSA_GRP):
                s = s_all[:, i * t:(i + 1) * t] + mb
                if bias_row0 is not None:
                    s = s + bt_ref[g * DSA_GRP + i, bias_row0:bias_row0 + t, :]
                parts.append(s)
            m, l, alpha, pr = _softmax_step(m_ref[:, cols], l_ref[:, cols], jnp.concatenate(parts, axis=1))
            m_ref[:, cols] = m
            l_ref[:, cols] = l
            acc_ref[:, cols] = alpha * acc_ref[:, cols] + _dot(kt_t, pr)

    process_tile(qt, BLK)

    @pl.when(qt >= 1)
    def _():
        process_tile(qt - 1, 0)

    def far_tile(kk, _):
        process_tile(kk, None)
        return 0

    lax.fori_loop(0, jnp.maximum(qt - 1, 0), far_tile, 0)

    for hp in range(n_heads // 2):
        out_t = jnp.zeros((LANES, t), F32)
        for hh in range(2):
            h = 2 * hp + hh
            cols = slice(h * t, (h + 1) * t)
            o_lat = (acc_ref[:, cols] / l_ref[:, cols]).astype(BF16)
            out_t = out_t + _dot(wuvt_ref[h], o_lat)
        o_ref[:, hp * LANES:(hp + 1) * LANES] = out_t.T.astype(o_ref.dtype)


def _dsa_attention(qn_t, ckv, ckv_t, mask_t, bt, w_uk_t, w_uv_t, n_heads):
    b, seq, _ = ckv.shape
    t = BLK
    d_q = qn_t.shape[1]
    consts = (bt, w_uk_t, w_uv_t)
    return pl.pallas_call(
        functools.partial(_dsa_attn_kernel, n_heads=n_heads),
        grid=(b, seq // t),
        in_specs=[pl.BlockSpec((None, d_q, t), lambda bi, q: (bi, 0, q)),
                  pl.BlockSpec((None, seq, DSA_LORA), lambda bi, q: (bi, 0, 0)),
                  pl.BlockSpec((None, seq // t, DSA_LORA, t), lambda bi, q: (bi, 0, 0, 0)),
                  pl.BlockSpec((None, None, seq, t), lambda bi, q: (bi, q, 0, 0))]
                 + [_full(a, 2) for a in consts],
        out_specs=pl.BlockSpec((None, t, d_q), lambda bi, q: (bi, q, 0)),
        out_shape=jax.ShapeDtypeStruct((b, seq, d_q), BF16),
        scratch_shapes=[pltpu.VMEM((DSA_LORA, n_heads * t), BF16),
                        pltpu.VMEM((DSA_LORA, n_heads * t), F32),
                        pltpu.VMEM((1, n_heads * t), F32),
                        pltpu.VMEM((1, n_heads * t), F32)],
        compiler_params=_cparams("parallel", "arbitrary"),
        name="dsa_attn",
    )(qn_t, ckv, ckv_t, mask_t, *consts)


def _moba_layer(x2, batch, seq, g, w_qkv, w_o, bt, n_heads):
    d = x2.shape[1]
    wq_t = (w_qkv[:, :d] * HEAD_DIM ** -0.5).T.astype(BF16)
    wk = w_qkv[:, d:2 * d].astype(BF16)
    wv_t = w_qkv[:, 2 * d:].T.astype(BF16)
    q_t, k, v_t = _moba_proj(x2.reshape(batch, seq, d), g, wq_t, wk, wv_t)
    attn = _moba_attention(q_t, k, v_t, bt, n_heads)
    return _matmul_res(attn.reshape(batch * seq, d), w_o.astype(BF16), x2)


def _dsa_layer(x2, batch, seq, g, w_in, g_q, g_kv, w_uq, w_qi, w_uk, w_uv, w_o, bt, n_heads):
    d = x2.shape[1]
    k_lo = 2 * DSA_LORA
    w_in_row = jnp.concatenate([w_in[:, :k_lo + IDX_DIM], jnp.zeros((d, LANES - IDX_DIM), F32)], axis=1).astype(BF16)
    w_kv_t = w_in[:, DSA_LORA:k_lo].T.astype(BF16)
    w_w_t = jnp.concatenate([w_in[:, k_lo + IDX_DIM:].T, jnp.zeros((BF16_ROWS - IDX_HEADS, d), F32)]).astype(BF16)
    w_uq_t = (w_uq * HEAD_DIM ** -0.5).T.astype(BF16)
    ckv, ckv_t, kk, widx_t, qn_t, qi_t = _dsa_proj(x2.reshape(batch, seq, d), g, w_in_row, w_kv_t, w_w_t,
                                                   g_q, g_kv, w_uq_t, w_qi.T.astype(BF16))
    mask_t = _dsa_select(qi_t, widx_t, kk)
    w_uk_t = jnp.transpose(w_uk.reshape(n_heads // 2, 2 * HEAD_DIM, DSA_LORA), (0, 2, 1)).astype(BF16)
    w_t = jnp.transpose(w_uv, (0, 2, 1))
    zeros = jnp.zeros_like(w_t)
    is_even = (jnp.arange(n_heads) % 2 == 0)[:, None, None]
    w_uv_t = jnp.where(is_even, jnp.concatenate([w_t, zeros], axis=1),
                       jnp.concatenate([zeros, w_t], axis=1)).astype(BF16)
    attn = _dsa_attention(qn_t, ckv, ckv_t, mask_t, bt, w_uk_t, w_uv_t, n_heads)
    return _matmul_res(attn.reshape(batch * seq, -1), w_o.astype(BF16), x2)


def kernel(x, rel_bias, ln_attn, ln_mlp, moba_w_qkv, moba_w_o, dsa_w_in, dsa_g_q, dsa_g_kv, dsa_w_uq, dsa_w_qi,
           dsa_w_uk, dsa_w_uv, dsa_w_o, mlp_w_up, mlp_w_down, final_norm):
    batch, seq, d = x.shape
    n_heads = rel_bias.shape[1]
    depth = ln_attn.shape[0]
    bt = _bias_tiles(rel_bias)
    x2 = x.reshape(batch * seq, d)
    for i in range(depth):
        j = i // 2
        if i % 2 == 0:
            x2 = _moba_layer(x2, batch, seq, ln_attn[i], moba_w_qkv[j], moba_w_o[j], bt, n_heads)
        else:
            x2 = _dsa_layer(x2, batch, seq, ln_attn[i], dsa_w_in[j], dsa_g_q[j], dsa_g_kv[j], dsa_w_uq[j],
                            dsa_w_qi[j], dsa_w_uk[j], dsa_w_uv[j], dsa_w_o[j], bt, n_heads)
        x2 = _mlp(x2, ln_mlp[i], mlp_w_up[i].astype(BF16), mlp_w_down[i].astype(BF16))
    return _final_norm(x2, final_norm).reshape(batch, seq, d)
```

```python
import functools
import math

import numpy as np
import jax
import jax.numpy as jnp
from jax import lax
from jax.experimental import pallas as pl
from jax.experimental.pallas import tpu as pltpu

F32 = jnp.float32
BF16 = jnp.bfloat16
I32 = jnp.int32

EPS = 1e-6
NEG = -1e30
INT_MIN = -(2 ** 31)
LOG2E = 1.0 / math.log(2.0)

NUM_BUCKETS = 32
MAX_DISTANCE = 128
HEAD_DIM = 64
LANES = 128
BF16_ROWS = 16
BLK = 256
MOBA_TOPK = 3
MOBA_GRP = 4
DSA_TOPK_MAX = 256
DSA_LORA = 256
DSA_GRP = 4
IDX_HEADS = 8
IDX_DIM = 64
VMEM_LIMIT = 56 * 1024 * 1024


def _dot(a, b):
    return jnp.dot(a, b, preferred_element_type=F32)


def _dot_nt(a, b):
    return lax.dot_general(a, b, (((1,), (1,)), ((), ())), preferred_element_type=F32)


def _rms(x, g):
    return x * lax.rsqrt(jnp.mean(x * x, axis=-1, keepdims=True) + EPS) * g


def _cparams(*sem):
    return pltpu.CompilerParams(dimension_semantics=sem, vmem_limit_bytes=VMEM_LIMIT)


def _full(a):
    return pl.BlockSpec(a.shape, lambda *_: (0,) * a.ndim, pipeline_mode=pl.Buffered(1))


def _bucket_map_t():
    jj = np.arange(2 * BLK)[:, None]
    i = np.arange(BLK)[None, :]
    d = i - jj + BLK
    n = np.maximum(d, 0)
    exact = NUM_BUCKETS // 2
    nf = np.maximum(n, 1).astype(np.float32)
    large = exact + (np.log(nf / exact) / math.log(MAX_DISTANCE / exact) * (NUM_BUCKETS - exact)).astype(np.int32)
    large = np.minimum(large, NUM_BUCKETS - 1)
    bucket = np.where(n < exact, n, large)
    return np.where(d >= 0, bucket, -1).astype(np.int32)


def _bias_tile_kernel(tab_ref, bmap_ref, o_ref):
    h = pl.program_id(0)
    bm = bmap_ref[...]
    base = tab_ref[NUM_BUCKETS - 1, h]
    acc = jnp.full(bm.shape, NEG, F32)
    for b in range(NUM_BUCKETS):
        acc = jnp.where(bm == b, (tab_ref[b, h] - base) * LOG2E, acc)
    o_ref[...] = acc


def _bias_tiles(rel_bias):
    n_heads = rel_bias.shape[1]
    bmap = jnp.asarray(_bucket_map_t())
    return pl.pallas_call(
        _bias_tile_kernel,
        grid=(n_heads,),
        in_specs=[pl.BlockSpec(memory_space=pltpu.SMEM),
                  pl.BlockSpec((2 * BLK, BLK), lambda h: (0, 0))],
        out_specs=pl.BlockSpec((None, 2 * BLK, BLK), lambda h: (h, 0, 0)),
        out_shape=jax.ShapeDtypeStruct((n_heads, 2 * BLK, BLK), F32),
        name="bias_tiles",
    )(rel_bias, bmap)


def _matmul_res_kernel(a_ref, w_ref, r_ref, o_ref):
    o_ref[...] = r_ref[...] + _dot(a_ref[...], w_ref[...])


def _matmul_res(a, w, res, tm=512):
    n, k = a.shape
    m = w.shape[1]
    return pl.pallas_call(
        _matmul_res_kernel,
        grid=(n // tm,),
        in_specs=[pl.BlockSpec((tm, k), lambda i: (i, 0)),
                  pl.BlockSpec((k, m), lambda i: (0, 0)),
                  pl.BlockSpec((tm, m), lambda i: (i, 0))],
        out_specs=pl.BlockSpec((tm, m), lambda i: (i, 0)),
        out_shape=jax.ShapeDtypeStruct((n, m), F32),
        compiler_params=_cparams("parallel"),
        name="matmul_res",
    )(a, w, res)


def _mlp_kernel(x_ref, g_ref, wu_ref, wd_ref, o_ref, xn_ref, acc_ref):
    f = pl.program_id(1)

    @pl.when(f == 0)
    def _():
        xn_ref[...] = _rms(x_ref[...], g_ref[...]).astype(BF16)
        acc_ref[...] = jnp.zeros_like(acc_ref)

    h = jnp.square(jnp.maximum(_dot(xn_ref[...], wu_ref[...]), 0.0)).astype(BF16)
    acc_ref[...] += _dot(h, wd_ref[...])

    @pl.when(f == pl.num_programs(1) - 1)
    def _():
        o_ref[...] = x_ref[...] + acc_ref[...]


def _mlp(x2, g, w_up, w_down, tm=512, tf=1024):
    n, d = x2.shape
    d_ff = w_up.shape[1]
    return pl.pallas_call(
        _mlp_kernel,
        grid=(n // tm, d_ff // tf),
        in_specs=[pl.BlockSpec((tm, d), lambda i, f: (i, 0)),
                  pl.BlockSpec((1, d), lambda i, f: (0, 0)),
                  pl.BlockSpec((d, tf), lambda i, f: (0, f)),
                  pl.BlockSpec((tf, d), lambda i, f: (f, 0))],
        out_specs=pl.BlockSpec((tm, d), lambda i, f: (i, 0)),
        out_shape=jax.ShapeDtypeStruct((n, d), F32),
        scratch_shapes=[pltpu.VMEM((tm, d), BF16), pltpu.VMEM((tm, d), F32)],
        compiler_params=_cparams("parallel", "arbitrary"),
        name="mlp",
    )(x2, g.reshape(1, d), w_up, w_down)


def _final_norm_kernel(x_ref, g_ref, o_ref):
    o_ref[...] = _rms(x_ref[...], g_ref[...])


def _final_norm(x2, g, tm=512):
    n, d = x2.shape
    return pl.pallas_call(
        _final_norm_kernel,
        grid=(n // tm,),
        in_specs=[pl.BlockSpec((tm, d), lambda i: (i, 0)),
                  pl.BlockSpec((1, d), lambda i: (0, 0))],
        out_specs=pl.BlockSpec((tm, d), lambda i: (i, 0)),
        out_shape=jax.ShapeDtypeStruct((n, d), F32),
        compiler_params=_cparams("parallel"),
        name="final_norm",
    )(x2, g.reshape(1, d))


def _softmax_step(m_old, l_old, s, ones_rows):
    m_new = jnp.maximum(m_old, jnp.max(s, axis=0, keepdims=True))
    alpha = jnp.exp2(m_old - m_new)
    p = jnp.exp2(s - m_new).astype(BF16)
    l_new = alpha * l_old + _dot(ones_rows, p)[0:1, :]
    return m_new, l_new, alpha, p


def _moba_proj_kernel(x_ref, g_ref, wqt_ref, wk_ref, wvt_ref, qt_ref, k_ref, vt_ref):
    xn = _rms(x_ref[...], g_ref[...]).astype(BF16)
    qt_ref[...] = _dot_nt(wqt_ref[...], xn).astype(BF16)
    k_ref[...] = _dot(xn, wk_ref[...]).astype(BF16)
    vt_ref[...] = _dot_nt(wvt_ref[...], xn).astype(BF16)


def _moba_proj(x3, g, wq_t, wk, wv_t):
    b, seq, d = x3.shape
    g2 = g.reshape(1, d)
    return pl.pallas_call(
        _moba_proj_kernel,
        grid=(b, seq // BLK),
        in_specs=[pl.BlockSpec((None, BLK, d), lambda bi, i: (bi, i, 0)),
                  _full(g2), _full(wq_t), _full(wk), _full(wv_t)],
        out_specs=[pl.BlockSpec((None, d, BLK), lambda bi, i: (bi, 0, i)),
                   pl.BlockSpec((None, BLK, d), lambda bi, i: (bi, i, 0)),
                   pl.BlockSpec((None, None, d, BLK), lambda bi, i: (bi, i, 0, 0))],
        out_shape=[jax.ShapeDtypeStruct((b, d, seq), BF16),
                   jax.ShapeDtypeStruct((b, seq, d), BF16),
                   jax.ShapeDtypeStruct((b, seq // BLK, d, BLK), BF16)],
        compiler_params=_cparams("parallel", "parallel"),
        name="moba_proj",
    )(x3, g2, wq_t, wk, wv_t)


def _moba_kernel(qt_ref, k_ref, vt_ref, bt_ref, o_ref, kaug_ref, kmean_ref, qfar_ref, m_ref, l_ref, acc_ref,
                 *, n_blk, k_sel, sel_rows):
    j = pl.program_id(2)
    seq = k_ref.shape[0]
    n_pairs = MOBA_GRP // 2
    t = BLK
    row = lax.broadcasted_iota(I32, (LANES, t), 0)
    blk_id = lax.broadcasted_iota(I32, (sel_rows, t), 0)
    ones_1 = jnp.ones((BF16_ROWS, t), BF16)
    ones_2 = jnp.ones((BF16_ROWS, 2 * t), BF16)

    @pl.when(j == 0)
    def _():
        krow = lax.broadcasted_iota(I32, (seq, LANES), 0)
        kcol = lax.broadcasted_iota(I32, (seq, LANES), 1)
        onehot = jnp.where(krow // BLK == kcol, 1.0, 0.0).astype(BF16)
        for p in range(n_pairs):
            kaug_ref[p, :, 0:LANES] = k_ref[:, p * LANES:(p + 1) * LANES]
            kaug_ref[p, :, LANES:2 * LANES] = onehot
            kmean_ref[p] = jnp.zeros((LANES, LANES), F32)
            for n in range(n_blk):
                kmean_ref[p, n:n + 1, :] = jnp.mean(
                    k_ref[n * BLK:(n + 1) * BLK, p * LANES:(p + 1) * LANES].astype(F32), axis=0, keepdims=True)

    own0 = pl.multiple_of(j * BLK, BLK)
    adj = jnp.maximum(j - 1, 0)
    adj0 = pl.multiple_of(adj * BLK, BLK)
    pad_rows = jnp.zeros((LANES - sel_rows, t), BF16)

    def update(p, s, v_t, ones_rows):
        m, l, alpha, pr = _softmax_step(m_ref[p], l_ref[p], s, ones_rows)
        m_ref[p] = m
        l_ref[p] = l
        acc_ref[p] = alpha * acc_ref[p] + _dot(v_t, pr)

    for p in range(n_pairs):
        feats = slice(p * LANES, (p + 1) * LANES)
        qp = qt_ref[feats, :]
        q_own, q_adj, q_far = [], [], []
        for hh in range(2):
            qh = jnp.where((row < HEAD_DIM) if hh == 0 else (row >= HEAD_DIM), qp, jnp.zeros_like(qp))
            gate = _dot(kmean_ref[p], qh.astype(F32))[0:sel_rows, :]
            g = jnp.where(blk_id < j, gate, NEG)
            g = jnp.where(blk_id < n_blk, g, -jnp.inf)
            sel_bias = jnp.full((sel_rows, t), NEG, F32)
            for it in range(k_sel):
                mx = jnp.max(g, axis=0, keepdims=True)
                first = jnp.min(jnp.where(g == mx, blk_id, LANES), axis=0, keepdims=True)
                pick = blk_id == first
                sel_bias = jnp.where(pick, jnp.where(it < j, 0.0, NEG), sel_bias)
                g = jnp.where(pick, -jnp.inf, g)
            far_bias = jnp.where(blk_id < j - 1, sel_bias, NEG)
            q_own.append(qh)
            q_adj.append(jnp.concatenate([qh, sel_bias.astype(BF16), pad_rows], axis=0))
            q_far.append(jnp.concatenate([qh, far_bias.astype(BF16), pad_rows], axis=0))
        qfar_ref[p] = jnp.concatenate(q_far, axis=1)

        m_ref[p] = jnp.full((1, 2 * t), NEG, F32)
        l_ref[p] = jnp.zeros((1, 2 * t), F32)
        acc_ref[p] = jnp.zeros((LANES, 2 * t), F32)
        bias_own = jnp.concatenate([bt_ref[2 * p, t:2 * t, :], bt_ref[2 * p + 1, t:2 * t, :]], axis=1)
        s = _dot(k_ref[pl.ds(own0, BLK), feats], jnp.concatenate(q_own, axis=1)) + bias_own
        update(p, s, vt_ref[j, feats, :], ones_1)
        bias_adj = jnp.concatenate([bt_ref[2 * p, 0:t, :], bt_ref[2 * p + 1, 0:t, :]], axis=1)
        s = _dot(kaug_ref[p, pl.ds(adj0, BLK), :], jnp.concatenate(q_adj, axis=1)) + bias_adj
        update(p, s, vt_ref[adj, feats, :], ones_1)

    def far_pair(pi, _):
        n0 = pl.multiple_of(pi * 2 * BLK, 2 * BLK)
        logits = [_dot(kaug_ref[p, pl.ds(n0, 2 * BLK), :], qfar_ref[p]) for p in range(n_pairs)]
        for p in range(n_pairs):
            feats = slice(p * LANES, (p + 1) * LANES)
            v_t = jnp.concatenate([vt_ref[2 * pi, feats, :], vt_ref[2 * pi + 1, feats, :]], axis=1)
            update(p, logits[p], v_t, ones_2)
        return 0

    lax.fori_loop(0, jnp.maximum(j, 1) // 2, far_pair, 0)

    for p in range(n_pairs):
        acc = acc_ref[p] / l_ref[p]
        out_t = jnp.where(row < HEAD_DIM, acc[:, 0:t], acc[:, t:2 * t])
        o_ref[:, p * LANES:(p + 1) * LANES] = out_t.T.astype(o_ref.dtype)


def _moba_attention(q_t, k, v_t, bt, n_heads):
    b, seq, d = k.shape
    n_blk = seq // BLK
    n_grp = n_heads // MOBA_GRP
    n_pairs = MOBA_GRP // 2
    w = MOBA_GRP * HEAD_DIM
    k_sel = max(1, min(MOBA_TOPK, n_blk - 1))
    sel_rows = BF16_ROWS * pl.cdiv(n_blk, BF16_ROWS)
    assert sel_rows <= LANES and n_blk % 2 == 0
    return pl.pallas_call(
        functools.partial(_moba_kernel, n_blk=n_blk, k_sel=k_sel, sel_rows=sel_rows),
        grid=(b, n_grp, n_blk),
        in_specs=[pl.BlockSpec((None, w, BLK), lambda bi, g, j: (bi, g, j)),
                  pl.BlockSpec((None, seq, w), lambda bi, g, j: (bi, 0, g)),
                  pl.BlockSpec((None, n_blk, w, BLK), lambda bi, g, j: (bi, 0, g, 0)),
                  pl.BlockSpec((MOBA_GRP, 2 * BLK, BLK), lambda bi, g, j: (g, 0, 0))],
        out_specs=pl.BlockSpec((None, BLK, w), lambda bi, g, j: (bi, j, g)),
        out_shape=jax.ShapeDtypeStruct((b, seq, d), BF16),
        scratch_shapes=[pltpu.VMEM((n_pairs, seq, 2 * LANES), BF16),
                        pltpu.VMEM((n_pairs, LANES, LANES), F32),
                        pltpu.VMEM((n_pairs, 2 * LANES, 2 * BLK), BF16),
                        pltpu.VMEM((n_pairs, 1, 2 * BLK), F32),
                        pltpu.VMEM((n_pairs, 1, 2 * BLK), F32),
                        pltpu.VMEM((n_pairs, LANES, 2 * BLK), F32)],
        compiler_params=_cparams("parallel", "parallel", "arbitrary"),
        name="moba_attn",
    )(q_t, k, v_t, bt)


def _dsa_proj_kernel(x_ref, g_ref, win_ref, wkvt_ref, wwt_ref, gq_ref, gkv_ref, gkvt_ref, wuqt_ref, wqit_ref,
                     wuvt_ref, ckv_ref, vt_ref, kk_ref, widx_ref, qnt_ref, qit_ref, *, idx_scale):
    hn = _rms(x_ref[...], g_ref[...]).astype(BF16)
    proj = _dot(hn, win_ref[...])
    c_q = _rms(proj[:, 0:DSA_LORA], gq_ref[...]).astype(BF16)
    ckv_ref[...] = _rms(proj[:, DSA_LORA:2 * DSA_LORA], gkv_ref[...]).astype(BF16)
    kk_ref[...] = proj[:, 2 * DSA_LORA:2 * DSA_LORA + IDX_DIM].astype(BF16)
    ct = _dot_nt(wkvt_ref[...], hn)
    scale = lax.rsqrt(jnp.mean(ct * ct, axis=0, keepdims=True) + EPS)
    gt = jnp.concatenate([gkvt_ref[...]] * (ct.shape[1] // LANES), axis=1)
    vt_ref[...] = _dot(wuvt_ref[...], (ct * scale * gt).astype(BF16)).astype(BF16)
    widx_ref[...] = _dot_nt(wwt_ref[...], hn)[0:IDX_HEADS, :] * idx_scale
    qnt_ref[...] = _dot_nt(wuqt_ref[...], c_q).astype(BF16)
    qit_ref[...] = _dot_nt(wqit_ref[...], c_q).astype(BF16)


def _dsa_proj(x3, g, w_in_row, w_kv_t, w_w_t, g_q, g_kv, w_uq_t, w_qi_t, w_uv_t):
    b, seq, d = x3.shape
    tm = BLK
    g2, gq2, gkv2 = g.reshape(1, d), g_q.reshape(1, -1), g_kv.reshape(1, -1)
    gkv_t = jnp.broadcast_to(g_kv.reshape(-1, 1), (DSA_LORA, LANES))
    n_q, n_i, n_v = w_uq_t.shape[0], w_qi_t.shape[0], w_uv_t.shape[0]
    consts = (g2, w_in_row, w_kv_t, w_w_t, gq2, gkv2, gkv_t, w_uq_t, w_qi_t, w_uv_t)
    idx_scale = IDX_HEADS ** -0.5 * IDX_DIM ** -0.5
    return pl.pallas_call(
        functools.partial(_dsa_proj_kernel, idx_scale=idx_scale),
        grid=(b, seq // tm),
        in_specs=[pl.BlockSpec((None, tm, d), lambda bi, i: (bi, i, 0))] + [_full(a) for a in consts],
        out_specs=[pl.BlockSpec((None, tm, DSA_LORA), lambda bi, i: (bi, i, 0)),
                   pl.BlockSpec((None, None, n_v, tm), lambda bi, i: (bi, i, 0, 0)),
                   pl.BlockSpec((None, tm, IDX_DIM), lambda bi, i: (bi, i, 0)),
                   pl.BlockSpec((None, IDX_HEADS, tm), lambda bi, i: (bi, 0, i)),
                   pl.BlockSpec((None, n_q, tm), lambda bi, i: (bi, 0, i)),
                   pl.BlockSpec((None, n_i, tm), lambda bi, i: (bi, 0, i))],
        out_shape=[jax.ShapeDtypeStruct((b, seq, DSA_LORA), BF16),
                   jax.ShapeDtypeStruct((b, seq // tm, n_v, tm), BF16),
                   jax.ShapeDtypeStruct((b, seq, IDX_DIM), BF16),
                   jax.ShapeDtypeStruct((b, IDX_HEADS, seq), F32),
                   jax.ShapeDtypeStruct((b, n_q, seq), BF16),
                   jax.ShapeDtypeStruct((b, n_i, seq), BF16)],
        compiler_params=_cparams("parallel", "parallel"),
        name="dsa_proj",
    )(x3, *consts)


def _dsa_select_kernel(qit_ref, w_ref, kk_ref, o_ref, keys_ref, *, top_k, idx_bits):
    c = pl.program_id(1)
    n_kc, tk, tq = keys_ref.shape
    n_proc = c + 1
    key_row = lax.broadcasted_iota(I32, (tk, tq), 0)
    q_pos = c * tq + lax.broadcasted_iota(I32, (tk, tq), 1)
    w = w_ref[...]

    def score_chunk(kc, _):
        kt = kk_ref[pl.ds(pl.multiple_of(kc * tk, tk), tk), :]
        acc = jnp.zeros((tk, tq), F32)
        for h in range(IDX_HEADS):
            x = _dot(kt, qit_ref[h * IDX_DIM:(h + 1) * IDX_DIM, :])
            acc = acc + w[h:h + 1, :] * jnp.maximum(x, 0.0)
        sc = jnp.where(kc * tk + key_row <= q_pos, acc, NEG)
        bits = pltpu.bitcast(sc, I32)
        keys_ref[kc] = jnp.where(bits < 0, bits ^ 0x7FFFFFFF, bits)
        return 0

    lax.fori_loop(0, n_proc, score_chunk, 0)

    def count(pred):
        def body(kc, tot):
            hit = jnp.where(pred(keys_ref[kc], kc * tk + key_row), 1, 0)
            return tot + jnp.sum(hit, axis=0, keepdims=True)
        return lax.fori_loop(0, n_proc, body, jnp.zeros((1, tq), I32))

    def value_bit(i, thr):
        cand = thr ^ jnp.left_shift(jnp.int32(1), 31 - i)
        cnt = count(lambda key, pos: key >= cand)
        return jnp.where(cnt >= top_k, cand, thr)

    thr = lax.fori_loop(0, 32, value_bit, jnp.full((1, tq), INT_MIN, I32))
    cnt_ge = count(lambda key, pos: key >= thr)
    cnt_gt = count(lambda key, pos: key > thr)
    need = top_k - cnt_gt

    def tie_cut():
        def index_bit(i, x):
            cand = x | jnp.left_shift(jnp.int32(1), idx_bits - 1 - i)
            cnt = count(lambda key, pos: (key == thr) & (pos < cand))
            return jnp.where(cnt < need, cand, x)
        return lax.fori_loop(0, idx_bits, index_bit, jnp.zeros((1, tq), I32))

    last_tie = lax.cond(jnp.max(cnt_ge) > top_k, tie_cut, lambda: jnp.full((1, tq), 2 ** idx_bits, I32))

    def write_chunk(kc, _):
        key = keys_ref[kc]
        pos = kc * tk + key_row
        chosen = (key > thr) | ((key == thr) & (pos <= last_tie))
        rows = pl.ds(pl.multiple_of(kc * tk, tk), tk)
        o_ref[rows, :] = jnp.where(chosen & (pos <= q_pos), 0.0, NEG).astype(o_ref.dtype)
        return 0

    lax.fori_loop(0, n_proc, write_chunk, 0)

    def fill_chunk(kc, _):
        o_ref[pl.ds(pl.multiple_of(kc * tk, tk), tk), :] = jnp.full((tk, tq), NEG, o_ref.dtype)
        return 0

    lax.fori_loop(n_proc, n_kc, fill_chunk, 0)


def _dsa_select(qi_t, widx_t, kk):
    b, seq, _ = kk.shape
    t = BLK
    n_kc = seq // t
    top_k = min(DSA_TOPK_MAX, seq // 4)
    idx_bits = max(1, (seq - 1).bit_length())
    return pl.pallas_call(
        functools.partial(_dsa_select_kernel, top_k=top_k, idx_bits=idx_bits),
        grid=(b, seq // t),
        in_specs=[pl.BlockSpec((None, qi_t.shape[1], t), lambda bi, c: (bi, 0, c)),
                  pl.BlockSpec((None, IDX_HEADS, t), lambda bi, c: (bi, 0, c)),
                  pl.BlockSpec((None, seq, IDX_DIM), lambda bi, c: (bi, 0, 0))],
        out_specs=pl.BlockSpec((None, None, seq, t), lambda bi, c: (bi, c, 0, 0)),
        out_shape=jax.ShapeDtypeStruct((b, seq // t, seq, t), BF16),
        scratch_shapes=[pltpu.VMEM((n_kc, t, t), I32)],
        compiler_params=_cparams("parallel", "arbitrary"),
        name="dsa_select",
    )(qi_t, widx_t, kk)


def _dsa_attn_kernel(qnt_ref, ckv_ref, vt_ref, mb_ref, bt_ref, wukt_ref, o_ref,
                     qlat_ref, acc_ref, m_ref, l_ref, *, n_heads):
    qt = pl.program_id(1)
    t = BLK
    n_grp = n_heads // DSA_GRP
    row = lax.broadcasted_iota(I32, (LANES, t), 0)
    ones_rows = jnp.ones((BF16_ROWS, t), BF16)

    for hp in range(n_heads // 2):
        qp = qnt_ref[hp * LANES:(hp + 1) * LANES, :]
        for hh in range(2):
            qh = jnp.where((row < HEAD_DIM) if hh == 0 else (row >= HEAD_DIM), qp, jnp.zeros_like(qp))
            h = 2 * hp + hh
            qlat_ref[:, h * t:(h + 1) * t] = _dot(wukt_ref[hp], qh).astype(BF16)

    m_ref[...] = jnp.full(m_ref.shape, NEG, F32)
    l_ref[...] = jnp.zeros_like(l_ref)
    acc_ref[...] = jnp.zeros_like(acc_ref)

    def process_tile(kk, bias_row0):
        rows = pl.ds(pl.multiple_of(kk * t, t), t)
        kt = ckv_ref[rows, :]
        mb = mb_ref[rows, :].astype(F32)
        logits = [_dot(kt, qlat_ref[:, g * DSA_GRP * t:(g + 1) * DSA_GRP * t]) for g in range(n_grp)]
        for g in range(n_grp):
            cols = slice(g * DSA_GRP * t, (g + 1) * DSA_GRP * t)
            parts = []
            for i in range(DSA_GRP):
                s = logits[g][:, i * t:(i + 1) * t] + mb
                if bias_row0 is not None:
                    s = s + bt_ref[g * DSA_GRP + i, bias_row0:bias_row0 + t, :]
                parts.append(s)
            m, l, alpha, pr = _softmax_step(m_ref[:, cols], l_ref[:, cols], jnp.concatenate(parts, axis=1), ones_rows)
            m_ref[:, cols] = m
            l_ref[:, cols] = l
            for i in range(DSA_GRP):
                h = g * DSA_GRP + i
                feats = slice(h * HEAD_DIM, (h + 1) * HEAD_DIM)
                hc = slice(i * t, (i + 1) * t)
                acc_ref[feats, :] = alpha[:, hc] * acc_ref[feats, :] + _dot(vt_ref[kk, feats, :], pr[:, hc])

    process_tile(qt, BLK)

    @pl.when(qt >= 1)
    def _():
        process_tile(qt - 1, 0)

    def far_tile(kk, _):
        process_tile(kk, None)
        return 0

    lax.fori_loop(0, jnp.maximum(qt - 1, 0), far_tile, 0)

    for h in range(n_heads):
        feats = slice(h * HEAD_DIM, (h + 1) * HEAD_DIM)
        acc_ref[feats, :] = acc_ref[feats, :] / l_ref[:, h * t:(h + 1) * t]
    o_ref[...] = acc_ref[...].T.astype(o_ref.dtype)


def _dsa_attention(qn_t, ckv, v_t, mask_t, bt, w_uk_t, n_heads):
    b, seq, _ = ckv.shape
    t = BLK
    d_q = qn_t.shape[1]
    return pl.pallas_call(
        functools.partial(_dsa_attn_kernel, n_heads=n_heads),
        grid=(b, seq // t),
        in_specs=[pl.BlockSpec((None, d_q, t), lambda bi, q: (bi, 0, q)),
                  pl.BlockSpec((None, seq, DSA_LORA), lambda bi, q: (bi, 0, 0)),
                  pl.BlockSpec((None, seq // t, d_q, t), lambda bi, q: (bi, 0, 0, 0)),
                  pl.BlockSpec((None, None, seq, t), lambda bi, q: (bi, q, 0, 0)),
                  _full(bt), _full(w_uk_t)],
        out_specs=pl.BlockSpec((None, t, d_q), lambda bi, q: (bi, q, 0)),
        out_shape=jax.ShapeDtypeStruct((b, seq, d_q), BF16),
        scratch_shapes=[pltpu.VMEM((DSA_LORA, n_heads * t), BF16),
                        pltpu.VMEM((d_q, t), F32),
                        pltpu.VMEM((1, n_heads * t), F32),
                        pltpu.VMEM((1, n_heads * t), F32)],
        compiler_params=_cparams("parallel", "arbitrary"),
        name="dsa_attn",
    )(qn_t, ckv, v_t, mask_t, bt, w_uk_t)


def _moba_layer(x2, batch, seq, g, w_qkv, w_o, bt, n_heads):
    d = x2.shape[1]
    wq_t = (w_qkv[:, :d] * (HEAD_DIM ** -0.5 * LOG2E)).T.astype(BF16)
    wk = w_qkv[:, d:2 * d].astype(BF16)
    wv_t = w_qkv[:, 2 * d:].T.astype(BF16)
    q_t, k, v_t = _moba_proj(x2.reshape(batch, seq, d), g, wq_t, wk, wv_t)
    attn = _moba_attention(q_t, k, v_t, bt, n_heads)
    return _matmul_res(attn.reshape(batch * seq, d), w_o.astype(BF16), x2)


def _dsa_layer(x2, batch, seq, g, w_in, g_q, g_kv, w_uq, w_qi, w_uk, w_uv, w_o, bt, n_heads):
    d = x2.shape[1]
    k_lo = 2 * DSA_LORA
    w_in_row = jnp.concatenate([w_in[:, :k_lo + IDX_DIM], jnp.zeros((d, LANES - IDX_DIM), F32)], axis=1).astype(BF16)
    w_kv_t = w_in[:, DSA_LORA:k_lo].T.astype(BF16)
    w_w_t = jnp.concatenate([w_in[:, k_lo + IDX_DIM:].T, jnp.zeros((BF16_ROWS - IDX_HEADS, d), F32)]).astype(BF16)
    w_uq_t = (w_uq * (HEAD_DIM ** -0.5 * LOG2E)).T.astype(BF16)
    w_uv_t = jnp.transpose(w_uv, (0, 2, 1)).reshape(n_heads * HEAD_DIM, DSA_LORA).astype(BF16)
    ckv, v_t, kk, widx_t, qn_t, qi_t = _dsa_proj(x2.reshape(batch, seq, d), g, w_in_row, w_kv_t, w_w_t,
                                                 g_q, g_kv, w_uq_t, w_qi.T.astype(BF16), w_uv_t)
    mask_t = _dsa_select(qi_t, widx_t, kk)
    w_uk_t = jnp.transpose(w_uk.reshape(n_heads // 2, 2 * HEAD_DIM, DSA_LORA), (0, 2, 1)).astype(BF16)
    attn = _dsa_attention(qn_t, ckv, v_t, mask_t, bt, w_uk_t, n_heads)
    return _matmul_res(attn.reshape(batch * seq, -1), w_o.astype(BF16), x2)


def kernel(x, rel_bias, ln_attn, ln_mlp, moba_w_qkv, moba_w_o, dsa_w_in, dsa_g_q, dsa_g_kv, dsa_w_uq, dsa_w_qi,
           dsa_w_uk, dsa_w_uv, dsa_w_o, mlp_w_up, mlp_w_down, final_norm):
    batch, seq, d = x.shape
    n_heads = rel_bias.shape[1]
    depth = ln_attn.shape[0]
    bt = _bias_tiles(rel_bias)
    x2 = x.reshape(batch * seq, d)
    for i in range(depth):
        j = i // 2
        if i % 2 == 0:
            x2 = _moba_layer(x2, batch, seq, ln_attn[i], moba_w_qkv[j], moba_w_o[j], bt, n_heads)
        else:
            x2 = _dsa_layer(x2, batch, seq, ln_attn[i], dsa_w_in[j], dsa_g_q[j], dsa_g_kv[j], dsa_w_uq[j],
                            dsa_w_qi[j], dsa_w_uk[j], dsa_w_uv[j], dsa_w_o[j], bt, n_heads)
        x2 = _mlp(x2, ln_mlp[i], mlp_w_up[i].astype(BF16), mlp_w_down[i].astype(BF16))
    return _final_norm(x2, final_norm).reshape(batch, seq, d)
```

```python
import functools
import math

import numpy as np
import jax
import jax.numpy as jnp
from jax import lax
from jax.experimental import pallas as pl
from jax.experimental.pallas import tpu as pltpu

F32 = jnp.float32
BF16 = jnp.bfloat16
I32 = jnp.int32

EPS = 1e-6
NEG = -1e30
INT_MIN = -(2 ** 31)
LOG2E = 1.0 / math.log(2.0)

NUM_BUCKETS = 32
MAX_DISTANCE = 128
HEAD_DIM = 64
LANES = 128
BF16_ROWS = 16
BLK = 256
MOBA_TOPK = 3
MOBA_GRP = 4
DSA_TOPK_MAX = 256
DSA_LORA = 256
DSA_GRP = 4
IDX_HEADS = 8
IDX_DIM = 64
VMEM_LIMIT = 56 * 1024 * 1024


def _dot(a, b):
    return jnp.dot(a, b, preferred_element_type=F32)


def _dot_nt(a, b):
    return lax.dot_general(a, b, (((1,), (1,)), ((), ())), preferred_element_type=F32)


def _rms(x, g):
    return x * lax.rsqrt(jnp.mean(x * x, axis=-1, keepdims=True) + EPS) * g


def _cparams(*sem):
    return pltpu.CompilerParams(dimension_semantics=sem, vmem_limit_bytes=VMEM_LIMIT)


def _full(a):
    return pl.BlockSpec(a.shape, lambda *_: (0,) * a.ndim, pipeline_mode=pl.Buffered(1))


def _bucket_map_t():
    jj = np.arange(2 * BLK)[:, None]
    i = np.arange(BLK)[None, :]
    d = i - jj + BLK
    n = np.maximum(d, 0)
    exact = NUM_BUCKETS // 2
    nf = np.maximum(n, 1).astype(np.float32)
    large = exact + (np.log(nf / exact) / math.log(MAX_DISTANCE / exact) * (NUM_BUCKETS - exact)).astype(np.int32)
    large = np.minimum(large, NUM_BUCKETS - 1)
    bucket = np.where(n < exact, n, large)
    bmap = np.where(d >= 0, bucket, -1).astype(np.int32)
    return np.concatenate([bmap, bmap[:BLK]], axis=0)


def _bias_tile_kernel(tab_ref, bmap_ref, o_ref):
    h = pl.program_id(0)
    bm = bmap_ref[...]
    base = tab_ref[NUM_BUCKETS - 1, h]
    acc = jnp.full(bm.shape, NEG, F32)
    for b in range(NUM_BUCKETS):
        acc = jnp.where(bm == b, (tab_ref[b, h] - base) * LOG2E, acc)
    o_ref[...] = acc


def _bias_tiles(rel_bias):
    n_heads = rel_bias.shape[1]
    bmap = jnp.asarray(_bucket_map_t())
    return pl.pallas_call(
        _bias_tile_kernel,
        grid=(n_heads,),
        in_specs=[pl.BlockSpec(memory_space=pltpu.SMEM),
                  pl.BlockSpec((3 * BLK, BLK), lambda h: (0, 0))],
        out_specs=pl.BlockSpec((None, 3 * BLK, BLK), lambda h: (h, 0, 0)),
        out_shape=jax.ShapeDtypeStruct((n_heads, 3 * BLK, BLK), F32),
        name="bias_tiles",
    )(rel_bias, bmap)


def _matmul_res_kernel(a_ref, w_ref, r_ref, o_ref):
    o_ref[...] = r_ref[...] + _dot(a_ref[...], w_ref[...])


def _matmul_res(a, w, res, tm=512):
    n, k = a.shape
    m = w.shape[1]
    return pl.pallas_call(
        _matmul_res_kernel,
        grid=(n // tm,),
        in_specs=[pl.BlockSpec((tm, k), lambda i: (i, 0)),
                  pl.BlockSpec((k, m), lambda i: (0, 0)),
                  pl.BlockSpec((tm, m), lambda i: (i, 0))],
        out_specs=pl.BlockSpec((tm, m), lambda i: (i, 0)),
        out_shape=jax.ShapeDtypeStruct((n, m), F32),
        compiler_params=_cparams("parallel"),
        name="matmul_res",
    )(a, w, res)


def _mlp_kernel(x_ref, g_ref, wu_ref, wd_ref, o_ref, xn_ref, acc_ref):
    f = pl.program_id(1)

    @pl.when(f == 0)
    def _():
        xn_ref[...] = _rms(x_ref[...], g_ref[...]).astype(BF16)
        acc_ref[...] = jnp.zeros_like(acc_ref)

    h = jnp.square(jnp.maximum(_dot(xn_ref[...], wu_ref[...]), 0.0)).astype(BF16)
    acc_ref[...] += _dot(h, wd_ref[...])

    @pl.when(f == pl.num_programs(1) - 1)
    def _():
        o_ref[...] = x_ref[...] + acc_ref[...]


def _mlp(x2, g, w_up, w_down, tm=512, tf=1024):
    n, d = x2.shape
    d_ff = w_up.shape[1]
    return pl.pallas_call(
        _mlp_kernel,
        grid=(n // tm, d_ff // tf),
        in_specs=[pl.BlockSpec((tm, d), lambda i, f: (i, 0)),
                  pl.BlockSpec((1, d), lambda i, f: (0, 0)),
                  pl.BlockSpec((d, tf), lambda i, f: (0, f)),
                  pl.BlockSpec((tf, d), lambda i, f: (f, 0))],
        out_specs=pl.BlockSpec((tm, d), lambda i, f: (i, 0)),
        out_shape=jax.ShapeDtypeStruct((n, d), F32),
        scratch_shapes=[pltpu.VMEM((tm, d), BF16), pltpu.VMEM((tm, d), F32)],
        compiler_params=_cparams("parallel", "arbitrary"),
        name="mlp",
    )(x2, g.reshape(1, d), w_up, w_down)


def _final_norm_kernel(x_ref, g_ref, o_ref):
    o_ref[...] = _rms(x_ref[...], g_ref[...])


def _final_norm(x2, g, tm=512):
    n, d = x2.shape
    return pl.pallas_call(
        _final_norm_kernel,
        grid=(n // tm,),
        in_specs=[pl.BlockSpec((tm, d), lambda i: (i, 0)),
                  pl.BlockSpec((1, d), lambda i: (0, 0))],
        out_specs=pl.BlockSpec((tm, d), lambda i: (i, 0)),
        out_shape=jax.ShapeDtypeStruct((n, d), F32),
        compiler_params=_cparams("parallel"),
        name="final_norm",
    )(x2, g.reshape(1, d))


def _softmax_step(m_old, l_old, s, ones_rows):
    m_new = jnp.maximum(m_old, jnp.max(s, axis=0, keepdims=True))
    alpha = jnp.exp2(m_old - m_new)
    p = jnp.exp2(s - m_new).astype(BF16)
    l_new = alpha * l_old + _dot(ones_rows, p)[0:1, :]
    return m_new, l_new, alpha, p


def _moba_proj_kernel(x_ref, g_ref, wqt_ref, wk_ref, wvt_ref, qt_ref, k_ref, vt_ref):
    xn = _rms(x_ref[...], g_ref[...]).astype(BF16)
    qt_ref[...] = _dot_nt(wqt_ref[...], xn).astype(BF16)
    k_ref[...] = _dot(xn, wk_ref[...]).astype(BF16)
    vt_ref[...] = _dot_nt(wvt_ref[...], xn).astype(BF16)


def _moba_proj(x3, g, wq_t, wk, wv_t):
    b, seq, d = x3.shape
    g2 = g.reshape(1, d)
    return pl.pallas_call(
        _moba_proj_kernel,
        grid=(b, seq // BLK),
        in_specs=[pl.BlockSpec((None, BLK, d), lambda bi, i: (bi, i, 0)),
                  _full(g2), _full(wq_t), _full(wk), _full(wv_t)],
        out_specs=[pl.BlockSpec((None, d, BLK), lambda bi, i: (bi, 0, i)),
                   pl.BlockSpec((None, BLK, d), lambda bi, i: (bi, i, 0)),
                   pl.BlockSpec((None, None, d, BLK), lambda bi, i: (bi, i, 0, 0))],
        out_shape=[jax.ShapeDtypeStruct((b, d, seq), BF16),
                   jax.ShapeDtypeStruct((b, seq, d), BF16),
                   jax.ShapeDtypeStruct((b, seq // BLK, d, BLK), BF16)],
        compiler_params=_cparams("parallel", "parallel"),
        name="moba_proj",
    )(x3, g2, wq_t, wk, wv_t)


def _moba_kernel(qt_ref, k_ref, vt_ref, bt_ref, o_ref, kaug_ref, kmean_ref, qfar_ref, m_ref, l_ref, acc_ref,
                 *, n_blk, k_sel, sel_rows):
    j = pl.program_id(2)
    seq = k_ref.shape[0]
    n_pairs = MOBA_GRP // 2
    t = BLK
    row = lax.broadcasted_iota(I32, (LANES, t), 0)
    blk_id = lax.broadcasted_iota(I32, (sel_rows, t), 0)
    ones_2 = jnp.ones((BF16_ROWS, 2 * t), BF16)

    @pl.when(j == 0)
    def _():
        krow = lax.broadcasted_iota(I32, (seq, LANES), 0)
        kcol = lax.broadcasted_iota(I32, (seq, LANES), 1)
        onehot = jnp.where(krow // BLK == kcol, 1.0, 0.0).astype(BF16)
        for p in range(n_pairs):
            kaug_ref[p, :, 0:LANES] = k_ref[:, p * LANES:(p + 1) * LANES]
            kaug_ref[p, :, LANES:2 * LANES] = onehot
            kmean_ref[p] = jnp.zeros((LANES, LANES), F32)
            for n in range(n_blk):
                kmean_ref[p, n:n + 1, :] = jnp.mean(
                    k_ref[n * BLK:(n + 1) * BLK, p * LANES:(p + 1) * LANES].astype(F32), axis=0, keepdims=True)

    near_blk = jnp.maximum(j - 1, 0)
    near0 = pl.multiple_of(near_blk * BLK, BLK)
    bias0 = pl.multiple_of(jnp.where(j == 0, BLK, 0), BLK)
    pad_rows = jnp.zeros((LANES - sel_rows, t), BF16)

    def update(p, s, v_t):
        m, l, alpha, pr = _softmax_step(m_ref[p], l_ref[p], s, ones_2)
        m_ref[p] = m
        l_ref[p] = l
        acc_ref[p] = alpha * acc_ref[p] + _dot(v_t, pr)

    q_near = []
    for p in range(n_pairs):
        qp = qt_ref[p * LANES:(p + 1) * LANES, :]
        near, far = [], []
        for hh in range(2):
            qh = jnp.where((row < HEAD_DIM) if hh == 0 else (row >= HEAD_DIM), qp, jnp.zeros_like(qp))
            gate = _dot(kmean_ref[p], qh.astype(F32))[0:sel_rows, :]
            g = jnp.where(blk_id < j, gate, NEG)
            g = jnp.where(blk_id < n_blk, g, -jnp.inf)
            sel_bias = jnp.full((sel_rows, t), NEG, F32)
            for it in range(k_sel):
                mx = jnp.max(g, axis=0, keepdims=True)
                first = jnp.min(jnp.where(g == mx, blk_id, LANES), axis=0, keepdims=True)
                pick = blk_id == first
                sel_bias = jnp.where(pick, jnp.where(it < j, 0.0, NEG), sel_bias)
                g = jnp.where(pick, -jnp.inf, g)
            near_bias = jnp.where(blk_id == j, 0.0, sel_bias)
            far_bias = jnp.where(blk_id < j - 1, sel_bias, NEG)
            near.append(jnp.concatenate([qh, near_bias.astype(BF16), pad_rows], axis=0))
            far.append(jnp.concatenate([qh, far_bias.astype(BF16), pad_rows], axis=0))
        q_near.append(jnp.concatenate(near, axis=1))
        qfar_ref[p] = jnp.concatenate(far, axis=1)
        m_ref[p] = jnp.full((1, 2 * t), NEG, F32)
        l_ref[p] = jnp.zeros((1, 2 * t), F32)
        acc_ref[p] = jnp.zeros((LANES, 2 * t), F32)

    logits = [_dot(kaug_ref[p, pl.ds(near0, 2 * BLK), :], q_near[p]) for p in range(n_pairs)]
    for p in range(n_pairs):
        feats = slice(p * LANES, (p + 1) * LANES)
        bias = jnp.concatenate([bt_ref[2 * p, pl.ds(bias0, 2 * BLK), :], bt_ref[2 * p + 1, pl.ds(bias0, 2 * BLK), :]],
                               axis=1)
        v_t = jnp.concatenate([vt_ref[near_blk, feats, :], vt_ref[near_blk + 1, feats, :]], axis=1)
        update(p, logits[p] + bias, v_t)

    def far_pair(pi, _):
        n0 = pl.multiple_of(pi * 2 * BLK, 2 * BLK)
        logits = [_dot(kaug_ref[p, pl.ds(n0, 2 * BLK), :], qfar_ref[p]) for p in range(n_pairs)]
        for p in range(n_pairs):
            feats = slice(p * LANES, (p + 1) * LANES)
            v_t = jnp.concatenate([vt_ref[2 * pi, feats, :], vt_ref[2 * pi + 1, feats, :]], axis=1)
            update(p, logits[p], v_t)
        return 0

    lax.fori_loop(0, jnp.maximum(j, 1) // 2, far_pair, 0)

    for p in range(n_pairs):
        acc = acc_ref[p] / l_ref[p]
        out_t = jnp.where(row < HEAD_DIM, acc[:, 0:t], acc[:, t:2 * t])
        o_ref[:, p * LANES:(p + 1) * LANES] = out_t.T.astype(o_ref.dtype)


def _moba_attention(q_t, k, v_t, bt, n_heads):
    b, seq, d = k.shape
    n_blk = seq // BLK
    n_grp = n_heads // MOBA_GRP
    n_pairs = MOBA_GRP // 2
    w = MOBA_GRP * HEAD_DIM
    k_sel = max(1, min(MOBA_TOPK, n_blk - 1))
    sel_rows = BF16_ROWS * pl.cdiv(n_blk, BF16_ROWS)
    assert sel_rows <= LANES and n_blk % 2 == 0
    return pl.pallas_call(
        functools.partial(_moba_kernel, n_blk=n_blk, k_sel=k_sel, sel_rows=sel_rows),
        grid=(b, n_grp, n_blk),
        in_specs=[pl.BlockSpec((None, w, BLK), lambda bi, g, j: (bi, g, j)),
                  pl.BlockSpec((None, seq, w), lambda bi, g, j: (bi, 0, g)),
                  pl.BlockSpec((None, n_blk, w, BLK), lambda bi, g, j: (bi, 0, g, 0)),
                  pl.BlockSpec((MOBA_GRP, 3 * BLK, BLK), lambda bi, g, j: (g, 0, 0))],
        out_specs=pl.BlockSpec((None, BLK, w), lambda bi, g, j: (bi, j, g)),
        out_shape=jax.ShapeDtypeStruct((b, seq, d), BF16),
        scratch_shapes=[pltpu.VMEM((n_pairs, seq, 2 * LANES), BF16),
                        pltpu.VMEM((n_pairs, LANES, LANES), F32),
                        pltpu.VMEM((n_pairs, 2 * LANES, 2 * BLK), BF16),
                        pltpu.VMEM((n_pairs, 1, 2 * BLK), F32),
                        pltpu.VMEM((n_pairs, 1, 2 * BLK), F32),
                        pltpu.VMEM((n_pairs, LANES, 2 * BLK), F32)],
        compiler_params=_cparams("parallel", "parallel", "arbitrary"),
        name="moba_attn",
    )(q_t, k, v_t, bt)


def _dsa_proj_kernel(x_ref, g_ref, win_ref, wkvt_ref, wwt_ref, gq_ref, gkv_ref, gkvt_ref, wuqt_ref, wqit_ref,
                     wuvt_ref, ckv_ref, vt_ref, kk_ref, widx_ref, qnt_ref, qit_ref, *, idx_scale):
    hn = _rms(x_ref[...], g_ref[...]).astype(BF16)
    proj = _dot(hn, win_ref[...])
    c_q = _rms(proj[:, 0:DSA_LORA], gq_ref[...]).astype(BF16)
    ckv_ref[...] = _rms(proj[:, DSA_LORA:2 * DSA_LORA], gkv_ref[...]).astype(BF16)
    kk_ref[...] = proj[:, 2 * DSA_LORA:2 * DSA_LORA + IDX_DIM].astype(BF16)
    ct = _dot_nt(wkvt_ref[...], hn)
    scale = lax.rsqrt(jnp.mean(ct * ct, axis=0, keepdims=True) + EPS)
    gt = jnp.concatenate([gkvt_ref[...]] * (ct.shape[1] // LANES), axis=1)
    vt_ref[...] = _dot(wuvt_ref[...], (ct * scale * gt).astype(BF16)).astype(BF16)
    widx_ref[...] = _dot_nt(wwt_ref[...], hn)[0:IDX_HEADS, :] * idx_scale
    qnt_ref[...] = _dot_nt(wuqt_ref[...], c_q).astype(BF16)
    qit_ref[...] = _dot_nt(wqit_ref[...], c_q).astype(BF16)


def _dsa_proj(x3, g, w_in_row, w_kv_t, w_w_t, g_q, g_kv, w_uq_t, w_qi_t, w_uv_t):
    b, seq, d = x3.shape
    tm = BLK
    g2, gq2, gkv2 = g.reshape(1, d), g_q.reshape(1, -1), g_kv.reshape(1, -1)
    gkv_t = jnp.broadcast_to(g_kv.reshape(-1, 1), (DSA_LORA, LANES))
    n_q, n_i, n_v = w_uq_t.shape[0], w_qi_t.shape[0], w_uv_t.shape[0]
    consts = (g2, w_in_row, w_kv_t, w_w_t, gq2, gkv2, gkv_t, w_uq_t, w_qi_t, w_uv_t)
    idx_scale = IDX_HEADS ** -0.5 * IDX_DIM ** -0.5
    return pl.pallas_call(
        functools.partial(_dsa_proj_kernel, idx_scale=idx_scale),
        grid=(b, seq // tm),
        in_specs=[pl.BlockSpec((None, tm, d), lambda bi, i: (bi, i, 0))] + [_full(a) for a in consts],
        out_specs=[pl.BlockSpec((None, tm, DSA_LORA), lambda bi, i: (bi, i, 0)),
                   pl.BlockSpec((None, None, n_v, tm), lambda bi, i: (bi, i, 0, 0)),
                   pl.BlockSpec((None, tm, IDX_DIM), lambda bi, i: (bi, i, 0)),
                   pl.BlockSpec((None, IDX_HEADS, tm), lambda bi, i: (bi, 0, i)),
                   pl.BlockSpec((None, n_q, tm), lambda bi, i: (bi, 0, i)),
                   pl.BlockSpec((None, n_i, tm), lambda bi, i: (bi, 0, i))],
        out_shape=[jax.ShapeDtypeStruct((b, seq, DSA_LORA), BF16),
                   jax.ShapeDtypeStruct((b, seq // tm, n_v, tm), BF16),
                   jax.ShapeDtypeStruct((b, seq, IDX_DIM), BF16),
                   jax.ShapeDtypeStruct((b, IDX_HEADS, seq), F32),
                   jax.ShapeDtypeStruct((b, n_q, seq), BF16),
                   jax.ShapeDtypeStruct((b, n_i, seq), BF16)],
        compiler_params=_cparams("parallel", "parallel"),
        name="dsa_proj",
    )(x3, *consts)


def _dsa_select_kernel(qit_ref, w_ref, kk_ref, o_ref, keys_ref, *, top_k, idx_bits):
    c = pl.program_id(1)
    n_kc, tk, tq = keys_ref.shape
    n_proc = c + 1
    key_row = lax.broadcasted_iota(I32, (tk, tq), 0)
    q_pos = c * tq + lax.broadcasted_iota(I32, (tk, tq), 1)
    w = w_ref[...]

    def to_key(x):
        bits = pltpu.bitcast(x, I32)
        return jnp.where(bits < 0, bits ^ 0x7FFFFFFF, bits)

    def from_key(k):
        return pltpu.bitcast(jnp.where(k < 0, k ^ 0x7FFFFFFF, k), F32)

    def score_chunk(kc, carry):
        lo_v, hi_v = carry
        kt = kk_ref[pl.ds(pl.multiple_of(kc * tk, tk), tk), :]
        acc = jnp.zeros((tk, tq), F32)
        for h in range(IDX_HEADS):
            x = _dot(kt, qit_ref[h * IDX_DIM:(h + 1) * IDX_DIM, :])
            acc = acc + w[h:h + 1, :] * jnp.maximum(x, 0.0)
        causal = kc * tk + key_row <= q_pos
        sc = jnp.where(causal, acc, NEG)
        keys_ref[kc] = to_key(sc)
        lo_v = jnp.minimum(lo_v, jnp.min(jnp.where(causal, acc, jnp.inf), axis=0, keepdims=True))
        hi_v = jnp.maximum(hi_v, jnp.max(sc, axis=0, keepdims=True))
        return lo_v, hi_v

    lo_v, hi_v = lax.fori_loop(0, n_proc, score_chunk,
                               (jnp.full((1, tq), jnp.inf, F32), jnp.full((1, tq), -jnp.inf, F32)))

    def count(pred):
        def body(kc, tot):
            hit = jnp.where(pred(keys_ref[kc], kc * tk + key_row), 1, 0)
            return tot + jnp.sum(hit, axis=0, keepdims=True)
        return lax.fori_loop(0, n_proc, body, jnp.zeros((1, tq), I32))

    lo0 = to_key(lo_v)
    c0 = count(lambda key, pos: key >= lo0)
    few = c0 < top_k
    state0 = (jnp.int32(0), jnp.int32(1), jnp.where(few, INT_MIN, lo0), to_key(hi_v) + 1,
              jnp.where(few, top_k, c0), jnp.zeros((1, tq), I32))

    def settled(lo, hi, c_lo):
        return (c_lo <= top_k) | (hi - lo == 1)

    def search_step(state):
        it, _, lo, hi, c_lo, c_hi = state
        done = settled(lo, hi, c_lo)
        lo_f, hi_f = from_key(lo), from_key(hi)
        frac = ((c_lo - top_k).astype(F32) + 0.5) / ((c_lo - c_hi).astype(F32) + 1.0)
        cand_i = to_key(lo_f + (hi_f - lo_f) * frac)
        cand_b = lo + lax.shift_right_logical(hi - lo, 1)
        cand = jnp.where(it % 3 == 2, cand_b, cand_i)
        cand = jnp.minimum(jnp.maximum(cand, lo + 1), hi - 1)
        cnt = count(lambda key, pos: key >= cand)
        up = (cnt >= top_k) & jnp.logical_not(done)
        down = (cnt < top_k) & jnp.logical_not(done)
        lo, c_lo = jnp.where(up, cand, lo), jnp.where(up, cnt, c_lo)
        hi, c_hi = jnp.where(down, cand, hi), jnp.where(down, cnt, c_hi)
        active = jnp.max(jnp.where(settled(lo, hi, c_lo), 0, 1))
        return it + 1, active, lo, hi, c_lo, c_hi

    _, _, thr, _, cnt_ge, _ = lax.while_loop(lambda st: (st[1] > 0) & (st[0] < 128), search_step, state0)

    def tie_cut():
        need = top_k - count(lambda key, pos: key > thr)

        def index_bit(i, x):
            cand = x | jnp.left_shift(jnp.int32(1), idx_bits - 1 - i)
            cnt = count(lambda key, pos: (key == thr) & (pos < cand))
            return jnp.where(cnt < need, cand, x)
        return lax.fori_loop(0, idx_bits, index_bit, jnp.zeros((1, tq), I32))

    last_tie = lax.cond(jnp.max(cnt_ge) > top_k, tie_cut, lambda: jnp.full((1, tq), 2 ** idx_bits, I32))

    def write_chunk(kc, _):
        key = keys_ref[kc]
        pos = kc * tk + key_row
        chosen = (key > thr) | ((key == thr) & (pos <= last_tie))
        rows = pl.ds(pl.multiple_of(kc * tk, tk), tk)
        o_ref[rows, :] = jnp.where(chosen & (pos <= q_pos), 0.0, NEG).astype(o_ref.dtype)
        return 0

    lax.fori_loop(0, n_proc, write_chunk, 0)

    def fill_chunk(kc, _):
        o_ref[pl.ds(pl.multiple_of(kc * tk, tk), tk), :] = jnp.full((tk, tq), NEG, o_ref.dtype)
        return 0

    lax.fori_loop(n_proc, n_kc, fill_chunk, 0)


def _dsa_select(qi_t, widx_t, kk):
    b, seq, _ = kk.shape
    t = BLK
    n_kc = seq // t
    top_k = min(DSA_TOPK_MAX, seq // 4)
    idx_bits = max(1, (seq - 1).bit_length())
    return pl.pallas_call(
        functools.partial(_dsa_select_kernel, top_k=top_k, idx_bits=idx_bits),
        grid=(b, seq // t),
        in_specs=[pl.BlockSpec((None, qi_t.shape[1], t), lambda bi, c: (bi, 0, c)),
                  pl.BlockSpec((None, IDX_HEADS, t), lambda bi, c: (bi, 0, c)),
                  pl.BlockSpec((None, seq, IDX_DIM), lambda bi, c: (bi, 0, 0))],
        out_specs=pl.BlockSpec((None, None, seq, t), lambda bi, c: (bi, c, 0, 0)),
        out_shape=jax.ShapeDtypeStruct((b, seq // t, seq, t), BF16),
        scratch_shapes=[pltpu.VMEM((n_kc, t, t), I32)],
        compiler_params=_cparams("parallel", "arbitrary"),
        name="dsa_select",
    )(qi_t, widx_t, kk)


def _dsa_attn_kernel(qnt_ref, ckv_ref, vt_ref, mb_ref, bt_ref, wukt_ref, o_ref,
                     qlat_ref, acc_ref, m_ref, l_ref, *, n_heads):
    qt = pl.program_id(1)
    t = BLK
    n_grp = n_heads // DSA_GRP
    row = lax.broadcasted_iota(I32, (LANES, t), 0)
    ones_rows = jnp.ones((BF16_ROWS, t), BF16)

    for hp in range(n_heads // 2):
        qp = qnt_ref[hp * LANES:(hp + 1) * LANES, :]
        for hh in range(2):
            qh = jnp.where((row < HEAD_DIM) if hh == 0 else (row >= HEAD_DIM), qp, jnp.zeros_like(qp))
            h = 2 * hp + hh
            qlat_ref[:, h * t:(h + 1) * t] = _dot(wukt_ref[hp], qh).astype(BF16)

    m_ref[...] = jnp.full(m_ref.shape, NEG, F32)
    l_ref[...] = jnp.zeros_like(l_ref)
    acc_ref[...] = jnp.zeros_like(acc_ref)

    def process_tile(kk, bias_row0):
        rows = pl.ds(pl.multiple_of(kk * t, t), t)
        kt = ckv_ref[rows, :]
        mb = mb_ref[rows, :].astype(F32)
        logits = [_dot(kt, qlat_ref[:, g * DSA_GRP * t:(g + 1) * DSA_GRP * t]) for g in range(n_grp)]
        for g in range(n_grp):
            cols = slice(g * DSA_GRP * t, (g + 1) * DSA_GRP * t)
            parts = []
            for i in range(DSA_GRP):
                s = logits[g][:, i * t:(i + 1) * t] + mb
                if bias_row0 is not None:
                    s = s + bt_ref[g * DSA_GRP + i, bias_row0:bias_row0 + t, :]
                parts.append(s)
            m, l, alpha, pr = _softmax_step(m_ref[:, cols], l_ref[:, cols], jnp.concatenate(parts, axis=1), ones_rows)
            m_ref[:, cols] = m
            l_ref[:, cols] = l
            for i in range(DSA_GRP):
                h = g * DSA_GRP + i
                feats = slice(h * HEAD_DIM, (h + 1) * HEAD_DIM)
                hc = slice(i * t, (i + 1) * t)
                acc_ref[feats, :] = alpha[:, hc] * acc_ref[feats, :] + _dot(vt_ref[kk, feats, :], pr[:, hc])

    process_tile(qt, BLK)

    @pl.when(qt >= 1)
    def _():
        process_tile(qt - 1, 0)

    def far_tile(kk, _):
        process_tile(kk, None)
        return 0

    lax.fori_loop(0, jnp.maximum(qt - 1, 0), far_tile, 0)

    for h in range(n_heads):
        feats = slice(h * HEAD_DIM, (h + 1) * HEAD_DIM)
        acc_ref[feats, :] = acc_ref[feats, :] / l_ref[:, h * t:(h + 1) * t]
    o_ref[...] = acc_ref[...].T.astype(o_ref.dtype)


def _dsa_attention(qn_t, ckv, v_t, mask_t, bt, w_uk_t, n_heads):
    b, seq, _ = ckv.shape
    t = BLK
    d_q = qn_t.shape[1]
    return pl.pallas_call(
        functools.partial(_dsa_attn_kernel, n_heads=n_heads),
        grid=(b, seq // t),
        in_specs=[pl.BlockSpec((None, d_q, t), lambda bi, q: (bi, 0, q)),
                  pl.BlockSpec((None, seq, DSA_LORA), lambda bi, q: (bi, 0, 0)),
                  pl.BlockSpec((None, seq // t, d_q, t), lambda bi, q: (bi, 0, 0, 0)),
                  pl.BlockSpec((None, None, seq, t), lambda bi, q: (bi, q, 0, 0)),
                  _full(bt), _full(w_uk_t)],
        out_specs=pl.BlockSpec((None, t, d_q), lambda bi, q: (bi, q, 0)),
        out_shape=jax.ShapeDtypeStruct((b, seq, d_q), BF16),
        scratch_shapes=[pltpu.VMEM((DSA_LORA, n_heads * t), BF16),
                        pltpu.VMEM((d_q, t), F32),
                        pltpu.VMEM((1, n_heads * t), F32),
                        pltpu.VMEM((1, n_heads * t), F32)],
        compiler_params=_cparams("parallel", "arbitrary"),
        name="dsa_attn",
    )(qn_t, ckv, v_t, mask_t, bt, w_uk_t)


def _moba_layer(x2, batch, seq, g, w_qkv, w_o, bt, n_heads):
    d = x2.shape[1]
    wq_t = (w_qkv[:, :d] * (HEAD_DIM ** -0.5 * LOG2E)).T.astype(BF16)
    wk = w_qkv[:, d:2 * d].astype(BF16)
    wv_t = w_qkv[:, 2 * d:].T.astype(BF16)
    q_t, k, v_t = _moba_proj(x2.reshape(batch, seq, d), g, wq_t, wk, wv_t)
    attn = _moba_attention(q_t, k, v_t, bt, n_heads)
    return _matmul_res(attn.reshape(batch * seq, d), w_o.astype(BF16), x2)


def _dsa_layer(x2, batch, seq, g, w_in, g_q, g_kv, w_uq, w_qi, w_uk, w_uv, w_o, bt, n_heads):
    d = x2.shape[1]
    k_lo = 2 * DSA_LORA
    w_in_row = jnp.concatenate([w_in[:, :k_lo + IDX_DIM], jnp.zeros((d, LANES - IDX_DIM), F32)], axis=1).astype(BF16)
    w_kv_t = w_in[:, DSA_LORA:k_lo].T.astype(BF16)
    w_w_t = jnp.concatenate([w_in[:, k_lo + IDX_DIM:].T, jnp.zeros((BF16_ROWS - IDX_HEADS, d), F32)]).astype(BF16)
    w_uq_t = (w_uq * (HEAD_DIM ** -0.5 * LOG2E)).T.astype(BF16)
    w_uv_t = jnp.transpose(w_uv, (0, 2, 1)).reshape(n_heads * HEAD_DIM, DSA_LORA).astype(BF16)
    ckv, v_t, kk, widx_t, qn_t, qi_t = _dsa_proj(x2.reshape(batch, seq, d), g, w_in_row, w_kv_t, w_w_t,
                                                 g_q, g_kv, w_uq_t, w_qi.T.astype(BF16), w_uv_t)
    mask_t = _dsa_select(qi_t, widx_t, kk)
    w_uk_t = jnp.transpose(w_uk.reshape(n_heads // 2, 2 * HEAD_DIM, DSA_LORA), (0, 2, 1)).astype(BF16)
    attn = _dsa_attention(qn_t, ckv, v_t, mask_t, bt[:, :2 * BLK], w_uk_t, n_heads)
    return _matmul_res(attn.reshape(batch * seq, -1), w_o.astype(BF16), x2)


def kernel(x, rel_bias, ln_attn, ln_mlp, moba_w_qkv, moba_w_o, dsa_w_in, dsa_g_q, dsa_g_kv, dsa_w_uq, dsa_w_qi,
           dsa_w_uk, dsa_w_uv, dsa_w_o, mlp_w_up, mlp_w_down, final_norm):
    batch, seq, d = x.shape
    n_heads = rel_bias.shape[1]
    depth = ln_attn.shape[0]
    bt = _bias_tiles(rel_bias)
    x2 = x.reshape(batch * seq, d)
    for i in range(depth):
        j = i // 2
        if i % 2 == 0:
            x2 = _moba_layer(x2, batch, seq, ln_attn[i], moba_w_qkv[j], moba_w_o[j], bt, n_heads)
        else:
            x2 = _dsa_layer(x2, batch, seq, ln_attn[i], dsa_w_in[j], dsa_g_q[j], dsa_g_kv[j], dsa_w_uq[j],
                            dsa_w_qi[j], dsa_w_uk[j], dsa_w_uv[j], dsa_w_o[j], bt, n_heads)
        x2 = _mlp(x2, ln_mlp[i], mlp_w_up[i].astype(BF16), mlp_w_down[i].astype(BF16))
    return _final_norm(x2, final_norm).reshape(batch, seq, d)
```

```python
import functools
import math

import numpy as np
import jax
import jax.numpy as jnp
from jax import lax
from jax.experimental import pallas as pl
from jax.experimental.pallas import tpu as pltpu

F32 = jnp.float32
BF16 = jnp.bfloat16
I32 = jnp.int32
I16 = jnp.int16

EPS = 1e-6
NEG = -1e30
INT_MIN = -(2 ** 31)
LOG2E = 1.0 / math.log(2.0)

NUM_BUCKETS = 32
MAX_DISTANCE = 128
HEAD_DIM = 64
LANES = 128
BF16_ROWS = 16
BLK = 256
MOBA_TOPK = 3
MOBA_GRP = 8
DSA_TOPK_MAX = 256
DSA_LORA = 256
DSA_GRP = 4
IDX_HEADS = 8
IDX_DIM = 64
VMEM_LIMIT = 56 * 1024 * 1024


def _dot(a, b):
    return jnp.dot(a, b, preferred_element_type=F32)


def _dot_nt(a, b):
    return lax.dot_general(a, b, (((1,), (1,)), ((), ())), preferred_element_type=F32)


def _rms(x, g):
    return x * lax.rsqrt(jnp.mean(x * x, axis=-1, keepdims=True) + EPS) * g


def _cparams(*sem):
    return pltpu.CompilerParams(dimension_semantics=sem, vmem_limit_bytes=VMEM_LIMIT)


def _full(a):
    return pl.BlockSpec(a.shape, lambda *_: (0,) * a.ndim, pipeline_mode=pl.Buffered(1))


def _bucket_map_t():
    jj = np.arange(2 * BLK)[:, None]
    i = np.arange(BLK)[None, :]
    d = i - jj + BLK
    n = np.maximum(d, 0)
    exact = NUM_BUCKETS // 2
    nf = np.maximum(n, 1).astype(np.float32)
    large = exact + (np.log(nf / exact) / math.log(MAX_DISTANCE / exact) * (NUM_BUCKETS - exact)).astype(np.int32)
    large = np.minimum(large, NUM_BUCKETS - 1)
    bucket = np.where(n < exact, n, large)
    bmap = np.where(d >= 0, bucket, -1).astype(np.int32)
    return np.concatenate([bmap, bmap[:BLK]], axis=0)


def _bias_tile_kernel(tab_ref, bmap_ref, o_ref):
    h = pl.program_id(0)
    bm = bmap_ref[...]
    base = tab_ref[NUM_BUCKETS - 1, h]
    acc = jnp.full(bm.shape, NEG, F32)
    for b in range(NUM_BUCKETS):
        acc = jnp.where(bm == b, (tab_ref[b, h] - base) * LOG2E, acc)
    o_ref[...] = acc


def _bias_tiles(rel_bias):
    n_heads = rel_bias.shape[1]
    bmap = jnp.asarray(_bucket_map_t())
    return pl.pallas_call(
        _bias_tile_kernel,
        grid=(n_heads,),
        in_specs=[pl.BlockSpec(memory_space=pltpu.SMEM),
                  pl.BlockSpec((3 * BLK, BLK), lambda h: (0, 0))],
        out_specs=pl.BlockSpec((None, 3 * BLK, BLK), lambda h: (h, 0, 0)),
        out_shape=jax.ShapeDtypeStruct((n_heads, 3 * BLK, BLK), F32),
        name="bias_tiles",
    )(rel_bias, bmap)


def _matmul_res_kernel(a_ref, w_ref, r_ref, o_ref):
    o_ref[...] = r_ref[...] + _dot(a_ref[...], w_ref[...])


def _matmul_res(a, w, res, tm=512):
    n, k = a.shape
    m = w.shape[1]
    return pl.pallas_call(
        _matmul_res_kernel,
        grid=(n // tm,),
        in_specs=[pl.BlockSpec((tm, k), lambda i: (i, 0)),
                  pl.BlockSpec((k, m), lambda i: (0, 0)),
                  pl.BlockSpec((tm, m), lambda i: (i, 0))],
        out_specs=pl.BlockSpec((tm, m), lambda i: (i, 0)),
        out_shape=jax.ShapeDtypeStruct((n, m), F32),
        compiler_params=_cparams("parallel"),
        name="matmul_res",
    )(a, w, res)


def _mlp_kernel(x_ref, g_ref, wu_ref, wd_ref, o_ref, xn_ref, acc_ref):
    f = pl.program_id(1)

    @pl.when(f == 0)
    def _():
        xn_ref[...] = _rms(x_ref[...], g_ref[...]).astype(BF16)
        acc_ref[...] = jnp.zeros_like(acc_ref)

    h = jnp.square(jnp.maximum(_dot(xn_ref[...], wu_ref[...]), 0.0)).astype(BF16)
    acc_ref[...] += _dot(h, wd_ref[...])

    @pl.when(f == pl.num_programs(1) - 1)
    def _():
        o_ref[...] = x_ref[...] + acc_ref[...]


def _mlp(x2, g, w_up, w_down, tm=512, tf=1024):
    n, d = x2.shape
    d_ff = w_up.shape[1]
    return pl.pallas_call(
        _mlp_kernel,
        grid=(n // tm, d_ff // tf),
        in_specs=[pl.BlockSpec((tm, d), lambda i, f: (i, 0)),
                  pl.BlockSpec((1, d), lambda i, f: (0, 0)),
                  pl.BlockSpec((d, tf), lambda i, f: (0, f)),
                  pl.BlockSpec((tf, d), lambda i, f: (f, 0))],
        out_specs=pl.BlockSpec((tm, d), lambda i, f: (i, 0)),
        out_shape=jax.ShapeDtypeStruct((n, d), F32),
        scratch_shapes=[pltpu.VMEM((tm, d), BF16), pltpu.VMEM((tm, d), F32)],
        compiler_params=_cparams("parallel", "arbitrary"),
        name="mlp",
    )(x2, g.reshape(1, d), w_up, w_down)


def _final_norm_kernel(x_ref, g_ref, o_ref):
    o_ref[...] = _rms(x_ref[...], g_ref[...])


def _final_norm(x2, g, tm=512):
    n, d = x2.shape
    return pl.pallas_call(
        _final_norm_kernel,
        grid=(n // tm,),
        in_specs=[pl.BlockSpec((tm, d), lambda i: (i, 0)),
                  pl.BlockSpec((1, d), lambda i: (0, 0))],
        out_specs=pl.BlockSpec((tm, d), lambda i: (i, 0)),
        out_shape=jax.ShapeDtypeStruct((n, d), F32),
        compiler_params=_cparams("parallel"),
        name="final_norm",
    )(x2, g.reshape(1, d))


def _softmax_step(m_old, l_old, s, ones_rows):
    m_new = jnp.maximum(m_old, jnp.max(s, axis=0, keepdims=True))
    alpha = jnp.exp2(m_old - m_new)
    p = jnp.exp2(s - m_new).astype(BF16)
    l_new = alpha * l_old + _dot(ones_rows, p)[0:1, :]
    return m_new, l_new, alpha, p


def _moba_proj_kernel(x_ref, g_ref, wqt_ref, wk_ref, wvt_ref, qt_ref, k_ref, vt_ref):
    xn = _rms(x_ref[...], g_ref[...]).astype(BF16)
    qt_ref[...] = _dot_nt(wqt_ref[...], xn).astype(BF16)
    k_ref[...] = _dot(xn, wk_ref[...]).astype(BF16)
    vt_ref[...] = _dot_nt(wvt_ref[...], xn).astype(BF16)


def _moba_proj(x3, g, wq_t, wk, wv_t):
    b, seq, d = x3.shape
    g2 = g.reshape(1, d)
    return pl.pallas_call(
        _moba_proj_kernel,
        grid=(b, seq // BLK),
        in_specs=[pl.BlockSpec((None, BLK, d), lambda bi, i: (bi, i, 0)),
                  _full(g2), _full(wq_t), _full(wk), _full(wv_t)],
        out_specs=[pl.BlockSpec((None, d, BLK), lambda bi, i: (bi, 0, i)),
                   pl.BlockSpec((None, BLK, d), lambda bi, i: (bi, i, 0)),
                   pl.BlockSpec((None, None, d, BLK), lambda bi, i: (bi, i, 0, 0))],
        out_shape=[jax.ShapeDtypeStruct((b, d, seq), BF16),
                   jax.ShapeDtypeStruct((b, seq, d), BF16),
                   jax.ShapeDtypeStruct((b, seq // BLK, d, BLK), BF16)],
        compiler_params=_cparams("parallel", "parallel"),
        name="moba_proj",
    )(x3, g2, wq_t, wk, wv_t)


def _moba_kernel(qt_ref, k_ref, vt_ref, bt_ref, o_ref, kaug_ref, kmean_ref, qfar_ref, m_ref, l_ref, acc_ref,
                 *, n_blk, k_sel, sel_rows):
    j = pl.program_id(2)
    seq = k_ref.shape[0]
    n_pairs = MOBA_GRP // 2
    t = BLK
    row = lax.broadcasted_iota(I32, (LANES, t), 0)
    blk_id = lax.broadcasted_iota(I32, (sel_rows, t), 0)
    ones_2 = jnp.ones((BF16_ROWS, 2 * t), BF16)

    @pl.when(j == 0)
    def _():
        krow = lax.broadcasted_iota(I32, (seq, LANES), 0)
        kcol = lax.broadcasted_iota(I32, (seq, LANES), 1)
        onehot = jnp.where(krow // BLK == kcol, 1.0, 0.0).astype(BF16)
        for p in range(n_pairs):
            kaug_ref[p, :, 0:LANES] = k_ref[:, p * LANES:(p + 1) * LANES]
            kaug_ref[p, :, LANES:2 * LANES] = onehot
            kmean_ref[p] = jnp.zeros((LANES, LANES), F32)
            for n in range(n_blk):
                kmean_ref[p, n:n + 1, :] = jnp.mean(
                    k_ref[n * BLK:(n + 1) * BLK, p * LANES:(p + 1) * LANES].astype(F32), axis=0, keepdims=True)

    near_blk = jnp.maximum(j - 1, 0)
    near0 = pl.multiple_of(near_blk * BLK, BLK)
    bias0 = pl.multiple_of(jnp.where(j == 0, BLK, 0), BLK)
    pad_rows = jnp.zeros((LANES - sel_rows, t), BF16)

    def update(p, s, v_t):
        m, l, alpha, pr = _softmax_step(m_ref[p], l_ref[p], s, ones_2)
        m_ref[p] = m
        l_ref[p] = l
        acc_ref[p] = alpha * acc_ref[p] + _dot(v_t, pr)

    q_near = []
    for p in range(n_pairs):
        qp = qt_ref[p * LANES:(p + 1) * LANES, :]
        near, far = [], []
        for hh in range(2):
            qh = jnp.where((row < HEAD_DIM) if hh == 0 else (row >= HEAD_DIM), qp, jnp.zeros_like(qp))
            gate = _dot(kmean_ref[p], qh.astype(F32))[0:sel_rows, :]
            g = jnp.where(blk_id < j, gate, NEG)
            g = jnp.where(blk_id < n_blk, g, -jnp.inf)
            sel_bias = jnp.full((sel_rows, t), NEG, F32)
            for it in range(k_sel):
                mx = jnp.max(g, axis=0, keepdims=True)
                first = jnp.min(jnp.where(g == mx, blk_id, LANES), axis=0, keepdims=True)
                pick = blk_id == first
                sel_bias = jnp.where(pick, jnp.where(it < j, 0.0, NEG), sel_bias)
                g = jnp.where(pick, -jnp.inf, g)
            near_bias = jnp.where(blk_id == j, 0.0, sel_bias)
            far_bias = jnp.where(blk_id < j - 1, sel_bias, NEG)
            near.append(jnp.concatenate([qh, near_bias.astype(BF16), pad_rows], axis=0))
            far.append(jnp.concatenate([qh, far_bias.astype(BF16), pad_rows], axis=0))
        q_near.append(jnp.concatenate(near, axis=1))
        qfar_ref[p] = jnp.concatenate(far, axis=1)
        m_ref[p] = jnp.full((1, 2 * t), NEG, F32)
        l_ref[p] = jnp.zeros((1, 2 * t), F32)
        acc_ref[p] = jnp.zeros((LANES, 2 * t), F32)

    logits = [_dot(kaug_ref[p, pl.ds(near0, 2 * BLK), :], q_near[p]) for p in range(n_pairs)]
    for p in range(n_pairs):
        feats = slice(p * LANES, (p + 1) * LANES)
        bias = jnp.concatenate([bt_ref[2 * p, pl.ds(bias0, 2 * BLK), :], bt_ref[2 * p + 1, pl.ds(bias0, 2 * BLK), :]],
                               axis=1)
        v_t = jnp.concatenate([vt_ref[near_blk, feats, :], vt_ref[near_blk + 1, feats, :]], axis=1)
        update(p, logits[p] + bias, v_t)

    def far_pair(pi, _):
        n0 = pl.multiple_of(pi * 2 * BLK, 2 * BLK)
        logits = [_dot(kaug_ref[p, pl.ds(n0, 2 * BLK), :], qfar_ref[p]) for p in range(n_pairs)]
        for p in range(n_pairs):
            feats = slice(p * LANES, (p + 1) * LANES)
            v_t = jnp.concatenate([vt_ref[2 * pi, feats, :], vt_ref[2 * pi + 1, feats, :]], axis=1)
            update(p, logits[p], v_t)
        return 0

    lax.fori_loop(0, jnp.maximum(j, 1) // 2, far_pair, 0)

    for p in range(n_pairs):
        acc = acc_ref[p] / l_ref[p]
        out_t = jnp.where(row < HEAD_DIM, acc[:, 0:t], acc[:, t:2 * t])
        o_ref[:, p * LANES:(p + 1) * LANES] = out_t.T.astype(o_ref.dtype)


def _moba_attention(q_t, k, v_t, bt, n_heads):
    b, seq, d = k.shape
    n_blk = seq // BLK
    n_grp = n_heads // MOBA_GRP
    n_pairs = MOBA_GRP // 2
    w = MOBA_GRP * HEAD_DIM
    k_sel = max(1, min(MOBA_TOPK, n_blk - 1))
    sel_rows = BF16_ROWS * pl.cdiv(n_blk, BF16_ROWS)
    assert sel_rows <= LANES and n_blk % 2 == 0
    return pl.pallas_call(
        functools.partial(_moba_kernel, n_blk=n_blk, k_sel=k_sel, sel_rows=sel_rows),
        grid=(b, n_grp, n_blk),
        in_specs=[pl.BlockSpec((None, w, BLK), lambda bi, g, j: (bi, g, j)),
                  pl.BlockSpec((None, seq, w), lambda bi, g, j: (bi, 0, g)),
                  pl.BlockSpec((None, n_blk, w, BLK), lambda bi, g, j: (bi, 0, g, 0)),
                  pl.BlockSpec((MOBA_GRP, 3 * BLK, BLK), lambda bi, g, j: (g, 0, 0))],
        out_specs=pl.BlockSpec((None, BLK, w), lambda bi, g, j: (bi, j, g)),
        out_shape=jax.ShapeDtypeStruct((b, seq, d), BF16),
        scratch_shapes=[pltpu.VMEM((n_pairs, seq, 2 * LANES), BF16),
                        pltpu.VMEM((n_pairs, LANES, LANES), F32),
                        pltpu.VMEM((n_pairs, 2 * LANES, 2 * BLK), BF16),
                        pltpu.VMEM((n_pairs, 1, 2 * BLK), F32),
                        pltpu.VMEM((n_pairs, 1, 2 * BLK), F32),
                        pltpu.VMEM((n_pairs, LANES, 2 * BLK), F32)],
        compiler_params=_cparams("parallel", "parallel", "arbitrary"),
        name="moba_attn",
    )(q_t, k, v_t, bt)


def _dsa_proj_kernel(x_ref, g_ref, win_ref, wkvt_ref, wwt_ref, gq_ref, gkv_ref, gkvt_ref, wuqt_ref, wqit_ref,
                     wuvt_ref, ckv_ref, vt_ref, kk_ref, widx_ref, qnt_ref, qit_ref, *, idx_scale):
    hn = _rms(x_ref[...], g_ref[...]).astype(BF16)
    proj = _dot(hn, win_ref[...])
    c_q = _rms(proj[:, 0:DSA_LORA], gq_ref[...]).astype(BF16)
    ckv_ref[...] = _rms(proj[:, DSA_LORA:2 * DSA_LORA], gkv_ref[...]).astype(BF16)
    kk_ref[...] = proj[:, 2 * DSA_LORA:2 * DSA_LORA + IDX_DIM].astype(BF16)
    ct = _dot_nt(wkvt_ref[...], hn)
    scale = lax.rsqrt(jnp.mean(ct * ct, axis=0, keepdims=True) + EPS)
    gt = jnp.concatenate([gkvt_ref[...]] * (ct.shape[1] // LANES), axis=1)
    vt_ref[...] = _dot(wuvt_ref[...], (ct * scale * gt).astype(BF16)).astype(BF16)
    widx_ref[...] = _dot_nt(wwt_ref[...], hn)[0:IDX_HEADS, :] * idx_scale
    qnt_ref[...] = _dot_nt(wuqt_ref[...], c_q).astype(BF16)
    qit_ref[...] = _dot_nt(wqit_ref[...], c_q).astype(BF16)


def _dsa_proj(x3, g, w_in_row, w_kv_t, w_w_t, g_q, g_kv, w_uq_t, w_qi_t, w_uv_t):
    b, seq, d = x3.shape
    tm = BLK
    g2, gq2, gkv2 = g.reshape(1, d), g_q.reshape(1, -1), g_kv.reshape(1, -1)
    gkv_t = jnp.broadcast_to(g_kv.reshape(-1, 1), (DSA_LORA, LANES))
    n_q, n_i, n_v = w_uq_t.shape[0], w_qi_t.shape[0], w_uv_t.shape[0]
    consts = (g2, w_in_row, w_kv_t, w_w_t, gq2, gkv2, gkv_t, w_uq_t, w_qi_t, w_uv_t)
    idx_scale = IDX_HEADS ** -0.5 * IDX_DIM ** -0.5
    return pl.pallas_call(
        functools.partial(_dsa_proj_kernel, idx_scale=idx_scale),
        grid=(b, seq // tm),
        in_specs=[pl.BlockSpec((None, tm, d), lambda bi, i: (bi, i, 0))] + [_full(a) for a in consts],
        out_specs=[pl.BlockSpec((None, tm, DSA_LORA), lambda bi, i: (bi, i, 0)),
                   pl.BlockSpec((None, None, n_v, tm), lambda bi, i: (bi, i, 0, 0)),
                   pl.BlockSpec((None, tm, IDX_DIM), lambda bi, i: (bi, i, 0)),
                   pl.BlockSpec((None, IDX_HEADS, tm), lambda bi, i: (bi, 0, i)),
                   pl.BlockSpec((None, n_q, tm), lambda bi, i: (bi, 0, i)),
                   pl.BlockSpec((None, n_i, tm), lambda bi, i: (bi, 0, i))],
        out_shape=[jax.ShapeDtypeStruct((b, seq, DSA_LORA), BF16),
                   jax.ShapeDtypeStruct((b, seq // tm, n_v, tm), BF16),
                   jax.ShapeDtypeStruct((b, seq, IDX_DIM), BF16),
                   jax.ShapeDtypeStruct((b, IDX_HEADS, seq), F32),
                   jax.ShapeDtypeStruct((b, n_q, seq), BF16),
                   jax.ShapeDtypeStruct((b, n_i, seq), BF16)],
        compiler_params=_cparams("parallel", "parallel"),
        name="dsa_proj",
    )(x3, *consts)


def _dsa_select_kernel(qit_ref, w_ref, kk_ref, o_ref, keys_ref, half_ref, *, top_k, idx_bits):
    c = pl.program_id(1)
    n_kc, tk, tq = keys_ref.shape
    n_proc = c + 1
    key_row = lax.broadcasted_iota(I32, (tk, tq), 0)
    q_pos = c * tq + lax.broadcasted_iota(I32, (tk, tq), 1)
    w = w_ref[...]

    def score_chunk(kc, _):
        kt = kk_ref[pl.ds(pl.multiple_of(kc * tk, tk), tk), :]
        acc = jnp.zeros((tk, tq), F32)
        for h in range(IDX_HEADS):
            x = _dot(kt, qit_ref[h * IDX_DIM:(h + 1) * IDX_DIM, :])
            acc = acc + w[h:h + 1, :] * jnp.maximum(x, 0.0)
        sc = jnp.where(kc * tk + key_row <= q_pos, acc, NEG)
        bits = pltpu.bitcast(sc, I32)
        key = jnp.where(bits < 0, bits ^ 0x7FFFFFFF, bits)
        keys_ref[kc] = key
        half_ref[kc] = jnp.right_shift(key, 16).astype(I16)
        return 0

    lax.fori_loop(0, n_proc, score_chunk, 0)

    def count(pred):
        def body(kc, tot):
            hit = jnp.where(pred(keys_ref[kc], kc * tk + key_row), 1, 0)
            return tot + jnp.sum(hit, axis=0, keepdims=True)
        return lax.fori_loop(0, n_proc, body, jnp.zeros((1, tq), I32))

    def count_half(cand, strict=False):
        cand16 = cand.astype(I16)

        def body(kc, part):
            x = half_ref[kc]
            hit = jnp.where((x > cand16) if strict else (x >= cand16), jnp.int16(1), jnp.int16(0))
            tiles = [hit[r:r + BF16_ROWS, :] for r in range(0, tk, BF16_ROWS)]
            while len(tiles) > 1:
                tiles = [a + b for a, b in zip(tiles[::2], tiles[1::2])]
            return part + tiles[0]
        part = lax.fori_loop(0, n_proc, body, jnp.zeros((BF16_ROWS, tq), I16))
        return jnp.sum(part.astype(I32), axis=0, keepdims=True)

    def largest_half(need):
        def bit(i, u):
            cand_u = u | jnp.left_shift(jnp.int32(1), 15 - i)
            return jnp.where(count_half(cand_u - 32768) >= need, cand_u, u)
        return lax.fori_loop(0, 16, bit, jnp.zeros((1, tq), I32)) - 32768

    thr_hi = largest_half(top_k)
    above = count_half(thr_hi, strict=True)

    def low_chunk(kc, _):
        key = keys_ref[kc]
        low = jnp.bitwise_and(key, 0xFFFF) - 32768
        half_ref[kc] = jnp.where(jnp.right_shift(key, 16) == thr_hi, low, -32768).astype(I16)
        return 0

    lax.fori_loop(0, n_proc, low_chunk, 0)
    thr_lo = largest_half(top_k - above)
    cnt_ge = above + count_half(thr_lo)
    thr = jnp.left_shift(thr_hi, 16) + (thr_lo + 32768)

    def tie_cut():
        need = top_k - count(lambda key, pos: key > thr)

        def index_bit(i, x):
            cand = x | jnp.left_shift(jnp.int32(1), idx_bits - 1 - i)
            cnt = count(lambda key, pos: (key == thr) & (pos < cand))
            return jnp.where(cnt < need, cand, x)
        return lax.fori_loop(0, idx_bits, index_bit, jnp.zeros((1, tq), I32))

    last_tie = lax.cond(jnp.max(cnt_ge) > top_k, tie_cut, lambda: jnp.full((1, tq), 2 ** idx_bits, I32))

    def write_chunk(kc, _):
        key = keys_ref[kc]
        pos = kc * tk + key_row
        chosen = (key > thr) | ((key == thr) & (pos <= last_tie))
        rows = pl.ds(pl.multiple_of(kc * tk, tk), tk)
        o_ref[rows, :] = jnp.where(chosen & (pos <= q_pos), 0.0, NEG).astype(o_ref.dtype)
        return 0

    lax.fori_loop(0, n_proc, write_chunk, 0)

    def fill_chunk(kc, _):
        o_ref[pl.ds(pl.multiple_of(kc * tk, tk), tk), :] = jnp.full((tk, tq), NEG, o_ref.dtype)
        return 0

    lax.fori_loop(n_proc, n_kc, fill_chunk, 0)


def _dsa_select(qi_t, widx_t, kk):
    b, seq, _ = kk.shape
    t = BLK
    n_kc = seq // t
    top_k = min(DSA_TOPK_MAX, seq // 4)
    idx_bits = max(1, (seq - 1).bit_length())
    return pl.pallas_call(
        functools.partial(_dsa_select_kernel, top_k=top_k, idx_bits=idx_bits),
        grid=(b, seq // t),
        in_specs=[pl.BlockSpec((None, qi_t.shape[1], t), lambda bi, c: (bi, 0, c)),
                  pl.BlockSpec((None, IDX_HEADS, t), lambda bi, c: (bi, 0, c)),
                  pl.BlockSpec((None, seq, IDX_DIM), lambda bi, c: (bi, 0, 0))],
        out_specs=pl.BlockSpec((None, None, seq, t), lambda bi, c: (bi, c, 0, 0)),
        out_shape=jax.ShapeDtypeStruct((b, seq // t, seq, t), BF16),
        scratch_shapes=[pltpu.VMEM((n_kc, t, t), I32),
                        pltpu.VMEM((n_kc, t, t), I16)],
        compiler_params=_cparams("parallel", "arbitrary"),
        name="dsa_select",
    )(qi_t, widx_t, kk)


def _dsa_attn_kernel(qnt_ref, ckv_ref, vt_ref, mb_ref, bt_ref, wukt_ref, o_ref,
                     qlat_ref, acc_ref, m_ref, l_ref, *, n_heads):
    qt = pl.program_id(1)
    t = BLK
    n_grp = n_heads // DSA_GRP
    row = lax.broadcasted_iota(I32, (LANES, t), 0)
    ones_rows = jnp.ones((BF16_ROWS, t), BF16)

    for hp in range(n_heads // 2):
        qp = qnt_ref[hp * LANES:(hp + 1) * LANES, :]
        for hh in range(2):
            qh = jnp.where((row < HEAD_DIM) if hh == 0 else (row >= HEAD_DIM), qp, jnp.zeros_like(qp))
            h = 2 * hp + hh
            qlat_ref[:, h * t:(h + 1) * t] = _dot(wukt_ref[hp], qh).astype(BF16)

    m_ref[...] = jnp.full(m_ref.shape, NEG, F32)
    l_ref[...] = jnp.zeros_like(l_ref)
    acc_ref[...] = jnp.zeros_like(acc_ref)

    def process_tile(kk, bias_row0):
        rows = pl.ds(pl.multiple_of(kk * t, t), t)
        kt = ckv_ref[rows, :]
        mb = mb_ref[rows, :].astype(F32)
        logits = [_dot(kt, qlat_ref[:, g * DSA_GRP * t:(g + 1) * DSA_GRP * t]) for g in range(n_grp)]
        for g in range(n_grp):
            cols = slice(g * DSA_GRP * t, (g + 1) * DSA_GRP * t)
            parts = []
            for i in range(DSA_GRP):
                s = logits[g][:, i * t:(i + 1) * t] + mb
                if bias_row0 is not None:
                    s = s + bt_ref[g * DSA_GRP + i, bias_row0:bias_row0 + t, :]
                parts.append(s)
            m, l, alpha, pr = _softmax_step(m_ref[:, cols], l_ref[:, cols], jnp.concatenate(parts, axis=1), ones_rows)
            m_ref[:, cols] = m
            l_ref[:, cols] = l
            for i in range(DSA_GRP):
                h = g * DSA_GRP + i
                feats = slice(h * HEAD_DIM, (h + 1) * HEAD_DIM)
                hc = slice(i * t, (i + 1) * t)
                acc_ref[feats, :] = alpha[:, hc] * acc_ref[feats, :] + _dot(vt_ref[kk, feats, :], pr[:, hc])

    process_tile(qt, BLK)

    @pl.when(qt >= 1)
    def _():
        process_tile(qt - 1, 0)

    def far_tile(kk, _):
        process_tile(kk, None)
        return 0

    lax.fori_loop(0, jnp.maximum(qt - 1, 0), far_tile, 0)

    for h in range(n_heads):
        feats = slice(h * HEAD_DIM, (h + 1) * HEAD_DIM)
        acc_ref[feats, :] = acc_ref[feats, :] / l_ref[:, h * t:(h + 1) * t]
    o_ref[...] = acc_ref[...].T.astype(o_ref.dtype)


def _dsa_attention(qn_t, ckv, v_t, mask_t, bt, w_uk_t, n_heads):
    b, seq, _ = ckv.shape
    t = BLK
    d_q = qn_t.shape[1]
    return pl.pallas_call(
        functools.partial(_dsa_attn_kernel, n_heads=n_heads),
        grid=(b, seq // t),
        in_specs=[pl.BlockSpec((None, d_q, t), lambda bi, q: (bi, 0, q)),
                  pl.BlockSpec((None, seq, DSA_LORA), lambda bi, q: (bi, 0, 0)),
                  pl.BlockSpec((None, seq // t, d_q, t), lambda bi, q: (bi, 0, 0, 0)),
                  pl.BlockSpec((None, None, seq, t), lambda bi, q: (bi, q, 0, 0)),
                  _full(bt), _full(w_uk_t)],
        out_specs=pl.BlockSpec((None, t, d_q), lambda bi, q: (bi, q, 0)),
        out_shape=jax.ShapeDtypeStruct((b, seq, d_q), BF16),
        scratch_shapes=[pltpu.VMEM((DSA_LORA, n_heads * t), BF16),
                        pltpu.VMEM((d_q, t), F32),
                        pltpu.VMEM((1, n_heads * t), F32),
                        pltpu.VMEM((1, n_heads * t), F32)],
        compiler_params=_cparams("parallel", "arbitrary"),
        name="dsa_attn",
    )(qn_t, ckv, v_t, mask_t, bt, w_uk_t)


def _moba_layer(x2, batch, seq, g, w_qkv, w_o, bt, n_heads):
    d = x2.shape[1]
    wq_t = (w_qkv[:, :d] * (HEAD_DIM ** -0.5 * LOG2E)).T.astype(BF16)
    wk = w_qkv[:, d:2 * d].astype(BF16)
    wv_t = w_qkv[:, 2 * d:].T.astype(BF16)
    q_t, k, v_t = _moba_proj(x2.reshape(batch, seq, d), g, wq_t, wk, wv_t)
    attn = _moba_attention(q_t, k, v_t, bt, n_heads)
    return _matmul_res(attn.reshape(batch * seq, d), w_o.astype(BF16), x2)


def _dsa_layer(x2, batch, seq, g, w_in, g_q, g_kv, w_uq, w_qi, w_uk, w_uv, w_o, bt, n_heads):
    d = x2.shape[1]
    k_lo = 2 * DSA_LORA
    w_in_row = jnp.concatenate([w_in[:, :k_lo + IDX_DIM], jnp.zeros((d, LANES - IDX_DIM), F32)], axis=1).astype(BF16)
    w_kv_t = w_in[:, DSA_LORA:k_lo].T.astype(BF16)
    w_w_t = jnp.concatenate([w_in[:, k_lo + IDX_DIM:].T, jnp.zeros((BF16_ROWS - IDX_HEADS, d), F32)]).astype(BF16)
    w_uq_t = (w_uq * (HEAD_DIM ** -0.5 * LOG2E)).T.astype(BF16)
    w_uv_t = jnp.transpose(w_uv, (0, 2, 1)).reshape(n_heads * HEAD_DIM, DSA_LORA).astype(BF16)
    ckv, v_t, kk, widx_t, qn_t, qi_t = _dsa_proj(x2.reshape(batch, seq, d), g, w_in_row, w_kv_t, w_w_t,
                                                 g_q, g_kv, w_uq_t, w_qi.T.astype(BF16), w_uv_t)
    mask_t = _dsa_select(qi_t, widx_t, kk)
    w_uk_t = jnp.transpose(w_uk.reshape(n_heads // 2, 2 * HEAD_DIM, DSA_LORA), (0, 2, 1)).astype(BF16)
    attn = _dsa_attention(qn_t, ckv, v_t, mask_t, bt[:, :2 * BLK], w_uk_t, n_heads)
    return _matmul_res(attn.reshape(batch * seq, -1), w_o.astype(BF16), x2)


def kernel(x, rel_bias, ln_attn, ln_mlp, moba_w_qkv, moba_w_o, dsa_w_in, dsa_g_q, dsa_g_kv, dsa_w_uq, dsa_w_qi,
           dsa_w_uk, dsa_w_uv, dsa_w_o, mlp_w_up, mlp_w_down, final_norm):
    batch, seq, d = x.shape
    n_heads = rel_bias.shape[1]
    depth = ln_attn.shape[0]
    bt = _bias_tiles(rel_bias)
    x2 = x.reshape(batch * seq, d)
    for i in range(depth):
        j = i // 2
        if i % 2 == 0:
            x2 = _moba_layer(x2, batch, seq, ln_attn[i], moba_w_qkv[j], moba_w_o[j], bt, n_heads)
        else:
            x2 = _dsa_layer(x2, batch, seq, ln_attn[i], dsa_w_in[j], dsa_g_q[j], dsa_g_kv[j], dsa_w_uq[j],
                            dsa_w_qi[j], dsa_w_uk[j], dsa_w_uv[j], dsa_w_o[j], bt, n_heads)
        x2 = _mlp(x2, ln_mlp[i], mlp_w_up[i].astype(BF16), mlp_w_down[i].astype(BF16))
    return _final_norm(x2, final_norm).reshape(batch, seq, d)
```

```python
import functools
import math

import numpy as np
import jax
import jax.numpy as jnp
from jax import lax
from jax.experimental import pallas as pl
from jax.experimental.pallas import tpu as pltpu

F32 = jnp.float32
BF16 = jnp.bfloat16
I32 = jnp.int32
I16 = jnp.int16

EPS = 1e-6
NEG = -1e30
INT_MIN = -(2 ** 31)
LOG2E = 1.0 / math.log(2.0)

NUM_BUCKETS = 32
MAX_DISTANCE = 128
HEAD_DIM = 64
LANES = 128
BF16_ROWS = 16
BLK = 256
MOBA_TOPK = 3
MOBA_GRP = 8
DSA_TOPK_MAX = 256
DSA_LORA = 256
DSA_GRP = 4
IDX_HEADS = 8
IDX_DIM = 64
VMEM_LIMIT = 56 * 1024 * 1024


def _dot(a, b):
    return jnp.dot(a, b, preferred_element_type=F32)


def _dot_nt(a, b):
    return lax.dot_general(a, b, (((1,), (1,)), ((), ())), preferred_element_type=F32)


def _rms(x, g):
    return x * lax.rsqrt(jnp.mean(x * x, axis=-1, keepdims=True) + EPS) * g


def _cparams(*sem):
    return pltpu.CompilerParams(dimension_semantics=sem, vmem_limit_bytes=VMEM_LIMIT)


def _full(a):
    return pl.BlockSpec(a.shape, lambda *_: (0,) * a.ndim, pipeline_mode=pl.Buffered(1))


def _bucket_map_t():
    jj = np.arange(2 * BLK)[:, None]
    i = np.arange(BLK)[None, :]
    d = i - jj + BLK
    n = np.maximum(d, 0)
    exact = NUM_BUCKETS // 2
    nf = np.maximum(n, 1).astype(np.float32)
    large = exact + (np.log(nf / exact) / math.log(MAX_DISTANCE / exact) * (NUM_BUCKETS - exact)).astype(np.int32)
    large = np.minimum(large, NUM_BUCKETS - 1)
    bucket = np.where(n < exact, n, large)
    bmap = np.where(d >= 0, bucket, -1).astype(np.int32)
    return np.concatenate([bmap, bmap[:BLK]], axis=0)


def _bias_tile_kernel(tab_ref, bmap_ref, o_ref):
    h = pl.program_id(0)
    bm = bmap_ref[...]
    base = tab_ref[NUM_BUCKETS - 1, h]
    acc = jnp.full(bm.shape, NEG, F32)
    for b in range(NUM_BUCKETS):
        acc = jnp.where(bm == b, (tab_ref[b, h] - base) * LOG2E, acc)
    o_ref[...] = acc


def _bias_tiles(rel_bias):
    n_heads = rel_bias.shape[1]
    bmap = jnp.asarray(_bucket_map_t())
    return pl.pallas_call(
        _bias_tile_kernel,
        grid=(n_heads,),
        in_specs=[pl.BlockSpec(memory_space=pltpu.SMEM),
                  pl.BlockSpec((3 * BLK, BLK), lambda h: (0, 0))],
        out_specs=pl.BlockSpec((None, 3 * BLK, BLK), lambda h: (h, 0, 0)),
        out_shape=jax.ShapeDtypeStruct((n_heads, 3 * BLK, BLK), F32),
        name="bias_tiles",
    )(rel_bias, bmap)


def _block_tail_kernel(x_ref, a_ref, wo_ref, g_ref, wu_ref, wd_ref, *rest, final):
    if final:
        gf_ref, o_ref, x1_ref, xn_ref, acc_ref = rest
    else:
        o_ref, x1_ref, xn_ref, acc_ref = rest
    f = pl.program_id(1)

    @pl.when(f == 0)
    def _():
        x1 = x_ref[...] + _dot(a_ref[...], wo_ref[...])
        x1_ref[...] = x1
        xn_ref[...] = _rms(x1, g_ref[...]).astype(BF16)
        acc_ref[...] = jnp.zeros_like(acc_ref)

    h = jnp.square(jnp.maximum(_dot(xn_ref[...], wu_ref[...]), 0.0)).astype(BF16)
    acc_ref[...] += _dot(h, wd_ref[...])

    @pl.when(f == pl.num_programs(1) - 1)
    def _():
        out = x1_ref[...] + acc_ref[...]
        o_ref[...] = _rms(out, gf_ref[...]) if final else out


def _block_tail(x2, attn, w_o, g, w_up, w_down, g_final=None, tm=512, tf=1024):
    n, d = x2.shape
    k = attn.shape[1]
    d_ff = w_up.shape[1]
    final = g_final is not None
    row = lambda w: pl.BlockSpec((tm, w), lambda i, f: (i, 0))
    vec = pl.BlockSpec((1, d), lambda i, f: (0, 0))
    in_specs = [row(d), row(k), pl.BlockSpec((k, d), lambda i, f: (0, 0), pipeline_mode=pl.Buffered(1)), vec,
                pl.BlockSpec((d, tf), lambda i, f: (0, f)), pl.BlockSpec((tf, d), lambda i, f: (f, 0))]
    args = [x2, attn, w_o, g.reshape(1, d), w_up, w_down]
    if final:
        in_specs.append(vec)
        args.append(g_final.reshape(1, d))
    return pl.pallas_call(
        functools.partial(_block_tail_kernel, final=final),
        grid=(n // tm, d_ff // tf),
        in_specs=in_specs,
        out_specs=row(d),
        out_shape=jax.ShapeDtypeStruct((n, d), F32),
        scratch_shapes=[pltpu.VMEM((tm, d), F32), pltpu.VMEM((tm, d), BF16), pltpu.VMEM((tm, d), F32)],
        compiler_params=_cparams("parallel", "arbitrary"),
        name="block_tail",
    )(*args)


def _softmax_step(m_old, s):
    m_new = jnp.maximum(m_old, jnp.max(s, axis=0, keepdims=True))
    return m_new, jnp.exp2(m_old - m_new), jnp.exp2(s - m_new).astype(BF16)


def _moba_proj_kernel(x_ref, g_ref, wqt_ref, wk_ref, wvt_ref, qt_ref, k_ref, vt_ref):
    xn = _rms(x_ref[...], g_ref[...]).astype(BF16)
    qt_ref[...] = _dot_nt(wqt_ref[...], xn).astype(BF16)
    k_ref[...] = _dot(xn, wk_ref[...]).astype(BF16)
    vt_ref[...] = _dot_nt(wvt_ref[...], xn).astype(BF16)


def _moba_proj(x3, g, wq_t, wk, wv_t):
    b, seq, d = x3.shape
    g2 = g.reshape(1, d)
    return pl.pallas_call(
        _moba_proj_kernel,
        grid=(b, seq // BLK),
        in_specs=[pl.BlockSpec((None, BLK, d), lambda bi, i: (bi, i, 0)),
                  _full(g2), _full(wq_t), _full(wk), _full(wv_t)],
        out_specs=[pl.BlockSpec((None, d, BLK), lambda bi, i: (bi, 0, i)),
                   pl.BlockSpec((None, BLK, d), lambda bi, i: (bi, i, 0)),
                   pl.BlockSpec((None, None, d, BLK), lambda bi, i: (bi, i, 0, 0))],
        out_shape=[jax.ShapeDtypeStruct((b, d, seq), BF16),
                   jax.ShapeDtypeStruct((b, seq, d), BF16),
                   jax.ShapeDtypeStruct((b, seq // BLK, d, BLK), BF16)],
        compiler_params=_cparams("parallel", "parallel"),
        name="moba_proj",
    )(x3, g2, wq_t, wk, wv_t)


def _moba_kernel(qt_ref, k_ref, vt_ref, bt_ref, o_ref, kaug_ref, kmean_ref, qfar_ref, m_ref, l_ref, acc_ref,
                 *, n_blk, k_sel, sel_rows):
    j = pl.program_id(2)
    seq = k_ref.shape[0]
    n_pairs = MOBA_GRP // 2
    t = BLK
    row = lax.broadcasted_iota(I32, (LANES, t), 0)
    blk_id = lax.broadcasted_iota(I32, (sel_rows, t), 0)
    ones_2 = jnp.ones((BF16_ROWS, 2 * t), BF16)

    @pl.when(j == 0)
    def _():
        krow = lax.broadcasted_iota(I32, (seq, LANES), 0)
        kcol = lax.broadcasted_iota(I32, (seq, LANES), 1)
        onehot = jnp.where(krow // BLK == kcol, 1.0, 0.0).astype(BF16)
        for p in range(n_pairs):
            kaug_ref[p, :, 0:LANES] = k_ref[:, p * LANES:(p + 1) * LANES]
            kaug_ref[p, :, LANES:2 * LANES] = onehot
            kmean_ref[p] = jnp.zeros((LANES, LANES), F32)
            for n in range(n_blk):
                kmean_ref[p, n:n + 1, :] = jnp.mean(
                    k_ref[n * BLK:(n + 1) * BLK, p * LANES:(p + 1) * LANES].astype(F32), axis=0, keepdims=True)

    near_blk = jnp.maximum(j - 1, 0)
    near0 = pl.multiple_of(near_blk * BLK, BLK)
    bias0 = pl.multiple_of(jnp.where(j == 0, BLK, 0), BLK)
    pad_rows = jnp.zeros((LANES - sel_rows, t), BF16)

    def update(p, s, v_t):
        m, alpha, pr = _softmax_step(m_ref[p], s)
        m_ref[p] = m
        pv = _dot(jnp.concatenate([v_t, ones_2], axis=0), pr)
        acc_ref[p] = alpha * acc_ref[p] + pv[0:LANES, :]
        l_ref[p] = alpha * l_ref[p] + pv[LANES:LANES + 1, :]

    q_near = []
    for p in range(n_pairs):
        qp = qt_ref[p * LANES:(p + 1) * LANES, :]
        near, far = [], []
        for hh in range(2):
            qh = jnp.where((row < HEAD_DIM) if hh == 0 else (row >= HEAD_DIM), qp, jnp.zeros_like(qp))
            gate = _dot(kmean_ref[p], qh.astype(F32))[0:sel_rows, :]
            g = jnp.where(blk_id < j, gate, NEG)
            g = jnp.where(blk_id < n_blk, g, -jnp.inf)
            sel_bias = jnp.full((sel_rows, t), NEG, F32)
            for it in range(k_sel):
                mx = jnp.max(g, axis=0, keepdims=True)
                first = jnp.min(jnp.where(g == mx, blk_id, LANES), axis=0, keepdims=True)
                pick = blk_id == first
                sel_bias = jnp.where(pick, jnp.where(it < j, 0.0, NEG), sel_bias)
                g = jnp.where(pick, -jnp.inf, g)
            near_bias = jnp.where(blk_id == j, 0.0, sel_bias)
            far_bias = jnp.where(blk_id < j - 1, sel_bias, NEG)
            near.append(jnp.concatenate([qh, near_bias.astype(BF16), pad_rows], axis=0))
            far.append(jnp.concatenate([qh, far_bias.astype(BF16), pad_rows], axis=0))
        q_near.append(jnp.concatenate(near, axis=1))
        qfar_ref[p] = jnp.concatenate(far, axis=1)
        m_ref[p] = jnp.full((1, 2 * t), NEG, F32)
        l_ref[p] = jnp.zeros((1, 2 * t), F32)
        acc_ref[p] = jnp.zeros((LANES, 2 * t), F32)

    logits = [_dot(kaug_ref[p, pl.ds(near0, 2 * BLK), :], q_near[p]) for p in range(n_pairs)]
    for p in range(n_pairs):
        feats = slice(p * LANES, (p + 1) * LANES)
        bias = jnp.concatenate([bt_ref[2 * p, pl.ds(bias0, 2 * BLK), :], bt_ref[2 * p + 1, pl.ds(bias0, 2 * BLK), :]],
                               axis=1)
        v_t = jnp.concatenate([vt_ref[near_blk, feats, :], vt_ref[near_blk + 1, feats, :]], axis=1)
        update(p, logits[p] + bias, v_t)

    def far_pair(pi, _):
        n0 = pl.multiple_of(pi * 2 * BLK, 2 * BLK)
        logits = [_dot(kaug_ref[p, pl.ds(n0, 2 * BLK), :], qfar_ref[p]) for p in range(n_pairs)]
        for p in range(n_pairs):
            feats = slice(p * LANES, (p + 1) * LANES)
            v_t = jnp.concatenate([vt_ref[2 * pi, feats, :], vt_ref[2 * pi + 1, feats, :]], axis=1)
            update(p, logits[p], v_t)
        return 0

    lax.fori_loop(0, jnp.maximum(j, 1) // 2, far_pair, 0)

    for p in range(n_pairs):
        acc = acc_ref[p] / l_ref[p]
        out_t = jnp.where(row < HEAD_DIM, acc[:, 0:t], acc[:, t:2 * t])
        o_ref[:, p * LANES:(p + 1) * LANES] = out_t.T.astype(o_ref.dtype)


def _moba_attention(q_t, k, v_t, bt, n_heads):
    b, seq, d = k.shape
    n_blk = seq // BLK
    n_grp = n_heads // MOBA_GRP
    n_pairs = MOBA_GRP // 2
    w = MOBA_GRP * HEAD_DIM
    k_sel = max(1, min(MOBA_TOPK, n_blk - 1))
    sel_rows = BF16_ROWS * pl.cdiv(n_blk, BF16_ROWS)
    assert sel_rows <= LANES and n_blk % 2 == 0
    return pl.pallas_call(
        functools.partial(_moba_kernel, n_blk=n_blk, k_sel=k_sel, sel_rows=sel_rows),
        grid=(b, n_grp, n_blk),
        in_specs=[pl.BlockSpec((None, w, BLK), lambda bi, g, j: (bi, g, j)),
                  pl.BlockSpec((None, seq, w), lambda bi, g, j: (bi, 0, g)),
                  pl.BlockSpec((None, n_blk, w, BLK), lambda bi, g, j: (bi, 0, g, 0)),
                  pl.BlockSpec((MOBA_GRP, 3 * BLK, BLK), lambda bi, g, j: (g, 0, 0))],
        out_specs=pl.BlockSpec((None, BLK, w), lambda bi, g, j: (bi, j, g)),
        out_shape=jax.ShapeDtypeStruct((b, seq, d), BF16),
        scratch_shapes=[pltpu.VMEM((n_pairs, seq, 2 * LANES), BF16),
                        pltpu.VMEM((n_pairs, LANES, LANES), F32),
                        pltpu.VMEM((n_pairs, 2 * LANES, 2 * BLK), BF16),
                        pltpu.VMEM((n_pairs, 1, 2 * BLK), F32),
                        pltpu.VMEM((n_pairs, 1, 2 * BLK), F32),
                        pltpu.VMEM((n_pairs, LANES, 2 * BLK), F32)],
        compiler_params=_cparams("parallel", "parallel", "arbitrary"),
        name="moba_attn",
    )(q_t, k, v_t, bt)


def _dsa_proj_kernel(x_ref, g_ref, win_ref, wkvt_ref, wwt_ref, gq_ref, gkv_ref, gkvt_ref, wuqt_ref, wqit_ref,
                     wuvt_ref, ckv_ref, vt_ref, kk_ref, widx_ref, qnt_ref, qit_ref, *, idx_scale):
    hn = _rms(x_ref[...], g_ref[...]).astype(BF16)
    proj = _dot(hn, win_ref[...])
    c_q = _rms(proj[:, 0:DSA_LORA], gq_ref[...]).astype(BF16)
    ckv_ref[...] = _rms(proj[:, DSA_LORA:2 * DSA_LORA], gkv_ref[...]).astype(BF16)
    kk_ref[...] = proj[:, 2 * DSA_LORA:2 * DSA_LORA + IDX_DIM].astype(BF16)
    ct = _dot_nt(wkvt_ref[...], hn)
    scale = lax.rsqrt(jnp.mean(ct * ct, axis=0, keepdims=True) + EPS)
    gt = jnp.concatenate([gkvt_ref[...]] * (ct.shape[1] // LANES), axis=1)
    vt_ref[...] = _dot(wuvt_ref[...], (ct * scale * gt).astype(BF16)).astype(BF16)
    widx_ref[...] = _dot_nt(wwt_ref[...], hn)[0:IDX_HEADS, :] * idx_scale
    qnt_ref[...] = _dot_nt(wuqt_ref[...], c_q).astype(BF16)
    qit_ref[...] = _dot_nt(wqit_ref[...], c_q).astype(BF16)


def _dsa_proj(x3, g, w_in_row, w_kv_t, w_w_t, g_q, g_kv, w_uq_t, w_qi_t, w_uv_t):
    b, seq, d = x3.shape
    tm = BLK
    g2, gq2, gkv2 = g.reshape(1, d), g_q.reshape(1, -1), g_kv.reshape(1, -1)
    gkv_t = jnp.broadcast_to(g_kv.reshape(-1, 1), (DSA_LORA, LANES))
    n_q, n_i, n_v = w_uq_t.shape[0], w_qi_t.shape[0], w_uv_t.shape[0]
    consts = (g2, w_in_row, w_kv_t, w_w_t, gq2, gkv2, gkv_t, w_uq_t, w_qi_t, w_uv_t)
    idx_scale = IDX_HEADS ** -0.5 * IDX_DIM ** -0.5
    return pl.pallas_call(
        functools.partial(_dsa_proj_kernel, idx_scale=idx_scale),
        grid=(b, seq // tm),
        in_specs=[pl.BlockSpec((None, tm, d), lambda bi, i: (bi, i, 0))] + [_full(a) for a in consts],
        out_specs=[pl.BlockSpec((None, tm, DSA_LORA), lambda bi, i: (bi, i, 0)),
                   pl.BlockSpec((None, None, n_v, tm), lambda bi, i: (bi, i, 0, 0)),
                   pl.BlockSpec((None, tm, IDX_DIM), lambda bi, i: (bi, i, 0)),
                   pl.BlockSpec((None, IDX_HEADS, tm), lambda bi, i: (bi, 0, i)),
                   pl.BlockSpec((None, n_q, tm), lambda bi, i: (bi, 0, i)),
                   pl.BlockSpec((None, n_i, tm), lambda bi, i: (bi, 0, i))],
        out_shape=[jax.ShapeDtypeStruct((b, seq, DSA_LORA), BF16),
                   jax.ShapeDtypeStruct((b, seq // tm, n_v, tm), BF16),
                   jax.ShapeDtypeStruct((b, seq, IDX_DIM), BF16),
                   jax.ShapeDtypeStruct((b, IDX_HEADS, seq), F32),
                   jax.ShapeDtypeStruct((b, n_q, seq), BF16),
                   jax.ShapeDtypeStruct((b, n_i, seq), BF16)],
        compiler_params=_cparams("parallel", "parallel"),
        name="dsa_proj",
    )(x3, *consts)


def _dsa_select_kernel(qit_ref, w_ref, kk_ref, o_ref, keys_ref, half_ref, *, top_k, idx_bits):
    c = pl.program_id(1)
    n_kc, tk, tq = keys_ref.shape
    n_proc = c + 1
    key_row = lax.broadcasted_iota(I32, (tk, tq), 0)
    q_pos = c * tq + lax.broadcasted_iota(I32, (tk, tq), 1)
    w = w_ref[...]

    def score_chunk(kc, _):
        kt = kk_ref[pl.ds(pl.multiple_of(kc * tk, tk), tk), :]
        acc = jnp.zeros((tk, tq), F32)
        for h in range(IDX_HEADS):
            x = _dot(kt, qit_ref[h * IDX_DIM:(h + 1) * IDX_DIM, :])
            acc = acc + w[h:h + 1, :] * jnp.maximum(x, 0.0)
        sc = jnp.where(kc * tk + key_row <= q_pos, acc, NEG)
        bits = pltpu.bitcast(sc, I32)
        key = jnp.where(bits < 0, bits ^ 0x7FFFFFFF, bits)
        keys_ref[kc] = key
        half_ref[kc] = jnp.right_shift(key, 16).astype(I16)
        return 0

    lax.fori_loop(0, n_proc, score_chunk, 0)

    def count(pred):
        def body(kc, tot):
            hit = jnp.where(pred(keys_ref[kc], kc * tk + key_row), 1, 0)
            return tot + jnp.sum(hit, axis=0, keepdims=True)
        return lax.fori_loop(0, n_proc, body, jnp.zeros((1, tq), I32))

    def count_half(cand, strict=False):
        cand16 = cand.astype(I16)

        def body(kc, part):
            x = half_ref[kc]
            hit = jnp.where((x > cand16) if strict else (x >= cand16), jnp.int16(1), jnp.int16(0))
            tiles = [hit[r:r + BF16_ROWS, :] for r in range(0, tk, BF16_ROWS)]
            while len(tiles) > 1:
                tiles = [a + b for a, b in zip(tiles[::2], tiles[1::2])]
            return part + tiles[0]
        part = lax.fori_loop(0, n_proc, body, jnp.zeros((BF16_ROWS, tq), I16))
        return jnp.sum(part.astype(I32), axis=0, keepdims=True)

    def largest_half(need):
        def bit(i, u):
            cand_u = u | jnp.left_shift(jnp.int32(1), 15 - i)
            return jnp.where(count_half(cand_u - 32768) >= need, cand_u, u)
        return lax.fori_loop(0, 16, bit, jnp.zeros((1, tq), I32)) - 32768

    thr_hi = largest_half(top_k)
    above = count_half(thr_hi, strict=True)

    def low_chunk(kc, _):
        key = keys_ref[kc]
        low = jnp.bitwise_and(key, 0xFFFF) - 32768
        half_ref[kc] = jnp.where(jnp.right_shift(key, 16) == thr_hi, low, -32768).astype(I16)
        return 0

    lax.fori_loop(0, n_proc, low_chunk, 0)
    thr_lo = largest_half(top_k - above)
    cnt_ge = above + count_half(thr_lo)
    thr = jnp.left_shift(thr_hi, 16) + (thr_lo + 32768)

    def tie_cut():
        need = top_k - count(lambda key, pos: key > thr)

        def index_bit(i, x):
            cand = x | jnp.left_shift(jnp.int32(1), idx_bits - 1 - i)
            cnt = count(lambda key, pos: (key == thr) & (pos < cand))
            return jnp.where(cnt < need, cand, x)
        return lax.fori_loop(0, idx_bits, index_bit, jnp.zeros((1, tq), I32))

    last_tie = lax.cond(jnp.max(cnt_ge) > top_k, tie_cut, lambda: jnp.full((1, tq), 2 ** idx_bits, I32))

    def write_chunk(kc, _):
        key = keys_ref[kc]
        pos = kc * tk + key_row
        chosen = (key > thr) | ((key == thr) & (pos <= last_tie))
        rows = pl.ds(pl.multiple_of(kc * tk, tk), tk)
        o_ref[rows, :] = jnp.where(chosen & (pos <= q_pos), 0.0, NEG).astype(o_ref.dtype)
        return 0

    lax.fori_loop(0, n_proc, write_chunk, 0)

    def fill_chunk(kc, _):
        o_ref[pl.ds(pl.multiple_of(kc * tk, tk), tk), :] = jnp.full((tk, tq), NEG, o_ref.dtype)
        return 0

    lax.fori_loop(n_proc, n_kc, fill_chunk, 0)


def _dsa_select(qi_t, widx_t, kk):
    b, seq, _ = kk.shape
    t = BLK
    n_kc = seq // t
    top_k = min(DSA_TOPK_MAX, seq // 4)
    idx_bits = max(1, (seq - 1).bit_length())
    return pl.pallas_call(
        functools.partial(_dsa_select_kernel, top_k=top_k, idx_bits=idx_bits),
        grid=(b, seq // t),
        in_specs=[pl.BlockSpec((None, qi_t.shape[1], t), lambda bi, c: (bi, 0, c)),
                  pl.BlockSpec((None, IDX_HEADS, t), lambda bi, c: (bi, 0, c)),
                  pl.BlockSpec((None, seq, IDX_DIM), lambda bi, c: (bi, 0, 0))],
        out_specs=pl.BlockSpec((None, None, seq, t), lambda bi, c: (bi, c, 0, 0)),
        out_shape=jax.ShapeDtypeStruct((b, seq // t, seq, t), BF16),
        scratch_shapes=[pltpu.VMEM((n_kc, t, t), I32),
                        pltpu.VMEM((n_kc, t, t), I16)],
        compiler_params=_cparams("parallel", "arbitrary"),
        name="dsa_select",
    )(qi_t, widx_t, kk)


def _dsa_attn_kernel(qnt_ref, ckv_ref, vt_ref, mb_ref, bt_ref, wukt_ref, o_ref,
                     qlat_ref, acc_ref, m_ref, l_ref, *, n_heads):
    qt = pl.program_id(1)
    t = BLK
    n_grp = n_heads // DSA_GRP
    row = lax.broadcasted_iota(I32, (LANES, t), 0)
    ones_rows = jnp.ones((BF16_ROWS, t), BF16)

    for hp in range(n_heads // 2):
        qp = qnt_ref[hp * LANES:(hp + 1) * LANES, :]
        for hh in range(2):
            qh = jnp.where((row < HEAD_DIM) if hh == 0 else (row >= HEAD_DIM), qp, jnp.zeros_like(qp))
            h = 2 * hp + hh
            qlat_ref[:, h * t:(h + 1) * t] = _dot(wukt_ref[hp], qh).astype(BF16)

    m_ref[...] = jnp.full(m_ref.shape, NEG, F32)
    l_ref[...] = jnp.zeros_like(l_ref)
    acc_ref[...] = jnp.zeros_like(acc_ref)

    def process_tile(kk, bias_row0):
        rows = pl.ds(pl.multiple_of(kk * t, t), t)
        kt = ckv_ref[rows, :]
        mb = mb_ref[rows, :].astype(F32)
        logits = [_dot(kt, qlat_ref[:, g * DSA_GRP * t:(g + 1) * DSA_GRP * t]) for g in range(n_grp)]
        for g in range(n_grp):
            cols = slice(g * DSA_GRP * t, (g + 1) * DSA_GRP * t)
            parts = []
            for i in range(DSA_GRP):
                s = logits[g][:, i * t:(i + 1) * t] + mb
                if bias_row0 is not None:
                    s = s + bt_ref[g * DSA_GRP + i, bias_row0:bias_row0 + t, :]
                parts.append(s)
            m, alpha, pr = _softmax_step(m_ref[:, cols], jnp.concatenate(parts, axis=1))
            m_ref[:, cols] = m
            for i in range(DSA_GRP):
                h = g * DSA_GRP + i
                feats = slice(h * HEAD_DIM, (h + 1) * HEAD_DIM)
                hc = slice(i * t, (i + 1) * t)
                lc = slice(h * t, (h + 1) * t)
                pv = _dot(jnp.concatenate([vt_ref[kk, feats, :], ones_rows], axis=0), pr[:, hc])
                acc_ref[feats, :] = alpha[:, hc] * acc_ref[feats, :] + pv[0:HEAD_DIM, :]
                l_ref[:, lc] = alpha[:, hc] * l_ref[:, lc] + pv[HEAD_DIM:HEAD_DIM + 1, :]

    process_tile(qt, BLK)

    @pl.when(qt >= 1)
    def _():
        process_tile(qt - 1, 0)

    def far_tile(kk, _):
        process_tile(kk, None)
        return 0

    lax.fori_loop(0, jnp.maximum(qt - 1, 0), far_tile, 0)

    for h in range(n_heads):
        feats = slice(h * HEAD_DIM, (h + 1) * HEAD_DIM)
        acc_ref[feats, :] = acc_ref[feats, :] / l_ref[:, h * t:(h + 1) * t]
    o_ref[...] = acc_ref[...].T.astype(o_ref.dtype)


def _dsa_attention(qn_t, ckv, v_t, mask_t, bt, w_uk_t, n_heads):
    b, seq, _ = ckv.shape
    t = BLK
    d_q = qn_t.shape[1]
    return pl.pallas_call(
        functools.partial(_dsa_attn_kernel, n_heads=n_heads),
        grid=(b, seq // t),
        in_specs=[pl.BlockSpec((None, d_q, t), lambda bi, q: (bi, 0, q)),
                  pl.BlockSpec((None, seq, DSA_LORA), lambda bi, q: (bi, 0, 0)),
                  pl.BlockSpec((None, seq // t, d_q, t), lambda bi, q: (bi, 0, 0, 0)),
                  pl.BlockSpec((None, None, seq, t), lambda bi, q: (bi, q, 0, 0)),
                  _full(bt), _full(w_uk_t)],
        out_specs=pl.BlockSpec((None, t, d_q), lambda bi, q: (bi, q, 0)),
        out_shape=jax.ShapeDtypeStruct((b, seq, d_q), BF16),
        scratch_shapes=[pltpu.VMEM((DSA_LORA, n_heads * t), BF16),
                        pltpu.VMEM((d_q, t), F32),
                        pltpu.VMEM((1, n_heads * t), F32),
                        pltpu.VMEM((1, n_heads * t), F32)],
        compiler_params=_cparams("parallel", "arbitrary"),
        name="dsa_attn",
    )(qn_t, ckv, v_t, mask_t, bt, w_uk_t)


def _moba_layer(x2, batch, seq, g, w_qkv, bt, n_heads):
    d = x2.shape[1]
    wq_t = (w_qkv[:, :d] * (HEAD_DIM ** -0.5 * LOG2E)).T.astype(BF16)
    wk = w_qkv[:, d:2 * d].astype(BF16)
    wv_t = w_qkv[:, 2 * d:].T.astype(BF16)
    q_t, k, v_t = _moba_proj(x2.reshape(batch, seq, d), g, wq_t, wk, wv_t)
    attn = _moba_attention(q_t, k, v_t, bt, n_heads)
    return attn.reshape(batch * seq, d)


def _dsa_layer(x2, batch, seq, g, w_in, g_q, g_kv, w_uq, w_qi, w_uk, w_uv, bt, n_heads):
    d = x2.shape[1]
    k_lo = 2 * DSA_LORA
    w_in_row = jnp.concatenate([w_in[:, :k_lo + IDX_DIM], jnp.zeros((d, LANES - IDX_DIM), F32)], axis=1).astype(BF16)
    w_kv_t = w_in[:, DSA_LORA:k_lo].T.astype(BF16)
    w_w_t = jnp.concatenate([w_in[:, k_lo + IDX_DIM:].T, jnp.zeros((BF16_ROWS - IDX_HEADS, d), F32)]).astype(BF16)
    w_uq_t = (w_uq * (HEAD_DIM ** -0.5 * LOG2E)).T.astype(BF16)
    w_uv_t = jnp.transpose(w_uv, (0, 2, 1)).reshape(n_heads * HEAD_DIM, DSA_LORA).astype(BF16)
    ckv, v_t, kk, widx_t, qn_t, qi_t = _dsa_proj(x2.reshape(batch, seq, d), g, w_in_row, w_kv_t, w_w_t,
                                                 g_q, g_kv, w_uq_t, w_qi.T.astype(BF16), w_uv_t)
    mask_t = _dsa_select(qi_t, widx_t, kk)
    w_uk_t = jnp.transpose(w_uk.reshape(n_heads // 2, 2 * HEAD_DIM, DSA_LORA), (0, 2, 1)).astype(BF16)
    attn = _dsa_attention(qn_t, ckv, v_t, mask_t, bt[:, :2 * BLK], w_uk_t, n_heads)
    return attn.reshape(batch * seq, -1)


def kernel(x, rel_bias, ln_attn, ln_mlp, moba_w_qkv, moba_w_o, dsa_w_in, dsa_g_q, dsa_g_kv, dsa_w_uq, dsa_w_qi,
           dsa_w_uk, dsa_w_uv, dsa_w_o, mlp_w_up, mlp_w_down, final_norm):
    batch, seq, d = x.shape
    n_heads = rel_bias.shape[1]
    depth = ln_attn.shape[0]
    bt = _bias_tiles(rel_bias)
    x2 = x.reshape(batch * seq, d)
    for i in range(depth):
        j = i // 2
        if i % 2 == 0:
            attn, w_o = _moba_layer(x2, batch, seq, ln_attn[i], moba_w_qkv[j], bt, n_heads), moba_w_o[j]
        else:
            attn, w_o = _dsa_layer(x2, batch, seq, ln_attn[i], dsa_w_in[j], dsa_g_q[j], dsa_g_kv[j], dsa_w_uq[j],
                                   dsa_w_qi[j], dsa_w_uk[j], dsa_w_uv[j], bt, n_heads), dsa_w_o[j]
        x2 = _block_tail(x2, attn, w_o.astype(BF16), ln_mlp[i], mlp_w_up[i].astype(BF16), mlp_w_down[i].astype(BF16),
                         g_final=final_norm if i == depth - 1 else None)
    return x2.reshape(batch, seq, d)
```

```python
import functools
import math

import numpy as np
import jax
import jax.numpy as jnp
from jax import lax
from jax.experimental import pallas as pl
from jax.experimental.pallas import tpu as pltpu

F32 = jnp.float32
BF16 = jnp.bfloat16
I32 = jnp.int32
I16 = jnp.int16

EPS = 1e-6
NEG = -1e30
INT_MIN = -(2 ** 31)
LOG2E = 1.0 / math.log(2.0)

NUM_BUCKETS = 32
MAX_DISTANCE = 128
HEAD_DIM = 64
LANES = 128
BF16_ROWS = 16
BLK = 256
MOBA_TOPK = 3
MOBA_GRP = 8
DSA_TOPK_MAX = 256
DSA_LORA = 256
DSA_GRP = 4
IDX_HEADS = 8
IDX_DIM = 64
VMEM_LIMIT = 56 * 1024 * 1024


def _dot(a, b):
    return jnp.dot(a, b, preferred_element_type=F32)


def _dot_nt(a, b):
    return lax.dot_general(a, b, (((1,), (1,)), ((), ())), preferred_element_type=F32)


def _rms(x, g):
    return x * lax.rsqrt(jnp.mean(x * x, axis=-1, keepdims=True) + EPS) * g


def _cparams(*sem):
    return pltpu.CompilerParams(dimension_semantics=sem, vmem_limit_bytes=VMEM_LIMIT)


def _full(a):
    return pl.BlockSpec(a.shape, lambda *_: (0,) * a.ndim, pipeline_mode=pl.Buffered(1))


def _bucket_map_t():
    jj = np.arange(2 * BLK)[:, None]
    i = np.arange(BLK)[None, :]
    d = i - jj + BLK
    n = np.maximum(d, 0)
    exact = NUM_BUCKETS // 2
    nf = np.maximum(n, 1).astype(np.float32)
    large = exact + (np.log(nf / exact) / math.log(MAX_DISTANCE / exact) * (NUM_BUCKETS - exact)).astype(np.int32)
    large = np.minimum(large, NUM_BUCKETS - 1)
    bucket = np.where(n < exact, n, large)
    bmap = np.where(d >= 0, bucket, -1).astype(np.int32)
    return np.concatenate([bmap, bmap[:BLK]], axis=0)


def _bias_tile_kernel(tab_ref, bmap_ref, o_ref):
    h = pl.program_id(0)
    bm = bmap_ref[...]
    base = tab_ref[NUM_BUCKETS - 1, h]
    acc = jnp.full(bm.shape, NEG, F32)
    for b in range(NUM_BUCKETS):
        acc = jnp.where(bm == b, (tab_ref[b, h] - base) * LOG2E, acc)
    o_ref[...] = acc


def _bias_tiles(rel_bias):
    n_heads = rel_bias.shape[1]
    bmap = jnp.asarray(_bucket_map_t())
    return pl.pallas_call(
        _bias_tile_kernel,
        grid=(n_heads,),
        in_specs=[pl.BlockSpec(memory_space=pltpu.SMEM),
                  pl.BlockSpec((3 * BLK, BLK), lambda h: (0, 0))],
        out_specs=pl.BlockSpec((None, 3 * BLK, BLK), lambda h: (h, 0, 0)),
        out_shape=jax.ShapeDtypeStruct((n_heads, 3 * BLK, BLK), F32),
        name="bias_tiles",
    )(rel_bias, bmap)


def _block_tail_kernel(x_ref, a_ref, wo_ref, g_ref, wu_ref, wd_ref, *rest, final):
    if final:
        gf_ref, o_ref, x1_ref, xn_ref, acc_ref = rest
    else:
        o_ref, x1_ref, xn_ref, acc_ref = rest
    f = pl.program_id(1)

    @pl.when(f == 0)
    def _():
        x1 = x_ref[...] + _dot(a_ref[...], wo_ref[...])
        x1_ref[...] = x1
        xn_ref[...] = _rms(x1, g_ref[...]).astype(BF16)
        acc_ref[...] = jnp.zeros_like(acc_ref)

    h = jnp.square(jnp.maximum(_dot(xn_ref[...], wu_ref[...]), 0.0)).astype(BF16)
    acc_ref[...] += _dot(h, wd_ref[...])

    @pl.when(f == pl.num_programs(1) - 1)
    def _():
        out = x1_ref[...] + acc_ref[...]
        o_ref[...] = _rms(out, gf_ref[...]) if final else out


def _block_tail(x2, attn, w_o, g, w_up, w_down, g_final=None, tm=512, tf=1024):
    n, d = x2.shape
    k = attn.shape[1]
    d_ff = w_up.shape[1]
    final = g_final is not None
    row = lambda w: pl.BlockSpec((tm, w), lambda i, f: (i, 0))
    vec = pl.BlockSpec((1, d), lambda i, f: (0, 0))
    in_specs = [row(d), row(k), pl.BlockSpec((k, d), lambda i, f: (0, 0), pipeline_mode=pl.Buffered(1)), vec,
                pl.BlockSpec((d, tf), lambda i, f: (0, f)), pl.BlockSpec((tf, d), lambda i, f: (f, 0))]
    args = [x2, attn, w_o, g.reshape(1, d), w_up, w_down]
    if final:
        in_specs.append(vec)
        args.append(g_final.reshape(1, d))
    return pl.pallas_call(
        functools.partial(_block_tail_kernel, final=final),
        grid=(n // tm, d_ff // tf),
        in_specs=in_specs,
        out_specs=row(d),
        out_shape=jax.ShapeDtypeStruct((n, d), F32),
        scratch_shapes=[pltpu.VMEM((tm, d), F32), pltpu.VMEM((tm, d), BF16), pltpu.VMEM((tm, d), F32)],
        compiler_params=_cparams("parallel", "arbitrary"),
        name="block_tail",
    )(*args)


def _softmax_step(m_old, s):
    m_new = jnp.maximum(m_old, jnp.max(s, axis=0, keepdims=True))
    return m_new, jnp.exp2(m_old - m_new), jnp.exp2(s - m_new).astype(BF16)


def _moba_proj_kernel(x_ref, g_ref, wqt_ref, wk_ref, wvt_ref, qt_ref, k_ref, vt_ref):
    xn = _rms(x_ref[...], g_ref[...]).astype(BF16)
    qt_ref[...] = _dot_nt(wqt_ref[...], xn).astype(BF16)
    k_ref[...] = _dot(xn, wk_ref[...]).astype(BF16)
    vt_ref[...] = _dot_nt(wvt_ref[...], xn).astype(BF16)


def _moba_proj(x3, g, wq_t, wk, wv_t):
    b, seq, d = x3.shape
    g2 = g.reshape(1, d)
    return pl.pallas_call(
        _moba_proj_kernel,
        grid=(b, seq // BLK),
        in_specs=[pl.BlockSpec((None, BLK, d), lambda bi, i: (bi, i, 0)),
                  _full(g2), _full(wq_t), _full(wk), _full(wv_t)],
        out_specs=[pl.BlockSpec((None, d, BLK), lambda bi, i: (bi, 0, i)),
                   pl.BlockSpec((None, BLK, d), lambda bi, i: (bi, i, 0)),
                   pl.BlockSpec((None, None, d, BLK), lambda bi, i: (bi, i, 0, 0))],
        out_shape=[jax.ShapeDtypeStruct((b, d, seq), BF16),
                   jax.ShapeDtypeStruct((b, seq, d), BF16),
                   jax.ShapeDtypeStruct((b, seq // BLK, d, BLK), BF16)],
        compiler_params=_cparams("parallel", "parallel"),
        name="moba_proj",
    )(x3, g2, wq_t, wk, wv_t)


def _moba_kernel(qt_ref, k_ref, vt_ref, bt_ref, o_ref, kaug_ref, kmean_ref, qfar_ref, m_ref, l_ref, acc_ref,
                 *, n_blk, k_sel, sel_rows):
    j = pl.program_id(2)
    seq = k_ref.shape[0]
    n_pairs = MOBA_GRP // 2
    t = BLK
    row = lax.broadcasted_iota(I32, (LANES, t), 0)
    blk_id = lax.broadcasted_iota(I32, (sel_rows, t), 0)
    ones_2 = jnp.ones((BF16_ROWS, 2 * t), BF16)

    @pl.when(j == 0)
    def _():
        krow = lax.broadcasted_iota(I32, (seq, LANES), 0)
        kcol = lax.broadcasted_iota(I32, (seq, LANES), 1)
        onehot = jnp.where(krow // BLK == kcol, 1.0, 0.0).astype(BF16)
        for p in range(n_pairs):
            kaug_ref[p, :, 0:LANES] = k_ref[:, p * LANES:(p + 1) * LANES]
            kaug_ref[p, :, LANES:2 * LANES] = onehot
            kmean_ref[p] = jnp.zeros((LANES, LANES), F32)
            for n in range(n_blk):
                kmean_ref[p, n:n + 1, :] = jnp.mean(
                    k_ref[n * BLK:(n + 1) * BLK, p * LANES:(p + 1) * LANES].astype(F32), axis=0, keepdims=True)

    near_blk = jnp.maximum(j - 1, 0)
    near0 = pl.multiple_of(near_blk * BLK, BLK)
    bias0 = pl.multiple_of(jnp.where(j == 0, BLK, 0), BLK)
    pad_rows = jnp.zeros((LANES - sel_rows, t), BF16)

    def update(p, s, v_t):
        m, alpha, pr = _softmax_step(m_ref[p], s)
        m_ref[p] = m
        pv = _dot(jnp.concatenate([v_t, ones_2], axis=0), pr)
        acc_ref[p] = alpha * acc_ref[p] + pv[0:LANES, :]
        l_ref[p] = alpha * l_ref[p] + pv[LANES:LANES + 1, :]

    q_near = []
    for p in range(n_pairs):
        qp = qt_ref[p * LANES:(p + 1) * LANES, :]
        near, far = [], []
        for hh in range(2):
            qh = jnp.where((row < HEAD_DIM) if hh == 0 else (row >= HEAD_DIM), qp, jnp.zeros_like(qp))
            gate = _dot(kmean_ref[p], qh.astype(F32))[0:sel_rows, :]
            g = jnp.where(blk_id < j, gate, NEG)
            g = jnp.where(blk_id < n_blk, g, -jnp.inf)
            sel_bias = jnp.full((sel_rows, t), NEG, F32)
            for it in range(k_sel):
                mx = jnp.max(g, axis=0, keepdims=True)
                first = jnp.min(jnp.where(g == mx, blk_id, LANES), axis=0, keepdims=True)
                pick = blk_id == first
                sel_bias = jnp.where(pick, jnp.where(it < j, 0.0, NEG), sel_bias)
                g = jnp.where(pick, -jnp.inf, g)
            near_bias = jnp.where(blk_id == j, 0.0, sel_bias)
            far_bias = jnp.where(blk_id < j - 1, sel_bias, NEG)
            near.append(jnp.concatenate([qh, near_bias.astype(BF16), pad_rows], axis=0))
            far.append(jnp.concatenate([qh, far_bias.astype(BF16), pad_rows], axis=0))
        q_near.append(jnp.concatenate(near, axis=1))
        qfar_ref[p] = jnp.concatenate(far, axis=1)
        m_ref[p] = jnp.full((1, 2 * t), NEG, F32)
        l_ref[p] = jnp.zeros((1, 2 * t), F32)
        acc_ref[p] = jnp.zeros((LANES, 2 * t), F32)

    logits = [_dot(kaug_ref[p, pl.ds(near0, 2 * BLK), :], q_near[p]) for p in range(n_pairs)]
    for p in range(n_pairs):
        feats = slice(p * LANES, (p + 1) * LANES)
        bias = jnp.concatenate([bt_ref[2 * p, pl.ds(bias0, 2 * BLK), :], bt_ref[2 * p + 1, pl.ds(bias0, 2 * BLK), :]],
                               axis=1)
        v_t = jnp.concatenate([vt_ref[near_blk, feats, :], vt_ref[near_blk + 1, feats, :]], axis=1)
        update(p, logits[p] + bias, v_t)

    def far_pair(pi, _):
        n0 = pl.multiple_of(pi * 2 * BLK, 2 * BLK)
        logits = [_dot(kaug_ref[p, pl.ds(n0, 2 * BLK), :], qfar_ref[p]) for p in range(n_pairs)]
        for p in range(n_pairs):
            feats = slice(p * LANES, (p + 1) * LANES)
            v_t = jnp.concatenate([vt_ref[2 * pi, feats, :], vt_ref[2 * pi + 1, feats, :]], axis=1)
            update(p, logits[p], v_t)
        return 0

    lax.fori_loop(0, jnp.maximum(j, 1) // 2, far_pair, 0)

    for p in range(n_pairs):
        acc = acc_ref[p] / l_ref[p]
        out_t = jnp.where(row < HEAD_DIM, acc[:, 0:t], acc[:, t:2 * t])
        o_ref[:, p * LANES:(p + 1) * LANES] = out_t.T.astype(o_ref.dtype)


def _moba_attention(q_t, k, v_t, bt, n_heads):
    b, seq, d = k.shape
    n_blk = seq // BLK
    n_grp = n_heads // MOBA_GRP
    n_pairs = MOBA_GRP // 2
    w = MOBA_GRP * HEAD_DIM
    k_sel = max(1, min(MOBA_TOPK, n_blk - 1))
    sel_rows = BF16_ROWS * pl.cdiv(n_blk, BF16_ROWS)
    assert sel_rows <= LANES and n_blk % 2 == 0
    return pl.pallas_call(
        functools.partial(_moba_kernel, n_blk=n_blk, k_sel=k_sel, sel_rows=sel_rows),
        grid=(b, n_grp, n_blk),
        in_specs=[pl.BlockSpec((None, w, BLK), lambda bi, g, j: (bi, g, j)),
                  pl.BlockSpec((None, seq, w), lambda bi, g, j: (bi, 0, g)),
                  pl.BlockSpec((None, n_blk, w, BLK), lambda bi, g, j: (bi, 0, g, 0)),
                  pl.BlockSpec((MOBA_GRP, 3 * BLK, BLK), lambda bi, g, j: (g, 0, 0))],
        out_specs=pl.BlockSpec((None, BLK, w), lambda bi, g, j: (bi, j, g)),
        out_shape=jax.ShapeDtypeStruct((b, seq, d), BF16),
        scratch_shapes=[pltpu.VMEM((n_pairs, seq, 2 * LANES), BF16),
                        pltpu.VMEM((n_pairs, LANES, LANES), F32),
                        pltpu.VMEM((n_pairs, 2 * LANES, 2 * BLK), BF16),
                        pltpu.VMEM((n_pairs, 1, 2 * BLK), F32),
                        pltpu.VMEM((n_pairs, 1, 2 * BLK), F32),
                        pltpu.VMEM((n_pairs, LANES, 2 * BLK), F32)],
        compiler_params=_cparams("parallel", "parallel", "arbitrary"),
        name="moba_attn",
    )(q_t, k, v_t, bt)


def _dsa_proj_kernel(x_ref, g_ref, win_ref, wkvt_ref, wwt_ref, gq_ref, gkv_ref, gkvt_ref, wuqt_ref, wqit_ref,
                     wuvt_ref, ckv_ref, vt_ref, kk_ref, widx_ref, qnt_ref, qit_ref, *, idx_scale):
    hn = _rms(x_ref[...], g_ref[...]).astype(BF16)
    proj = _dot(hn, win_ref[...])
    c_q = _rms(proj[:, 0:DSA_LORA], gq_ref[...]).astype(BF16)
    ckv_ref[...] = _rms(proj[:, DSA_LORA:2 * DSA_LORA], gkv_ref[...]).astype(BF16)
    kk_ref[...] = proj[:, 2 * DSA_LORA:2 * DSA_LORA + IDX_DIM].astype(BF16)
    ct = _dot_nt(wkvt_ref[...], hn)
    scale = lax.rsqrt(jnp.mean(ct * ct, axis=0, keepdims=True) + EPS)
    gt = jnp.concatenate([gkvt_ref[...]] * (ct.shape[1] // LANES), axis=1)
    vt_ref[...] = _dot(wuvt_ref[...], (ct * scale * gt).astype(BF16)).astype(BF16)
    widx_ref[...] = _dot_nt(wwt_ref[...], hn)[0:IDX_HEADS, :] * idx_scale
    qnt_ref[...] = _dot_nt(wuqt_ref[...], c_q).astype(BF16)
    qit_ref[...] = _dot_nt(wqit_ref[...], c_q).astype(BF16)


def _dsa_proj(x3, g, w_in_row, w_kv_t, w_w_t, g_q, g_kv, w_uq_t, w_qi_t, w_uv_t):
    b, seq, d = x3.shape
    tm = BLK
    g2, gq2, gkv2 = g.reshape(1, d), g_q.reshape(1, -1), g_kv.reshape(1, -1)
    gkv_t = jnp.broadcast_to(g_kv.reshape(-1, 1), (DSA_LORA, LANES))
    n_q, n_i, n_v = w_uq_t.shape[0], w_qi_t.shape[0], w_uv_t.shape[0]
    consts = (g2, w_in_row, w_kv_t, w_w_t, gq2, gkv2, gkv_t, w_uq_t, w_qi_t, w_uv_t)
    idx_scale = IDX_HEADS ** -0.5 * IDX_DIM ** -0.5
    return pl.pallas_call(
        functools.partial(_dsa_proj_kernel, idx_scale=idx_scale),
        grid=(b, seq // tm),
        in_specs=[pl.BlockSpec((None, tm, d), lambda bi, i: (bi, i, 0))] + [_full(a) for a in consts],
        out_specs=[pl.BlockSpec((None, tm, DSA_LORA), lambda bi, i: (bi, i, 0)),
                   pl.BlockSpec((None, None, n_v, tm), lambda bi, i: (bi, i, 0, 0)),
                   pl.BlockSpec((None, tm, IDX_DIM), lambda bi, i: (bi, i, 0)),
                   pl.BlockSpec((None, IDX_HEADS, tm), lambda bi, i: (bi, 0, i)),
                   pl.BlockSpec((None, n_q, tm), lambda bi, i: (bi, 0, i)),
                   pl.BlockSpec((None, n_i, tm), lambda bi, i: (bi, 0, i))],
        out_shape=[jax.ShapeDtypeStruct((b, seq, DSA_LORA), BF16),
                   jax.ShapeDtypeStruct((b, seq // tm, n_v, tm), BF16),
                   jax.ShapeDtypeStruct((b, seq, IDX_DIM), BF16),
                   jax.ShapeDtypeStruct((b, IDX_HEADS, seq), F32),
                   jax.ShapeDtypeStruct((b, n_q, seq), BF16),
                   jax.ShapeDtypeStruct((b, n_i, seq), BF16)],
        compiler_params=_cparams("parallel", "parallel"),
        name="dsa_proj",
    )(x3, *consts)


def _dsa_select_kernel(qit_ref, w_ref, kk_ref, o_ref, keys_ref, half_ref, *, top_k, idx_bits):
    c = pl.program_id(1)
    n_kc, tk, tq = keys_ref.shape
    n_proc = c + 1
    key_row = lax.broadcasted_iota(I32, (tk, tq), 0)
    q_pos = c * tq + lax.broadcasted_iota(I32, (tk, tq), 1)
    w = w_ref[...]

    def score_chunk(kc, _):
        kt = kk_ref[pl.ds(pl.multiple_of(kc * tk, tk), tk), :]
        acc = jnp.zeros((tk, tq), F32)
        for h in range(IDX_HEADS):
            x = _dot(kt, qit_ref[h * IDX_DIM:(h + 1) * IDX_DIM, :])
            acc = acc + w[h:h + 1, :] * jnp.maximum(x, 0.0)
        sc = jnp.where(kc * tk + key_row <= q_pos, acc, NEG)
        bits = pltpu.bitcast(sc, I32)
        key = jnp.where(bits < 0, bits ^ 0x7FFFFFFF, bits)
        keys_ref[kc] = key
        half_ref[kc] = jnp.right_shift(key, 16).astype(I16)
        return 0

    lax.fori_loop(0, n_proc, score_chunk, 0)

    def count(pred):
        def body(kc, tot):
            hit = jnp.where(pred(keys_ref[kc], kc * tk + key_row), 1, 0)
            return tot + jnp.sum(hit, axis=0, keepdims=True)
        return lax.fori_loop(0, n_proc, body, jnp.zeros((1, tq), I32))

    def count_half(cand, strict=False):
        cand16 = cand.astype(I16)

        def body(kc, part):
            x = half_ref[kc]
            hit = jnp.where((x > cand16) if strict else (x >= cand16), jnp.int16(1), jnp.int16(0))
            tiles = [hit[r:r + BF16_ROWS, :] for r in range(0, tk, BF16_ROWS)]
            while len(tiles) > 1:
                tiles = [a + b for a, b in zip(tiles[::2], tiles[1::2])]
            return part + tiles[0]
        part = lax.fori_loop(0, n_proc, body, jnp.zeros((BF16_ROWS, tq), I16))
        return jnp.sum(part.astype(I32), axis=0, keepdims=True)

    def largest_half(need):
        def bit(i, u):
            cand_u = u | jnp.left_shift(jnp.int32(1), 15 - i)
            return jnp.where(count_half(cand_u - 32768) >= need, cand_u, u)
        return lax.fori_loop(0, 16, bit, jnp.zeros((1, tq), I32)) - 32768

    thr_hi = largest_half(top_k)
    above = count_half(thr_hi, strict=True)

    def low_chunk(kc, _):
        key = keys_ref[kc]
        low = jnp.bitwise_and(key, 0xFFFF) - 32768
        half_ref[kc] = jnp.where(jnp.right_shift(key, 16) == thr_hi, low, -32768).astype(I16)
        return 0

    lax.fori_loop(0, n_proc, low_chunk, 0)
    thr_lo = largest_half(top_k - above)
    cnt_ge = above + count_half(thr_lo)
    thr = jnp.left_shift(thr_hi, 16) + (thr_lo + 32768)

    def tie_cut():
        need = top_k - count(lambda key, pos: key > thr)

        def index_bit(i, x):
            cand = x | jnp.left_shift(jnp.int32(1), idx_bits - 1 - i)
            cnt = count(lambda key, pos: (key == thr) & (pos < cand))
            return jnp.where(cnt < need, cand, x)
        return lax.fori_loop(0, idx_bits, index_bit, jnp.zeros((1, tq), I32))

    last_tie = lax.cond(jnp.max(cnt_ge) > top_k, tie_cut, lambda: jnp.full((1, tq), 2 ** idx_bits, I32))

    def write_chunk(kc, _):
        key = keys_ref[kc]
        pos = kc * tk + key_row
        chosen = (key > thr) | ((key == thr) & (pos <= last_tie))
        rows = pl.ds(pl.multiple_of(kc * tk, tk), tk)
        o_ref[rows, :] = jnp.where(chosen & (pos <= q_pos), 0.0, NEG).astype(o_ref.dtype)
        return 0

    lax.fori_loop(0, n_proc, write_chunk, 0)

    def fill_chunk(kc, _):
        o_ref[pl.ds(pl.multiple_of(kc * tk, tk), tk), :] = jnp.full((tk, tq), NEG, o_ref.dtype)
        return 0

    lax.fori_loop(n_proc, n_kc, fill_chunk, 0)


def _dsa_select(qi_t, widx_t, kk):
    b, seq, _ = kk.shape
    t = BLK
    n_kc = seq // t
    top_k = min(DSA_TOPK_MAX, seq // 4)
    idx_bits = max(1, (seq - 1).bit_length())
    return pl.pallas_call(
        functools.partial(_dsa_select_kernel, top_k=top_k, idx_bits=idx_bits),
        grid=(b, seq // t),
        in_specs=[pl.BlockSpec((None, qi_t.shape[1], t), lambda bi, c: (bi, 0, c)),
                  pl.BlockSpec((None, IDX_HEADS, t), lambda bi, c: (bi, 0, c)),
                  pl.BlockSpec((None, seq, IDX_DIM), lambda bi, c: (bi, 0, 0))],
        out_specs=pl.BlockSpec((None, None, seq, t), lambda bi, c: (bi, c, 0, 0)),
        out_shape=jax.ShapeDtypeStruct((b, seq // t, seq, t), BF16),
        scratch_shapes=[pltpu.VMEM((n_kc, t, t), I32),
                        pltpu.VMEM((n_kc, t, t), I16)],
        compiler_params=_cparams("parallel", "arbitrary"),
        name="dsa_select",
    )(qi_t, widx_t, kk)


def _dsa_attn_kernel(qnt_ref, ckv_ref, vt_ref, mb_ref, bt_ref, wukt_ref, o_ref,
                     qlat_ref, acc_ref, m_ref, l_ref, s_ref, *, n_heads):
    qt = pl.program_id(1)
    t = BLK
    n_grp = n_heads // DSA_GRP
    row = lax.broadcasted_iota(I32, (LANES, t), 0)
    ones_rows = jnp.ones((BF16_ROWS, t), BF16)

    for hp in range(n_heads // 2):
        qp = qnt_ref[hp * LANES:(hp + 1) * LANES, :]
        for hh in range(2):
            qh = jnp.where((row < HEAD_DIM) if hh == 0 else (row >= HEAD_DIM), qp, jnp.zeros_like(qp))
            h = 2 * hp + hh
            qlat_ref[:, h * t:(h + 1) * t] = _dot(wukt_ref[hp], qh).astype(BF16)

    m_ref[...] = jnp.full(m_ref.shape, NEG, F32)
    l_ref[...] = jnp.zeros_like(l_ref)
    acc_ref[...] = jnp.zeros_like(acc_ref)

    def grp_cols(g):
        return slice(g * DSA_GRP * t, (g + 1) * DSA_GRP * t)

    def key_rows(kk):
        return pl.ds(pl.multiple_of(kk * t, t), t)

    def fill(slot, kk, bias_row0):
        kt = ckv_ref[key_rows(kk), :]
        mb = mb_ref[key_rows(kk), :].astype(F32)
        for g in range(n_grp):
            s_grp = _dot(kt, qlat_ref[:, grp_cols(g)])
            parts = []
            for i in range(DSA_GRP):
                s = s_grp[:, i * t:(i + 1) * t] + mb
                if bias_row0 is not None:
                    s = s + bt_ref[g * DSA_GRP + i, bias_row0:bias_row0 + t, :]
                parts.append(s)
            s_ref[slot, :, grp_cols(g)] = jnp.concatenate(parts, axis=1)

    def drain(slot, kk):
        for g in range(n_grp):
            cols = grp_cols(g)
            m_old = m_ref[:, cols]
            m_new = jnp.maximum(m_old, jnp.max(s_ref[slot, :, cols], axis=0, keepdims=True))
            alpha = jnp.exp2(m_old - m_new)
            pr = jnp.exp2(s_ref[slot, :, cols] - m_new).astype(BF16)
            m_ref[:, cols] = m_new
            for i in range(DSA_GRP):
                h = g * DSA_GRP + i
                feats = slice(h * HEAD_DIM, (h + 1) * HEAD_DIM)
                hc = slice(i * t, (i + 1) * t)
                lc = slice(h * t, (h + 1) * t)
                pv = _dot(jnp.concatenate([vt_ref[kk, feats, :], ones_rows], axis=0), pr[:, hc])
                acc_ref[feats, :] = alpha[:, hc] * acc_ref[feats, :] + pv[0:HEAD_DIM, :]
                l_ref[:, lc] = alpha[:, hc] * l_ref[:, lc] + pv[HEAD_DIM:HEAD_DIM + 1, :]

    fill(0, qt, BLK)
    drain(0, qt)

    @pl.when(qt >= 1)
    def _():
        fill(1, qt - 1, 0)
        drain(1, qt - 1)

    n_far = jnp.maximum(qt - 1, 0)

    @pl.when(n_far > 0)
    def _():
        fill(0, 0, None)

    def far_pair(i, _):
        kk = 2 * i
        fill(1, kk + 1, None)
        drain(0, kk)
        fill(0, jnp.minimum(kk + 2, n_far - 1), None)
        drain(1, kk + 1)
        return 0

    lax.fori_loop(0, n_far // 2, far_pair, 0)

    @pl.when(n_far % 2 == 1)
    def _():
        drain(0, n_far - 1)

    for h in range(n_heads):
        feats = slice(h * HEAD_DIM, (h + 1) * HEAD_DIM)
        acc_ref[feats, :] = acc_ref[feats, :] / l_ref[:, h * t:(h + 1) * t]
    o_ref[...] = acc_ref[...].T.astype(o_ref.dtype)


def _dsa_attention(qn_t, ckv, v_t, mask_t, bt, w_uk_t, n_heads):
    b, seq, _ = ckv.shape
    t = BLK
    d_q = qn_t.shape[1]
    return pl.pallas_call(
        functools.partial(_dsa_attn_kernel, n_heads=n_heads),
        grid=(b, seq // t),
        in_specs=[pl.BlockSpec((None, d_q, t), lambda bi, q: (bi, 0, q)),
                  pl.BlockSpec((None, seq, DSA_LORA), lambda bi, q: (bi, 0, 0)),
                  pl.BlockSpec((None, seq // t, d_q, t), lambda bi, q: (bi, 0, 0, 0)),
                  pl.BlockSpec((None, None, seq, t), lambda bi, q: (bi, q, 0, 0)),
                  _full(bt), _full(w_uk_t)],
        out_specs=pl.BlockSpec((None, t, d_q), lambda bi, q: (bi, q, 0)),
        out_shape=jax.ShapeDtypeStruct((b, seq, d_q), BF16),
        scratch_shapes=[pltpu.VMEM((DSA_LORA, n_heads * t), BF16),
                        pltpu.VMEM((d_q, t), F32),
                        pltpu.VMEM((1, n_heads * t), F32),
                        pltpu.VMEM((1, n_heads * t), F32),
                        pltpu.VMEM((2, t, n_heads * t), F32)],
        compiler_params=_cparams("parallel", "arbitrary"),
        name="dsa_attn",
    )(qn_t, ckv, v_t, mask_t, bt, w_uk_t)


def _moba_layer(x2, batch, seq, g, w_qkv, bt, n_heads):
    d = x2.shape[1]
    wq_t = (w_qkv[:, :d] * (HEAD_DIM ** -0.5 * LOG2E)).T.astype(BF16)
    wk = w_qkv[:, d:2 * d].astype(BF16)
    wv_t = w_qkv[:, 2 * d:].T.astype(BF16)
    q_t, k, v_t = _moba_proj(x2.reshape(batch, seq, d), g, wq_t, wk, wv_t)
    attn = _moba_attention(q_t, k, v_t, bt, n_heads)
    return attn.reshape(batch * seq, d)


def _dsa_layer(x2, batch, seq, g, w_in, g_q, g_kv, w_uq, w_qi, w_uk, w_uv, bt, n_heads):
    d = x2.shape[1]
    k_lo = 2 * DSA_LORA
    w_in_row = jnp.concatenate([w_in[:, :k_lo + IDX_DIM], jnp.zeros((d, LANES - IDX_DIM), F32)], axis=1).astype(BF16)
    w_kv_t = w_in[:, DSA_LORA:k_lo].T.astype(BF16)
    w_w_t = jnp.concatenate([w_in[:, k_lo + IDX_DIM:].T, jnp.zeros((BF16_ROWS - IDX_HEADS, d), F32)]).astype(BF16)
    w_uq_t = (w_uq * (HEAD_DIM ** -0.5 * LOG2E)).T.astype(BF16)
    w_uv_t = jnp.transpose(w_uv, (0, 2, 1)).reshape(n_heads * HEAD_DIM, DSA_LORA).astype(BF16)
    ckv, v_t, kk, widx_t, qn_t, qi_t = _dsa_proj(x2.reshape(batch, seq, d), g, w_in_row, w_kv_t, w_w_t,
                                                 g_q, g_kv, w_uq_t, w_qi.T.astype(BF16), w_uv_t)
    mask_t = _dsa_select(qi_t, widx_t, kk)
    w_uk_t = jnp.transpose(w_uk.reshape(n_heads // 2, 2 * HEAD_DIM, DSA_LORA), (0, 2, 1)).astype(BF16)
    attn = _dsa_attention(qn_t, ckv, v_t, mask_t, bt[:, :2 * BLK], w_uk_t, n_heads)
    return attn.reshape(batch * seq, -1)


def kernel(x, rel_bias, ln_attn, ln_mlp, moba_w_qkv, moba_w_o, dsa_w_in, dsa_g_q, dsa_g_kv, dsa_w_uq, dsa_w_qi,
           dsa_w_uk, dsa_w_uv, dsa_w_o, mlp_w_up, mlp_w_down, final_norm):
    batch, seq, d = x.shape
    n_heads = rel_bias.shape[1]
    depth = ln_attn.shape[0]
    bt = _bias_tiles(rel_bias)
    x2 = x.reshape(batch * seq, d)
    for i in range(depth):
        j = i // 2
        if i % 2 == 0:
            attn, w_o = _moba_layer(x2, batch, seq, ln_attn[i], moba_w_qkv[j], bt, n_heads), moba_w_o[j]
        else:
            attn, w_o = _dsa_layer(x2, batch, seq, ln_attn[i], dsa_w_in[j], dsa_g_q[j], dsa_g_kv[j], dsa_w_uq[j],
                                   dsa_w_qi[j], dsa_w_uk[j], dsa_w_uv[j], bt, n_heads), dsa_w_o[j]
        x2 = _block_tail(x2, attn, w_o.astype(BF16), ln_mlp[i], mlp_w_up[i].astype(BF16), mlp_w_down[i].astype(BF16),
                         g_final=final_norm if i == depth - 1 else None)
    return x2.reshape(batch, seq, d)
```

```python
import functools
import math

import numpy as np
import jax
import jax.numpy as jnp
from jax import lax
from jax.experimental import pallas as pl
from jax.experimental.pallas import tpu as pltpu

F32 = jnp.float32
BF16 = jnp.bfloat16
I32 = jnp.int32
I16 = jnp.int16

EPS = 1e-6
NEG = -1e30
INT_MIN = -(2 ** 31)
LOG2E = 1.0 / math.log(2.0)

NUM_BUCKETS = 32
MAX_DISTANCE = 128
HEAD_DIM = 64
LANES = 128
BF16_ROWS = 16
BLK = 256
MOBA_TOPK = 3
MOBA_GRP = 8
DSA_TOPK_MAX = 256
DSA_LORA = 256
DSA_GRP = 4
IDX_HEADS = 8
IDX_DIM = 64
VMEM_LIMIT = 56 * 1024 * 1024


def _dot(a, b):
    return jnp.dot(a, b, preferred_element_type=F32)


def _dot_nt(a, b):
    return lax.dot_general(a, b, (((1,), (1,)), ((), ())), preferred_element_type=F32)


def _rms(x, g):
    return x * lax.rsqrt(jnp.mean(x * x, axis=-1, keepdims=True) + EPS) * g


def _cparams(*sem):
    return pltpu.CompilerParams(dimension_semantics=sem, vmem_limit_bytes=VMEM_LIMIT)


def _full(a):
    return pl.BlockSpec(a.shape, lambda *_: (0,) * a.ndim, pipeline_mode=pl.Buffered(1))


def _bucket_map_t():
    jj = np.arange(2 * BLK)[:, None]
    i = np.arange(BLK)[None, :]
    d = i - jj + BLK
    n = np.maximum(d, 0)
    exact = NUM_BUCKETS // 2
    nf = np.maximum(n, 1).astype(np.float32)
    large = exact + (np.log(nf / exact) / math.log(MAX_DISTANCE / exact) * (NUM_BUCKETS - exact)).astype(np.int32)
    large = np.minimum(large, NUM_BUCKETS - 1)
    bucket = np.where(n < exact, n, large)
    bmap = np.where(d >= 0, bucket, -1).astype(np.int32)
    return np.concatenate([bmap, bmap[:BLK]], axis=0)


def _bias_tile_kernel(tab_ref, bmap_ref, o_ref):
    h = pl.program_id(0)
    bm = bmap_ref[...]
    base = tab_ref[NUM_BUCKETS - 1, h]
    acc = jnp.full(bm.shape, NEG, F32)
    for b in range(NUM_BUCKETS):
        acc = jnp.where(bm == b, (tab_ref[b, h] - base) * LOG2E, acc)
    o_ref[...] = acc


def _bias_tiles(rel_bias):
    n_heads = rel_bias.shape[1]
    bmap = jnp.asarray(_bucket_map_t())
    return pl.pallas_call(
        _bias_tile_kernel,
        grid=(n_heads,),
        in_specs=[pl.BlockSpec(memory_space=pltpu.SMEM),
                  pl.BlockSpec((3 * BLK, BLK), lambda h: (0, 0))],
        out_specs=pl.BlockSpec((None, 3 * BLK, BLK), lambda h: (h, 0, 0)),
        out_shape=jax.ShapeDtypeStruct((n_heads, 3 * BLK, BLK), F32),
        name="bias_tiles",
    )(rel_bias, bmap)


def _block_tail_kernel(x_ref, a_ref, wo_ref, g_ref, wu_ref, wd_ref, *rest, final):
    if final:
        gf_ref, o_ref, x1_ref, xn_ref, acc_ref = rest
    else:
        o_ref, x1_ref, xn_ref, acc_ref = rest
    f = pl.program_id(1)

    @pl.when(f == 0)
    def _():
        x1 = x_ref[...] + _dot(a_ref[...], wo_ref[...])
        x1_ref[...] = x1
        xn_ref[...] = _rms(x1, g_ref[...]).astype(BF16)
        acc_ref[...] = jnp.zeros_like(acc_ref)

    h = jnp.square(jnp.maximum(_dot(xn_ref[...], wu_ref[...]), 0.0)).astype(BF16)
    acc_ref[...] += _dot(h, wd_ref[...])

    @pl.when(f == pl.num_programs(1) - 1)
    def _():
        out = x1_ref[...] + acc_ref[...]
        o_ref[...] = _rms(out, gf_ref[...]) if final else out


def _block_tail(x2, attn, w_o, g, w_up, w_down, g_final=None, tm=512, tf=1024):
    n, d = x2.shape
    k = attn.shape[1]
    d_ff = w_up.shape[1]
    final = g_final is not None
    row = lambda w: pl.BlockSpec((tm, w), lambda i, f: (i, 0))
    vec = pl.BlockSpec((1, d), lambda i, f: (0, 0))
    in_specs = [row(d), row(k), pl.BlockSpec((k, d), lambda i, f: (0, 0), pipeline_mode=pl.Buffered(1)), vec,
                pl.BlockSpec((d, tf), lambda i, f: (0, f)), pl.BlockSpec((tf, d), lambda i, f: (f, 0))]
    args = [x2, attn, w_o, g.reshape(1, d), w_up, w_down]
    if final:
        in_specs.append(vec)
        args.append(g_final.reshape(1, d))
    return pl.pallas_call(
        functools.partial(_block_tail_kernel, final=final),
        grid=(n // tm, d_ff // tf),
        in_specs=in_specs,
        out_specs=row(d),
        out_shape=jax.ShapeDtypeStruct((n, d), F32),
        scratch_shapes=[pltpu.VMEM((tm, d), F32), pltpu.VMEM((tm, d), BF16), pltpu.VMEM((tm, d), F32)],
        compiler_params=_cparams("parallel", "arbitrary"),
        name="block_tail",
    )(*args)


def _moba_proj_kernel(x_ref, g_ref, wqt_ref, wk_ref, wvt_ref, qt_ref, k_ref, vt_ref):
    xn = _rms(x_ref[...], g_ref[...]).astype(BF16)
    qt_ref[...] = _dot_nt(wqt_ref[...], xn).astype(BF16)
    k_ref[...] = _dot(xn, wk_ref[...]).astype(BF16)
    vt_ref[...] = _dot_nt(wvt_ref[...], xn).astype(BF16)


def _moba_proj(x3, g, wq_t, wk, wv_t):
    b, seq, d = x3.shape
    g2 = g.reshape(1, d)
    return pl.pallas_call(
        _moba_proj_kernel,
        grid=(b, seq // BLK),
        in_specs=[pl.BlockSpec((None, BLK, d), lambda bi, i: (bi, i, 0)),
                  _full(g2), _full(wq_t), _full(wk), _full(wv_t)],
        out_specs=[pl.BlockSpec((None, d, BLK), lambda bi, i: (bi, 0, i)),
                   pl.BlockSpec((None, BLK, d), lambda bi, i: (bi, i, 0)),
                   pl.BlockSpec((None, None, d, BLK), lambda bi, i: (bi, i, 0, 0))],
        out_shape=[jax.ShapeDtypeStruct((b, d, seq), BF16),
                   jax.ShapeDtypeStruct((b, seq, d), BF16),
                   jax.ShapeDtypeStruct((b, seq // BLK, d, BLK), BF16)],
        compiler_params=_cparams("parallel", "parallel"),
        name="moba_proj",
    )(x3, g2, wq_t, wk, wv_t)


def _moba_kernel(qt_ref, k_ref, vt_ref, bt_ref, o_ref, kaug_ref, kmean_ref, qfar_ref, m_ref, l_ref, acc_ref, s_ref,
                 *, n_blk, k_sel, sel_rows):
    j = pl.program_id(2)
    seq = k_ref.shape[0]
    n_pairs = MOBA_GRP // 2
    t = BLK
    row = lax.broadcasted_iota(I32, (LANES, t), 0)
    blk_id = lax.broadcasted_iota(I32, (sel_rows, t), 0)
    ones_2 = jnp.ones((BF16_ROWS, 2 * t), BF16)

    @pl.when(j == 0)
    def _():
        krow = lax.broadcasted_iota(I32, (seq, LANES), 0)
        kcol = lax.broadcasted_iota(I32, (seq, LANES), 1)
        onehot = jnp.where(krow // BLK == kcol, 1.0, 0.0).astype(BF16)
        for p in range(n_pairs):
            kaug_ref[p, :, 0:LANES] = k_ref[:, p * LANES:(p + 1) * LANES]
            kaug_ref[p, :, LANES:2 * LANES] = onehot
            kmean_ref[p] = jnp.zeros((LANES, LANES), F32)
            for n in range(n_blk):
                kmean_ref[p, n:n + 1, :] = jnp.mean(
                    k_ref[n * BLK:(n + 1) * BLK, p * LANES:(p + 1) * LANES].astype(F32), axis=0, keepdims=True)

    near_blk = jnp.maximum(j - 1, 0)
    near0 = pl.multiple_of(near_blk * BLK, BLK)
    bias0 = pl.multiple_of(jnp.where(j == 0, BLK, 0), BLK)
    pad_rows = jnp.zeros((LANES - sel_rows, t), BF16)

    def drain(slot, blk0):
        for p in range(n_pairs):
            feats = slice(p * LANES, (p + 1) * LANES)
            m_old = m_ref[p]
            m_new = jnp.maximum(m_old, jnp.max(s_ref[slot, p], axis=0, keepdims=True))
            alpha = jnp.exp2(m_old - m_new)
            pr = jnp.exp2(s_ref[slot, p] - m_new).astype(BF16)
            m_ref[p] = m_new
            v_t = jnp.concatenate([vt_ref[blk0, feats, :], vt_ref[blk0 + 1, feats, :]], axis=1)
            pv = _dot(jnp.concatenate([v_t, ones_2], axis=0), pr)
            acc_ref[p] = alpha * acc_ref[p] + pv[0:LANES, :]
            l_ref[p] = alpha * l_ref[p] + pv[LANES:LANES + 1, :]

    def fill_far(slot, pi):
        n0 = pl.multiple_of(pi * 2 * BLK, 2 * BLK)
        for p in range(n_pairs):
            s_ref[slot, p] = _dot(kaug_ref[p, pl.ds(n0, 2 * BLK), :], qfar_ref[p])

    q_near = []
    for p in range(n_pairs):
        qp = qt_ref[p * LANES:(p + 1) * LANES, :]
        near, far = [], []
        for hh in range(2):
            qh = jnp.where((row < HEAD_DIM) if hh == 0 else (row >= HEAD_DIM), qp, jnp.zeros_like(qp))
            gate = _dot(kmean_ref[p], qh.astype(F32))[0:sel_rows, :]
            g = jnp.where(blk_id < j, gate, NEG)
            g = jnp.where(blk_id < n_blk, g, -jnp.inf)
            sel_bias = jnp.full((sel_rows, t), NEG, F32)
            for it in range(k_sel):
                mx = jnp.max(g, axis=0, keepdims=True)
                first = jnp.min(jnp.where(g == mx, blk_id, LANES), axis=0, keepdims=True)
                pick = blk_id == first
                sel_bias = jnp.where(pick, jnp.where(it < j, 0.0, NEG), sel_bias)
                g = jnp.where(pick, -jnp.inf, g)
            near_bias = jnp.where(blk_id == j, 0.0, sel_bias)
            far_bias = jnp.where(blk_id < j - 1, sel_bias, NEG)
            near.append(jnp.concatenate([qh, near_bias.astype(BF16), pad_rows], axis=0))
            far.append(jnp.concatenate([qh, far_bias.astype(BF16), pad_rows], axis=0))
        q_near.append(jnp.concatenate(near, axis=1))
        qfar_ref[p] = jnp.concatenate(far, axis=1)
        m_ref[p] = jnp.full((1, 2 * t), NEG, F32)
        l_ref[p] = jnp.zeros((1, 2 * t), F32)
        acc_ref[p] = jnp.zeros((LANES, 2 * t), F32)

    for p in range(n_pairs):
        bias = jnp.concatenate([bt_ref[2 * p, pl.ds(bias0, 2 * BLK), :], bt_ref[2 * p + 1, pl.ds(bias0, 2 * BLK), :]],
                               axis=1)
        s_ref[0, p] = _dot(kaug_ref[p, pl.ds(near0, 2 * BLK), :], q_near[p]) + bias
    drain(0, near_blk)

    n_far = jnp.maximum(j, 1) // 2

    @pl.when(n_far > 0)
    def _():
        fill_far(0, 0)

    def far_two(i, _):
        pi = 2 * i
        fill_far(1, pi + 1)
        drain(0, 2 * pi)
        fill_far(0, jnp.minimum(pi + 2, n_far - 1))
        drain(1, 2 * pi + 2)
        return 0

    lax.fori_loop(0, n_far // 2, far_two, 0)

    @pl.when(n_far % 2 == 1)
    def _():
        drain(0, 2 * (n_far - 1))

    for p in range(n_pairs):
        acc = acc_ref[p] / l_ref[p]
        out_t = jnp.where(row < HEAD_DIM, acc[:, 0:t], acc[:, t:2 * t])
        o_ref[:, p * LANES:(p + 1) * LANES] = out_t.T.astype(o_ref.dtype)


def _moba_attention(q_t, k, v_t, bt, n_heads):
    b, seq, d = k.shape
    n_blk = seq // BLK
    n_grp = n_heads // MOBA_GRP
    n_pairs = MOBA_GRP // 2
    w = MOBA_GRP * HEAD_DIM
    k_sel = max(1, min(MOBA_TOPK, n_blk - 1))
    sel_rows = BF16_ROWS * pl.cdiv(n_blk, BF16_ROWS)
    assert sel_rows <= LANES and n_blk % 2 == 0
    return pl.pallas_call(
        functools.partial(_moba_kernel, n_blk=n_blk, k_sel=k_sel, sel_rows=sel_rows),
        grid=(b, n_grp, n_blk),
        in_specs=[pl.BlockSpec((None, w, BLK), lambda bi, g, j: (bi, g, j)),
                  pl.BlockSpec((None, seq, w), lambda bi, g, j: (bi, 0, g)),
                  pl.BlockSpec((None, n_blk, w, BLK), lambda bi, g, j: (bi, 0, g, 0)),
                  pl.BlockSpec((MOBA_GRP, 3 * BLK, BLK), lambda bi, g, j: (g, 0, 0), pipeline_mode=pl.Buffered(1))],
        out_specs=pl.BlockSpec((None, BLK, w), lambda bi, g, j: (bi, j, g)),
        out_shape=jax.ShapeDtypeStruct((b, seq, d), BF16),
        scratch_shapes=[pltpu.VMEM((n_pairs, seq, 2 * LANES), BF16),
                        pltpu.VMEM((n_pairs, LANES, LANES), F32),
                        pltpu.VMEM((n_pairs, 2 * LANES, 2 * BLK), BF16),
                        pltpu.VMEM((n_pairs, 1, 2 * BLK), F32),
                        pltpu.VMEM((n_pairs, 1, 2 * BLK), F32),
                        pltpu.VMEM((n_pairs, LANES, 2 * BLK), F32),
                        pltpu.VMEM((2, n_pairs, 2 * BLK, 2 * BLK), F32)],
        compiler_params=_cparams("parallel", "parallel", "arbitrary"),
        name="moba_attn",
    )(q_t, k, v_t, bt)


def _dsa_proj_kernel(x_ref, g_ref, win_ref, wkvt_ref, wwt_ref, gq_ref, gkv_ref, gkvt_ref, wuqt_ref, wqit_ref,
                     wuvt_ref, ckv_ref, vt_ref, kk_ref, widx_ref, qnt_ref, qit_ref, *, idx_scale):
    hn = _rms(x_ref[...], g_ref[...]).astype(BF16)
    proj = _dot(hn, win_ref[...])
    c_q = _rms(proj[:, 0:DSA_LORA], gq_ref[...]).astype(BF16)
    ckv_ref[...] = _rms(proj[:, DSA_LORA:2 * DSA_LORA], gkv_ref[...]).astype(BF16)
    kk_ref[...] = proj[:, 2 * DSA_LORA:2 * DSA_LORA + IDX_DIM].astype(BF16)
    ct = _dot_nt(wkvt_ref[...], hn)
    scale = lax.rsqrt(jnp.mean(ct * ct, axis=0, keepdims=True) + EPS)
    gt = jnp.concatenate([gkvt_ref[...]] * (ct.shape[1] // LANES), axis=1)
    vt_ref[...] = _dot(wuvt_ref[...], (ct * scale * gt).astype(BF16)).astype(BF16)
    widx_ref[...] = _dot_nt(wwt_ref[...], hn)[0:IDX_HEADS, :] * idx_scale
    qnt_ref[...] = _dot_nt(wuqt_ref[...], c_q).astype(BF16)
    qit_ref[...] = _dot_nt(wqit_ref[...], c_q).astype(BF16)


def _dsa_proj(x3, g, w_in_row, w_kv_t, w_w_t, g_q, g_kv, w_uq_t, w_qi_t, w_uv_t):
    b, seq, d = x3.shape
    tm = BLK
    g2, gq2, gkv2 = g.reshape(1, d), g_q.reshape(1, -1), g_kv.reshape(1, -1)
    gkv_t = jnp.broadcast_to(g_kv.reshape(-1, 1), (DSA_LORA, LANES))
    n_q, n_i, n_v = w_uq_t.shape[0], w_qi_t.shape[0], w_uv_t.shape[0]
    consts = (g2, w_in_row, w_kv_t, w_w_t, gq2, gkv2, gkv_t, w_uq_t, w_qi_t, w_uv_t)
    idx_scale = IDX_HEADS ** -0.5 * IDX_DIM ** -0.5
    return pl.pallas_call(
        functools.partial(_dsa_proj_kernel, idx_scale=idx_scale),
        grid=(b, seq // tm),
        in_specs=[pl.BlockSpec((None, tm, d), lambda bi, i: (bi, i, 0))] + [_full(a) for a in consts],
        out_specs=[pl.BlockSpec((None, tm, DSA_LORA), lambda bi, i: (bi, i, 0)),
                   pl.BlockSpec((None, None, n_v, tm), lambda bi, i: (bi, i, 0, 0)),
                   pl.BlockSpec((None, tm, IDX_DIM), lambda bi, i: (bi, i, 0)),
                   pl.BlockSpec((None, IDX_HEADS, tm), lambda bi, i: (bi, 0, i)),
                   pl.BlockSpec((None, n_q, tm), lambda bi, i: (bi, 0, i)),
                   pl.BlockSpec((None, n_i, tm), lambda bi, i: (bi, 0, i))],
        out_shape=[jax.ShapeDtypeStruct((b, seq, DSA_LORA), BF16),
                   jax.ShapeDtypeStruct((b, seq // tm, n_v, tm), BF16),
                   jax.ShapeDtypeStruct((b, seq, IDX_DIM), BF16),
                   jax.ShapeDtypeStruct((b, IDX_HEADS, seq), F32),
                   jax.ShapeDtypeStruct((b, n_q, seq), BF16),
                   jax.ShapeDtypeStruct((b, n_i, seq), BF16)],
        compiler_params=_cparams("parallel", "parallel"),
        name="dsa_proj",
    )(x3, *consts)


def _dsa_select_kernel(qit_ref, w_ref, kk_ref, o_ref, keys_ref, half_ref, *, top_k, idx_bits):
    c = pl.program_id(1)
    n_kc, tk, tq = keys_ref.shape
    n_proc = c + 1
    key_row = lax.broadcasted_iota(I32, (tk, tq), 0)
    q_pos = c * tq + lax.broadcasted_iota(I32, (tk, tq), 1)
    w = w_ref[...]

    def score_chunk(kc, _):
        kt = kk_ref[pl.ds(pl.multiple_of(kc * tk, tk), tk), :]
        acc = jnp.zeros((tk, tq), F32)
        for h in range(IDX_HEADS):
            x = _dot(kt, qit_ref[h * IDX_DIM:(h + 1) * IDX_DIM, :])
            acc = acc + w[h:h + 1, :] * jnp.maximum(x, 0.0)
        sc = jnp.where(kc * tk + key_row <= q_pos, acc, NEG)
        bits = pltpu.bitcast(sc, I32)
        key = jnp.where(bits < 0, bits ^ 0x7FFFFFFF, bits)
        keys_ref[kc] = key
        half_ref[kc] = jnp.right_shift(key, 16).astype(I16)
        return 0

    lax.fori_loop(0, n_proc, score_chunk, 0)

    def count(pred):
        def body(kc, tot):
            hit = jnp.where(pred(keys_ref[kc], kc * tk + key_row), 1, 0)
            return tot + jnp.sum(hit, axis=0, keepdims=True)
        return lax.fori_loop(0, n_proc, body, jnp.zeros((1, tq), I32))

    def count_half(cand, strict=False):
        cand16 = cand.astype(I16)

        def body(kc, part):
            x = half_ref[kc]
            hit = jnp.where((x > cand16) if strict else (x >= cand16), jnp.int16(1), jnp.int16(0))
            tiles = [hit[r:r + BF16_ROWS, :] for r in range(0, tk, BF16_ROWS)]
            while len(tiles) > 1:
                tiles = [a + b for a, b in zip(tiles[::2], tiles[1::2])]
            return part + tiles[0]
        part = lax.fori_loop(0, n_proc, body, jnp.zeros((BF16_ROWS, tq), I16))
        return jnp.sum(part.astype(I32), axis=0, keepdims=True)

    def largest_half(need):
        def bit(i, u):
            cand_u = u | jnp.left_shift(jnp.int32(1), 15 - i)
            return jnp.where(count_half(cand_u - 32768) >= need, cand_u, u)
        return lax.fori_loop(0, 16, bit, jnp.zeros((1, tq), I32)) - 32768

    thr_hi = largest_half(top_k)
    above = count_half(thr_hi, strict=True)

    def low_chunk(kc, _):
        key = keys_ref[kc]
        low = jnp.bitwise_and(key, 0xFFFF) - 32768
        half_ref[kc] = jnp.where(jnp.right_shift(key, 16) == thr_hi, low, -32768).astype(I16)
        return 0

    lax.fori_loop(0, n_proc, low_chunk, 0)
    thr_lo = largest_half(top_k - above)
    cnt_ge = above + count_half(thr_lo)
    thr = jnp.left_shift(thr_hi, 16) + (thr_lo + 32768)

    def tie_cut():
        need = top_k - count(lambda key, pos: key > thr)

        def index_bit(i, x):
            cand = x | jnp.left_shift(jnp.int32(1), idx_bits - 1 - i)
            cnt = count(lambda key, pos: (key == thr) & (pos < cand))
            return jnp.where(cnt < need, cand, x)
        return lax.fori_loop(0, idx_bits, index_bit, jnp.zeros((1, tq), I32))

    last_tie = lax.cond(jnp.max(cnt_ge) > top_k, tie_cut, lambda: jnp.full((1, tq), 2 ** idx_bits, I32))

    def write_chunk(kc, _):
        key = keys_ref[kc]
        pos = kc * tk + key_row
        chosen = (key > thr) | ((key == thr) & (pos <= last_tie))
        rows = pl.ds(pl.multiple_of(kc * tk, tk), tk)
        o_ref[rows, :] = jnp.where(chosen & (pos <= q_pos), 0.0, NEG).astype(o_ref.dtype)
        return 0

    lax.fori_loop(0, n_proc, write_chunk, 0)

    def fill_chunk(kc, _):
        o_ref[pl.ds(pl.multiple_of(kc * tk, tk), tk), :] = jnp.full((tk, tq), NEG, o_ref.dtype)
        return 0

    lax.fori_loop(n_proc, n_kc, fill_chunk, 0)


def _dsa_select(qi_t, widx_t, kk):
    b, seq, _ = kk.shape
    t = BLK
    n_kc = seq // t
    top_k = min(DSA_TOPK_MAX, seq // 4)
    idx_bits = max(1, (seq - 1).bit_length())
    return pl.pallas_call(
        functools.partial(_dsa_select_kernel, top_k=top_k, idx_bits=idx_bits),
        grid=(b, seq // t),
        in_specs=[pl.BlockSpec((None, qi_t.shape[1], t), lambda bi, c: (bi, 0, c)),
                  pl.BlockSpec((None, IDX_HEADS, t), lambda bi, c: (bi, 0, c)),
                  pl.BlockSpec((None, seq, IDX_DIM), lambda bi, c: (bi, 0, 0))],
        out_specs=pl.BlockSpec((None, None, seq, t), lambda bi, c: (bi, c, 0, 0)),
        out_shape=jax.ShapeDtypeStruct((b, seq // t, seq, t), BF16),
        scratch_shapes=[pltpu.VMEM((n_kc, t, t), I32),
                        pltpu.VMEM((n_kc, t, t), I16)],
        compiler_params=_cparams("parallel", "arbitrary"),
        name="dsa_select",
    )(qi_t, widx_t, kk)


def _dsa_attn_kernel(qnt_ref, ckv_ref, vt_ref, mb_ref, bt_ref, wukt_ref, o_ref,
                     qlat_ref, acc_ref, m_ref, l_ref, s_ref, *, n_heads):
    qt = pl.program_id(1)
    t = BLK
    n_grp = n_heads // DSA_GRP
    row = lax.broadcasted_iota(I32, (LANES, t), 0)
    ones_rows = jnp.ones((BF16_ROWS, t), BF16)

    for hp in range(n_heads // 2):
        qp = qnt_ref[hp * LANES:(hp + 1) * LANES, :]
        for hh in range(2):
            qh = jnp.where((row < HEAD_DIM) if hh == 0 else (row >= HEAD_DIM), qp, jnp.zeros_like(qp))
            h = 2 * hp + hh
            qlat_ref[:, h * t:(h + 1) * t] = _dot(wukt_ref[hp], qh).astype(BF16)

    m_ref[...] = jnp.full(m_ref.shape, NEG, F32)
    l_ref[...] = jnp.zeros_like(l_ref)
    acc_ref[...] = jnp.zeros_like(acc_ref)

    def grp_cols(g):
        return slice(g * DSA_GRP * t, (g + 1) * DSA_GRP * t)

    def key_rows(kk):
        return pl.ds(pl.multiple_of(kk * t, t), t)

    def fill(slot, kk, bias_row0):
        kt = ckv_ref[key_rows(kk), :]
        mb = mb_ref[key_rows(kk), :].astype(F32)
        for g in range(n_grp):
            s_grp = _dot(kt, qlat_ref[:, grp_cols(g)])
            parts = []
            for i in range(DSA_GRP):
                s = s_grp[:, i * t:(i + 1) * t] + mb
                if bias_row0 is not None:
                    s = s + bt_ref[g * DSA_GRP + i, bias_row0:bias_row0 + t, :]
                parts.append(s)
            s_ref[slot, :, grp_cols(g)] = jnp.concatenate(parts, axis=1)

    def drain(slot, kk):
        for g in range(n_grp):
            cols = grp_cols(g)
            m_old = m_ref[:, cols]
            m_new = jnp.maximum(m_old, jnp.max(s_ref[slot, :, cols], axis=0, keepdims=True))
            alpha = jnp.exp2(m_old - m_new)
            pr = jnp.exp2(s_ref[slot, :, cols] - m_new).astype(BF16)
            m_ref[:, cols] = m_new
            for i in range(DSA_GRP):
                h = g * DSA_GRP + i
                feats = slice(h * HEAD_DIM, (h + 1) * HEAD_DIM)
                hc = slice(i * t, (i + 1) * t)
                lc = slice(h * t, (h + 1) * t)
                pv = _dot(jnp.concatenate([vt_ref[kk, feats, :], ones_rows], axis=0), pr[:, hc])
                acc_ref[feats, :] = alpha[:, hc] * acc_ref[feats, :] + pv[0:HEAD_DIM, :]
                l_ref[:, lc] = alpha[:, hc] * l_ref[:, lc] + pv[HEAD_DIM:HEAD_DIM + 1, :]

    fill(0, qt, BLK)
    drain(0, qt)

    @pl.when(qt >= 1)
    def _():
        fill(1, qt - 1, 0)
        drain(1, qt - 1)

    n_far = jnp.maximum(qt - 1, 0)

    @pl.when(n_far > 0)
    def _():
        fill(0, 0, None)

    def far_pair(i, _):
        kk = 2 * i
        fill(1, kk + 1, None)
        drain(0, kk)
        fill(0, jnp.minimum(kk + 2, n_far - 1), None)
        drain(1, kk + 1)
        return 0

    lax.fori_loop(0, n_far // 2, far_pair, 0)

    @pl.when(n_far % 2 == 1)
    def _():
        drain(0, n_far - 1)

    for h in range(n_heads):
        feats = slice(h * HEAD_DIM, (h + 1) * HEAD_DIM)
        acc_ref[feats, :] = acc_ref[feats, :] / l_ref[:, h * t:(h + 1) * t]
    o_ref[...] = acc_ref[...].T.astype(o_ref.dtype)


def _dsa_attention(qn_t, ckv, v_t, mask_t, bt, w_uk_t, n_heads):
    b, seq, _ = ckv.shape
    t = BLK
    d_q = qn_t.shape[1]
    return pl.pallas_call(
        functools.partial(_dsa_attn_kernel, n_heads=n_heads),
        grid=(b, seq // t),
        in_specs=[pl.BlockSpec((None, d_q, t), lambda bi, q: (bi, 0, q)),
                  pl.BlockSpec((None, seq, DSA_LORA), lambda bi, q: (bi, 0, 0)),
                  pl.BlockSpec((None, seq // t, d_q, t), lambda bi, q: (bi, 0, 0, 0)),
                  pl.BlockSpec((None, None, seq, t), lambda bi, q: (bi, q, 0, 0)),
                  _full(bt), _full(w_uk_t)],
        out_specs=pl.BlockSpec((None, t, d_q), lambda bi, q: (bi, q, 0)),
        out_shape=jax.ShapeDtypeStruct((b, seq, d_q), BF16),
        scratch_shapes=[pltpu.VMEM((DSA_LORA, n_heads * t), BF16),
                        pltpu.VMEM((d_q, t), F32),
                        pltpu.VMEM((1, n_heads * t), F32),
                        pltpu.VMEM((1, n_heads * t), F32),
                        pltpu.VMEM((2, t, n_heads * t), F32)],
        compiler_params=_cparams("parallel", "arbitrary"),
        name="dsa_attn",
    )(qn_t, ckv, v_t, mask_t, bt, w_uk_t)


def _moba_layer(x2, batch, seq, g, w_qkv, bt, n_heads):
    d = x2.shape[1]
    wq_t = (w_qkv[:, :d] * (HEAD_DIM ** -0.5 * LOG2E)).T.astype(BF16)
    wk = w_qkv[:, d:2 * d].astype(BF16)
    wv_t = w_qkv[:, 2 * d:].T.astype(BF16)
    q_t, k, v_t = _moba_proj(x2.reshape(batch, seq, d), g, wq_t, wk, wv_t)
    attn = _moba_attention(q_t, k, v_t, bt, n_heads)
    return attn.reshape(batch * seq, d)


def _dsa_layer(x2, batch, seq, g, w_in, g_q, g_kv, w_uq, w_qi, w_uk, w_uv, bt, n_heads):
    d = x2.shape[1]
    k_lo = 2 * DSA_LORA
    w_in_row = jnp.concatenate([w_in[:, :k_lo + IDX_DIM], jnp.zeros((d, LANES - IDX_DIM), F32)], axis=1).astype(BF16)
    w_kv_t = w_in[:, DSA_LORA:k_lo].T.astype(BF16)
    w_w_t = jnp.concatenate([w_in[:, k_lo + IDX_DIM:].T, jnp.zeros((BF16_ROWS - IDX_HEADS, d), F32)]).astype(BF16)
    w_uq_t = (w_uq * (HEAD_DIM ** -0.5 * LOG2E)).T.astype(BF16)
    w_uv_t = jnp.transpose(w_uv, (0, 2, 1)).reshape(n_heads * HEAD_DIM, DSA_LORA).astype(BF16)
    ckv, v_t, kk, widx_t, qn_t, qi_t = _dsa_proj(x2.reshape(batch, seq, d), g, w_in_row, w_kv_t, w_w_t,
                                                 g_q, g_kv, w_uq_t, w_qi.T.astype(BF16), w_uv_t)
    mask_t = _dsa_select(qi_t, widx_t, kk)
    w_uk_t = jnp.transpose(w_uk.reshape(n_heads // 2, 2 * HEAD_DIM, DSA_LORA), (0, 2, 1)).astype(BF16)
    attn = _dsa_attention(qn_t, ckv, v_t, mask_t, bt[:, :2 * BLK], w_uk_t, n_heads)
    return attn.reshape(batch * seq, -1)


def kernel(x, rel_bias, ln_attn, ln_mlp, moba_w_qkv, moba_w_o, dsa_w_in, dsa_g_q, dsa_g_kv, dsa_w_uq, dsa_w_qi,
           dsa_w_uk, dsa_w_uv, dsa_w_o, mlp_w_up, mlp_w_down, final_norm):
    batch, seq, d = x.shape
    n_heads = rel_bias.shape[1]
    depth = ln_attn.shape[0]
    bt = _bias_tiles(rel_bias)
    x2 = x.reshape(batch * seq, d)
    for i in range(depth):
        j = i // 2
        if i % 2 == 0:
            attn, w_o = _moba_layer(x2, batch, seq, ln_attn[i], moba_w_qkv[j], bt, n_heads), moba_w_o[j]
        else:
            attn, w_o = _dsa_layer(x2, batch, seq, ln_attn[i], dsa_w_in[j], dsa_g_q[j], dsa_g_kv[j], dsa_w_uq[j],
                                   dsa_w_qi[j], dsa_w_uk[j], dsa_w_uv[j], bt, n_heads), dsa_w_o[j]
        x2 = _block_tail(x2, attn, w_o.astype(BF16), ln_mlp[i], mlp_w_up[i].astype(BF16), mlp_w_down[i].astype(BF16),
                         g_final=final_norm if i == depth - 1 else None)
    return x2.reshape(batch, seq, d)
```

```python
import functools
import math

import numpy as np
import jax
import jax.numpy as jnp
from jax import lax
from jax.experimental import pallas as pl
from jax.experimental.pallas import tpu as pltpu

F32 = jnp.float32
BF16 = jnp.bfloat16
I32 = jnp.int32
I16 = jnp.int16

EPS = 1e-6
NEG = -1e30
INT_MIN = -(2 ** 31)
LOG2E = 1.0 / math.log(2.0)

NUM_BUCKETS = 32
MAX_DISTANCE = 128
HEAD_DIM = 64
LANES = 128
BF16_ROWS = 16
BLK = 256
MOBA_TOPK = 3
MOBA_GRP = 8
DSA_TOPK_MAX = 256
DSA_LORA = 256
DSA_GRP = 4
IDX_HEADS = 8
IDX_DIM = 64
VMEM_LIMIT = 56 * 1024 * 1024


def _dot(a, b):
    return jnp.dot(a, b, preferred_element_type=F32)


def _dot_nt(a, b):
    return lax.dot_general(a, b, (((1,), (1,)), ((), ())), preferred_element_type=F32)


def _rms(x, g):
    return x * lax.rsqrt(jnp.mean(x * x, axis=-1, keepdims=True) + EPS) * g


def _cparams(*sem):
    return pltpu.CompilerParams(dimension_semantics=sem, vmem_limit_bytes=VMEM_LIMIT)


def _full(a):
    return pl.BlockSpec(a.shape, lambda *_: (0,) * a.ndim, pipeline_mode=pl.Buffered(1))


def _bucket_map_t():
    jj = np.arange(2 * BLK)[:, None]
    i = np.arange(BLK)[None, :]
    d = i - jj + BLK
    n = np.maximum(d, 0)
    exact = NUM_BUCKETS // 2
    nf = np.maximum(n, 1).astype(np.float32)
    large = exact + (np.log(nf / exact) / math.log(MAX_DISTANCE / exact) * (NUM_BUCKETS - exact)).astype(np.int32)
    large = np.minimum(large, NUM_BUCKETS - 1)
    bucket = np.where(n < exact, n, large)
    bmap = np.where(d >= 0, bucket, -1).astype(np.int32)
    return np.concatenate([bmap, bmap[:BLK]], axis=0)


def _bias_tile_kernel(tab_ref, bmap_ref, o_ref):
    h = pl.program_id(0)
    bm = bmap_ref[...]
    base = tab_ref[NUM_BUCKETS - 1, h]
    acc = jnp.full(bm.shape, NEG, F32)
    for b in range(NUM_BUCKETS):
        acc = jnp.where(bm == b, (tab_ref[b, h] - base) * LOG2E, acc)
    o_ref[...] = acc


def _bias_tiles(rel_bias):
    n_heads = rel_bias.shape[1]
    bmap = jnp.asarray(_bucket_map_t())
    return pl.pallas_call(
        _bias_tile_kernel,
        grid=(n_heads,),
        in_specs=[pl.BlockSpec(memory_space=pltpu.SMEM),
                  pl.BlockSpec((3 * BLK, BLK), lambda h: (0, 0))],
        out_specs=pl.BlockSpec((None, 3 * BLK, BLK), lambda h: (h, 0, 0)),
        out_shape=jax.ShapeDtypeStruct((n_heads, 3 * BLK, BLK), F32),
        name="bias_tiles",
    )(rel_bias, bmap)


def _block_tail_kernel(x_ref, a_ref, wo_ref, g_ref, wu_ref, wd_ref, *rest, final):
    if final:
        gf_ref, o_ref, x1_ref, xn_ref, acc_ref = rest
    else:
        o_ref, x1_ref, xn_ref, acc_ref = rest
    f = pl.program_id(1)

    @pl.when(f == 0)
    def _():
        x1 = x_ref[...] + _dot(a_ref[...], wo_ref[...])
        x1_ref[...] = x1
        xn_ref[...] = _rms(x1, g_ref[...]).astype(BF16)
        acc_ref[...] = jnp.zeros_like(acc_ref)

    h = jnp.square(jnp.maximum(_dot(xn_ref[...], wu_ref[...]), 0.0)).astype(BF16)
    acc_ref[...] += _dot(h, wd_ref[...])

    @pl.when(f == pl.num_programs(1) - 1)
    def _():
        out = x1_ref[...] + acc_ref[...]
        o_ref[...] = _rms(out, gf_ref[...]) if final else out


def _block_tail(x2, attn, w_o, g, w_up, w_down, g_final=None, tm=1024, tf=1024):
    n, d = x2.shape
    k = attn.shape[1]
    d_ff = w_up.shape[1]
    final = g_final is not None
    row = lambda w: pl.BlockSpec((tm, w), lambda i, f: (i, 0))
    vec = pl.BlockSpec((1, d), lambda i, f: (0, 0))
    in_specs = [row(d), row(k), pl.BlockSpec((k, d), lambda i, f: (0, 0), pipeline_mode=pl.Buffered(1)), vec,
                pl.BlockSpec((d, tf), lambda i, f: (0, f)), pl.BlockSpec((tf, d), lambda i, f: (f, 0))]
    args = [x2, attn, w_o, g.reshape(1, d), w_up, w_down]
    if final:
        in_specs.append(vec)
        args.append(g_final.reshape(1, d))
    return pl.pallas_call(
        functools.partial(_block_tail_kernel, final=final),
        grid=(n // tm, d_ff // tf),
        in_specs=in_specs,
        out_specs=row(d),
        out_shape=jax.ShapeDtypeStruct((n, d), F32),
        scratch_shapes=[pltpu.VMEM((tm, d), F32), pltpu.VMEM((tm, d), BF16), pltpu.VMEM((tm, d), F32)],
        compiler_params=_cparams("parallel", "arbitrary"),
        name="block_tail",
    )(*args)


def _moba_proj_kernel(x_ref, g_ref, wqt_ref, wk_ref, wvt_ref, qt_ref, k_ref, vt_ref):
    xn = _rms(x_ref[...], g_ref[...]).astype(BF16)
    qt_ref[...] = _dot_nt(wqt_ref[...], xn).astype(BF16)
    k_ref[...] = _dot(xn, wk_ref[...]).astype(BF16)
    vt_ref[...] = _dot_nt(wvt_ref[...], xn).astype(BF16)


def _moba_proj(x3, g, wq_t, wk, wv_t):
    b, seq, d = x3.shape
    g2 = g.reshape(1, d)
    return pl.pallas_call(
        _moba_proj_kernel,
        grid=(b, seq // BLK),
        in_specs=[pl.BlockSpec((None, BLK, d), lambda bi, i: (bi, i, 0)),
                  _full(g2), _full(wq_t), _full(wk), _full(wv_t)],
        out_specs=[pl.BlockSpec((None, d, BLK), lambda bi, i: (bi, 0, i)),
                   pl.BlockSpec((None, BLK, d), lambda bi, i: (bi, i, 0)),
                   pl.BlockSpec((None, None, d, BLK), lambda bi, i: (bi, i, 0, 0))],
        out_shape=[jax.ShapeDtypeStruct((b, d, seq), BF16),
                   jax.ShapeDtypeStruct((b, seq, d), BF16),
                   jax.ShapeDtypeStruct((b, seq // BLK, d, BLK), BF16)],
        compiler_params=_cparams("parallel", "parallel"),
        name="moba_proj",
    )(x3, g2, wq_t, wk, wv_t)


def _moba_kernel(qt_ref, k_ref, vt_ref, bt_ref, o_ref, kaug_ref, kmean_ref, qfar_ref, m_ref, l_ref, acc_ref, s_ref,
                 *, n_blk, k_sel, sel_rows):
    j = pl.program_id(2)
    seq = k_ref.shape[0]
    n_pairs = MOBA_GRP // 2
    t = BLK
    row = lax.broadcasted_iota(I32, (LANES, t), 0)
    blk_id = lax.broadcasted_iota(I32, (sel_rows, t), 0)
    ones_2 = jnp.ones((BF16_ROWS, 2 * t), BF16)

    @pl.when(j == 0)
    def _():
        krow = lax.broadcasted_iota(I32, (seq, LANES), 0)
        kcol = lax.broadcasted_iota(I32, (seq, LANES), 1)
        onehot = jnp.where(krow // BLK == kcol, 1.0, 0.0).astype(BF16)
        for p in range(n_pairs):
            kaug_ref[p, :, 0:LANES] = k_ref[:, p * LANES:(p + 1) * LANES]
            kaug_ref[p, :, LANES:2 * LANES] = onehot
            kmean_ref[p] = jnp.zeros((LANES, LANES), F32)
            for n in range(n_blk):
                kmean_ref[p, n:n + 1, :] = jnp.mean(
                    k_ref[n * BLK:(n + 1) * BLK, p * LANES:(p + 1) * LANES].astype(F32), axis=0, keepdims=True)

    near_blk = jnp.maximum(j - 1, 0)
    near0 = pl.multiple_of(near_blk * BLK, BLK)
    bias0 = pl.multiple_of(jnp.where(j == 0, BLK, 0), BLK)
    pad_rows = jnp.zeros((LANES - sel_rows, t), BF16)

    def drain(slot, blk0):
        for p in range(n_pairs):
            feats = slice(p * LANES, (p + 1) * LANES)
            m_old = m_ref[p]
            m_new = jnp.maximum(m_old, jnp.max(s_ref[slot, p], axis=0, keepdims=True))
            alpha = jnp.exp2(m_old - m_new)
            pr = jnp.exp2(s_ref[slot, p] - m_new).astype(BF16)
            m_ref[p] = m_new
            v_t = jnp.concatenate([vt_ref[blk0, feats, :], vt_ref[blk0 + 1, feats, :]], axis=1)
            pv = _dot(jnp.concatenate([v_t, ones_2], axis=0), pr)
            acc_ref[p] = alpha * acc_ref[p] + pv[0:LANES, :]
            l_ref[p] = alpha * l_ref[p] + pv[LANES:LANES + 1, :]

    def fill_far(slot, pi):
        n0 = pl.multiple_of(pi * 2 * BLK, 2 * BLK)
        for p in range(n_pairs):
            s_ref[slot, p] = _dot(kaug_ref[p, pl.ds(n0, 2 * BLK), :], qfar_ref[p])

    q_near = []
    for p in range(n_pairs):
        qp = qt_ref[p * LANES:(p + 1) * LANES, :]
        near, far = [], []
        for hh in range(2):
            qh = jnp.where((row < HEAD_DIM) if hh == 0 else (row >= HEAD_DIM), qp, jnp.zeros_like(qp))
            gate = _dot(kmean_ref[p], qh.astype(F32))[0:sel_rows, :]
            g = jnp.where(blk_id < j, gate, NEG)
            g = jnp.where(blk_id < n_blk, g, -jnp.inf)
            sel_bias = jnp.full((sel_rows, t), NEG, F32)
            for it in range(k_sel):
                mx = jnp.max(g, axis=0, keepdims=True)
                first = jnp.min(jnp.where(g == mx, blk_id, LANES), axis=0, keepdims=True)
                pick = blk_id == first
                sel_bias = jnp.where(pick, jnp.where(it < j, 0.0, NEG), sel_bias)
                g = jnp.where(pick, -jnp.inf, g)
            near_bias = jnp.where(blk_id == j, 0.0, sel_bias)
            far_bias = jnp.where(blk_id < j - 1, sel_bias, NEG)
            near.append(jnp.concatenate([qh, near_bias.astype(BF16), pad_rows], axis=0))
            far.append(jnp.concatenate([qh, far_bias.astype(BF16), pad_rows], axis=0))
        q_near.append(jnp.concatenate(near, axis=1))
        qfar_ref[p] = jnp.concatenate(far, axis=1)
        m_ref[p] = jnp.full((1, 2 * t), NEG, F32)
        l_ref[p] = jnp.zeros((1, 2 * t), F32)
        acc_ref[p] = jnp.zeros((LANES, 2 * t), F32)

    for p in range(n_pairs):
        bias = jnp.concatenate([bt_ref[2 * p, pl.ds(bias0, 2 * BLK), :], bt_ref[2 * p + 1, pl.ds(bias0, 2 * BLK), :]],
                               axis=1)
        s_ref[0, p] = _dot(kaug_ref[p, pl.ds(near0, 2 * BLK), :], q_near[p]) + bias

    n_far = jnp.maximum(j, 1) // 2
    fill_far(1, 0)
    drain(0, near_blk)

    def far_two(i, _):
        pi = 2 * i
        fill_far(0, pi + 1)
        drain(1, 2 * pi)
        fill_far(1, jnp.minimum(pi + 2, n_far - 1))
        drain(0, 2 * pi + 2)
        return 0

    lax.fori_loop(0, n_far // 2, far_two, 0)

    @pl.when(n_far % 2 == 1)
    def _():
        drain(1, 2 * (n_far - 1))

    for p in range(n_pairs):
        acc = acc_ref[p] / l_ref[p]
        out_t = jnp.where(row < HEAD_DIM, acc[:, 0:t], acc[:, t:2 * t])
        o_ref[:, p * LANES:(p + 1) * LANES] = out_t.T.astype(o_ref.dtype)


def _moba_attention(q_t, k, v_t, bt, n_heads):
    b, seq, d = k.shape
    n_blk = seq // BLK
    n_grp = n_heads // MOBA_GRP
    n_pairs = MOBA_GRP // 2
    w = MOBA_GRP * HEAD_DIM
    k_sel = max(1, min(MOBA_TOPK, n_blk - 1))
    sel_rows = BF16_ROWS * pl.cdiv(n_blk, BF16_ROWS)
    assert sel_rows <= LANES and n_blk % 2 == 0
    return pl.pallas_call(
        functools.partial(_moba_kernel, n_blk=n_blk, k_sel=k_sel, sel_rows=sel_rows),
        grid=(b, n_grp, n_blk),
        in_specs=[pl.BlockSpec((None, w, BLK), lambda bi, g, j: (bi, g, j)),
                  pl.BlockSpec((None, seq, w), lambda bi, g, j: (bi, 0, g)),
                  pl.BlockSpec((None, n_blk, w, BLK), lambda bi, g, j: (bi, 0, g, 0)),
                  pl.BlockSpec((MOBA_GRP, 3 * BLK, BLK), lambda bi, g, j: (g, 0, 0), pipeline_mode=pl.Buffered(1))],
        out_specs=pl.BlockSpec((None, BLK, w), lambda bi, g, j: (bi, j, g)),
        out_shape=jax.ShapeDtypeStruct((b, seq, d), BF16),
        scratch_shapes=[pltpu.VMEM((n_pairs, seq, 2 * LANES), BF16),
                        pltpu.VMEM((n_pairs, LANES, LANES), F32),
                        pltpu.VMEM((n_pairs, 2 * LANES, 2 * BLK), BF16),
                        pltpu.VMEM((n_pairs, 1, 2 * BLK), F32),
                        pltpu.VMEM((n_pairs, 1, 2 * BLK), F32),
                        pltpu.VMEM((n_pairs, LANES, 2 * BLK), F32),
                        pltpu.VMEM((2, n_pairs, 2 * BLK, 2 * BLK), F32)],
        compiler_params=_cparams("parallel", "parallel", "arbitrary"),
        name="moba_attn",
    )(q_t, k, v_t, bt)


def _dsa_proj_kernel(x_ref, g_ref, win_ref, wkvt_ref, wwt_ref, gq_ref, gkv_ref, gkvt_ref, wuqt_ref, wqit_ref,
                     wuvt_ref, ckv_ref, vt_ref, kk_ref, widx_ref, qnt_ref, qit_ref, *, idx_scale):
    hn = _rms(x_ref[...], g_ref[...]).astype(BF16)
    proj = _dot(hn, win_ref[...])
    c_q = _rms(proj[:, 0:DSA_LORA], gq_ref[...]).astype(BF16)
    ckv_ref[...] = _rms(proj[:, DSA_LORA:2 * DSA_LORA], gkv_ref[...]).astype(BF16)
    kk_ref[...] = proj[:, 2 * DSA_LORA:2 * DSA_LORA + IDX_DIM].astype(BF16)
    ct = _dot_nt(wkvt_ref[...], hn)
    scale = lax.rsqrt(jnp.mean(ct * ct, axis=0, keepdims=True) + EPS)
    gt = jnp.concatenate([gkvt_ref[...]] * (ct.shape[1] // LANES), axis=1)
    vt_ref[...] = _dot(wuvt_ref[...], (ct * scale * gt).astype(BF16)).astype(BF16)
    widx_ref[...] = _dot_nt(wwt_ref[...], hn)[0:IDX_HEADS, :] * idx_scale
    qnt_ref[...] = _dot_nt(wuqt_ref[...], c_q).astype(BF16)
    qit_ref[...] = _dot_nt(wqit_ref[...], c_q).astype(BF16)


def _dsa_proj(x3, g, w_in_row, w_kv_t, w_w_t, g_q, g_kv, w_uq_t, w_qi_t, w_uv_t):
    b, seq, d = x3.shape
    tm = BLK
    g2, gq2, gkv2 = g.reshape(1, d), g_q.reshape(1, -1), g_kv.reshape(1, -1)
    gkv_t = jnp.broadcast_to(g_kv.reshape(-1, 1), (DSA_LORA, LANES))
    n_q, n_i, n_v = w_uq_t.shape[0], w_qi_t.shape[0], w_uv_t.shape[0]
    consts = (g2, w_in_row, w_kv_t, w_w_t, gq2, gkv2, gkv_t, w_uq_t, w_qi_t, w_uv_t)
    idx_scale = IDX_HEADS ** -0.5 * IDX_DIM ** -0.5
    return pl.pallas_call(
        functools.partial(_dsa_proj_kernel, idx_scale=idx_scale),
        grid=(b, seq // tm),
        in_specs=[pl.BlockSpec((None, tm, d), lambda bi, i: (bi, i, 0))] + [_full(a) for a in consts],
        out_specs=[pl.BlockSpec((None, tm, DSA_LORA), lambda bi, i: (bi, i, 0)),
                   pl.BlockSpec((None, None, n_v, tm), lambda bi, i: (bi, i, 0, 0)),
                   pl.BlockSpec((None, tm, IDX_DIM), lambda bi, i: (bi, i, 0)),
                   pl.BlockSpec((None, IDX_HEADS, tm), lambda bi, i: (bi, 0, i)),
                   pl.BlockSpec((None, n_q, tm), lambda bi, i: (bi, 0, i)),
                   pl.BlockSpec((None, n_i, tm), lambda bi, i: (bi, 0, i))],
        out_shape=[jax.ShapeDtypeStruct((b, seq, DSA_LORA), BF16),
                   jax.ShapeDtypeStruct((b, seq // tm, n_v, tm), BF16),
                   jax.ShapeDtypeStruct((b, seq, IDX_DIM), BF16),
                   jax.ShapeDtypeStruct((b, IDX_HEADS, seq), F32),
                   jax.ShapeDtypeStruct((b, n_q, seq), BF16),
                   jax.ShapeDtypeStruct((b, n_i, seq), BF16)],
        compiler_params=_cparams("parallel", "parallel"),
        name="dsa_proj",
    )(x3, *consts)


def _dsa_select_kernel(qit_ref, w_ref, kk_ref, o_ref, keys_ref, half_ref, *, top_k, idx_bits):
    c = pl.program_id(1)
    n_kc, tk, tq = keys_ref.shape
    n_proc = c + 1
    key_row = lax.broadcasted_iota(I32, (tk, tq), 0)
    q_pos = c * tq + lax.broadcasted_iota(I32, (tk, tq), 1)
    w = w_ref[...]

    def score_chunk(kc, _):
        kt = kk_ref[pl.ds(pl.multiple_of(kc * tk, tk), tk), :]
        acc = jnp.zeros((tk, tq), F32)
        for h in range(IDX_HEADS):
            x = _dot(kt, qit_ref[h * IDX_DIM:(h + 1) * IDX_DIM, :])
            acc = acc + w[h:h + 1, :] * jnp.maximum(x, 0.0)
        sc = jnp.where(kc * tk + key_row <= q_pos, acc, NEG)
        bits = pltpu.bitcast(sc, I32)
        key = jnp.where(bits < 0, bits ^ 0x7FFFFFFF, bits)
        keys_ref[kc] = key
        half_ref[kc] = jnp.right_shift(key, 16).astype(I16)
        return 0

    lax.fori_loop(0, n_proc, score_chunk, 0)

    def count(pred):
        def body(kc, tot):
            hit = jnp.where(pred(keys_ref[kc], kc * tk + key_row), 1, 0)
            return tot + jnp.sum(hit, axis=0, keepdims=True)
        return lax.fori_loop(0, n_proc, body, jnp.zeros((1, tq), I32))

    def count_half(cand, strict=False):
        cand16 = cand.astype(I16)

        def body(kc, part):
            x = half_ref[kc]
            hit = jnp.where((x > cand16) if strict else (x >= cand16), jnp.int16(1), jnp.int16(0))
            tiles = [hit[r:r + BF16_ROWS, :] for r in range(0, tk, BF16_ROWS)]
            while len(tiles) > 1:
                tiles = [a + b for a, b in zip(tiles[::2], tiles[1::2])]
            return part + tiles[0]
        part = lax.fori_loop(0, n_proc, body, jnp.zeros((BF16_ROWS, tq), I16))
        return jnp.sum(part.astype(I32), axis=0, keepdims=True)

    def largest_half(need):
        def bit(i, u):
            cand_u = u | jnp.left_shift(jnp.int32(1), 15 - i)
            return jnp.where(count_half(cand_u - 32768) >= need, cand_u, u)
        return lax.fori_loop(0, 16, bit, jnp.zeros((1, tq), I32)) - 32768

    thr_hi = largest_half(top_k)
    above = count_half(thr_hi, strict=True)

    def low_chunk(kc, _):
        key = keys_ref[kc]
        low = jnp.bitwise_and(key, 0xFFFF) - 32768
        half_ref[kc] = jnp.where(jnp.right_shift(key, 16) == thr_hi, low, -32768).astype(I16)
        return 0

    lax.fori_loop(0, n_proc, low_chunk, 0)
    thr_lo = largest_half(top_k - above)
    cnt_ge = above + count_half(thr_lo)
    thr = jnp.left_shift(thr_hi, 16) + (thr_lo + 32768)

    def tie_cut():
        need = top_k - count(lambda key, pos: key > thr)

        def index_bit(i, x):
            cand = x | jnp.left_shift(jnp.int32(1), idx_bits - 1 - i)
            cnt = count(lambda key, pos: (key == thr) & (pos < cand))
            return jnp.where(cnt < need, cand, x)
        return lax.fori_loop(0, idx_bits, index_bit, jnp.zeros((1, tq), I32))

    last_tie = lax.cond(jnp.max(cnt_ge) > top_k, tie_cut, lambda: jnp.full((1, tq), 2 ** idx_bits, I32))

    def write_chunk(kc, _):
        key = keys_ref[kc]
        pos = kc * tk + key_row
        chosen = (key > thr) | ((key == thr) & (pos <= last_tie))
        rows = pl.ds(pl.multiple_of(kc * tk, tk), tk)
        o_ref[rows, :] = jnp.where(chosen & (pos <= q_pos), 0.0, NEG).astype(o_ref.dtype)
        return 0

    lax.fori_loop(0, n_proc, write_chunk, 0)

    def fill_chunk(kc, _):
        o_ref[pl.ds(pl.multiple_of(kc * tk, tk), tk), :] = jnp.full((tk, tq), NEG, o_ref.dtype)
        return 0

    lax.fori_loop(n_proc, n_kc, fill_chunk, 0)


def _dsa_select(qi_t, widx_t, kk):
    b, seq, _ = kk.shape
    t = BLK
    n_kc = seq // t
    top_k = min(DSA_TOPK_MAX, seq // 4)
    idx_bits = max(1, (seq - 1).bit_length())
    return pl.pallas_call(
        functools.partial(_dsa_select_kernel, top_k=top_k, idx_bits=idx_bits),
        grid=(b, seq // t),
        in_specs=[pl.BlockSpec((None, qi_t.shape[1], t), lambda bi, c: (bi, 0, c)),
                  pl.BlockSpec((None, IDX_HEADS, t), lambda bi, c: (bi, 0, c)),
                  pl.BlockSpec((None, seq, IDX_DIM), lambda bi, c: (bi, 0, 0))],
        out_specs=pl.BlockSpec((None, None, seq, t), lambda bi, c: (bi, c, 0, 0)),
        out_shape=jax.ShapeDtypeStruct((b, seq // t, seq, t), BF16),
        scratch_shapes=[pltpu.VMEM((n_kc, t, t), I32),
                        pltpu.VMEM((n_kc, t, t), I16)],
        compiler_params=_cparams("parallel", "arbitrary"),
        name="dsa_select",
    )(qi_t, widx_t, kk)


def _dsa_attn_kernel(qnt_ref, ckv_ref, vt_ref, mb_ref, bt_ref, wukt_ref, o_ref,
                     qlat_ref, acc_ref, m_ref, l_ref, s_ref, *, n_heads):
    qt = pl.program_id(1)
    t = BLK
    n_grp = n_heads // DSA_GRP
    row = lax.broadcasted_iota(I32, (LANES, t), 0)
    ones_rows = jnp.ones((BF16_ROWS, t), BF16)

    for hp in range(n_heads // 2):
        qp = qnt_ref[hp * LANES:(hp + 1) * LANES, :]
        for hh in range(2):
            qh = jnp.where((row < HEAD_DIM) if hh == 0 else (row >= HEAD_DIM), qp, jnp.zeros_like(qp))
            h = 2 * hp + hh
            qlat_ref[:, h * t:(h + 1) * t] = _dot(wukt_ref[hp], qh).astype(BF16)

    m_ref[...] = jnp.full(m_ref.shape, NEG, F32)
    l_ref[...] = jnp.zeros_like(l_ref)
    acc_ref[...] = jnp.zeros_like(acc_ref)

    def grp_cols(g):
        return slice(g * DSA_GRP * t, (g + 1) * DSA_GRP * t)

    def key_rows(kk):
        return pl.ds(pl.multiple_of(kk * t, t), t)

    def fill(slot, kk, bias_row0, off=None):
        kt = ckv_ref[key_rows(kk), :]
        mb = mb_ref[key_rows(kk), :].astype(F32)
        if off is not None:
            mb = mb + off
        for g in range(n_grp):
            s_grp = _dot(kt, qlat_ref[:, grp_cols(g)])
            parts = []
            for i in range(DSA_GRP):
                s = s_grp[:, i * t:(i + 1) * t] + mb
                if bias_row0 is not None:
                    s = s + bt_ref[g * DSA_GRP + i, bias_row0:bias_row0 + t, :]
                parts.append(s)
            s_ref[slot, :, grp_cols(g)] = jnp.concatenate(parts, axis=1)

    def drain(slot, kk):
        for g in range(n_grp):
            cols = grp_cols(g)
            m_old = m_ref[:, cols]
            m_new = jnp.maximum(m_old, jnp.max(s_ref[slot, :, cols], axis=0, keepdims=True))
            alpha = jnp.exp2(m_old - m_new)
            pr = jnp.exp2(s_ref[slot, :, cols] - m_new).astype(BF16)
            m_ref[:, cols] = m_new
            for i in range(DSA_GRP):
                h = g * DSA_GRP + i
                feats = slice(h * HEAD_DIM, (h + 1) * HEAD_DIM)
                hc = slice(i * t, (i + 1) * t)
                lc = slice(h * t, (h + 1) * t)
                pv = _dot(jnp.concatenate([vt_ref[kk, feats, :], ones_rows], axis=0), pr[:, hc])
                acc_ref[feats, :] = alpha[:, hc] * acc_ref[feats, :] + pv[0:HEAD_DIM, :]
                l_ref[:, lc] = alpha[:, hc] * l_ref[:, lc] + pv[HEAD_DIM:HEAD_DIM + 1, :]

    n_far = jnp.maximum(qt - 1, 0)
    fill(0, qt, BLK)
    fill(1, jnp.maximum(qt - 1, 0), 0, off=jnp.where(qt >= 1, 0.0, NEG))
    drain(0, qt)
    fill(0, 0, None)
    drain(1, jnp.maximum(qt - 1, 0))

    def far_pair(i, _):
        kk = 2 * i
        fill(1, kk + 1, None)
        drain(0, kk)
        fill(0, jnp.minimum(kk + 2, n_far - 1), None)
        drain(1, kk + 1)
        return 0

    lax.fori_loop(0, n_far // 2, far_pair, 0)

    @pl.when(n_far % 2 == 1)
    def _():
        drain(0, n_far - 1)

    for h in range(n_heads):
        feats = slice(h * HEAD_DIM, (h + 1) * HEAD_DIM)
        acc_ref[feats, :] = acc_ref[feats, :] / l_ref[:, h * t:(h + 1) * t]
    o_ref[...] = acc_ref[...].T.astype(o_ref.dtype)


def _dsa_attention(qn_t, ckv, v_t, mask_t, bt, w_uk_t, n_heads):
    b, seq, _ = ckv.shape
    t = BLK
    d_q = qn_t.shape[1]
    return pl.pallas_call(
        functools.partial(_dsa_attn_kernel, n_heads=n_heads),
        grid=(b, seq // t),
        in_specs=[pl.BlockSpec((None, d_q, t), lambda bi, q: (bi, 0, q)),
                  pl.BlockSpec((None, seq, DSA_LORA), lambda bi, q: (bi, 0, 0)),
                  pl.BlockSpec((None, seq // t, d_q, t), lambda bi, q: (bi, 0, 0, 0)),
                  pl.BlockSpec((None, None, seq, t), lambda bi, q: (bi, q, 0, 0)),
                  _full(bt), _full(w_uk_t)],
        out_specs=pl.BlockSpec((None, t, d_q), lambda bi, q: (bi, q, 0)),
        out_shape=jax.ShapeDtypeStruct((b, seq, d_q), BF16),
        scratch_shapes=[pltpu.VMEM((DSA_LORA, n_heads * t), BF16),
                        pltpu.VMEM((d_q, t), F32),
                        pltpu.VMEM((1, n_heads * t), F32),
                        pltpu.VMEM((1, n_heads * t), F32),
                        pltpu.VMEM((2, t, n_heads * t), F32)],
        compiler_params=_cparams("parallel", "arbitrary"),
        name="dsa_attn",
    )(qn_t, ckv, v_t, mask_t, bt, w_uk_t)


def _moba_layer(x2, batch, seq, g, w_qkv, bt, n_heads):
    d = x2.shape[1]
    wq_t = (w_qkv[:, :d] * (HEAD_DIM ** -0.5 * LOG2E)).T.astype(BF16)
    wk = w_qkv[:, d:2 * d].astype(BF16)
    wv_t = w_qkv[:, 2 * d:].T.astype(BF16)
    q_t, k, v_t = _moba_proj(x2.reshape(batch, seq, d), g, wq_t, wk, wv_t)
    attn = _moba_attention(q_t, k, v_t, bt, n_heads)
    return attn.reshape(batch * seq, d)


def _dsa_layer(x2, batch, seq, g, w_in, g_q, g_kv, w_uq, w_qi, w_uk, w_uv, bt, n_heads):
    d = x2.shape[1]
    k_lo = 2 * DSA_LORA
    w_in_row = jnp.concatenate([w_in[:, :k_lo + IDX_DIM], jnp.zeros((d, LANES - IDX_DIM), F32)], axis=1).astype(BF16)
    w_kv_t = w_in[:, DSA_LORA:k_lo].T.astype(BF16)
    w_w_t = jnp.concatenate([w_in[:, k_lo + IDX_DIM:].T, jnp.zeros((BF16_ROWS - IDX_HEADS, d), F32)]).astype(BF16)
    w_uq_t = (w_uq * (HEAD_DIM ** -0.5 * LOG2E)).T.astype(BF16)
    w_uv_t = jnp.transpose(w_uv, (0, 2, 1)).reshape(n_heads * HEAD_DIM, DSA_LORA).astype(BF16)
    ckv, v_t, kk, widx_t, qn_t, qi_t = _dsa_proj(x2.reshape(batch, seq, d), g, w_in_row, w_kv_t, w_w_t,
                                                 g_q, g_kv, w_uq_t, w_qi.T.astype(BF16), w_uv_t)
    mask_t = _dsa_select(qi_t, widx_t, kk)
    w_uk_t = jnp.transpose(w_uk.reshape(n_heads // 2, 2 * HEAD_DIM, DSA_LORA), (0, 2, 1)).astype(BF16)
    attn = _dsa_attention(qn_t, ckv, v_t, mask_t, bt[:, :2 * BLK], w_uk_t, n_heads)
    return attn.reshape(batch * seq, -1)


def kernel(x, rel_bias, ln_attn, ln_mlp, moba_w_qkv, moba_w_o, dsa_w_in, dsa_g_q, dsa_g_kv, dsa_w_uq, dsa_w_qi,
           dsa_w_uk, dsa_w_uv, dsa_w_o, mlp_w_up, mlp_w_down, final_norm):
    batch, seq, d = x.shape
    n_heads = rel_bias.shape[1]
    depth = ln_attn.shape[0]
    bt = _bias_tiles(rel_bias)
    x2 = x.reshape(batch * seq, d)
    for i in range(depth):
        j = i // 2
        if i % 2 == 0:
            attn, w_o = _moba_layer(x2, batch, seq, ln_attn[i], moba_w_qkv[j], bt, n_heads), moba_w_o[j]
        else:
            attn, w_o = _dsa_layer(x2, batch, seq, ln_attn[i], dsa_w_in[j], dsa_g_q[j], dsa_g_kv[j], dsa_w_uq[j],
                                   dsa_w_qi[j], dsa_w_uk[j], dsa_w_uv[j], bt, n_heads), dsa_w_o[j]
        x2 = _block_tail(x2, attn, w_o.astype(BF16), ln_mlp[i], mlp_w_up[i].astype(BF16), mlp_w_down[i].astype(BF16),
                         g_final=final_norm if i == depth - 1 else None)
    return x2.reshape(batch, seq, d)
```

```python
import functools
import math

import numpy as np
import jax
import jax.numpy as jnp
from jax import lax
from jax.experimental import pallas as pl
from jax.experimental.pallas import tpu as pltpu

F32 = jnp.float32
BF16 = jnp.bfloat16
I32 = jnp.int32
I16 = jnp.int16

EPS = 1e-6
NEG = -1e30
INT_MIN = -(2 ** 31)
LOG2E = 1.0 / math.log(2.0)

NUM_BUCKETS = 32
MAX_DISTANCE = 128
HEAD_DIM = 64
LANES = 128
BF16_ROWS = 16
BLK = 256
PROJ_TM = 512
MOBA_TOPK = 3
MOBA_GRP = 8
DSA_TOPK_MAX = 256
DSA_LORA = 256
DSA_GRP = 4
IDX_HEADS = 8
IDX_DIM = 64
VMEM_LIMIT = 56 * 1024 * 1024


def _dot(a, b):
    return jnp.dot(a, b, preferred_element_type=F32)


def _dot_nt(a, b):
    return lax.dot_general(a, b, (((1,), (1,)), ((), ())), preferred_element_type=F32)


def _rms(x, g):
    return x * lax.rsqrt(jnp.mean(x * x, axis=-1, keepdims=True) + EPS) * g


def _cparams(*sem):
    return pltpu.CompilerParams(dimension_semantics=sem, vmem_limit_bytes=VMEM_LIMIT)


def _full(a):
    return pl.BlockSpec(a.shape, lambda *_: (0,) * a.ndim, pipeline_mode=pl.Buffered(1))


def _bucket_map_t():
    jj = np.arange(2 * BLK)[:, None]
    i = np.arange(BLK)[None, :]
    d = i - jj + BLK
    n = np.maximum(d, 0)
    exact = NUM_BUCKETS // 2
    nf = np.maximum(n, 1).astype(np.float32)
    large = exact + (np.log(nf / exact) / math.log(MAX_DISTANCE / exact) * (NUM_BUCKETS - exact)).astype(np.int32)
    large = np.minimum(large, NUM_BUCKETS - 1)
    bucket = np.where(n < exact, n, large)
    bmap = np.where(d >= 0, bucket, -1).astype(np.int32)
    return np.concatenate([bmap, bmap[:BLK]], axis=0)


def _bias_tile_kernel(tab_ref, bmap_ref, o_ref):
    h = pl.program_id(0)
    bm = bmap_ref[...]
    base = tab_ref[NUM_BUCKETS - 1, h]
    acc = jnp.full(bm.shape, NEG, F32)
    for b in range(NUM_BUCKETS):
        acc = jnp.where(bm == b, (tab_ref[b, h] - base) * LOG2E, acc)
    o_ref[...] = acc


def _bias_tiles(rel_bias):
    n_heads = rel_bias.shape[1]
    bmap = jnp.asarray(_bucket_map_t())
    return pl.pallas_call(
        _bias_tile_kernel,
        grid=(n_heads,),
        in_specs=[pl.BlockSpec(memory_space=pltpu.SMEM),
                  pl.BlockSpec((3 * BLK, BLK), lambda h: (0, 0))],
        out_specs=pl.BlockSpec((None, 3 * BLK, BLK), lambda h: (h, 0, 0)),
        out_shape=jax.ShapeDtypeStruct((n_heads, 3 * BLK, BLK), F32),
        name="bias_tiles",
    )(rel_bias, bmap)


def _block_tail_kernel(x_ref, a_ref, wo_ref, g_ref, wu_ref, wd_ref, *rest, final):
    if final:
        gf_ref, o_ref, x1_ref, xn_ref, acc_ref = rest
    else:
        o_ref, x1_ref, xn_ref, acc_ref = rest
    f = pl.program_id(1)

    @pl.when(f == 0)
    def _():
        x1 = x_ref[...] + _dot(a_ref[...], wo_ref[...])
        x1_ref[...] = x1
        xn_ref[...] = _rms(x1, g_ref[...]).astype(BF16)
        acc_ref[...] = jnp.zeros_like(acc_ref)

    h = jnp.square(jnp.maximum(_dot(xn_ref[...], wu_ref[...]), 0.0)).astype(BF16)
    acc_ref[...] += _dot(h, wd_ref[...])

    @pl.when(f == pl.num_programs(1) - 1)
    def _():
        out = x1_ref[...] + acc_ref[...]
        o_ref[...] = _rms(out, gf_ref[...]) if final else out


def _block_tail(x2, attn, w_o, g, w_up, w_down, g_final=None, tm=1024, tf=1024):
    n, d = x2.shape
    k = attn.shape[1]
    d_ff = w_up.shape[1]
    final = g_final is not None
    row = lambda w: pl.BlockSpec((tm, w), lambda i, f: (i, 0))
    vec = pl.BlockSpec((1, d), lambda i, f: (0, 0))
    in_specs = [row(d), row(k), pl.BlockSpec((k, d), lambda i, f: (0, 0), pipeline_mode=pl.Buffered(1)), vec,
                pl.BlockSpec((d, tf), lambda i, f: (0, f)), pl.BlockSpec((tf, d), lambda i, f: (f, 0))]
    args = [x2, attn, w_o, g.reshape(1, d), w_up, w_down]
    if final:
        in_specs.append(vec)
        args.append(g_final.reshape(1, d))
    return pl.pallas_call(
        functools.partial(_block_tail_kernel, final=final),
        grid=(n // tm, d_ff // tf),
        in_specs=in_specs,
        out_specs=row(d),
        out_shape=jax.ShapeDtypeStruct((n, d), F32),
        scratch_shapes=[pltpu.VMEM((tm, d), F32), pltpu.VMEM((tm, d), BF16), pltpu.VMEM((tm, d), F32)],
        compiler_params=_cparams("parallel", "arbitrary"),
        name="block_tail",
    )(*args)


def _moba_proj_kernel(x_ref, g_ref, wqt_ref, wk_ref, wvt_ref, qt_ref, k_ref, vt_ref):
    xn = _rms(x_ref[...], g_ref[...]).astype(BF16)
    qt_ref[...] = _dot_nt(wqt_ref[...], xn).astype(BF16)
    k_ref[...] = _dot(xn, wk_ref[...]).astype(BF16)
    v_t = _dot_nt(wvt_ref[...], xn).astype(BF16)
    for r in range(vt_ref.shape[0]):
        vt_ref[r] = v_t[:, r * BLK:(r + 1) * BLK]


def _moba_proj(x3, g, wq_t, wk, wv_t):
    b, seq, d = x3.shape
    g2 = g.reshape(1, d)
    return pl.pallas_call(
        _moba_proj_kernel,
        grid=(b, seq // PROJ_TM),
        in_specs=[pl.BlockSpec((None, PROJ_TM, d), lambda bi, i: (bi, i, 0)),
                  _full(g2), _full(wq_t), _full(wk), _full(wv_t)],
        out_specs=[pl.BlockSpec((None, d, PROJ_TM), lambda bi, i: (bi, 0, i)),
                   pl.BlockSpec((None, PROJ_TM, d), lambda bi, i: (bi, i, 0)),
                   pl.BlockSpec((None, PROJ_TM // BLK, d, BLK), lambda bi, i: (bi, i, 0, 0))],
        out_shape=[jax.ShapeDtypeStruct((b, d, seq), BF16),
                   jax.ShapeDtypeStruct((b, seq, d), BF16),
                   jax.ShapeDtypeStruct((b, seq // BLK, d, BLK), BF16)],
        compiler_params=_cparams("parallel", "parallel"),
        name="moba_proj",
    )(x3, g2, wq_t, wk, wv_t)


def _moba_kernel(qt_ref, k_ref, vt_ref, bt_ref, o_ref, kaug_ref, kmean_ref, qfar_ref, m_ref, l_ref, acc_ref, s_ref,
                 *, n_blk, k_sel, sel_rows):
    j = pl.program_id(2)
    seq = k_ref.shape[0]
    n_pairs = MOBA_GRP // 2
    t = BLK
    row = lax.broadcasted_iota(I32, (LANES, t), 0)
    blk_id = lax.broadcasted_iota(I32, (sel_rows, t), 0)
    ones_2 = jnp.ones((BF16_ROWS, 2 * t), BF16)

    @pl.when(j == 0)
    def _():
        krow = lax.broadcasted_iota(I32, (seq, LANES), 0)
        kcol = lax.broadcasted_iota(I32, (seq, LANES), 1)
        onehot = jnp.where(krow // BLK == kcol, 1.0, 0.0).astype(BF16)
        for p in range(n_pairs):
            kaug_ref[p, :, 0:LANES] = k_ref[:, p * LANES:(p + 1) * LANES]
            kaug_ref[p, :, LANES:2 * LANES] = onehot
            kmean_ref[p] = jnp.zeros((LANES, LANES), F32)
            for n in range(n_blk):
                kmean_ref[p, n:n + 1, :] = jnp.mean(
                    k_ref[n * BLK:(n + 1) * BLK, p * LANES:(p + 1) * LANES].astype(F32), axis=0, keepdims=True)

    near_blk = jnp.maximum(j - 1, 0)
    near0 = pl.multiple_of(near_blk * BLK, BLK)
    bias0 = pl.multiple_of(jnp.where(j == 0, BLK, 0), BLK)
    pad_rows = jnp.zeros((LANES - sel_rows, t), BF16)

    def drain(slot, blk0):
        for p in range(n_pairs):
            feats = slice(p * LANES, (p + 1) * LANES)
            m_old = m_ref[p]
            m_new = jnp.maximum(m_old, jnp.max(s_ref[slot, p], axis=0, keepdims=True))
            alpha = jnp.exp2(m_old - m_new)
            pr = jnp.exp2(s_ref[slot, p] - m_new).astype(BF16)
            m_ref[p] = m_new
            v_t = jnp.concatenate([vt_ref[blk0, feats, :], vt_ref[blk0 + 1, feats, :]], axis=1)
            pv = _dot(jnp.concatenate([v_t, ones_2], axis=0), pr)
            acc_ref[p] = alpha * acc_ref[p] + pv[0:LANES, :]
            l_ref[p] = alpha * l_ref[p] + pv[LANES:LANES + 1, :]

    def fill_far(slot, pi):
        n0 = pl.multiple_of(pi * 2 * BLK, 2 * BLK)
        for p in range(n_pairs):
            s_ref[slot, p] = _dot(kaug_ref[p, pl.ds(n0, 2 * BLK), :], qfar_ref[p])

    q_near = []
    for p in range(n_pairs):
        qp = qt_ref[p * LANES:(p + 1) * LANES, :]
        near, far = [], []
        for hh in range(2):
            qh = jnp.where((row < HEAD_DIM) if hh == 0 else (row >= HEAD_DIM), qp, jnp.zeros_like(qp))
            gate = _dot(kmean_ref[p], qh.astype(F32))[0:sel_rows, :]
            g = jnp.where(blk_id < j, gate, NEG)
            g = jnp.where(blk_id < n_blk, g, -jnp.inf)
            sel_bias = jnp.full((sel_rows, t), NEG, F32)
            for it in range(k_sel):
                mx = jnp.max(g, axis=0, keepdims=True)
                first = jnp.min(jnp.where(g == mx, blk_id, LANES), axis=0, keepdims=True)
                pick = blk_id == first
                sel_bias = jnp.where(pick, jnp.where(it < j, 0.0, NEG), sel_bias)
                g = jnp.where(pick, -jnp.inf, g)
            near_bias = jnp.where(blk_id == j, 0.0, sel_bias)
            far_bias = jnp.where(blk_id < j - 1, sel_bias, NEG)
            near.append(jnp.concatenate([qh, near_bias.astype(BF16), pad_rows], axis=0))
            far.append(jnp.concatenate([qh, far_bias.astype(BF16), pad_rows], axis=0))
        q_near.append(jnp.concatenate(near, axis=1))
        qfar_ref[p] = jnp.concatenate(far, axis=1)
        m_ref[p] = jnp.full((1, 2 * t), NEG, F32)
        l_ref[p] = jnp.zeros((1, 2 * t), F32)
        acc_ref[p] = jnp.zeros((LANES, 2 * t), F32)

    for p in range(n_pairs):
        bias = jnp.concatenate([bt_ref[2 * p, pl.ds(bias0, 2 * BLK), :], bt_ref[2 * p + 1, pl.ds(bias0, 2 * BLK), :]],
                               axis=1)
        s_ref[0, p] = _dot(kaug_ref[p, pl.ds(near0, 2 * BLK), :], q_near[p]) + bias

    n_far = jnp.maximum(j, 1) // 2
    fill_far(1, 0)
    drain(0, near_blk)

    def far_two(i, _):
        pi = 2 * i
        fill_far(0, pi + 1)
        drain(1, 2 * pi)
        fill_far(1, jnp.minimum(pi + 2, n_far - 1))
        drain(0, 2 * pi + 2)
        return 0

    lax.fori_loop(0, n_far // 2, far_two, 0)

    @pl.when(n_far % 2 == 1)
    def _():
        drain(1, 2 * (n_far - 1))

    for p in range(n_pairs):
        acc = acc_ref[p] / l_ref[p]
        out_t = jnp.where(row < HEAD_DIM, acc[:, 0:t], acc[:, t:2 * t])
        o_ref[:, p * LANES:(p + 1) * LANES] = out_t.T.astype(o_ref.dtype)


def _moba_attention(q_t, k, v_t, bt, n_heads):
    b, seq, d = k.shape
    n_blk = seq // BLK
    n_grp = n_heads // MOBA_GRP
    n_pairs = MOBA_GRP // 2
    w = MOBA_GRP * HEAD_DIM
    k_sel = max(1, min(MOBA_TOPK, n_blk - 1))
    sel_rows = BF16_ROWS * pl.cdiv(n_blk, BF16_ROWS)
    assert sel_rows <= LANES and n_blk % 2 == 0
    return pl.pallas_call(
        functools.partial(_moba_kernel, n_blk=n_blk, k_sel=k_sel, sel_rows=sel_rows),
        grid=(b, n_grp, n_blk),
        in_specs=[pl.BlockSpec((None, w, BLK), lambda bi, g, j: (bi, g, j)),
                  pl.BlockSpec((None, seq, w), lambda bi, g, j: (bi, 0, g)),
                  pl.BlockSpec((None, n_blk, w, BLK), lambda bi, g, j: (bi, 0, g, 0)),
                  pl.BlockSpec((MOBA_GRP, 3 * BLK, BLK), lambda bi, g, j: (g, 0, 0), pipeline_mode=pl.Buffered(1))],
        out_specs=pl.BlockSpec((None, BLK, w), lambda bi, g, j: (bi, j, g)),
        out_shape=jax.ShapeDtypeStruct((b, seq, d), BF16),
        scratch_shapes=[pltpu.VMEM((n_pairs, seq, 2 * LANES), BF16),
                        pltpu.VMEM((n_pairs, LANES, LANES), F32),
                        pltpu.VMEM((n_pairs, 2 * LANES, 2 * BLK), BF16),
                        pltpu.VMEM((n_pairs, 1, 2 * BLK), F32),
                        pltpu.VMEM((n_pairs, 1, 2 * BLK), F32),
                        pltpu.VMEM((n_pairs, LANES, 2 * BLK), F32),
                        pltpu.VMEM((2, n_pairs, 2 * BLK, 2 * BLK), F32)],
        compiler_params=_cparams("parallel", "parallel", "arbitrary"),
        name="moba_attn",
    )(q_t, k, v_t, bt)


def _dsa_proj_kernel(x_ref, g_ref, win_ref, wkvt_ref, wwt_ref, gq_ref, gkv_ref, gkvt_ref, wuqt_ref, wqit_ref,
                     wuvt_ref, ckv_ref, vt_ref, kk_ref, widx_ref, qnt_ref, qit_ref, *, idx_scale):
    hn = _rms(x_ref[...], g_ref[...]).astype(BF16)
    proj = _dot(hn, win_ref[...])
    c_q = _rms(proj[:, 0:DSA_LORA], gq_ref[...]).astype(BF16)
    ckv_ref[...] = _rms(proj[:, DSA_LORA:2 * DSA_LORA], gkv_ref[...]).astype(BF16)
    kk_ref[...] = proj[:, 2 * DSA_LORA:2 * DSA_LORA + IDX_DIM].astype(BF16)
    ct = _dot_nt(wkvt_ref[...], hn)
    scale = lax.rsqrt(jnp.mean(ct * ct, axis=0, keepdims=True) + EPS)
    gt = jnp.concatenate([gkvt_ref[...]] * (ct.shape[1] // LANES), axis=1)
    v_t = _dot(wuvt_ref[...], (ct * scale * gt).astype(BF16)).astype(BF16)
    for r in range(vt_ref.shape[0]):
        vt_ref[r] = v_t[:, r * BLK:(r + 1) * BLK]
    widx_ref[...] = _dot_nt(wwt_ref[...], hn)[0:IDX_HEADS, :] * idx_scale
    qnt_ref[...] = _dot_nt(wuqt_ref[...], c_q).astype(BF16)
    qit_ref[...] = _dot_nt(wqit_ref[...], c_q).astype(BF16)


def _dsa_proj(x3, g, w_in_row, w_kv_t, w_w_t, g_q, g_kv, w_uq_t, w_qi_t, w_uv_t):
    b, seq, d = x3.shape
    tm = PROJ_TM
    g2, gq2, gkv2 = g.reshape(1, d), g_q.reshape(1, -1), g_kv.reshape(1, -1)
    gkv_t = jnp.broadcast_to(g_kv.reshape(-1, 1), (DSA_LORA, LANES))
    n_q, n_i, n_v = w_uq_t.shape[0], w_qi_t.shape[0], w_uv_t.shape[0]
    consts = (g2, w_in_row, w_kv_t, w_w_t, gq2, gkv2, gkv_t, w_uq_t, w_qi_t, w_uv_t)
    idx_scale = IDX_HEADS ** -0.5 * IDX_DIM ** -0.5
    return pl.pallas_call(
        functools.partial(_dsa_proj_kernel, idx_scale=idx_scale),
        grid=(b, seq // tm),
        in_specs=[pl.BlockSpec((None, tm, d), lambda bi, i: (bi, i, 0))] + [_full(a) for a in consts],
        out_specs=[pl.BlockSpec((None, tm, DSA_LORA), lambda bi, i: (bi, i, 0)),
                   pl.BlockSpec((None, tm // BLK, n_v, BLK), lambda bi, i: (bi, i, 0, 0)),
                   pl.BlockSpec((None, tm, IDX_DIM), lambda bi, i: (bi, i, 0)),
                   pl.BlockSpec((None, IDX_HEADS, tm), lambda bi, i: (bi, 0, i)),
                   pl.BlockSpec((None, n_q, tm), lambda bi, i: (bi, 0, i)),
                   pl.BlockSpec((None, n_i, tm), lambda bi, i: (bi, 0, i))],
        out_shape=[jax.ShapeDtypeStruct((b, seq, DSA_LORA), BF16),
                   jax.ShapeDtypeStruct((b, seq // BLK, n_v, BLK), BF16),
                   jax.ShapeDtypeStruct((b, seq, IDX_DIM), BF16),
                   jax.ShapeDtypeStruct((b, IDX_HEADS, seq), F32),
                   jax.ShapeDtypeStruct((b, n_q, seq), BF16),
                   jax.ShapeDtypeStruct((b, n_i, seq), BF16)],
        compiler_params=_cparams("parallel", "parallel"),
        name="dsa_proj",
    )(x3, *consts)


def _dsa_select_kernel(qit_ref, w_ref, kk_ref, o_ref, keys_ref, half_ref, *, top_k, idx_bits):
    c = pl.program_id(1)
    n_kc, tk, tq = keys_ref.shape
    n_proc = c + 1
    key_row = lax.broadcasted_iota(I32, (tk, tq), 0)
    q_pos = c * tq + lax.broadcasted_iota(I32, (tk, tq), 1)
    w = w_ref[...]

    def score_chunk(kc, _):
        kt = kk_ref[pl.ds(pl.multiple_of(kc * tk, tk), tk), :]
        acc = jnp.zeros((tk, tq), F32)
        for h in range(IDX_HEADS):
            x = _dot(kt, qit_ref[h * IDX_DIM:(h + 1) * IDX_DIM, :])
            acc = acc + w[h:h + 1, :] * jnp.maximum(x, 0.0)
        sc = jnp.where(kc * tk + key_row <= q_pos, acc, NEG)
        bits = pltpu.bitcast(sc, I32)
        key = jnp.where(bits < 0, bits ^ 0x7FFFFFFF, bits)
        keys_ref[kc] = key
        half_ref[kc] = jnp.right_shift(key, 16).astype(I16)
        return 0

    lax.fori_loop(0, n_proc, score_chunk, 0)

    n_pair = (n_proc + 1) // 2

    @pl.when(n_proc % 2 == 1)
    def _():
        half_ref[n_proc] = jnp.full((tk, tq), -32768, I16)

    def count(pred):
        def body(kc, tot):
            hit = jnp.where(pred(keys_ref[kc], kc * tk + key_row), 1, 0)
            return tot + jnp.sum(hit, axis=0, keepdims=True)
        return lax.fori_loop(0, n_proc, body, jnp.zeros((1, tq), I32))

    def count_half(cand, strict=False):
        cand16 = cand.astype(I16)

        def body(i, part):
            tiles = []
            for kc in (2 * i, 2 * i + 1):
                x = half_ref[kc]
                hit = jnp.where((x > cand16) if strict else (x >= cand16), jnp.int16(1), jnp.int16(0))
                tiles += [hit[r:r + BF16_ROWS, :] for r in range(0, tk, BF16_ROWS)]
            while len(tiles) > 1:
                tiles = [a + b for a, b in zip(tiles[::2], tiles[1::2])]
            return part + tiles[0]
        part = lax.fori_loop(0, n_pair, body, jnp.zeros((BF16_ROWS, tq), I16))
        return jnp.sum(part.astype(I32), axis=0, keepdims=True)

    def largest_half(need):
        def bit(i, u):
            cand_u = u | jnp.left_shift(jnp.int32(1), 15 - i)
            return jnp.where(count_half(cand_u - 32768) >= need, cand_u, u)
        return lax.fori_loop(0, 16, bit, jnp.zeros((1, tq), I32)) - 32768

    thr_hi = largest_half(top_k)
    above = count_half(thr_hi, strict=True)

    def low_chunk(kc, _):
        key = keys_ref[kc]
        low = jnp.bitwise_and(key, 0xFFFF) - 32768
        half_ref[kc] = jnp.where(jnp.right_shift(key, 16) == thr_hi, low, -32768).astype(I16)
        return 0

    lax.fori_loop(0, n_proc, low_chunk, 0)
    thr_lo = largest_half(top_k - above)
    cnt_ge = above + count_half(thr_lo)
    thr = jnp.left_shift(thr_hi, 16) + (thr_lo + 32768)

    def tie_cut():
        need = top_k - count(lambda key, pos: key > thr)

        def index_bit(i, x):
            cand = x | jnp.left_shift(jnp.int32(1), idx_bits - 1 - i)
            cnt = count(lambda key, pos: (key == thr) & (pos < cand))
            return jnp.where(cnt < need, cand, x)
        return lax.fori_loop(0, idx_bits, index_bit, jnp.zeros((1, tq), I32))

    last_tie = lax.cond(jnp.max(cnt_ge) > top_k, tie_cut, lambda: jnp.full((1, tq), 2 ** idx_bits, I32))

    def write_chunk(kc, _):
        key = keys_ref[kc]
        pos = kc * tk + key_row
        chosen = (key > thr) | ((key == thr) & (pos <= last_tie))
        rows = pl.ds(pl.multiple_of(kc * tk, tk), tk)
        o_ref[rows, :] = jnp.where(chosen & (pos <= q_pos), 0.0, NEG).astype(o_ref.dtype)
        return 0

    lax.fori_loop(0, n_proc, write_chunk, 0)

    def fill_chunk(kc, _):
        o_ref[pl.ds(pl.multiple_of(kc * tk, tk), tk), :] = jnp.full((tk, tq), NEG, o_ref.dtype)
        return 0

    lax.fori_loop(n_proc, n_kc, fill_chunk, 0)


def _dsa_select(qi_t, widx_t, kk):
    b, seq, _ = kk.shape
    t = BLK
    n_kc = seq // t
    top_k = min(DSA_TOPK_MAX, seq // 4)
    assert top_k <= t
    assert n_kc % 2 == 0
    idx_bits = max(1, (seq - 1).bit_length())
    return pl.pallas_call(
        functools.partial(_dsa_select_kernel, top_k=top_k, idx_bits=idx_bits),
        grid=(b, seq // t),
        in_specs=[pl.BlockSpec((None, qi_t.shape[1], t), lambda bi, c: (bi, 0, c)),
                  pl.BlockSpec((None, IDX_HEADS, t), lambda bi, c: (bi, 0, c)),
                  pl.BlockSpec((None, seq, IDX_DIM), lambda bi, c: (bi, 0, 0))],
        out_specs=pl.BlockSpec((None, None, seq, t), lambda bi, c: (bi, c, 0, 0)),
        out_shape=jax.ShapeDtypeStruct((b, seq // t, seq, t), BF16),
        scratch_shapes=[pltpu.VMEM((n_kc, t, t), I32),
                        pltpu.VMEM((n_kc, t, t), I16)],
        compiler_params=_cparams("parallel", "arbitrary"),
        name="dsa_select",
    )(qi_t, widx_t, kk)


def _dsa_attn_kernel(qnt_ref, ckv_ref, vt_ref, mb_ref, bt_ref, wukt_ref, o_ref,
                     qlat_ref, acc_ref, m_ref, l_ref, s_ref, *, n_heads):
    qt = pl.program_id(1)
    t = BLK
    n_grp = n_heads // DSA_GRP
    row = lax.broadcasted_iota(I32, (LANES, t), 0)
    ones_rows = jnp.ones((BF16_ROWS, t), BF16)

    for hp in range(n_heads // 2):
        qp = qnt_ref[hp * LANES:(hp + 1) * LANES, :]
        for hh in range(2):
            qh = jnp.where((row < HEAD_DIM) if hh == 0 else (row >= HEAD_DIM), qp, jnp.zeros_like(qp))
            h = 2 * hp + hh
            qlat_ref[:, h * t:(h + 1) * t] = _dot(wukt_ref[hp], qh).astype(BF16)

    m_ref[...] = jnp.full(m_ref.shape, NEG, F32)
    l_ref[...] = jnp.zeros_like(l_ref)
    acc_ref[...] = jnp.zeros_like(acc_ref)

    def grp_cols(g):
        return slice(g * DSA_GRP * t, (g + 1) * DSA_GRP * t)

    def key_rows(kk):
        return pl.ds(pl.multiple_of(kk * t, t), t)

    def fill(slot, kk, bias_row0, off=None):
        kt = ckv_ref[key_rows(kk), :]
        mb = mb_ref[key_rows(kk), :].astype(F32)
        if off is not None:
            mb = mb + off
        for g in range(n_grp):
            s_grp = _dot(kt, qlat_ref[:, grp_cols(g)])
            parts = []
            for i in range(DSA_GRP):
                s = s_grp[:, i * t:(i + 1) * t] + mb
                if bias_row0 is not None:
                    s = s + bt_ref[g * DSA_GRP + i, bias_row0:bias_row0 + t, :]
                parts.append(s)
            s_ref[slot, :, grp_cols(g)] = jnp.concatenate(parts, axis=1)

    def drain(slot, kk):
        for g in range(n_grp):
            cols = grp_cols(g)
            m_old = m_ref[:, cols]
            m_new = jnp.maximum(m_old, jnp.max(s_ref[slot, :, cols], axis=0, keepdims=True))
            alpha = jnp.exp2(m_old - m_new)
            pr = jnp.exp2(s_ref[slot, :, cols] - m_new).astype(BF16)
            m_ref[:, cols] = m_new
            for i in range(DSA_GRP):
                h = g * DSA_GRP + i
                feats = slice(h * HEAD_DIM, (h + 1) * HEAD_DIM)
                hc = slice(i * t, (i + 1) * t)
                lc = slice(h * t, (h + 1) * t)
                pv = _dot(jnp.concatenate([vt_ref[kk, feats, :], ones_rows], axis=0), pr[:, hc])
                acc_ref[feats, :] = alpha[:, hc] * acc_ref[feats, :] + pv[0:HEAD_DIM, :]
                l_ref[:, lc] = alpha[:, hc] * l_ref[:, lc] + pv[HEAD_DIM:HEAD_DIM + 1, :]

    n_far = jnp.maximum(qt - 1, 0)
    fill(0, qt, BLK)
    fill(1, jnp.maximum(qt - 1, 0), 0, off=jnp.where(qt >= 1, 0.0, NEG))
    drain(0, qt)
    fill(0, 0, None)
    drain(1, jnp.maximum(qt - 1, 0))

    def far_pair(i, _):
        kk = 2 * i
        fill(1, kk + 1, None)
        drain(0, kk)
        fill(0, jnp.minimum(kk + 2, n_far - 1), None)
        drain(1, kk + 1)
        return 0

    lax.fori_loop(0, n_far // 2, far_pair, 0)

    @pl.when(n_far % 2 == 1)
    def _():
        drain(0, n_far - 1)

    for h in range(n_heads):
        feats = slice(h * HEAD_DIM, (h + 1) * HEAD_DIM)
        acc_ref[feats, :] = acc_ref[feats, :] / l_ref[:, h * t:(h + 1) * t]
    o_ref[...] = acc_ref[...].T.astype(o_ref.dtype)


def _dsa_attention(qn_t, ckv, v_t, mask_t, bt, w_uk_t, n_heads):
    b, seq, _ = ckv.shape
    t = BLK
    d_q = qn_t.shape[1]
    return pl.pallas_call(
        functools.partial(_dsa_attn_kernel, n_heads=n_heads),
        grid=(b, seq // t),
        in_specs=[pl.BlockSpec((None, d_q, t), lambda bi, q: (bi, 0, q)),
                  pl.BlockSpec((None, seq, DSA_LORA), lambda bi, q: (bi, 0, 0)),
                  pl.BlockSpec((None, seq // t, d_q, t), lambda bi, q: (bi, 0, 0, 0)),
                  pl.BlockSpec((None, None, seq, t), lambda bi, q: (bi, q, 0, 0)),
                  _full(bt), _full(w_uk_t)],
        out_specs=pl.BlockSpec((None, t, d_q), lambda bi, q: (bi, q, 0)),
        out_shape=jax.ShapeDtypeStruct((b, seq, d_q), BF16),
        scratch_shapes=[pltpu.VMEM((DSA_LORA, n_heads * t), BF16),
                        pltpu.VMEM((d_q, t), F32),
                        pltpu.VMEM((1, n_heads * t), F32),
                        pltpu.VMEM((1, n_heads * t), F32),
                        pltpu.VMEM((2, t, n_heads * t), F32)],
        compiler_params=_cparams("parallel", "arbitrary"),
        name="dsa_attn",
    )(qn_t, ckv, v_t, mask_t, bt, w_uk_t)


def _moba_layer(x2, batch, seq, g, w_qkv, bt, n_heads):
    d = x2.shape[1]
    wq_t = (w_qkv[:, :d] * (HEAD_DIM ** -0.5 * LOG2E)).T.astype(BF16)
    wk = w_qkv[:, d:2 * d].astype(BF16)
    wv_t = w_qkv[:, 2 * d:].T.astype(BF16)
    q_t, k, v_t = _moba_proj(x2.reshape(batch, seq, d), g, wq_t, wk, wv_t)
    attn = _moba_attention(q_t, k, v_t, bt, n_heads)
    return attn.reshape(batch * seq, d)


def _dsa_layer(x2, batch, seq, g, w_in, g_q, g_kv, w_uq, w_qi, w_uk, w_uv, bt, n_heads):
    d = x2.shape[1]
    k_lo = 2 * DSA_LORA
    w_in_row = jnp.concatenate([w_in[:, :k_lo + IDX_DIM], jnp.zeros((d, LANES - IDX_DIM), F32)], axis=1).astype(BF16)
    w_kv_t = w_in[:, DSA_LORA:k_lo].T.astype(BF16)
    w_w_t = jnp.concatenate([w_in[:, k_lo + IDX_DIM:].T, jnp.zeros((BF16_ROWS - IDX_HEADS, d), F32)]).astype(BF16)
    w_uq_t = (w_uq * (HEAD_DIM ** -0.5 * LOG2E)).T.astype(BF16)
    w_uv_t = jnp.transpose(w_uv, (0, 2, 1)).reshape(n_heads * HEAD_DIM, DSA_LORA).astype(BF16)
    ckv, v_t, kk, widx_t, qn_t, qi_t = _dsa_proj(x2.reshape(batch, seq, d), g, w_in_row, w_kv_t, w_w_t,
                                                 g_q, g_kv, w_uq_t, w_qi.T.astype(BF16), w_uv_t)
    mask_t = _dsa_select(qi_t, widx_t, kk)
    w_uk_t = jnp.transpose(w_uk.reshape(n_heads // 2, 2 * HEAD_DIM, DSA_LORA), (0, 2, 1)).astype(BF16)
    attn = _dsa_attention(qn_t, ckv, v_t, mask_t, bt[:, :2 * BLK], w_uk_t, n_heads)
    return attn.reshape(batch * seq, -1)


def kernel(x, rel_bias, ln_attn, ln_mlp, moba_w_qkv, moba_w_o, dsa_w_in, dsa_g_q, dsa_g_kv, dsa_w_uq, dsa_w_qi,
           dsa_w_uk, dsa_w_uv, dsa_w_o, mlp_w_up, mlp_w_down, final_norm):
    batch, seq, d = x.shape
    n_heads = rel_bias.shape[1]
    depth = ln_attn.shape[0]
    bt = _bias_tiles(rel_bias)
    x2 = x.reshape(batch * seq, d)
    for i in range(depth):
        j = i // 2
        if i % 2 == 0:
            attn, w_o = _moba_layer(x2, batch, seq, ln_attn[i], moba_w_qkv[j], bt, n_heads), moba_w_o[j]
        else:
            attn, w_o = _dsa_layer(x2, batch, seq, ln_attn[i], dsa_w_in[j], dsa_g_q[j], dsa_g_kv[j], dsa_w_uq[j],
                                   dsa_w_qi[j], dsa_w_uk[j], dsa_w_uv[j], bt, n_heads), dsa_w_o[j]
        x2 = _block_tail(x2, attn, w_o.astype(BF16), ln_mlp[i], mlp_w_up[i].astype(BF16), mlp_w_down[i].astype(BF16),
                         g_final=final_norm if i == depth - 1 else None)
    return x2.reshape(batch, seq, d)
```

```python
import functools
import math

import numpy as np
import jax
import jax.numpy as jnp
from jax import lax
from jax.experimental import pallas as pl
from jax.experimental.pallas import tpu as pltpu

F32 = jnp.float32
BF16 = jnp.bfloat16
I32 = jnp.int32
I16 = jnp.int16

EPS = 1e-6
NEG = -1e30
INT_MIN = -(2 ** 31)
LOG2E = 1.0 / math.log(2.0)

NUM_BUCKETS = 32
MAX_DISTANCE = 128
HEAD_DIM = 64
LANES = 128
BF16_ROWS = 16
BLK = 256
PROJ_TM = 512
MOBA_TOPK = 3
MOBA_GRP = 8
DSA_TOPK_MAX = 256
DSA_LORA = 256
DSA_GRP = 4
IDX_HEADS = 8
IDX_DIM = 64
VMEM_LIMIT = 56 * 1024 * 1024


def _dot(a, b):
    return jnp.dot(a, b, preferred_element_type=F32)


def _dot_nt(a, b):
    return lax.dot_general(a, b, (((1,), (1,)), ((), ())), preferred_element_type=F32)


def _rms(x, g):
    return x * lax.rsqrt(jnp.mean(x * x, axis=-1, keepdims=True) + EPS) * g


def _cparams(*sem):
    return pltpu.CompilerParams(dimension_semantics=sem, vmem_limit_bytes=VMEM_LIMIT)


def _full(a):
    return pl.BlockSpec(a.shape, lambda *_: (0,) * a.ndim, pipeline_mode=pl.Buffered(1))


def _bucket_map_t():
    jj = np.arange(2 * BLK)[:, None]
    i = np.arange(BLK)[None, :]
    d = i - jj + BLK
    n = np.maximum(d, 0)
    exact = NUM_BUCKETS // 2
    nf = np.maximum(n, 1).astype(np.float32)
    large = exact + (np.log(nf / exact) / math.log(MAX_DISTANCE / exact) * (NUM_BUCKETS - exact)).astype(np.int32)
    large = np.minimum(large, NUM_BUCKETS - 1)
    bucket = np.where(n < exact, n, large)
    bmap = np.where(d >= 0, bucket, -1).astype(np.int32)
    return np.concatenate([bmap, bmap[:BLK]], axis=0)


def _bias_tile_kernel(tab_ref, bmap_ref, o_ref):
    h = pl.program_id(0)
    bm = bmap_ref[...]
    base = tab_ref[NUM_BUCKETS - 1, h]
    acc = jnp.full(bm.shape, NEG, F32)
    for b in range(NUM_BUCKETS):
        acc = jnp.where(bm == b, (tab_ref[b, h] - base) * LOG2E, acc)
    o_ref[...] = acc


def _bias_tiles(rel_bias):
    n_heads = rel_bias.shape[1]
    bmap = jnp.asarray(_bucket_map_t())
    return pl.pallas_call(
        _bias_tile_kernel,
        grid=(n_heads,),
        in_specs=[pl.BlockSpec(memory_space=pltpu.SMEM),
                  pl.BlockSpec((3 * BLK, BLK), lambda h: (0, 0))],
        out_specs=pl.BlockSpec((None, 3 * BLK, BLK), lambda h: (h, 0, 0)),
        out_shape=jax.ShapeDtypeStruct((n_heads, 3 * BLK, BLK), F32),
        name="bias_tiles",
    )(rel_bias, bmap)


def _block_tail_kernel(x_ref, a_ref, wo_ref, g_ref, wu_ref, wd_ref, *rest, final):
    if final:
        gf_ref, o_ref, x1_ref, xn_ref, acc_ref = rest
    else:
        o_ref, x1_ref, xn_ref, acc_ref = rest
    f = pl.program_id(1)

    @pl.when(f == 0)
    def _():
        x1 = x_ref[...] + _dot(a_ref[...], wo_ref[...])
        x1_ref[...] = x1
        xn_ref[...] = _rms(x1, g_ref[...]).astype(BF16)
        acc_ref[...] = jnp.zeros_like(acc_ref)

    h = jnp.square(jnp.maximum(_dot(xn_ref[...], wu_ref[...]), 0.0)).astype(BF16)
    acc_ref[...] += _dot(h, wd_ref[...])

    @pl.when(f == pl.num_programs(1) - 1)
    def _():
        out = x1_ref[...] + acc_ref[...]
        o_ref[...] = _rms(out, gf_ref[...]) if final else out


def _block_tail(x2, attn, w_o, g, w_up, w_down, g_final=None, tm=1024, tf=1024):
    n, d = x2.shape
    k = attn.shape[1]
    d_ff = w_up.shape[1]
    final = g_final is not None
    row = lambda w: pl.BlockSpec((tm, w), lambda i, f: (i, 0))
    vec = pl.BlockSpec((1, d), lambda i, f: (0, 0))
    in_specs = [row(d), row(k), pl.BlockSpec((k, d), lambda i, f: (0, 0), pipeline_mode=pl.Buffered(1)), vec,
                pl.BlockSpec((d, tf), lambda i, f: (0, f)), pl.BlockSpec((tf, d), lambda i, f: (f, 0))]
    args = [x2, attn, w_o, g.reshape(1, d), w_up, w_down]
    if final:
        in_specs.append(vec)
        args.append(g_final.reshape(1, d))
    return pl.pallas_call(
        functools.partial(_block_tail_kernel, final=final),
        grid=(n // tm, d_ff // tf),
        in_specs=in_specs,
        out_specs=row(d),
        out_shape=jax.ShapeDtypeStruct((n, d), F32),
        scratch_shapes=[pltpu.VMEM((tm, d), F32), pltpu.VMEM((tm, d), BF16), pltpu.VMEM((tm, d), F32)],
        compiler_params=_cparams("parallel", "arbitrary"),
        name="block_tail",
    )(*args)


def _moba_proj_kernel(x_ref, g_ref, wqt_ref, wk_ref, wvt_ref, qt_ref, k_ref, vt_ref):
    xn = _rms(x_ref[...], g_ref[...]).astype(BF16)
    qt_ref[...] = _dot_nt(wqt_ref[...], xn).astype(BF16)
    k_ref[...] = _dot(xn, wk_ref[...]).astype(BF16)
    v_t = _dot_nt(wvt_ref[...], xn).astype(BF16)
    for r in range(vt_ref.shape[0]):
        vt_ref[r] = v_t[:, r * BLK:(r + 1) * BLK]


def _moba_proj(x3, g, wq_t, wk, wv_t):
    b, seq, d = x3.shape
    g2 = g.reshape(1, d)
    return pl.pallas_call(
        _moba_proj_kernel,
        grid=(b, seq // PROJ_TM),
        in_specs=[pl.BlockSpec((None, PROJ_TM, d), lambda bi, i: (bi, i, 0)),
                  _full(g2), _full(wq_t), _full(wk), _full(wv_t)],
        out_specs=[pl.BlockSpec((None, d, PROJ_TM), lambda bi, i: (bi, 0, i)),
                   pl.BlockSpec((None, PROJ_TM, d), lambda bi, i: (bi, i, 0)),
                   pl.BlockSpec((None, PROJ_TM // BLK, d, BLK), lambda bi, i: (bi, i, 0, 0))],
        out_shape=[jax.ShapeDtypeStruct((b, d, seq), BF16),
                   jax.ShapeDtypeStruct((b, seq, d), BF16),
                   jax.ShapeDtypeStruct((b, seq // BLK, d, BLK), BF16)],
        compiler_params=_cparams("parallel", "parallel"),
        name="moba_proj",
    )(x3, g2, wq_t, wk, wv_t)


def _moba_kernel(qt_ref, k_ref, vt_ref, bt_ref, o_ref, kaug_ref, kmean_ref, qfar_ref, m_ref, l_ref, acc_ref, s_ref,
                 smax_ref, *, n_blk, k_sel, sel_rows):
    j = pl.program_id(2)
    seq = k_ref.shape[0]
    n_pairs = MOBA_GRP // 2
    t = BLK
    row = lax.broadcasted_iota(I32, (LANES, t), 0)
    blk_id = lax.broadcasted_iota(I32, (sel_rows, t), 0)
    ones_2 = jnp.ones((BF16_ROWS, 2 * t), BF16)

    @pl.when(j == 0)
    def _():
        krow = lax.broadcasted_iota(I32, (seq, LANES), 0)
        kcol = lax.broadcasted_iota(I32, (seq, LANES), 1)
        onehot = jnp.where(krow // BLK == kcol, 1.0, 0.0).astype(BF16)
        for p in range(n_pairs):
            kaug_ref[p, :, 0:LANES] = k_ref[:, p * LANES:(p + 1) * LANES]
            kaug_ref[p, :, LANES:2 * LANES] = onehot
            kmean_ref[p] = jnp.zeros((LANES, LANES), F32)
            for n in range(n_blk):
                kmean_ref[p, n:n + 1, :] = jnp.mean(
                    k_ref[n * BLK:(n + 1) * BLK, p * LANES:(p + 1) * LANES].astype(F32), axis=0, keepdims=True)

    near_blk = jnp.maximum(j - 1, 0)
    near0 = pl.multiple_of(near_blk * BLK, BLK)
    bias0 = pl.multiple_of(jnp.where(j == 0, BLK, 0), BLK)
    pad_rows = jnp.zeros((LANES - sel_rows, t), BF16)

    def store(slot, p, s):
        s_ref[slot, p] = s
        smax_ref[slot, p] = jnp.max(s, axis=0, keepdims=True)

    def drain(slot, blk0):
        for p in range(n_pairs):
            feats = slice(p * LANES, (p + 1) * LANES)
            m_old = m_ref[p]
            m_new = jnp.maximum(m_old, smax_ref[slot, p])
            alpha = jnp.exp2(m_old - m_new)
            pr = jnp.exp2(s_ref[slot, p] - m_new).astype(BF16)
            m_ref[p] = m_new
            v_t = jnp.concatenate([vt_ref[blk0, feats, :], vt_ref[blk0 + 1, feats, :]], axis=1)
            pv = _dot(jnp.concatenate([v_t, ones_2], axis=0), pr)
            acc_ref[p] = alpha * acc_ref[p] + pv[0:LANES, :]
            l_ref[p] = alpha * l_ref[p] + pv[LANES:LANES + 1, :]

    def fill_far(slot, pi):
        n0 = pl.multiple_of(pi * 2 * BLK, 2 * BLK)
        for p in range(n_pairs):
            store(slot, p, _dot(kaug_ref[p, pl.ds(n0, 2 * BLK), :], qfar_ref[p]))

    q_near = []
    for p in range(n_pairs):
        qp = qt_ref[p * LANES:(p + 1) * LANES, :]
        near, far = [], []
        for hh in range(2):
            qh = jnp.where((row < HEAD_DIM) if hh == 0 else (row >= HEAD_DIM), qp, jnp.zeros_like(qp))
            gate = _dot(kmean_ref[p], qh.astype(F32))[0:sel_rows, :]
            g = jnp.where(blk_id < j, gate, NEG)
            g = jnp.where(blk_id < n_blk, g, -jnp.inf)
            sel_bias = jnp.full((sel_rows, t), NEG, F32)
            for it in range(k_sel):
                mx = jnp.max(g, axis=0, keepdims=True)
                first = jnp.min(jnp.where(g == mx, blk_id, LANES), axis=0, keepdims=True)
                pick = blk_id == first
                sel_bias = jnp.where(pick, jnp.where(it < j, 0.0, NEG), sel_bias)
                g = jnp.where(pick, -jnp.inf, g)
            near_bias = jnp.where(blk_id == j, 0.0, sel_bias)
            far_bias = jnp.where(blk_id < j - 1, sel_bias, NEG)
            near.append(jnp.concatenate([qh, near_bias.astype(BF16), pad_rows], axis=0))
            far.append(jnp.concatenate([qh, far_bias.astype(BF16), pad_rows], axis=0))
        q_near.append(jnp.concatenate(near, axis=1))
        qfar_ref[p] = jnp.concatenate(far, axis=1)
        m_ref[p] = jnp.full((1, 2 * t), NEG, F32)
        l_ref[p] = jnp.zeros((1, 2 * t), F32)
        acc_ref[p] = jnp.zeros((LANES, 2 * t), F32)

    for p in range(n_pairs):
        bias = jnp.concatenate([bt_ref[2 * p, pl.ds(bias0, 2 * BLK), :], bt_ref[2 * p + 1, pl.ds(bias0, 2 * BLK), :]],
                               axis=1)
        store(0, p, _dot(kaug_ref[p, pl.ds(near0, 2 * BLK), :], q_near[p]) + bias)

    n_far = jnp.maximum(j, 1) // 2
    fill_far(1, 0)
    drain(0, near_blk)

    def far_two(i, _):
        pi = 2 * i
        fill_far(0, pi + 1)
        drain(1, 2 * pi)
        fill_far(1, jnp.minimum(pi + 2, n_far - 1))
        drain(0, 2 * pi + 2)
        return 0

    lax.fori_loop(0, n_far // 2, far_two, 0)

    @pl.when(n_far % 2 == 1)
    def _():
        drain(1, 2 * (n_far - 1))

    for p in range(n_pairs):
        acc = acc_ref[p] / l_ref[p]
        out_t = jnp.where(row < HEAD_DIM, acc[:, 0:t], acc[:, t:2 * t])
        o_ref[:, p * LANES:(p + 1) * LANES] = out_t.T.astype(o_ref.dtype)


def _moba_attention(q_t, k, v_t, bt, n_heads):
    b, seq, d = k.shape
    n_blk = seq // BLK
    n_grp = n_heads // MOBA_GRP
    n_pairs = MOBA_GRP // 2
    w = MOBA_GRP * HEAD_DIM
    k_sel = max(1, min(MOBA_TOPK, n_blk - 1))
    sel_rows = BF16_ROWS * pl.cdiv(n_blk, BF16_ROWS)
    assert sel_rows <= LANES and n_blk % 2 == 0
    return pl.pallas_call(
        functools.partial(_moba_kernel, n_blk=n_blk, k_sel=k_sel, sel_rows=sel_rows),
        grid=(b, n_grp, n_blk),
        in_specs=[pl.BlockSpec((None, w, BLK), lambda bi, g, j: (bi, g, j)),
                  pl.BlockSpec((None, seq, w), lambda bi, g, j: (bi, 0, g)),
                  pl.BlockSpec((None, n_blk, w, BLK), lambda bi, g, j: (bi, 0, g, 0)),
                  pl.BlockSpec((MOBA_GRP, 3 * BLK, BLK), lambda bi, g, j: (g, 0, 0), pipeline_mode=pl.Buffered(1))],
        out_specs=pl.BlockSpec((None, BLK, w), lambda bi, g, j: (bi, j, g)),
        out_shape=jax.ShapeDtypeStruct((b, seq, d), BF16),
        scratch_shapes=[pltpu.VMEM((n_pairs, seq, 2 * LANES), BF16),
                        pltpu.VMEM((n_pairs, LANES, LANES), F32),
                        pltpu.VMEM((n_pairs, 2 * LANES, 2 * BLK), BF16),
                        pltpu.VMEM((n_pairs, 1, 2 * BLK), F32),
                        pltpu.VMEM((n_pairs, 1, 2 * BLK), F32),
                        pltpu.VMEM((n_pairs, LANES, 2 * BLK), F32),
                        pltpu.VMEM((2, n_pairs, 2 * BLK, 2 * BLK), F32),
                        pltpu.VMEM((2, n_pairs, 1, 2 * BLK), F32)],
        compiler_params=_cparams("parallel", "parallel", "arbitrary"),
        name="moba_attn",
    )(q_t, k, v_t, bt)


def _dsa_proj_kernel(x_ref, g_ref, win_ref, wkvt_ref, wwt_ref, gq_ref, gkv_ref, gkvt_ref, wuqt_ref, wqit_ref,
                     wuvt_ref, ckv_ref, vt_ref, kk_ref, widx_ref, qnt_ref, qit_ref, *, idx_scale):
    hn = _rms(x_ref[...], g_ref[...]).astype(BF16)
    proj = _dot(hn, win_ref[...])
    c_q = _rms(proj[:, 0:DSA_LORA], gq_ref[...]).astype(BF16)
    ckv_ref[...] = _rms(proj[:, DSA_LORA:2 * DSA_LORA], gkv_ref[...]).astype(BF16)
    kk_ref[...] = proj[:, 2 * DSA_LORA:2 * DSA_LORA + IDX_DIM].astype(BF16)
    ct = _dot_nt(wkvt_ref[...], hn)
    scale = lax.rsqrt(jnp.mean(ct * ct, axis=0, keepdims=True) + EPS)
    gt = jnp.concatenate([gkvt_ref[...]] * (ct.shape[1] // LANES), axis=1)
    v_t = _dot(wuvt_ref[...], (ct * scale * gt).astype(BF16)).astype(BF16)
    for r in range(vt_ref.shape[0]):
        vt_ref[r] = v_t[:, r * BLK:(r + 1) * BLK]
    widx_ref[...] = _dot_nt(wwt_ref[...], hn)[0:IDX_HEADS, :] * idx_scale
    qnt_ref[...] = _dot_nt(wuqt_ref[...], c_q).astype(BF16)
    qit_ref[...] = _dot_nt(wqit_ref[...], c_q).astype(BF16)


def _dsa_proj(x3, g, w_in_row, w_kv_t, w_w_t, g_q, g_kv, w_uq_t, w_qi_t, w_uv_t):
    b, seq, d = x3.shape
    tm = PROJ_TM
    g2, gq2, gkv2 = g.reshape(1, d), g_q.reshape(1, -1), g_kv.reshape(1, -1)
    gkv_t = jnp.broadcast_to(g_kv.reshape(-1, 1), (DSA_LORA, LANES))
    n_q, n_i, n_v = w_uq_t.shape[0], w_qi_t.shape[0], w_uv_t.shape[0]
    consts = (g2, w_in_row, w_kv_t, w_w_t, gq2, gkv2, gkv_t, w_uq_t, w_qi_t, w_uv_t)
    idx_scale = IDX_HEADS ** -0.5 * IDX_DIM ** -0.5
    return pl.pallas_call(
        functools.partial(_dsa_proj_kernel, idx_scale=idx_scale),
        grid=(b, seq // tm),
        in_specs=[pl.BlockSpec((None, tm, d), lambda bi, i: (bi, i, 0))] + [_full(a) for a in consts],
        out_specs=[pl.BlockSpec((None, tm, DSA_LORA), lambda bi, i: (bi, i, 0)),
                   pl.BlockSpec((None, tm // BLK, n_v, BLK), lambda bi, i: (bi, i, 0, 0)),
                   pl.BlockSpec((None, tm, IDX_DIM), lambda bi, i: (bi, i, 0)),
                   pl.BlockSpec((None, IDX_HEADS, tm), lambda bi, i: (bi, 0, i)),
                   pl.BlockSpec((None, n_q, tm), lambda bi, i: (bi, 0, i)),
                   pl.BlockSpec((None, n_i, tm), lambda bi, i: (bi, 0, i))],
        out_shape=[jax.ShapeDtypeStruct((b, seq, DSA_LORA), BF16),
                   jax.ShapeDtypeStruct((b, seq // BLK, n_v, BLK), BF16),
                   jax.ShapeDtypeStruct((b, seq, IDX_DIM), BF16),
                   jax.ShapeDtypeStruct((b, IDX_HEADS, seq), F32),
                   jax.ShapeDtypeStruct((b, n_q, seq), BF16),
                   jax.ShapeDtypeStruct((b, n_i, seq), BF16)],
        compiler_params=_cparams("parallel", "parallel"),
        name="dsa_proj",
    )(x3, *consts)


def _dsa_select_kernel(qit_ref, w_ref, kk_ref, o_ref, keys_ref, half_ref, *, top_k, idx_bits):
    c = pl.program_id(1)
    n_kc, tk, tq = keys_ref.shape
    n_proc = c + 1
    key_row = lax.broadcasted_iota(I32, (tk, tq), 0)
    q_pos = c * tq + lax.broadcasted_iota(I32, (tk, tq), 1)
    w = w_ref[...]

    def score_chunk(kc, _):
        kt = kk_ref[pl.ds(pl.multiple_of(kc * tk, tk), tk), :]
        acc = jnp.zeros((tk, tq), F32)
        for h in range(IDX_HEADS):
            x = _dot(kt, qit_ref[h * IDX_DIM:(h + 1) * IDX_DIM, :])
            acc = acc + w[h:h + 1, :] * jnp.maximum(x, 0.0)
        sc = jnp.where(kc * tk + key_row <= q_pos, acc, NEG)
        bits = pltpu.bitcast(sc, I32)
        key = jnp.where(bits < 0, bits ^ 0x7FFFFFFF, bits)
        keys_ref[kc] = key
        half_ref[kc] = jnp.right_shift(key, 16).astype(I16)
        return 0

    lax.fori_loop(0, n_proc, score_chunk, 0)

    n_pair = (n_proc + 1) // 2

    @pl.when(n_proc % 2 == 1)
    def _():
        half_ref[n_proc] = jnp.full((tk, tq), -32768, I16)

    def count(pred):
        def body(kc, tot):
            hit = jnp.where(pred(keys_ref[kc], kc * tk + key_row), 1, 0)
            return tot + jnp.sum(hit, axis=0, keepdims=True)
        return lax.fori_loop(0, n_proc, body, jnp.zeros((1, tq), I32))

    def count_half(cand, strict=False):
        cand16 = cand.astype(I16)

        def body(i, part):
            tiles = []
            for kc in (2 * i, 2 * i + 1):
                x = half_ref[kc]
                hit = jnp.where((x > cand16) if strict else (x >= cand16), jnp.int16(1), jnp.int16(0))
                tiles += [hit[r:r + BF16_ROWS, :] for r in range(0, tk, BF16_ROWS)]
            while len(tiles) > 1:
                tiles = [a + b for a, b in zip(tiles[::2], tiles[1::2])]
            return part + tiles[0]
        part = lax.fori_loop(0, n_pair, body, jnp.zeros((BF16_ROWS, tq), I16))
        return jnp.sum(part.astype(I32), axis=0, keepdims=True)

    def largest_half(need):
        def bit(i, u):
            cand_u = u | jnp.left_shift(jnp.int32(1), 15 - i)
            return jnp.where(count_half(cand_u - 32768) >= need, cand_u, u)
        return lax.fori_loop(0, 16, bit, jnp.zeros((1, tq), I32)) - 32768

    thr_hi = largest_half(top_k)
    above = count_half(thr_hi, strict=True)

    def low_chunk(kc, _):
        key = keys_ref[kc]
        low = jnp.bitwise_and(key, 0xFFFF) - 32768
        half_ref[kc] = jnp.where(jnp.right_shift(key, 16) == thr_hi, low, -32768).astype(I16)
        return 0

    lax.fori_loop(0, n_proc, low_chunk, 0)
    thr_lo = largest_half(top_k - above)
    cnt_ge = above + count_half(thr_lo)
    thr = jnp.left_shift(thr_hi, 16) + (thr_lo + 32768)

    def tie_cut():
        need = top_k - count(lambda key, pos: key > thr)

        def index_bit(i, x):
            cand = x | jnp.left_shift(jnp.int32(1), idx_bits - 1 - i)
            cnt = count(lambda key, pos: (key == thr) & (pos < cand))
            return jnp.where(cnt < need, cand, x)
        return lax.fori_loop(0, idx_bits, index_bit, jnp.zeros((1, tq), I32))

    last_tie = lax.cond(jnp.max(cnt_ge) > top_k, tie_cut, lambda: jnp.full((1, tq), 2 ** idx_bits, I32))

    def write_chunk(kc, _):
        key = keys_ref[kc]
        pos = kc * tk + key_row
        chosen = (key > thr) | ((key == thr) & (pos <= last_tie))
        rows = pl.ds(pl.multiple_of(kc * tk, tk), tk)
        o_ref[rows, :] = jnp.where(chosen & (pos <= q_pos), 0.0, NEG).astype(o_ref.dtype)
        return 0

    lax.fori_loop(0, n_proc, write_chunk, 0)

    def fill_chunk(kc, _):
        o_ref[pl.ds(pl.multiple_of(kc * tk, tk), tk), :] = jnp.full((tk, tq), NEG, o_ref.dtype)
        return 0

    lax.fori_loop(n_proc, n_kc, fill_chunk, 0)


def _dsa_select(qi_t, widx_t, kk):
    b, seq, _ = kk.shape
    t = BLK
    n_kc = seq // t
    top_k = min(DSA_TOPK_MAX, seq // 4)
    assert top_k <= t
    assert n_kc % 2 == 0
    idx_bits = max(1, (seq - 1).bit_length())
    return pl.pallas_call(
        functools.partial(_dsa_select_kernel, top_k=top_k, idx_bits=idx_bits),
        grid=(b, seq // t),
        in_specs=[pl.BlockSpec((None, qi_t.shape[1], t), lambda bi, c: (bi, 0, c)),
                  pl.BlockSpec((None, IDX_HEADS, t), lambda bi, c: (bi, 0, c)),
                  pl.BlockSpec((None, seq, IDX_DIM), lambda bi, c: (bi, 0, 0))],
        out_specs=pl.BlockSpec((None, None, seq, t), lambda bi, c: (bi, c, 0, 0)),
        out_shape=jax.ShapeDtypeStruct((b, seq // t, seq, t), BF16),
        scratch_shapes=[pltpu.VMEM((n_kc, t, t), I32),
                        pltpu.VMEM((n_kc, t, t), I16)],
        compiler_params=_cparams("parallel", "arbitrary"),
        name="dsa_select",
    )(qi_t, widx_t, kk)


def _dsa_attn_kernel(qnt_ref, ckv_ref, vt_ref, mb_ref, bt_ref, wukt_ref, o_ref,
                     qlat_ref, acc_ref, m_ref, l_ref, s_ref, smax_ref, *, n_heads):
    qt = pl.program_id(1)
    t = BLK
    n_grp = n_heads // DSA_GRP
    row = lax.broadcasted_iota(I32, (LANES, t), 0)
    ones_rows = jnp.ones((BF16_ROWS, t), BF16)

    for hp in range(n_heads // 2):
        qp = qnt_ref[hp * LANES:(hp + 1) * LANES, :]
        for hh in range(2):
            qh = jnp.where((row < HEAD_DIM) if hh == 0 else (row >= HEAD_DIM), qp, jnp.zeros_like(qp))
            h = 2 * hp + hh
            qlat_ref[:, h * t:(h + 1) * t] = _dot(wukt_ref[hp], qh).astype(BF16)

    m_ref[...] = jnp.full(m_ref.shape, NEG, F32)
    l_ref[...] = jnp.zeros_like(l_ref)
    acc_ref[...] = jnp.zeros_like(acc_ref)

    def grp_cols(g):
        return slice(g * DSA_GRP * t, (g + 1) * DSA_GRP * t)

    def key_rows(kk):
        return pl.ds(pl.multiple_of(kk * t, t), t)

    def fill(slot, kk, bias_row0, off=None):
        kt = ckv_ref[key_rows(kk), :]
        mb = mb_ref[key_rows(kk), :].astype(F32)
        if off is not None:
            mb = mb + off
        for g in range(n_grp):
            s_grp = _dot(kt, qlat_ref[:, grp_cols(g)])
            parts = []
            for i in range(DSA_GRP):
                s = s_grp[:, i * t:(i + 1) * t] + mb
                if bias_row0 is not None:
                    s = s + bt_ref[g * DSA_GRP + i, bias_row0:bias_row0 + t, :]
                parts.append(s)
            s = jnp.concatenate(parts, axis=1)
            s_ref[slot, :, grp_cols(g)] = s
            smax_ref[slot, :, grp_cols(g)] = jnp.max(s, axis=0, keepdims=True)

    def drain(slot, kk):
        for g in range(n_grp):
            cols = grp_cols(g)
            m_old = m_ref[:, cols]
            m_new = jnp.maximum(m_old, smax_ref[slot, :, cols])
            alpha = jnp.exp2(m_old - m_new)
            pr = jnp.exp2(s_ref[slot, :, cols] - m_new).astype(BF16)
            m_ref[:, cols] = m_new
            for i in range(DSA_GRP):
                h = g * DSA_GRP + i
                feats = slice(h * HEAD_DIM, (h + 1) * HEAD_DIM)
                hc = slice(i * t, (i + 1) * t)
                lc = slice(h * t, (h + 1) * t)
                pv = _dot(jnp.concatenate([vt_ref[kk, feats, :], ones_rows], axis=0), pr[:, hc])
                acc_ref[feats, :] = alpha[:, hc] * acc_ref[feats, :] + pv[0:HEAD_DIM, :]
                l_ref[:, lc] = alpha[:, hc] * l_ref[:, lc] + pv[HEAD_DIM:HEAD_DIM + 1, :]

    n_far = jnp.maximum(qt - 1, 0)
    fill(0, qt, BLK)
    fill(1, jnp.maximum(qt - 1, 0), 0, off=jnp.where(qt >= 1, 0.0, NEG))
    drain(0, qt)
    fill(0, 0, None)
    drain(1, jnp.maximum(qt - 1, 0))

    def far_pair(i, _):
        kk = 2 * i
        fill(1, kk + 1, None)
        drain(0, kk)
        fill(0, jnp.minimum(kk + 2, n_far - 1), None)
        drain(1, kk + 1)
        return 0

    lax.fori_loop(0, n_far // 2, far_pair, 0)

    @pl.when(n_far % 2 == 1)
    def _():
        drain(0, n_far - 1)

    for h in range(n_heads):
        feats = slice(h * HEAD_DIM, (h + 1) * HEAD_DIM)
        acc_ref[feats, :] = acc_ref[feats, :] / l_ref[:, h * t:(h + 1) * t]
    o_ref[...] = acc_ref[...].T.astype(o_ref.dtype)


def _dsa_attention(qn_t, ckv, v_t, mask_t, bt, w_uk_t, n_heads):
    b, seq, _ = ckv.shape
    t = BLK
    d_q = qn_t.shape[1]
    return pl.pallas_call(
        functools.partial(_dsa_attn_kernel, n_heads=n_heads),
        grid=(b, seq // t),
        in_specs=[pl.BlockSpec((None, d_q, t), lambda bi, q: (bi, 0, q)),
                  pl.BlockSpec((None, seq, DSA_LORA), lambda bi, q: (bi, 0, 0)),
                  pl.BlockSpec((None, seq // t, d_q, t), lambda bi, q: (bi, 0, 0, 0)),
                  pl.BlockSpec((None, None, seq, t), lambda bi, q: (bi, q, 0, 0)),
                  _full(bt), _full(w_uk_t)],
        out_specs=pl.BlockSpec((None, t, d_q), lambda bi, q: (bi, q, 0)),
        out_shape=jax.ShapeDtypeStruct((b, seq, d_q), BF16),
        scratch_shapes=[pltpu.VMEM((DSA_LORA, n_heads * t), BF16),
                        pltpu.VMEM((d_q, t), F32),
                        pltpu.VMEM((1, n_heads * t), F32),
                        pltpu.VMEM((1, n_heads * t), F32),
                        pltpu.VMEM((2, t, n_heads * t), F32),
                        pltpu.VMEM((2, 1, n_heads * t), F32)],
        compiler_params=_cparams("parallel", "arbitrary"),
        name="dsa_attn",
    )(qn_t, ckv, v_t, mask_t, bt, w_uk_t)


def _moba_layer(x2, batch, seq, g, w_qkv, bt, n_heads):
    d = x2.shape[1]
    wq_t = (w_qkv[:, :d] * (HEAD_DIM ** -0.5 * LOG2E)).T.astype(BF16)
    wk = w_qkv[:, d:2 * d].astype(BF16)
    wv_t = w_qkv[:, 2 * d:].T.astype(BF16)
    q_t, k, v_t = _moba_proj(x2.reshape(batch, seq, d), g, wq_t, wk, wv_t)
    attn = _moba_attention(q_t, k, v_t, bt, n_heads)
    return attn.reshape(batch * seq, d)


def _dsa_layer(x2, batch, seq, g, w_in, g_q, g_kv, w_uq, w_qi, w_uk, w_uv, bt, n_heads):
    d = x2.shape[1]
    k_lo = 2 * DSA_LORA
    w_in_row = jnp.concatenate([w_in[:, :k_lo + IDX_DIM], jnp.zeros((d, LANES - IDX_DIM), F32)], axis=1).astype(BF16)
    w_kv_t = w_in[:, DSA_LORA:k_lo].T.astype(BF16)
    w_w_t = jnp.concatenate([w_in[:, k_lo + IDX_DIM:].T, jnp.zeros((BF16_ROWS - IDX_HEADS, d), F32)]).astype(BF16)
    w_uq_t = (w_uq * (HEAD_DIM ** -0.5 * LOG2E)).T.astype(BF16)
    w_uv_t = jnp.transpose(w_uv, (0, 2, 1)).reshape(n_heads * HEAD_DIM, DSA_LORA).astype(BF16)
    ckv, v_t, kk, widx_t, qn_t, qi_t = _dsa_proj(x2.reshape(batch, seq, d), g, w_in_row, w_kv_t, w_w_t,
                                                 g_q, g_kv, w_uq_t, w_qi.T.astype(BF16), w_uv_t)
    mask_t = _dsa_select(qi_t, widx_t, kk)
    w_uk_t = jnp.transpose(w_uk.reshape(n_heads // 2, 2 * HEAD_DIM, DSA_LORA), (0, 2, 1)).astype(BF16)
    attn = _dsa_attention(qn_t, ckv, v_t, mask_t, bt[:, :2 * BLK], w_uk_t, n_heads)
    return attn.reshape(batch * seq, -1)


def kernel(x, rel_bias, ln_attn, ln_mlp, moba_w_qkv, moba_w_o, dsa_w_in, dsa_g_q, dsa_g_kv, dsa_w_uq, dsa_w_qi,
           dsa_w_uk, dsa_w_uv, dsa_w_o, mlp_w_up, mlp_w_down, final_norm):
    batch, seq, d = x.shape
    n_heads = rel_bias.shape[1]
    depth = ln_attn.shape[0]
    bt = _bias_tiles(rel_bias)
    x2 = x.reshape(batch * seq, d)
    for i in range(depth):
        j = i // 2
        if i % 2 == 0:
            attn, w_o = _moba_layer(x2, batch, seq, ln_attn[i], moba_w_qkv[j], bt, n_heads), moba_w_o[j]
        else:
            attn, w_o = _dsa_layer(x2, batch, seq, ln_attn[i], dsa_w_in[j], dsa_g_q[j], dsa_g_kv[j], dsa_w_uq[j],
                                   dsa_w_qi[j], dsa_w_uk[j], dsa_w_uv[j], bt, n_heads), dsa_w_o[j]
        x2 = _block_tail(x2, attn, w_o.astype(BF16), ln_mlp[i], mlp_w_up[i].astype(BF16), mlp_w_down[i].astype(BF16),
                         g_final=final_norm if i == depth - 1 else None)
    return x2.reshape(batch, seq, d)
```

```python
import functools
import math

import numpy as np
import jax
import jax.numpy as jnp
from jax import lax
from jax.experimental import pallas as pl
from jax.experimental.pallas import tpu as pltpu

F32 = jnp.float32
BF16 = jnp.bfloat16
I32 = jnp.int32
I16 = jnp.int16

EPS = 1e-6
NEG = -1e30
INT_MIN = -(2 ** 31)
LOG2E = 1.0 / math.log(2.0)

NUM_BUCKETS = 32
MAX_DISTANCE = 128
HEAD_DIM = 64
LANES = 128
BF16_ROWS = 16
BLK = 256
PROJ_TM = 512
SEL_TQ = 512
MOBA_TOPK = 3
MOBA_GRP = 8
DSA_TOPK_MAX = 256
DSA_LORA = 256
DSA_GRP = 4
IDX_HEADS = 8
IDX_DIM = 64
VMEM_LIMIT = 56 * 1024 * 1024


def _dot(a, b):
    return jnp.dot(a, b, preferred_element_type=F32)


def _dot_nt(a, b):
    return lax.dot_general(a, b, (((1,), (1,)), ((), ())), preferred_element_type=F32)


def _rms(x, g):
    return x * lax.rsqrt(jnp.mean(x * x, axis=-1, keepdims=True) + EPS) * g


def _cparams(*sem):
    return pltpu.CompilerParams(dimension_semantics=sem, vmem_limit_bytes=VMEM_LIMIT)


def _full(a):
    return pl.BlockSpec(a.shape, lambda *_: (0,) * a.ndim, pipeline_mode=pl.Buffered(1))


def _bucket_map_t():
    jj = np.arange(2 * BLK)[:, None]
    i = np.arange(BLK)[None, :]
    d = i - jj + BLK
    n = np.maximum(d, 0)
    exact = NUM_BUCKETS // 2
    nf = np.maximum(n, 1).astype(np.float32)
    large = exact + (np.log(nf / exact) / math.log(MAX_DISTANCE / exact) * (NUM_BUCKETS - exact)).astype(np.int32)
    large = np.minimum(large, NUM_BUCKETS - 1)
    bucket = np.where(n < exact, n, large)
    bmap = np.where(d >= 0, bucket, -1).astype(np.int32)
    return np.concatenate([bmap, bmap[:BLK]], axis=0)


def _bias_tile_kernel(tab_ref, bmap_ref, o_ref):
    h = pl.program_id(0)
    bm = bmap_ref[...]
    base = tab_ref[NUM_BUCKETS - 1, h]
    acc = jnp.full(bm.shape, NEG, F32)
    for b in range(NUM_BUCKETS):
        acc = jnp.where(bm == b, (tab_ref[b, h] - base) * LOG2E, acc)
    o_ref[...] = acc


def _bias_tiles(rel_bias):
    n_heads = rel_bias.shape[1]
    bmap = jnp.asarray(_bucket_map_t())
    return pl.pallas_call(
        _bias_tile_kernel,
        grid=(n_heads,),
        in_specs=[pl.BlockSpec(memory_space=pltpu.SMEM),
                  pl.BlockSpec((3 * BLK, BLK), lambda h: (0, 0))],
        out_specs=pl.BlockSpec((None, 3 * BLK, BLK), lambda h: (h, 0, 0)),
        out_shape=jax.ShapeDtypeStruct((n_heads, 3 * BLK, BLK), F32),
        name="bias_tiles",
    )(rel_bias, bmap)


def _block_tail_kernel(x_ref, a_ref, wo_ref, g_ref, wu_ref, wd_ref, *rest, final):
    if final:
        gf_ref, o_ref, x1_ref, xn_ref, acc_ref = rest
    else:
        o_ref, x1_ref, xn_ref, acc_ref = rest
    f = pl.program_id(1)

    @pl.when(f == 0)
    def _():
        x1 = x_ref[...] + _dot(a_ref[...], wo_ref[...])
        x1_ref[...] = x1
        xn_ref[...] = _rms(x1, g_ref[...]).astype(BF16)
        acc_ref[...] = jnp.zeros_like(acc_ref)

    h = jnp.square(jnp.maximum(_dot(xn_ref[...], wu_ref[...]), 0.0)).astype(BF16)
    acc_ref[...] += _dot(h, wd_ref[...])

    @pl.when(f == pl.num_programs(1) - 1)
    def _():
        out = x1_ref[...] + acc_ref[...]
        o_ref[...] = _rms(out, gf_ref[...]) if final else out


def _block_tail(x2, attn, w_o, g, w_up, w_down, g_final=None, tm=1024, tf=1024):
    n, d = x2.shape
    k = attn.shape[1]
    d_ff = w_up.shape[1]
    final = g_final is not None
    row = lambda w: pl.BlockSpec((tm, w), lambda i, f: (i, 0))
    vec = pl.BlockSpec((1, d), lambda i, f: (0, 0))
    in_specs = [row(d), row(k), pl.BlockSpec((k, d), lambda i, f: (0, 0), pipeline_mode=pl.Buffered(1)), vec,
                pl.BlockSpec((d, tf), lambda i, f: (0, f)), pl.BlockSpec((tf, d), lambda i, f: (f, 0))]
    args = [x2, attn, w_o, g.reshape(1, d), w_up, w_down]
    if final:
        in_specs.append(vec)
        args.append(g_final.reshape(1, d))
    return pl.pallas_call(
        functools.partial(_block_tail_kernel, final=final),
        grid=(n // tm, d_ff // tf),
        in_specs=in_specs,
        out_specs=row(d),
        out_shape=jax.ShapeDtypeStruct((n, d), F32),
        scratch_shapes=[pltpu.VMEM((tm, d), F32), pltpu.VMEM((tm, d), BF16), pltpu.VMEM((tm, d), F32)],
        compiler_params=_cparams("parallel", "arbitrary"),
        name="block_tail",
    )(*args)


def _moba_proj_kernel(x_ref, g_ref, wqt_ref, wk_ref, wvt_ref, qt_ref, k_ref, vt_ref):
    xn = _rms(x_ref[...], g_ref[...]).astype(BF16)
    qt_ref[...] = _dot_nt(wqt_ref[...], xn).astype(BF16)
    k_ref[...] = _dot(xn, wk_ref[...]).astype(BF16)
    v_t = _dot_nt(wvt_ref[...], xn).astype(BF16)
    for r in range(vt_ref.shape[0]):
        vt_ref[r] = v_t[:, r * BLK:(r + 1) * BLK]


def _moba_proj(x3, g, wq_t, wk, wv_t):
    b, seq, d = x3.shape
    g2 = g.reshape(1, d)
    return pl.pallas_call(
        _moba_proj_kernel,
        grid=(b, seq // PROJ_TM),
        in_specs=[pl.BlockSpec((None, PROJ_TM, d), lambda bi, i: (bi, i, 0)),
                  _full(g2), _full(wq_t), _full(wk), _full(wv_t)],
        out_specs=[pl.BlockSpec((None, d, PROJ_TM), lambda bi, i: (bi, 0, i)),
                   pl.BlockSpec((None, PROJ_TM, d), lambda bi, i: (bi, i, 0)),
                   pl.BlockSpec((None, PROJ_TM // BLK, d, BLK), lambda bi, i: (bi, i, 0, 0))],
        out_shape=[jax.ShapeDtypeStruct((b, d, seq), BF16),
                   jax.ShapeDtypeStruct((b, seq, d), BF16),
                   jax.ShapeDtypeStruct((b, seq // BLK, d, BLK), BF16)],
        compiler_params=_cparams("parallel", "parallel"),
        name="moba_proj",
    )(x3, g2, wq_t, wk, wv_t)


def _moba_kernel(qt_ref, k_ref, vt_ref, bt_ref, o_ref, kaug_ref, kmean_ref, qfar_ref, m_ref, l_ref, acc_ref, s_ref,
                 smax_ref, *, n_blk, k_sel, sel_rows):
    j = pl.program_id(2)
    seq = k_ref.shape[0]
    n_pairs = MOBA_GRP // 2
    t = BLK
    row = lax.broadcasted_iota(I32, (LANES, t), 0)
    blk_id = lax.broadcasted_iota(I32, (sel_rows, t), 0)
    ones_2 = jnp.ones((BF16_ROWS, 2 * t), BF16)

    @pl.when(j == 0)
    def _():
        krow = lax.broadcasted_iota(I32, (seq, LANES), 0)
        kcol = lax.broadcasted_iota(I32, (seq, LANES), 1)
        onehot = jnp.where(krow // BLK == kcol, 1.0, 0.0).astype(BF16)
        for p in range(n_pairs):
            kaug_ref[p, :, 0:LANES] = k_ref[:, p * LANES:(p + 1) * LANES]
            kaug_ref[p, :, LANES:2 * LANES] = onehot
            kmean_ref[p] = jnp.zeros((LANES, LANES), F32)
            for n in range(n_blk):
                kmean_ref[p, n:n + 1, :] = jnp.mean(
                    k_ref[n * BLK:(n + 1) * BLK, p * LANES:(p + 1) * LANES].astype(F32), axis=0, keepdims=True)

    near_blk = jnp.maximum(j - 1, 0)
    near0 = pl.multiple_of(near_blk * BLK, BLK)
    bias0 = pl.multiple_of(jnp.where(j == 0, BLK, 0), BLK)
    pad_rows = jnp.zeros((LANES - sel_rows, t), BF16)

    def store(slot, p, s):
        s_ref[slot, p] = s
        smax_ref[slot, p] = jnp.max(s, axis=0, keepdims=True)

    def drain(slot, blk0):
        for p in range(n_pairs):
            feats = slice(p * LANES, (p + 1) * LANES)
            m_old = m_ref[p]
            m_new = jnp.maximum(m_old, smax_ref[slot, p])
            alpha = jnp.exp2(m_old - m_new)
            pr = jnp.exp2(s_ref[slot, p] - m_new).astype(BF16)
            m_ref[p] = m_new
            v_t = jnp.concatenate([vt_ref[blk0, feats, :], vt_ref[blk0 + 1, feats, :]], axis=1)
            pv = _dot(jnp.concatenate([v_t, ones_2], axis=0), pr)
            acc_ref[p] = alpha * acc_ref[p] + pv[0:LANES, :]
            l_ref[p] = alpha * l_ref[p] + pv[LANES:LANES + 1, :]

    def fill_far(slot, pi):
        n0 = pl.multiple_of(pi * 2 * BLK, 2 * BLK)
        for p in range(n_pairs):
            store(slot, p, _dot(kaug_ref[p, pl.ds(n0, 2 * BLK), :], qfar_ref[p]))

    q_near = []
    for p in range(n_pairs):
        qp = qt_ref[p * LANES:(p + 1) * LANES, :]
        near, far = [], []
        for hh in range(2):
            qh = jnp.where((row < HEAD_DIM) if hh == 0 else (row >= HEAD_DIM), qp, jnp.zeros_like(qp))
            gate = _dot(kmean_ref[p], qh.astype(F32))[0:sel_rows, :]
            g = jnp.where(blk_id < j, gate, NEG)
            g = jnp.where(blk_id < n_blk, g, -jnp.inf)
            sel_bias = jnp.full((sel_rows, t), NEG, F32)
            for it in range(k_sel):
                mx = jnp.max(g, axis=0, keepdims=True)
                first = jnp.min(jnp.where(g == mx, blk_id, LANES), axis=0, keepdims=True)
                pick = blk_id == first
                sel_bias = jnp.where(pick, jnp.where(it < j, 0.0, NEG), sel_bias)
                g = jnp.where(pick, -jnp.inf, g)
            near_bias = jnp.where(blk_id == j, 0.0, sel_bias)
            far_bias = jnp.where(blk_id < j - 1, sel_bias, NEG)
            near.append(jnp.concatenate([qh, near_bias.astype(BF16), pad_rows], axis=0))
            far.append(jnp.concatenate([qh, far_bias.astype(BF16), pad_rows], axis=0))
        q_near.append(jnp.concatenate(near, axis=1))
        qfar_ref[p] = jnp.concatenate(far, axis=1)
        m_ref[p] = jnp.full((1, 2 * t), NEG, F32)
        l_ref[p] = jnp.zeros((1, 2 * t), F32)
        acc_ref[p] = jnp.zeros((LANES, 2 * t), F32)

    for p in range(n_pairs):
        bias = jnp.concatenate([bt_ref[2 * p, pl.ds(bias0, 2 * BLK), :], bt_ref[2 * p + 1, pl.ds(bias0, 2 * BLK), :]],
                               axis=1)
        store(0, p, _dot(kaug_ref[p, pl.ds(near0, 2 * BLK), :], q_near[p]) + bias)

    n_far = jnp.maximum(j, 1) // 2
    fill_far(1, 0)
    drain(0, near_blk)

    def far_two(i, _):
        pi = 2 * i
        fill_far(0, pi + 1)
        drain(1, 2 * pi)
        fill_far(1, jnp.minimum(pi + 2, n_far - 1))
        drain(0, 2 * pi + 2)
        return 0

    lax.fori_loop(0, n_far // 2, far_two, 0)

    @pl.when(n_far % 2 == 1)
    def _():
        drain(1, 2 * (n_far - 1))

    for p in range(n_pairs):
        acc = acc_ref[p] / l_ref[p]
        out_t = jnp.where(row < HEAD_DIM, acc[:, 0:t], acc[:, t:2 * t])
        o_ref[:, p * LANES:(p + 1) * LANES] = out_t.T.astype(o_ref.dtype)


def _moba_attention(q_t, k, v_t, bt, n_heads):
    b, seq, d = k.shape
    n_blk = seq // BLK
    n_grp = n_heads // MOBA_GRP
    n_pairs = MOBA_GRP // 2
    w = MOBA_GRP * HEAD_DIM
    k_sel = max(1, min(MOBA_TOPK, n_blk - 1))
    sel_rows = BF16_ROWS * pl.cdiv(n_blk, BF16_ROWS)
    assert sel_rows <= LANES and n_blk % 2 == 0
    return pl.pallas_call(
        functools.partial(_moba_kernel, n_blk=n_blk, k_sel=k_sel, sel_rows=sel_rows),
        grid=(b, n_grp, n_blk),
        in_specs=[pl.BlockSpec((None, w, BLK), lambda bi, g, j: (bi, g, j)),
                  pl.BlockSpec((None, seq, w), lambda bi, g, j: (bi, 0, g)),
                  pl.BlockSpec((None, n_blk, w, BLK), lambda bi, g, j: (bi, 0, g, 0)),
                  pl.BlockSpec((MOBA_GRP, 3 * BLK, BLK), lambda bi, g, j: (g, 0, 0), pipeline_mode=pl.Buffered(1))],
        out_specs=pl.BlockSpec((None, BLK, w), lambda bi, g, j: (bi, j, g)),
        out_shape=jax.ShapeDtypeStruct((b, seq, d), BF16),
        scratch_shapes=[pltpu.VMEM((n_pairs, seq, 2 * LANES), BF16),
                        pltpu.VMEM((n_pairs, LANES, LANES), F32),
                        pltpu.VMEM((n_pairs, 2 * LANES, 2 * BLK), BF16),
                        pltpu.VMEM((n_pairs, 1, 2 * BLK), F32),
                        pltpu.VMEM((n_pairs, 1, 2 * BLK), F32),
                        pltpu.VMEM((n_pairs, LANES, 2 * BLK), F32),
                        pltpu.VMEM((2, n_pairs, 2 * BLK, 2 * BLK), F32),
                        pltpu.VMEM((2, n_pairs, 1, 2 * BLK), F32)],
        compiler_params=_cparams("parallel", "parallel", "arbitrary"),
        name="moba_attn",
    )(q_t, k, v_t, bt)


def _dsa_proj_kernel(x_ref, g_ref, win_ref, wkvt_ref, wwt_ref, gq_ref, gkv_ref, gkvt_ref, wuqt_ref, wqit_ref,
                     wuvt_ref, ckv_ref, vt_ref, kk_ref, widx_ref, qnt_ref, qit_ref, *, idx_scale):
    hn = _rms(x_ref[...], g_ref[...]).astype(BF16)
    proj = _dot(hn, win_ref[...])
    c_q = _rms(proj[:, 0:DSA_LORA], gq_ref[...]).astype(BF16)
    ckv_ref[...] = _rms(proj[:, DSA_LORA:2 * DSA_LORA], gkv_ref[...]).astype(BF16)
    kk_ref[...] = proj[:, 2 * DSA_LORA:2 * DSA_LORA + IDX_DIM].astype(BF16)
    ct = _dot_nt(wkvt_ref[...], hn)
    scale = lax.rsqrt(jnp.mean(ct * ct, axis=0, keepdims=True) + EPS)
    gt = jnp.concatenate([gkvt_ref[...]] * (ct.shape[1] // LANES), axis=1)
    v_t = _dot(wuvt_ref[...], (ct * scale * gt).astype(BF16)).astype(BF16)
    for r in range(vt_ref.shape[0]):
        vt_ref[r] = v_t[:, r * BLK:(r + 1) * BLK]
    widx_ref[...] = _dot_nt(wwt_ref[...], hn)[0:IDX_HEADS, :] * idx_scale
    qnt_ref[...] = _dot_nt(wuqt_ref[...], c_q).astype(BF16)
    qit_ref[...] = _dot_nt(wqit_ref[...], c_q).astype(BF16)


def _dsa_proj(x3, g, w_in_row, w_kv_t, w_w_t, g_q, g_kv, w_uq_t, w_qi_t, w_uv_t):
    b, seq, d = x3.shape
    tm = PROJ_TM
    g2, gq2, gkv2 = g.reshape(1, d), g_q.reshape(1, -1), g_kv.reshape(1, -1)
    gkv_t = jnp.broadcast_to(g_kv.reshape(-1, 1), (DSA_LORA, LANES))
    n_q, n_i, n_v = w_uq_t.shape[0], w_qi_t.shape[0], w_uv_t.shape[0]
    consts = (g2, w_in_row, w_kv_t, w_w_t, gq2, gkv2, gkv_t, w_uq_t, w_qi_t, w_uv_t)
    idx_scale = IDX_HEADS ** -0.5 * IDX_DIM ** -0.5
    return pl.pallas_call(
        functools.partial(_dsa_proj_kernel, idx_scale=idx_scale),
        grid=(b, seq // tm),
        in_specs=[pl.BlockSpec((None, tm, d), lambda bi, i: (bi, i, 0))] + [_full(a) for a in consts],
        out_specs=[pl.BlockSpec((None, tm, DSA_LORA), lambda bi, i: (bi, i, 0)),
                   pl.BlockSpec((None, tm // BLK, n_v, BLK), lambda bi, i: (bi, i, 0, 0)),
                   pl.BlockSpec((None, tm, IDX_DIM), lambda bi, i: (bi, i, 0)),
                   pl.BlockSpec((None, IDX_HEADS, tm), lambda bi, i: (bi, 0, i)),
                   pl.BlockSpec((None, n_q, tm), lambda bi, i: (bi, 0, i)),
                   pl.BlockSpec((None, n_i, tm), lambda bi, i: (bi, 0, i))],
        out_shape=[jax.ShapeDtypeStruct((b, seq, DSA_LORA), BF16),
                   jax.ShapeDtypeStruct((b, seq // BLK, n_v, BLK), BF16),
                   jax.ShapeDtypeStruct((b, seq, IDX_DIM), BF16),
                   jax.ShapeDtypeStruct((b, IDX_HEADS, seq), F32),
                   jax.ShapeDtypeStruct((b, n_q, seq), BF16),
                   jax.ShapeDtypeStruct((b, n_i, seq), BF16)],
        compiler_params=_cparams("parallel", "parallel"),
        name="dsa_proj",
    )(x3, *consts)


def _dsa_select_kernel(qit_ref, w_ref, kk_ref, o_ref, keys_ref, half_ref, *, top_k, idx_bits):
    c = pl.program_id(1)
    n_kc, tk, tq = keys_ref.shape
    n_proc = (c + 1) * (tq // tk)
    key_row = lax.broadcasted_iota(I32, (tk, tq), 0)
    q_pos = c * tq + lax.broadcasted_iota(I32, (tk, tq), 1)
    w = w_ref[...]

    def score_chunk(kc, _):
        kt = kk_ref[pl.ds(pl.multiple_of(kc * tk, tk), tk), :]
        acc = jnp.zeros((tk, tq), F32)
        for h in range(IDX_HEADS):
            x = _dot(kt, qit_ref[h * IDX_DIM:(h + 1) * IDX_DIM, :])
            acc = acc + w[h:h + 1, :] * jnp.maximum(x, 0.0)
        sc = jnp.where(kc * tk + key_row <= q_pos, acc, NEG)
        bits = pltpu.bitcast(sc, I32)
        key = jnp.where(bits < 0, bits ^ 0x7FFFFFFF, bits)
        keys_ref[kc] = key
        half_ref[kc] = jnp.right_shift(key, 16).astype(I16)
        return 0

    lax.fori_loop(0, n_proc, score_chunk, 0)

    n_pair = n_proc // 2

    def count(pred):
        def body(kc, tot):
            hit = jnp.where(pred(keys_ref[kc], kc * tk + key_row), 1, 0)
            return tot + jnp.sum(hit, axis=0, keepdims=True)
        return lax.fori_loop(0, n_proc, body, jnp.zeros((1, tq), I32))

    def count_half(cand, strict=False):
        cand16 = cand.astype(I16)

        def body(i, part):
            tiles = []
            for kc in (2 * i, 2 * i + 1):
                x = half_ref[kc]
                hit = jnp.where((x > cand16) if strict else (x >= cand16), jnp.int16(1), jnp.int16(0))
                tiles += [hit[r:r + BF16_ROWS, :] for r in range(0, tk, BF16_ROWS)]
            while len(tiles) > 1:
                tiles = [a + b for a, b in zip(tiles[::2], tiles[1::2])]
            return part + tiles[0]
        part = lax.fori_loop(0, n_pair, body, jnp.zeros((BF16_ROWS, tq), I16))
        return jnp.sum(part.astype(I32), axis=0, keepdims=True)

    def largest_half(need):
        def bit(i, u):
            cand_u = u | jnp.left_shift(jnp.int32(1), 15 - i)
            return jnp.where(count_half(cand_u - 32768) >= need, cand_u, u)
        return lax.fori_loop(0, 16, bit, jnp.zeros((1, tq), I32)) - 32768

    thr_hi = largest_half(top_k)
    above = count_half(thr_hi, strict=True)

    def low_chunk(kc, _):
        key = keys_ref[kc]
        low = jnp.bitwise_and(key, 0xFFFF) - 32768
        half_ref[kc] = jnp.where(jnp.right_shift(key, 16) == thr_hi, low, -32768).astype(I16)
        return 0

    lax.fori_loop(0, n_proc, low_chunk, 0)
    thr_lo = largest_half(top_k - above)
    cnt_ge = above + count_half(thr_lo)
    thr = jnp.left_shift(thr_hi, 16) + (thr_lo + 32768)

    def tie_cut():
        need = top_k - count(lambda key, pos: key > thr)

        def index_bit(i, x):
            cand = x | jnp.left_shift(jnp.int32(1), idx_bits - 1 - i)
            cnt = count(lambda key, pos: (key == thr) & (pos < cand))
            return jnp.where(cnt < need, cand, x)
        return lax.fori_loop(0, idx_bits, index_bit, jnp.zeros((1, tq), I32))

    last_tie = lax.cond(jnp.max(cnt_ge) > top_k, tie_cut, lambda: jnp.full((1, tq), 2 ** idx_bits, I32))

    def write_chunk(kc, _):
        key = keys_ref[kc]
        pos = kc * tk + key_row
        chosen = (key > thr) | ((key == thr) & (pos <= last_tie))
        rows = pl.ds(pl.multiple_of(kc * tk, tk), tk)
        mask = jnp.where(chosen & (pos <= q_pos), 0.0, NEG).astype(o_ref.dtype)
        for r in range(tq // BLK):
            o_ref[r, rows, :] = mask[:, r * BLK:(r + 1) * BLK]
        return 0

    lax.fori_loop(0, n_proc, write_chunk, 0)

    def fill_chunk(kc, _):
        for r in range(tq // BLK):
            o_ref[r, pl.ds(pl.multiple_of(kc * tk, tk), tk), :] = jnp.full((tk, BLK), NEG, o_ref.dtype)
        return 0

    lax.fori_loop(n_proc, n_kc, fill_chunk, 0)


def _dsa_select(qi_t, widx_t, kk):
    b, seq, _ = kk.shape
    t = BLK
    tq = SEL_TQ
    n_kc = seq // t
    top_k = min(DSA_TOPK_MAX, seq // 4)
    assert top_k <= t
    assert tq % (2 * t) == 0 and seq % tq == 0
    idx_bits = max(1, (seq - 1).bit_length())
    return pl.pallas_call(
        functools.partial(_dsa_select_kernel, top_k=top_k, idx_bits=idx_bits),
        grid=(b, seq // tq),
        in_specs=[pl.BlockSpec((None, qi_t.shape[1], tq), lambda bi, c: (bi, 0, c)),
                  pl.BlockSpec((None, IDX_HEADS, tq), lambda bi, c: (bi, 0, c)),
                  pl.BlockSpec((None, seq, IDX_DIM), lambda bi, c: (bi, 0, 0))],
        out_specs=pl.BlockSpec((None, tq // t, seq, t), lambda bi, c: (bi, c, 0, 0)),
        out_shape=jax.ShapeDtypeStruct((b, seq // t, seq, t), BF16),
        scratch_shapes=[pltpu.VMEM((n_kc, t, tq), I32),
                        pltpu.VMEM((n_kc, t, tq), I16)],
        compiler_params=_cparams("parallel", "arbitrary"),
        name="dsa_select",
    )(qi_t, widx_t, kk)


def _dsa_attn_kernel(qnt_ref, ckv_ref, vt_ref, mb_ref, bt_ref, wukt_ref, o_ref,
                     qlat_ref, acc_ref, m_ref, l_ref, s_ref, smax_ref, *, n_heads):
    qt = pl.program_id(1)
    t = BLK
    n_grp = n_heads // DSA_GRP
    row = lax.broadcasted_iota(I32, (LANES, t), 0)
    ones_rows = jnp.ones((BF16_ROWS, t), BF16)

    for hp in range(n_heads // 2):
        qp = qnt_ref[hp * LANES:(hp + 1) * LANES, :]
        for hh in range(2):
            qh = jnp.where((row < HEAD_DIM) if hh == 0 else (row >= HEAD_DIM), qp, jnp.zeros_like(qp))
            h = 2 * hp + hh
            qlat_ref[:, h * t:(h + 1) * t] = _dot(wukt_ref[hp], qh).astype(BF16)

    m_ref[...] = jnp.full(m_ref.shape, NEG, F32)
    l_ref[...] = jnp.zeros_like(l_ref)
    acc_ref[...] = jnp.zeros_like(acc_ref)

    def grp_cols(g):
        return slice(g * DSA_GRP * t, (g + 1) * DSA_GRP * t)

    def key_rows(kk):
        return pl.ds(pl.multiple_of(kk * t, t), t)

    def fill(slot, kk, bias_row0, off=None):
        kt = ckv_ref[key_rows(kk), :]
        mb = mb_ref[key_rows(kk), :].astype(F32)
        if off is not None:
            mb = mb + off
        for g in range(n_grp):
            s_grp = _dot(kt, qlat_ref[:, grp_cols(g)])
            parts = []
            for i in range(DSA_GRP):
                s = s_grp[:, i * t:(i + 1) * t] + mb
                if bias_row0 is not None:
                    s = s + bt_ref[g * DSA_GRP + i, bias_row0:bias_row0 + t, :]
                parts.append(s)
            s = jnp.concatenate(parts, axis=1)
            s_ref[slot, :, grp_cols(g)] = s
            smax_ref[slot, :, grp_cols(g)] = jnp.max(s, axis=0, keepdims=True)

    def drain(slot, kk):
        for g in range(n_grp):
            cols = grp_cols(g)
            m_old = m_ref[:, cols]
            m_new = jnp.maximum(m_old, smax_ref[slot, :, cols])
            alpha = jnp.exp2(m_old - m_new)
            pr = jnp.exp2(s_ref[slot, :, cols] - m_new).astype(BF16)
            m_ref[:, cols] = m_new
            for i in range(DSA_GRP):
                h = g * DSA_GRP + i
                feats = slice(h * HEAD_DIM, (h + 1) * HEAD_DIM)
                hc = slice(i * t, (i + 1) * t)
                lc = slice(h * t, (h + 1) * t)
                pv = _dot(jnp.concatenate([vt_ref[kk, feats, :], ones_rows], axis=0), pr[:, hc])
                acc_ref[feats, :] = alpha[:, hc] * acc_ref[feats, :] + pv[0:HEAD_DIM, :]
                l_ref[:, lc] = alpha[:, hc] * l_ref[:, lc] + pv[HEAD_DIM:HEAD_DIM + 1, :]

    n_far = jnp.maximum(qt - 1, 0)
    fill(0, qt, BLK)
    fill(1, jnp.maximum(qt - 1, 0), 0, off=jnp.where(qt >= 1, 0.0, NEG))
    drain(0, qt)
    fill(0, 0, None)
    drain(1, jnp.maximum(qt - 1, 0))

    def far_pair(i, _):
        kk = 2 * i
        fill(1, kk + 1, None)
        drain(0, kk)
        fill(0, jnp.minimum(kk + 2, n_far - 1), None)
        drain(1, kk + 1)
        return 0

    lax.fori_loop(0, n_far // 2, far_pair, 0)

    @pl.when(n_far % 2 == 1)
    def _():
        drain(0, n_far - 1)

    for h in range(n_heads):
        feats = slice(h * HEAD_DIM, (h + 1) * HEAD_DIM)
        acc_ref[feats, :] = acc_ref[feats, :] / l_ref[:, h * t:(h + 1) * t]
    o_ref[...] = acc_ref[...].T.astype(o_ref.dtype)


def _dsa_attention(qn_t, ckv, v_t, mask_t, bt, w_uk_t, n_heads):
    b, seq, _ = ckv.shape
    t = BLK
    d_q = qn_t.shape[1]
    return pl.pallas_call(
        functools.partial(_dsa_attn_kernel, n_heads=n_heads),
        grid=(b, seq // t),
        in_specs=[pl.BlockSpec((None, d_q, t), lambda bi, q: (bi, 0, q)),
                  pl.BlockSpec((None, seq, DSA_LORA), lambda bi, q: (bi, 0, 0)),
                  pl.BlockSpec((None, seq // t, d_q, t), lambda bi, q: (bi, 0, 0, 0)),
                  pl.BlockSpec((None, None, seq, t), lambda bi, q: (bi, q, 0, 0)),
                  _full(bt), _full(w_uk_t)],
        out_specs=pl.BlockSpec((None, t, d_q), lambda bi, q: (bi, q, 0)),
        out_shape=jax.ShapeDtypeStruct((b, seq, d_q), BF16),
        scratch_shapes=[pltpu.VMEM((DSA_LORA, n_heads * t), BF16),
                        pltpu.VMEM((d_q, t), F32),
                        pltpu.VMEM((1, n_heads * t), F32),
                        pltpu.VMEM((1, n_heads * t), F32),
                        pltpu.VMEM((2, t, n_heads * t), F32),
                        pltpu.VMEM((2, 1, n_heads * t), F32)],
        compiler_params=_cparams("parallel", "arbitrary"),
        name="dsa_attn",
    )(qn_t, ckv, v_t, mask_t, bt, w_uk_t)


def _moba_layer(x2, batch, seq, g, w_qkv, bt, n_heads):
    d = x2.shape[1]
    wq_t = (w_qkv[:, :d] * (HEAD_DIM ** -0.5 * LOG2E)).T.astype(BF16)
    wk = w_qkv[:, d:2 * d].astype(BF16)
    wv_t = w_qkv[:, 2 * d:].T.astype(BF16)
    q_t, k, v_t = _moba_proj(x2.reshape(batch, seq, d), g, wq_t, wk, wv_t)
    attn = _moba_attention(q_t, k, v_t, bt, n_heads)
    return attn.reshape(batch * seq, d)


def _dsa_layer(x2, batch, seq, g, w_in, g_q, g_kv, w_uq, w_qi, w_uk, w_uv, bt, n_heads):
    d = x2.shape[1]
    k_lo = 2 * DSA_LORA
    w_in_row = jnp.concatenate([w_in[:, :k_lo + IDX_DIM], jnp.zeros((d, LANES - IDX_DIM), F32)], axis=1).astype(BF16)
    w_kv_t = w_in[:, DSA_LORA:k_lo].T.astype(BF16)
    w_w_t = jnp.concatenate([w_in[:, k_lo + IDX_DIM:].T, jnp.zeros((BF16_ROWS - IDX_HEADS, d), F32)]).astype(BF16)
    w_uq_t = (w_uq * (HEAD_DIM ** -0.5 * LOG2E)).T.astype(BF16)
    w_uv_t = jnp.transpose(w_uv, (0, 2, 1)).reshape(n_heads * HEAD_DIM, DSA_LORA).astype(BF16)
    ckv, v_t, kk, widx_t, qn_t, qi_t = _dsa_proj(x2.reshape(batch, seq, d), g, w_in_row, w_kv_t, w_w_t,
                                                 g_q, g_kv, w_uq_t, w_qi.T.astype(BF16), w_uv_t)
    mask_t = _dsa_select(qi_t, widx_t, kk)
    w_uk_t = jnp.transpose(w_uk.reshape(n_heads // 2, 2 * HEAD_DIM, DSA_LORA), (0, 2, 1)).astype(BF16)
    attn = _dsa_attention(qn_t, ckv, v_t, mask_t, bt[:, :2 * BLK], w_uk_t, n_heads)
    return attn.reshape(batch * seq, -1)


def kernel(x, rel_bias, ln_attn, ln_mlp, moba_w_qkv, moba_w_o, dsa_w_in, dsa_g_q, dsa_g_kv, dsa_w_uq, dsa_w_qi,
           dsa_w_uk, dsa_w_uv, dsa_w_o, mlp_w_up, mlp_w_down, final_norm):
    batch, seq, d = x.shape
    n_heads = rel_bias.shape[1]
    depth = ln_attn.shape[0]
    bt = _bias_tiles(rel_bias)
    x2 = x.reshape(batch * seq, d)
    for i in range(depth):
        j = i // 2
        if i % 2 == 0:
            attn, w_o = _moba_layer(x2, batch, seq, ln_attn[i], moba_w_qkv[j], bt, n_heads), moba_w_o[j]
        else:
            attn, w_o = _dsa_layer(x2, batch, seq, ln_attn[i], dsa_w_in[j], dsa_g_q[j], dsa_g_kv[j], dsa_w_uq[j],
                                   dsa_w_qi[j], dsa_w_uk[j], dsa_w_uv[j], bt, n_heads), dsa_w_o[j]
        x2 = _block_tail(x2, attn, w_o.astype(BF16), ln_mlp[i], mlp_w_up[i].astype(BF16), mlp_w_down[i].astype(BF16),
                         g_final=final_norm if i == depth - 1 else None)
    return x2.reshape(batch, seq, d)
```

```python
import functools
import math

import numpy as np
import jax
import jax.numpy as jnp
from jax import lax
from jax.experimental import pallas as pl
from jax.experimental.pallas import tpu as pltpu

F32 = jnp.float32
BF16 = jnp.bfloat16
I32 = jnp.int32
I16 = jnp.int16

EPS = 1e-6
NEG = -1e30
INT_MIN = -(2 ** 31)
LOG2E = 1.0 / math.log(2.0)

NUM_BUCKETS = 32
MAX_DISTANCE = 128
HEAD_DIM = 64
LANES = 128
BF16_ROWS = 16
BLK = 256
PROJ_TM = 512
MOBA_TOPK = 3
MOBA_GRP = 8
DSA_TOPK_MAX = 256
DSA_LORA = 256
DSA_GRP = 4
IDX_HEADS = 8
IDX_DIM = 64
VMEM_LIMIT = 56 * 1024 * 1024


def _dot(a, b):
    return jnp.dot(a, b, preferred_element_type=F32)


def _dot_nt(a, b):
    return lax.dot_general(a, b, (((1,), (1,)), ((), ())), preferred_element_type=F32)


def _rms(x, g):
    return x * lax.rsqrt(jnp.mean(x * x, axis=-1, keepdims=True) + EPS) * g


def _cparams(*sem):
    return pltpu.CompilerParams(dimension_semantics=sem, vmem_limit_bytes=VMEM_LIMIT)


def _full(a):
    return pl.BlockSpec(a.shape, lambda *_: (0,) * a.ndim, pipeline_mode=pl.Buffered(1))


def _bucket_map_t():
    jj = np.arange(2 * BLK)[:, None]
    i = np.arange(BLK)[None, :]
    d = i - jj + BLK
    n = np.maximum(d, 0)
    exact = NUM_BUCKETS // 2
    nf = np.maximum(n, 1).astype(np.float32)
    large = exact + (np.log(nf / exact) / math.log(MAX_DISTANCE / exact) * (NUM_BUCKETS - exact)).astype(np.int32)
    large = np.minimum(large, NUM_BUCKETS - 1)
    bucket = np.where(n < exact, n, large)
    bmap = np.where(d >= 0, bucket, -1).astype(np.int32)
    return np.concatenate([bmap, bmap[:BLK]], axis=0)


def _bias_tile_kernel(tab_ref, bmap_ref, o_ref):
    h = pl.program_id(0)
    bm = bmap_ref[...]
    base = tab_ref[NUM_BUCKETS - 1, h]
    acc = jnp.full(bm.shape, NEG, F32)
    for b in range(NUM_BUCKETS):
        acc = jnp.where(bm == b, (tab_ref[b, h] - base) * LOG2E, acc)
    o_ref[...] = acc


def _bias_tiles(rel_bias):
    n_heads = rel_bias.shape[1]
    bmap = jnp.asarray(_bucket_map_t())
    return pl.pallas_call(
        _bias_tile_kernel,
        grid=(n_heads,),
        in_specs=[pl.BlockSpec(memory_space=pltpu.SMEM),
                  pl.BlockSpec((3 * BLK, BLK), lambda h: (0, 0))],
        out_specs=pl.BlockSpec((None, 3 * BLK, BLK), lambda h: (h, 0, 0)),
        out_shape=jax.ShapeDtypeStruct((n_heads, 3 * BLK, BLK), F32),
        name="bias_tiles",
    )(rel_bias, bmap)


def _block_tail_kernel(x_ref, a_ref, wo_ref, g_ref, wu_ref, wd_ref, *rest, final):
    if final:
        gf_ref, o_ref, x1_ref, xn_ref, acc_ref = rest
    else:
        o_ref, x1_ref, xn_ref, acc_ref = rest
    f = pl.program_id(1)

    @pl.when(f == 0)
    def _():
        x1 = x_ref[...] + _dot(a_ref[...], wo_ref[...])
        x1_ref[...] = x1
        xn_ref[...] = _rms(x1, g_ref[...]).astype(BF16)
        acc_ref[...] = jnp.zeros_like(acc_ref)

    h = jnp.square(jnp.maximum(_dot(xn_ref[...], wu_ref[...]), 0.0)).astype(BF16)
    acc_ref[...] += _dot(h, wd_ref[...])

    @pl.when(f == pl.num_programs(1) - 1)
    def _():
        out = x1_ref[...] + acc_ref[...]
        o_ref[...] = _rms(out, gf_ref[...]) if final else out


def _block_tail(x2, attn, w_o, g, w_up, w_down, g_final=None, tm=1024, tf=2048):
    n, d = x2.shape
    k = attn.shape[1]
    d_ff = w_up.shape[1]
    final = g_final is not None
    row = lambda w: pl.BlockSpec((tm, w), lambda i, f: (i, 0))
    vec = pl.BlockSpec((1, d), lambda i, f: (0, 0))
    in_specs = [row(d), row(k), pl.BlockSpec((k, d), lambda i, f: (0, 0), pipeline_mode=pl.Buffered(1)), vec,
                pl.BlockSpec((d, tf), lambda i, f: (0, f)), pl.BlockSpec((tf, d), lambda i, f: (f, 0))]
    args = [x2, attn, w_o, g.reshape(1, d), w_up, w_down]
    if final:
        in_specs.append(vec)
        args.append(g_final.reshape(1, d))
    return pl.pallas_call(
        functools.partial(_block_tail_kernel, final=final),
        grid=(n // tm, d_ff // tf),
        in_specs=in_specs,
        out_specs=row(d),
        out_shape=jax.ShapeDtypeStruct((n, d), F32),
        scratch_shapes=[pltpu.VMEM((tm, d), F32), pltpu.VMEM((tm, d), BF16), pltpu.VMEM((tm, d), F32)],
        compiler_params=_cparams("parallel", "arbitrary"),
        name="block_tail",
    )(*args)


def _moba_proj_kernel(x_ref, g_ref, wqt_ref, wk_ref, wvt_ref, qt_ref, k_ref, vt_ref):
    xn = _rms(x_ref[...], g_ref[...]).astype(BF16)
    qt_ref[...] = _dot_nt(wqt_ref[...], xn).astype(BF16)
    k_ref[...] = _dot(xn, wk_ref[...]).astype(BF16)
    v_t = _dot_nt(wvt_ref[...], xn).astype(BF16)
    for r in range(vt_ref.shape[0]):
        vt_ref[r] = v_t[:, r * BLK:(r + 1) * BLK]


def _moba_proj(x3, g, wq_t, wk, wv_t):
    b, seq, d = x3.shape
    g2 = g.reshape(1, d)
    return pl.pallas_call(
        _moba_proj_kernel,
        grid=(b, seq // PROJ_TM),
        in_specs=[pl.BlockSpec((None, PROJ_TM, d), lambda bi, i: (bi, i, 0)),
                  _full(g2), _full(wq_t), _full(wk), _full(wv_t)],
        out_specs=[pl.BlockSpec((None, d, PROJ_TM), lambda bi, i: (bi, 0, i)),
                   pl.BlockSpec((None, PROJ_TM, d), lambda bi, i: (bi, i, 0)),
                   pl.BlockSpec((None, PROJ_TM // BLK, d, BLK), lambda bi, i: (bi, i, 0, 0))],
        out_shape=[jax.ShapeDtypeStruct((b, d, seq), BF16),
                   jax.ShapeDtypeStruct((b, seq, d), BF16),
                   jax.ShapeDtypeStruct((b, seq // BLK, d, BLK), BF16)],
        compiler_params=_cparams("parallel", "parallel"),
        name="moba_proj",
    )(x3, g2, wq_t, wk, wv_t)


def _moba_kernel(qt_ref, k_ref, vt_ref, bt_ref, o_ref, kaug_ref, kmean_ref, qfar_ref, m_ref, l_ref, acc_ref, s_ref,
                 smax_ref, *, n_blk, k_sel, sel_rows):
    j = pl.program_id(2)
    seq = k_ref.shape[0]
    n_pairs = MOBA_GRP // 2
    t = BLK
    row = lax.broadcasted_iota(I32, (LANES, t), 0)
    blk_id = lax.broadcasted_iota(I32, (sel_rows, t), 0)
    ones_2 = jnp.ones((BF16_ROWS, 2 * t), BF16)

    @pl.when(j == 0)
    def _():
        krow = lax.broadcasted_iota(I32, (seq, LANES), 0)
        kcol = lax.broadcasted_iota(I32, (seq, LANES), 1)
        onehot = jnp.where(krow // BLK == kcol, 1.0, 0.0).astype(BF16)
        for p in range(n_pairs):
            kaug_ref[p, :, 0:LANES] = k_ref[:, p * LANES:(p + 1) * LANES]
            kaug_ref[p, :, LANES:2 * LANES] = onehot
            kmean_ref[p] = jnp.zeros((LANES, LANES), F32)
            for n in range(n_blk):
                kmean_ref[p, n:n + 1, :] = jnp.mean(
                    k_ref[n * BLK:(n + 1) * BLK, p * LANES:(p + 1) * LANES].astype(F32), axis=0, keepdims=True)

    near_blk = jnp.maximum(j - 1, 0)
    near0 = pl.multiple_of(near_blk * BLK, BLK)
    bias0 = pl.multiple_of(jnp.where(j == 0, BLK, 0), BLK)
    pad_rows = jnp.zeros((LANES - sel_rows, t), BF16)

    def store(slot, p, s):
        s_ref[slot, p] = s
        smax_ref[slot, p] = jnp.max(s, axis=0, keepdims=True)

    def drain(slot, blk0):
        for p in range(n_pairs):
            feats = slice(p * LANES, (p + 1) * LANES)
            m_old = m_ref[p]
            m_new = jnp.maximum(m_old, smax_ref[slot, p])
            alpha = jnp.exp2(m_old - m_new)
            pr = jnp.exp2(s_ref[slot, p] - m_new).astype(BF16)
            m_ref[p] = m_new
            v_t = jnp.concatenate([vt_ref[blk0, feats, :], vt_ref[blk0 + 1, feats, :]], axis=1)
            pv = _dot(jnp.concatenate([v_t, ones_2], axis=0), pr)
            acc_ref[p] = alpha * acc_ref[p] + pv[0:LANES, :]
            l_ref[p] = alpha * l_ref[p] + pv[LANES:LANES + 1, :]

    def fill_far(slot, pi):
        n0 = pl.multiple_of(pi * 2 * BLK, 2 * BLK)
        for p in range(n_pairs):
            store(slot, p, _dot(kaug_ref[p, pl.ds(n0, 2 * BLK), :], qfar_ref[p]))

    q_near = []
    for p in range(n_pairs):
        qp = qt_ref[p * LANES:(p + 1) * LANES, :]
        near, far = [], []
        for hh in range(2):
            qh = jnp.where((row < HEAD_DIM) if hh == 0 else (row >= HEAD_DIM), qp, jnp.zeros_like(qp))
            gate = _dot(kmean_ref[p], qh.astype(F32))[0:sel_rows, :]
            g = jnp.where(blk_id < j, gate, NEG)
            g = jnp.where(blk_id < n_blk, g, -jnp.inf)
            sel_bias = jnp.full((sel_rows, t), NEG, F32)
            for it in range(k_sel):
                mx = jnp.max(g, axis=0, keepdims=True)
                first = jnp.min(jnp.where(g == mx, blk_id, LANES), axis=0, keepdims=True)
                pick = blk_id == first
                sel_bias = jnp.where(pick, jnp.where(it < j, 0.0, NEG), sel_bias)
                g = jnp.where(pick, -jnp.inf, g)
            near_bias = jnp.where(blk_id == j, 0.0, sel_bias)
            far_bias = jnp.where(blk_id < j - 1, sel_bias, NEG)
            near.append(jnp.concatenate([qh, near_bias.astype(BF16), pad_rows], axis=0))
            far.append(jnp.concatenate([qh, far_bias.astype(BF16), pad_rows], axis=0))
        q_near.append(jnp.concatenate(near, axis=1))
        qfar_ref[p] = jnp.concatenate(far, axis=1)
        m_ref[p] = jnp.full((1, 2 * t), NEG, F32)
        l_ref[p] = jnp.zeros((1, 2 * t), F32)
        acc_ref[p] = jnp.zeros((LANES, 2 * t), F32)

    for p in range(n_pairs):
        bias = jnp.concatenate([bt_ref[2 * p, pl.ds(bias0, 2 * BLK), :], bt_ref[2 * p + 1, pl.ds(bias0, 2 * BLK), :]],
                               axis=1)
        store(0, p, _dot(kaug_ref[p, pl.ds(near0, 2 * BLK), :], q_near[p]) + bias)

    n_far = jnp.maximum(j, 1) // 2
    fill_far(1, 0)
    drain(0, near_blk)

    def far_two(i, _):
        pi = 2 * i
        fill_far(0, pi + 1)
        drain(1, 2 * pi)
        fill_far(1, jnp.minimum(pi + 2, n_far - 1))
        drain(0, 2 * pi + 2)
        return 0

    lax.fori_loop(0, n_far // 2, far_two, 0)

    @pl.when(n_far % 2 == 1)
    def _():
        drain(1, 2 * (n_far - 1))

    for p in range(n_pairs):
        acc = acc_ref[p] / l_ref[p]
        out_t = jnp.where(row < HEAD_DIM, acc[:, 0:t], acc[:, t:2 * t])
        o_ref[:, p * LANES:(p + 1) * LANES] = out_t.T.astype(o_ref.dtype)


def _moba_attention(q_t, k, v_t, bt, n_heads):
    b, seq, d = k.shape
    n_blk = seq // BLK
    n_grp = n_heads // MOBA_GRP
    n_pairs = MOBA_GRP // 2
    w = MOBA_GRP * HEAD_DIM
    k_sel = max(1, min(MOBA_TOPK, n_blk - 1))
    sel_rows = BF16_ROWS * pl.cdiv(n_blk, BF16_ROWS)
    assert sel_rows <= LANES and n_blk % 2 == 0
    return pl.pallas_call(
        functools.partial(_moba_kernel, n_blk=n_blk, k_sel=k_sel, sel_rows=sel_rows),
        grid=(b, n_grp, n_blk),
        in_specs=[pl.BlockSpec((None, w, BLK), lambda bi, g, j: (bi, g, j)),
                  pl.BlockSpec((None, seq, w), lambda bi, g, j: (bi, 0, g)),
                  pl.BlockSpec((None, n_blk, w, BLK), lambda bi, g, j: (bi, 0, g, 0)),
                  pl.BlockSpec((MOBA_GRP, 3 * BLK, BLK), lambda bi, g, j: (g, 0, 0), pipeline_mode=pl.Buffered(1))],
        out_specs=pl.BlockSpec((None, BLK, w), lambda bi, g, j: (bi, j, g)),
        out_shape=jax.ShapeDtypeStruct((b, seq, d), BF16),
        scratch_shapes=[pltpu.VMEM((n_pairs, seq, 2 * LANES), BF16),
                        pltpu.VMEM((n_pairs, LANES, LANES), F32),
                        pltpu.VMEM((n_pairs, 2 * LANES, 2 * BLK), BF16),
                        pltpu.VMEM((n_pairs, 1, 2 * BLK), F32),
                        pltpu.VMEM((n_pairs, 1, 2 * BLK), F32),
                        pltpu.VMEM((n_pairs, LANES, 2 * BLK), F32),
                        pltpu.VMEM((2, n_pairs, 2 * BLK, 2 * BLK), F32),
                        pltpu.VMEM((2, n_pairs, 1, 2 * BLK), F32)],
        compiler_params=_cparams("parallel", "parallel", "arbitrary"),
        name="moba_attn",
    )(q_t, k, v_t, bt)


def _dsa_proj_kernel(x_ref, g_ref, win_ref, wkvt_ref, wwt_ref, gq_ref, gkv_ref, gkvt_ref, wuqt_ref, wqit_ref,
                     wuvt_ref, ckv_ref, vt_ref, kk_ref, widx_ref, qnt_ref, qit_ref, *, idx_scale):
    hn = _rms(x_ref[...], g_ref[...]).astype(BF16)
    proj = _dot(hn, win_ref[...])
    c_q = _rms(proj[:, 0:DSA_LORA], gq_ref[...]).astype(BF16)
    ckv_ref[...] = _rms(proj[:, DSA_LORA:2 * DSA_LORA], gkv_ref[...]).astype(BF16)
    kk_ref[...] = proj[:, 2 * DSA_LORA:2 * DSA_LORA + IDX_DIM].astype(BF16)
    ct = _dot_nt(wkvt_ref[...], hn)
    scale = lax.rsqrt(jnp.mean(ct * ct, axis=0, keepdims=True) + EPS)
    gt = jnp.concatenate([gkvt_ref[...]] * (ct.shape[1] // LANES), axis=1)
    v_t = _dot(wuvt_ref[...], (ct * scale * gt).astype(BF16)).astype(BF16)
    for r in range(vt_ref.shape[0]):
        vt_ref[r] = v_t[:, r * BLK:(r + 1) * BLK]
    widx_ref[...] = _dot_nt(wwt_ref[...], hn)[0:IDX_HEADS, :] * idx_scale
    qnt_ref[...] = _dot_nt(wuqt_ref[...], c_q).astype(BF16)
    qit_ref[...] = _dot_nt(wqit_ref[...], c_q).astype(BF16)


def _dsa_proj(x3, g, w_in_row, w_kv_t, w_w_t, g_q, g_kv, w_uq_t, w_qi_t, w_uv_t):
    b, seq, d = x3.shape
    tm = PROJ_TM
    g2, gq2, gkv2 = g.reshape(1, d), g_q.reshape(1, -1), g_kv.reshape(1, -1)
    gkv_t = jnp.broadcast_to(g_kv.reshape(-1, 1), (DSA_LORA, LANES))
    n_q, n_i, n_v = w_uq_t.shape[0], w_qi_t.shape[0], w_uv_t.shape[0]
    consts = (g2, w_in_row, w_kv_t, w_w_t, gq2, gkv2, gkv_t, w_uq_t, w_qi_t, w_uv_t)
    idx_scale = IDX_HEADS ** -0.5 * IDX_DIM ** -0.5
    return pl.pallas_call(
        functools.partial(_dsa_proj_kernel, idx_scale=idx_scale),
        grid=(b, seq // tm),
        in_specs=[pl.BlockSpec((None, tm, d), lambda bi, i: (bi, i, 0))] + [_full(a) for a in consts],
        out_specs=[pl.BlockSpec((None, tm, DSA_LORA), lambda bi, i: (bi, i, 0)),
                   pl.BlockSpec((None, tm // BLK, n_v, BLK), lambda bi, i: (bi, i, 0, 0)),
                   pl.BlockSpec((None, tm, IDX_DIM), lambda bi, i: (bi, i, 0)),
                   pl.BlockSpec((None, IDX_HEADS, tm), lambda bi, i: (bi, 0, i)),
                   pl.BlockSpec((None, n_q, tm), lambda bi, i: (bi, 0, i)),
                   pl.BlockSpec((None, n_i, tm), lambda bi, i: (bi, 0, i))],
        out_shape=[jax.ShapeDtypeStruct((b, seq, DSA_LORA), BF16),
                   jax.ShapeDtypeStruct((b, seq // BLK, n_v, BLK), BF16),
                   jax.ShapeDtypeStruct((b, seq, IDX_DIM), BF16),
                   jax.ShapeDtypeStruct((b, IDX_HEADS, seq), F32),
                   jax.ShapeDtypeStruct((b, n_q, seq), BF16),
                   jax.ShapeDtypeStruct((b, n_i, seq), BF16)],
        compiler_params=_cparams("parallel", "parallel"),
        name="dsa_proj",
    )(x3, *consts)


def _dsa_select_kernel(qit_ref, w_ref, kk_ref, o_ref, keys_ref, half_ref, *, top_k, idx_bits):
    c = pl.program_id(1)
    n_kc, tk, tq = keys_ref.shape
    n_proc = c + 1
    key_row = lax.broadcasted_iota(I32, (tk, tq), 0)
    q_pos = c * tq + lax.broadcasted_iota(I32, (tk, tq), 1)
    w = w_ref[...]

    def score_chunk(kc, _):
        kt = kk_ref[pl.ds(pl.multiple_of(kc * tk, tk), tk), :]
        acc = jnp.zeros((tk, tq), F32)
        for h in range(IDX_HEADS):
            x = _dot(kt, qit_ref[h * IDX_DIM:(h + 1) * IDX_DIM, :])
            acc = acc + w[h:h + 1, :] * jnp.maximum(x, 0.0)
        sc = jnp.where(kc * tk + key_row <= q_pos, acc, NEG)
        bits = pltpu.bitcast(sc, I32)
        key = jnp.where(bits < 0, bits ^ 0x7FFFFFFF, bits)
        keys_ref[kc] = key
        half_ref[kc] = jnp.right_shift(key, 16).astype(I16)
        return 0

    lax.fori_loop(0, n_proc, score_chunk, 0)

    n_pair = (n_proc + 1) // 2

    @pl.when(n_proc % 2 == 1)
    def _():
        half_ref[n_proc] = jnp.full((tk, tq), -32768, I16)

    def count(pred):
        def body(kc, tot):
            hit = jnp.where(pred(keys_ref[kc], kc * tk + key_row), 1, 0)
            return tot + jnp.sum(hit, axis=0, keepdims=True)
        return lax.fori_loop(0, n_proc, body, jnp.zeros((1, tq), I32))

    def count_half(cand, strict=False):
        cand16 = cand.astype(I16)

        def body(i, part):
            tiles = []
            for kc in (2 * i, 2 * i + 1):
                x = half_ref[kc]
                hit = jnp.where((x > cand16) if strict else (x >= cand16), jnp.int16(1), jnp.int16(0))
                tiles += [hit[r:r + BF16_ROWS, :] for r in range(0, tk, BF16_ROWS)]
            while len(tiles) > 1:
                tiles = [a + b for a, b in zip(tiles[::2], tiles[1::2])]
            return part + tiles[0]
        part = lax.fori_loop(0, n_pair, body, jnp.zeros((BF16_ROWS, tq), I16))
        return jnp.sum(part.astype(I32), axis=0, keepdims=True)

    def largest_half(need):
        def bit(i, u):
            cand_u = u | jnp.left_shift(jnp.int32(1), 15 - i)
            return jnp.where(count_half(cand_u - 32768) >= need, cand_u, u)
        return lax.fori_loop(0, 16, bit, jnp.zeros((1, tq), I32)) - 32768

    thr_hi = largest_half(top_k)
    above = count_half(thr_hi, strict=True)

    def low_chunk(kc, _):
        key = keys_ref[kc]
        low = jnp.bitwise_and(key, 0xFFFF) - 32768
        half_ref[kc] = jnp.where(jnp.right_shift(key, 16) == thr_hi, low, -32768).astype(I16)
        return 0

    lax.fori_loop(0, n_proc, low_chunk, 0)
    thr_lo = largest_half(top_k - above)
    cnt_ge = above + count_half(thr_lo)
    thr = jnp.left_shift(thr_hi, 16) + (thr_lo + 32768)

    def tie_cut():
        need = top_k - count(lambda key, pos: key > thr)

        def index_bit(i, x):
            cand = x | jnp.left_shift(jnp.int32(1), idx_bits - 1 - i)
            cnt = count(lambda key, pos: (key == thr) & (pos < cand))
            return jnp.where(cnt < need, cand, x)
        return lax.fori_loop(0, idx_bits, index_bit, jnp.zeros((1, tq), I32))

    def write(chosen):
        def write_chunk(kc, _):
            key = keys_ref[kc]
            pos = kc * tk + key_row
            rows = pl.ds(pl.multiple_of(kc * tk, tk), tk)
            o_ref[rows, :] = jnp.where(chosen(key, pos) & (pos <= q_pos), 0.0, NEG).astype(o_ref.dtype)
            return 0
        lax.fori_loop(0, n_proc, write_chunk, 0)

    tied = jnp.max(cnt_ge) > top_k

    @pl.when(jnp.logical_not(tied))
    def _():
        write(lambda key, pos: key >= thr)

    @pl.when(tied)
    def _():
        last_tie = tie_cut()
        write(lambda key, pos: (key > thr) | ((key == thr) & (pos <= last_tie)))

    def fill_chunk(kc, _):
        o_ref[pl.ds(pl.multiple_of(kc * tk, tk), tk), :] = jnp.full((tk, tq), NEG, o_ref.dtype)
        return 0

    lax.fori_loop(n_proc, n_kc, fill_chunk, 0)


def _dsa_select(qi_t, widx_t, kk):
    b, seq, _ = kk.shape
    t = BLK
    n_kc = seq // t
    top_k = min(DSA_TOPK_MAX, seq // 4)
    assert top_k <= t
    assert n_kc % 2 == 0
    idx_bits = max(1, (seq - 1).bit_length())
    return pl.pallas_call(
        functools.partial(_dsa_select_kernel, top_k=top_k, idx_bits=idx_bits),
        grid=(b, seq // t),
        in_specs=[pl.BlockSpec((None, qi_t.shape[1], t), lambda bi, c: (bi, 0, c)),
                  pl.BlockSpec((None, IDX_HEADS, t), lambda bi, c: (bi, 0, c)),
                  pl.BlockSpec((None, seq, IDX_DIM), lambda bi, c: (bi, 0, 0))],
        out_specs=pl.BlockSpec((None, None, seq, t), lambda bi, c: (bi, c, 0, 0)),
        out_shape=jax.ShapeDtypeStruct((b, seq // t, seq, t), BF16),
        scratch_shapes=[pltpu.VMEM((n_kc, t, t), I32),
                        pltpu.VMEM((n_kc, t, t), I16)],
        compiler_params=_cparams("parallel", "arbitrary"),
        name="dsa_select",
    )(qi_t, widx_t, kk)


def _dsa_attn_kernel(qnt_ref, ckv_ref, vt_ref, mb_ref, bt_ref, wukt_ref, o_ref,
                     qlat_ref, acc_ref, m_ref, l_ref, s_ref, smax_ref, *, n_heads):
    qt = pl.program_id(1)
    t = BLK
    n_grp = n_heads // DSA_GRP
    row = lax.broadcasted_iota(I32, (LANES, t), 0)
    ones_rows = jnp.ones((BF16_ROWS, t), BF16)

    for hp in range(n_heads // 2):
        qp = qnt_ref[hp * LANES:(hp + 1) * LANES, :]
        for hh in range(2):
            qh = jnp.where((row < HEAD_DIM) if hh == 0 else (row >= HEAD_DIM), qp, jnp.zeros_like(qp))
            h = 2 * hp + hh
            qlat_ref[:, h * t:(h + 1) * t] = _dot(wukt_ref[hp], qh).astype(BF16)

    m_ref[...] = jnp.full(m_ref.shape, NEG, F32)
    l_ref[...] = jnp.zeros_like(l_ref)
    acc_ref[...] = jnp.zeros_like(acc_ref)

    def grp_cols(g):
        return slice(g * DSA_GRP * t, (g + 1) * DSA_GRP * t)

    def key_rows(kk):
        return pl.ds(pl.multiple_of(kk * t, t), t)

    def fill(slot, kk, bias_row0, off=None):
        kt = ckv_ref[key_rows(kk), :]
        mb = mb_ref[key_rows(kk), :].astype(F32)
        if off is not None:
            mb = mb + off
        for g in range(n_grp):
            s_grp = _dot(kt, qlat_ref[:, grp_cols(g)])
            parts = []
            for i in range(DSA_GRP):
                s = s_grp[:, i * t:(i + 1) * t] + mb
                if bias_row0 is not None:
                    s = s + bt_ref[g * DSA_GRP + i, bias_row0:bias_row0 + t, :]
                parts.append(s)
            s = jnp.concatenate(parts, axis=1)
            s_ref[slot, :, grp_cols(g)] = s
            smax_ref[slot, :, grp_cols(g)] = jnp.max(s, axis=0, keepdims=True)

    def drain(slot, kk):
        for g in range(n_grp):
            cols = grp_cols(g)
            m_old = m_ref[:, cols]
            m_new = jnp.maximum(m_old, smax_ref[slot, :, cols])
            alpha = jnp.exp2(m_old - m_new)
            pr = jnp.exp2(s_ref[slot, :, cols] - m_new).astype(BF16)
            m_ref[:, cols] = m_new
            for i in range(DSA_GRP):
                h = g * DSA_GRP + i
                feats = slice(h * HEAD_DIM, (h + 1) * HEAD_DIM)
                hc = slice(i * t, (i + 1) * t)
                lc = slice(h * t, (h + 1) * t)
                pv = _dot(jnp.concatenate([vt_ref[kk, feats, :], ones_rows], axis=0), pr[:, hc])
                acc_ref[feats, :] = alpha[:, hc] * acc_ref[feats, :] + pv[0:HEAD_DIM, :]
                l_ref[:, lc] = alpha[:, hc] * l_ref[:, lc] + pv[HEAD_DIM:HEAD_DIM + 1, :]

    n_far = jnp.maximum(qt - 1, 0)
    fill(0, qt, BLK)
    fill(1, jnp.maximum(qt - 1, 0), 0, off=jnp.where(qt >= 1, 0.0, NEG))
    drain(0, qt)
    fill(0, 0, None)
    drain(1, jnp.maximum(qt - 1, 0))

    def far_pair(i, _):
        kk = 2 * i
        fill(1, kk + 1, None)
        drain(0, kk)
        fill(0, jnp.minimum(kk + 2, n_far - 1), None)
        drain(1, kk + 1)
        return 0

    lax.fori_loop(0, n_far // 2, far_pair, 0)

    @pl.when(n_far % 2 == 1)
    def _():
        drain(0, n_far - 1)

    for h in range(n_heads):
        feats = slice(h * HEAD_DIM, (h + 1) * HEAD_DIM)
        acc_ref[feats, :] = acc_ref[feats, :] / l_ref[:, h * t:(h + 1) * t]
    o_ref[...] = acc_ref[...].T.astype(o_ref.dtype)


def _dsa_attention(qn_t, ckv, v_t, mask_t, bt, w_uk_t, n_heads):
    b, seq, _ = ckv.shape
    t = BLK
    d_q = qn_t.shape[1]
    return pl.pallas_call(
        functools.partial(_dsa_attn_kernel, n_heads=n_heads),
        grid=(b, seq // t),
        in_specs=[pl.BlockSpec((None, d_q, t), lambda bi, q: (bi, 0, q)),
                  pl.BlockSpec((None, seq, DSA_LORA), lambda bi, q: (bi, 0, 0)),
                  pl.BlockSpec((None, seq // t, d_q, t), lambda bi, q: (bi, 0, 0, 0)),
                  pl.BlockSpec((None, None, seq, t), lambda bi, q: (bi, q, 0, 0)),
                  _full(bt), _full(w_uk_t)],
        out_specs=pl.BlockSpec((None, t, d_q), lambda bi, q: (bi, q, 0)),
        out_shape=jax.ShapeDtypeStruct((b, seq, d_q), BF16),
        scratch_shapes=[pltpu.VMEM((DSA_LORA, n_heads * t), BF16),
                        pltpu.VMEM((d_q, t), F32),
                        pltpu.VMEM((1, n_heads * t), F32),
                        pltpu.VMEM((1, n_heads * t), F32),
                        pltpu.VMEM((2, t, n_heads * t), F32),
                        pltpu.VMEM((2, 1, n_heads * t), F32)],
        compiler_params=_cparams("parallel", "arbitrary"),
        name="dsa_attn",
    )(qn_t, ckv, v_t, mask_t, bt, w_uk_t)


def _moba_layer(x2, batch, seq, g, w_qkv, bt, n_heads):
    d = x2.shape[1]
    wq_t = (w_qkv[:, :d] * (HEAD_DIM ** -0.5 * LOG2E)).T.astype(BF16)
    wk = w_qkv[:, d:2 * d].astype(BF16)
    wv_t = w_qkv[:, 2 * d:].T.astype(BF16)
    q_t, k, v_t = _moba_proj(x2.reshape(batch, seq, d), g, wq_t, wk, wv_t)
    attn = _moba_attention(q_t, k, v_t, bt, n_heads)
    return attn.reshape(batch * seq, d)


def _dsa_layer(x2, batch, seq, g, w_in, g_q, g_kv, w_uq, w_qi, w_uk, w_uv, bt, n_heads):
    d = x2.shape[1]
    k_lo = 2 * DSA_LORA
    w_in_row = jnp.concatenate([w_in[:, :k_lo + IDX_DIM], jnp.zeros((d, LANES - IDX_DIM), F32)], axis=1).astype(BF16)
    w_kv_t = w_in[:, DSA_LORA:k_lo].T.astype(BF16)
    w_w_t = jnp.concatenate([w_in[:, k_lo + IDX_DIM:].T, jnp.zeros((BF16_ROWS - IDX_HEADS, d), F32)]).astype(BF16)
    w_uq_t = (w_uq * (HEAD_DIM ** -0.5 * LOG2E)).T.astype(BF16)
    w_uv_t = jnp.transpose(w_uv, (0, 2, 1)).reshape(n_heads * HEAD_DIM, DSA_LORA).astype(BF16)
    ckv, v_t, kk, widx_t, qn_t, qi_t = _dsa_proj(x2.reshape(batch, seq, d), g, w_in_row, w_kv_t, w_w_t,
                                                 g_q, g_kv, w_uq_t, w_qi.T.astype(BF16), w_uv_t)
    mask_t = _dsa_select(qi_t, widx_t, kk)
    w_uk_t = jnp.transpose(w_uk.reshape(n_heads // 2, 2 * HEAD_DIM, DSA_LORA), (0, 2, 1)).astype(BF16)
    attn = _dsa_attention(qn_t, ckv, v_t, mask_t, bt[:, :2 * BLK], w_uk_t, n_heads)
    return attn.reshape(batch * seq, -1)


def kernel(x, rel_bias, ln_attn, ln_mlp, moba_w_qkv, moba_w_o, dsa_w_in, dsa_g_q, dsa_g_kv, dsa_w_uq, dsa_w_qi,
           dsa_w_uk, dsa_w_uv, dsa_w_o, mlp_w_up, mlp_w_down, final_norm):
    batch, seq, d = x.shape
    n_heads = rel_bias.shape[1]
    depth = ln_attn.shape[0]
    bt = _bias_tiles(rel_bias)
    x2 = x.reshape(batch * seq, d)
    for i in range(depth):
        j = i // 2
        if i % 2 == 0:
            attn, w_o = _moba_layer(x2, batch, seq, ln_attn[i], moba_w_qkv[j], bt, n_heads), moba_w_o[j]
        else:
            attn, w_o = _dsa_layer(x2, batch, seq, ln_attn[i], dsa_w_in[j], dsa_g_q[j], dsa_g_kv[j], dsa_w_uq[j],
                                   dsa_w_qi[j], dsa_w_uk[j], dsa_w_uv[j], bt, n_heads), dsa_w_o[j]
        x2 = _block_tail(x2, attn, w_o.astype(BF16), ln_mlp[i], mlp_w_up[i].astype(BF16), mlp_w_down[i].astype(BF16),
                         g_final=final_norm if i == depth - 1 else None)
    return x2.reshape(batch, seq, d)
```

```python
import functools
import math

import numpy as np
import jax
import jax.numpy as jnp
from jax import lax
from jax.experimental import pallas as pl
from jax.experimental.pallas import tpu as pltpu

F32 = jnp.float32
BF16 = jnp.bfloat16
I32 = jnp.int32
I16 = jnp.int16

EPS = 1e-6
NEG = -1e30
INT_MIN = -(2 ** 31)
LOG2E = 1.0 / math.log(2.0)

NUM_BUCKETS = 32
MAX_DISTANCE = 128
HEAD_DIM = 64
LANES = 128
BF16_ROWS = 16
BLK = 256
PROJ_TM = 512
MOBA_TOPK = 3
MOBA_GRP = 8
DSA_TOPK_MAX = 256
DSA_LORA = 256
DSA_GRP = 4
IDX_HEADS = 8
IDX_DIM = 64
VMEM_LIMIT = 56 * 1024 * 1024


def _dot(a, b):
    return jnp.dot(a, b, preferred_element_type=F32)


def _dot_nt(a, b):
    return lax.dot_general(a, b, (((1,), (1,)), ((), ())), preferred_element_type=F32)


def _rms(x, g):
    return x * lax.rsqrt(jnp.mean(x * x, axis=-1, keepdims=True) + EPS) * g


def _cparams(*sem):
    return pltpu.CompilerParams(dimension_semantics=sem, vmem_limit_bytes=VMEM_LIMIT)


def _full(a):
    return pl.BlockSpec(a.shape, lambda *_: (0,) * a.ndim, pipeline_mode=pl.Buffered(1))


def _bucket_map_t():
    jj = np.arange(2 * BLK)[:, None]
    i = np.arange(BLK)[None, :]
    d = i - jj + BLK
    n = np.maximum(d, 0)
    exact = NUM_BUCKETS // 2
    nf = np.maximum(n, 1).astype(np.float32)
    large = exact + (np.log(nf / exact) / math.log(MAX_DISTANCE / exact) * (NUM_BUCKETS - exact)).astype(np.int32)
    large = np.minimum(large, NUM_BUCKETS - 1)
    bucket = np.where(n < exact, n, large)
    bmap = np.where(d >= 0, bucket, -1).astype(np.int32)
    return np.concatenate([bmap, bmap[:BLK]], axis=0)


def _bias_tile_kernel(tab_ref, bmap_ref, o_ref):
    h = pl.program_id(0)
    bm = bmap_ref[...]
    base = tab_ref[NUM_BUCKETS - 1, h]
    acc = jnp.full(bm.shape, NEG, F32)
    for b in range(NUM_BUCKETS):
        acc = jnp.where(bm == b, (tab_ref[b, h] - base) * LOG2E, acc)
    o_ref[...] = acc


def _bias_tiles(rel_bias):
    n_heads = rel_bias.shape[1]
    bmap = jnp.asarray(_bucket_map_t())
    return pl.pallas_call(
        _bias_tile_kernel,
        grid=(n_heads,),
        in_specs=[pl.BlockSpec(memory_space=pltpu.SMEM),
                  pl.BlockSpec((3 * BLK, BLK), lambda h: (0, 0))],
        out_specs=pl.BlockSpec((None, 3 * BLK, BLK), lambda h: (h, 0, 0)),
        out_shape=jax.ShapeDtypeStruct((n_heads, 3 * BLK, BLK), F32),
        name="bias_tiles",
    )(rel_bias, bmap)


def _block_tail_kernel(x_ref, a_ref, wo_ref, g_ref, wu_ref, wd_ref, *rest, final):
    if final:
        gf_ref, o_ref, x1_ref, xn_ref, acc_ref = rest
    else:
        o_ref, x1_ref, xn_ref, acc_ref = rest
    f = pl.program_id(1)

    @pl.when(f == 0)
    def _():
        x1 = x_ref[...] + _dot(a_ref[...], wo_ref[...])
        x1_ref[...] = x1
        xn_ref[...] = _rms(x1, g_ref[...]).astype(BF16)
        acc_ref[...] = jnp.zeros_like(acc_ref)

    h = jnp.square(jnp.maximum(_dot(xn_ref[...], wu_ref[...]), 0.0)).astype(BF16)
    acc_ref[...] += _dot(h, wd_ref[...])

    @pl.when(f == pl.num_programs(1) - 1)
    def _():
        out = x1_ref[...] + acc_ref[...]
        o_ref[...] = _rms(out, gf_ref[...]) if final else out


def _block_tail(x2, attn, w_o, g, w_up, w_down, g_final=None, tm=1024, tf=2048):
    n, d = x2.shape
    k = attn.shape[1]
    d_ff = w_up.shape[1]
    final = g_final is not None
    row = lambda w: pl.BlockSpec((tm, w), lambda i, f: (i, 0))
    vec = pl.BlockSpec((1, d), lambda i, f: (0, 0))
    in_specs = [row(d), row(k), pl.BlockSpec((k, d), lambda i, f: (0, 0), pipeline_mode=pl.Buffered(1)), vec,
                pl.BlockSpec((d, tf), lambda i, f: (0, f)), pl.BlockSpec((tf, d), lambda i, f: (f, 0))]
    args = [x2, attn, w_o, g.reshape(1, d), w_up, w_down]
    if final:
        in_specs.append(vec)
        args.append(g_final.reshape(1, d))
    return pl.pallas_call(
        functools.partial(_block_tail_kernel, final=final),
        grid=(n // tm, d_ff // tf),
        in_specs=in_specs,
        out_specs=row(d),
        out_shape=jax.ShapeDtypeStruct((n, d), F32),
        scratch_shapes=[pltpu.VMEM((tm, d), F32), pltpu.VMEM((tm, d), BF16), pltpu.VMEM((tm, d), F32)],
        compiler_params=_cparams("parallel", "arbitrary"),
        name="block_tail",
    )(*args)


def _moba_proj_kernel(x_ref, g_ref, wqt_ref, wk_ref, wvt_ref, qt_ref, k_ref, vt_ref):
    xn = _rms(x_ref[...], g_ref[...]).astype(BF16)
    qt_ref[...] = _dot_nt(wqt_ref[...], xn).astype(BF16)
    k_ref[...] = _dot(xn, wk_ref[...]).astype(BF16)
    v_t = _dot_nt(wvt_ref[...], xn).astype(BF16)
    for r in range(vt_ref.shape[0]):
        vt_ref[r] = v_t[:, r * BLK:(r + 1) * BLK]


def _moba_proj(x3, g, wq_t, wk, wv_t):
    b, seq, d = x3.shape
    g2 = g.reshape(1, d)
    return pl.pallas_call(
        _moba_proj_kernel,
        grid=(b, seq // PROJ_TM),
        in_specs=[pl.BlockSpec((None, PROJ_TM, d), lambda bi, i: (bi, i, 0)),
                  _full(g2), _full(wq_t), _full(wk), _full(wv_t)],
        out_specs=[pl.BlockSpec((None, d, PROJ_TM), lambda bi, i: (bi, 0, i)),
                   pl.BlockSpec((None, PROJ_TM, d), lambda bi, i: (bi, i, 0)),
                   pl.BlockSpec((None, PROJ_TM // BLK, d, BLK), lambda bi, i: (bi, i, 0, 0))],
        out_shape=[jax.ShapeDtypeStruct((b, d, seq), BF16),
                   jax.ShapeDtypeStruct((b, seq, d), BF16),
                   jax.ShapeDtypeStruct((b, seq // BLK, d, BLK), BF16)],
        compiler_params=_cparams("parallel", "parallel"),
        name="moba_proj",
    )(x3, g2, wq_t, wk, wv_t)


def _moba_kernel(qt_ref, k_ref, vt_ref, bt_ref, o_ref, kaug_ref, kmean_ref, qfar_ref, m_ref, l_ref, acc_ref, s_ref,
                 smax_ref, *, n_blk, k_sel, sel_rows):
    j = pl.program_id(2)
    seq = k_ref.shape[0]
    n_pairs = MOBA_GRP // 2
    t = BLK
    row = lax.broadcasted_iota(I32, (LANES, t), 0)
    blk_id = lax.broadcasted_iota(I32, (sel_rows, t), 0)
    ones_2 = jnp.ones((BF16_ROWS, 2 * t), BF16)

    @pl.when(j == 0)
    def _():
        krow = lax.broadcasted_iota(I32, (seq, LANES), 0)
        kcol = lax.broadcasted_iota(I32, (seq, LANES), 1)
        onehot = jnp.where(krow // BLK == kcol, 1.0, 0.0).astype(BF16)
        for p in range(n_pairs):
            kaug_ref[p, :, 0:LANES] = k_ref[:, p * LANES:(p + 1) * LANES]
            kaug_ref[p, :, LANES:2 * LANES] = onehot
            kmean_ref[p] = jnp.zeros((LANES, LANES), F32)
            for n in range(n_blk):
                kmean_ref[p, n:n + 1, :] = jnp.mean(
                    k_ref[n * BLK:(n + 1) * BLK, p * LANES:(p + 1) * LANES].astype(F32), axis=0, keepdims=True)

    near_blk = jnp.maximum(j - 1, 0)
    near0 = pl.multiple_of(near_blk * BLK, BLK)
    bias0 = pl.multiple_of(jnp.where(j == 0, BLK, 0), BLK)
    pad_rows = jnp.zeros((LANES - sel_rows, t), BF16)

    def store(slot, p, s):
        s_ref[slot, p] = s
        smax_ref[slot, p] = jnp.max(s, axis=0, keepdims=True)

    def drain(slot, blk0):
        for p in range(n_pairs):
            feats = slice(p * LANES, (p + 1) * LANES)
            m_old = m_ref[p]
            m_new = jnp.maximum(m_old, smax_ref[slot, p])
            alpha = jnp.exp2(m_old - m_new)
            pr = jnp.exp2(s_ref[slot, p] - m_new).astype(BF16)
            m_ref[p] = m_new
            v_t = jnp.concatenate([vt_ref[blk0, feats, :], vt_ref[blk0 + 1, feats, :]], axis=1)
            pv = _dot(jnp.concatenate([v_t, ones_2], axis=0), pr)
            acc_ref[p] = alpha * acc_ref[p] + pv[0:LANES, :]
            l_ref[p] = alpha * l_ref[p] + pv[LANES:LANES + 1, :]

    def fill_far(slot, pi):
        n0 = pl.multiple_of(pi * 2 * BLK, 2 * BLK)
        for p in range(n_pairs):
            store(slot, p, _dot(kaug_ref[p, pl.ds(n0, 2 * BLK), :], qfar_ref[p]))

    q_near = []
    for p in range(n_pairs):
        qp = qt_ref[p * LANES:(p + 1) * LANES, :]
        near, far = [], []
        for hh in range(2):
            qh = jnp.where((row < HEAD_DIM) if hh == 0 else (row >= HEAD_DIM), qp, jnp.zeros_like(qp))
            gate = _dot(kmean_ref[p], qh.astype(F32))[0:sel_rows, :]
            g = jnp.where(blk_id < j, gate, NEG)
            g = jnp.where(blk_id < n_blk, g, -jnp.inf)
            sel_bias = jnp.full((sel_rows, t), NEG, F32)
            for it in range(k_sel):
                mx = jnp.max(g, axis=0, keepdims=True)
                first = jnp.min(jnp.where(g == mx, blk_id, LANES), axis=0, keepdims=True)
                pick = blk_id == first
                sel_bias = jnp.where(pick, jnp.where(it < j, 0.0, NEG), sel_bias)
                g = jnp.where(pick, -jnp.inf, g)
            near_bias = jnp.where(blk_id == j, 0.0, sel_bias)
            far_bias = jnp.where(blk_id < j - 1, sel_bias, NEG)
            near.append(jnp.concatenate([qh, near_bias.astype(BF16), pad_rows], axis=0))
            far.append(jnp.concatenate([qh, far_bias.astype(BF16), pad_rows], axis=0))
        q_near.append(jnp.concatenate(near, axis=1))
        qfar_ref[p] = jnp.concatenate(far, axis=1)
        m_ref[p] = jnp.full((1, 2 * t), NEG, F32)
        l_ref[p] = jnp.zeros((1, 2 * t), F32)
        acc_ref[p] = jnp.zeros((LANES, 2 * t), F32)

    for p in range(n_pairs):
        bias = jnp.concatenate([bt_ref[2 * p, pl.ds(bias0, 2 * BLK), :], bt_ref[2 * p + 1, pl.ds(bias0, 2 * BLK), :]],
                               axis=1)
        store(0, p, _dot(kaug_ref[p, pl.ds(near0, 2 * BLK), :], q_near[p]) + bias)

    n_far = jnp.maximum(j, 1) // 2
    fill_far(1, 0)
    drain(0, near_blk)

    def far_two(i, _):
        pi = 2 * i
        fill_far(0, pi + 1)
        drain(1, 2 * pi)
        fill_far(1, pi + 2)
        drain(0, 2 * pi + 2)
        return 0

    n_trip = jnp.maximum(n_far - 1, 0) // 2
    lax.fori_loop(0, n_trip, far_two, 0)
    left = n_far - 2 * n_trip
    last = 2 * n_trip

    @pl.when(left == 1)
    def _():
        drain(1, 2 * last)

    @pl.when(left == 2)
    def _():
        fill_far(0, last + 1)
        drain(1, 2 * last)
        drain(0, 2 * last + 2)

    for p in range(n_pairs):
        acc = acc_ref[p] / l_ref[p]
        out_t = jnp.where(row < HEAD_DIM, acc[:, 0:t], acc[:, t:2 * t])
        o_ref[:, p * LANES:(p + 1) * LANES] = out_t.T.astype(o_ref.dtype)


def _moba_attention(q_t, k, v_t, bt, n_heads):
    b, seq, d = k.shape
    n_blk = seq // BLK
    n_grp = n_heads // MOBA_GRP
    n_pairs = MOBA_GRP // 2
    w = MOBA_GRP * HEAD_DIM
    k_sel = max(1, min(MOBA_TOPK, n_blk - 1))
    sel_rows = BF16_ROWS * pl.cdiv(n_blk, BF16_ROWS)
    assert sel_rows <= LANES and n_blk % 2 == 0
    return pl.pallas_call(
        functools.partial(_moba_kernel, n_blk=n_blk, k_sel=k_sel, sel_rows=sel_rows),
        grid=(b, n_grp, n_blk),
        in_specs=[pl.BlockSpec((None, w, BLK), lambda bi, g, j: (bi, g, j)),
                  pl.BlockSpec((None, seq, w), lambda bi, g, j: (bi, 0, g)),
                  pl.BlockSpec((None, n_blk, w, BLK), lambda bi, g, j: (bi, 0, g, 0)),
                  pl.BlockSpec((MOBA_GRP, 3 * BLK, BLK), lambda bi, g, j: (g, 0, 0), pipeline_mode=pl.Buffered(1))],
        out_specs=pl.BlockSpec((None, BLK, w), lambda bi, g, j: (bi, j, g)),
        out_shape=jax.ShapeDtypeStruct((b, seq, d), BF16),
        scratch_shapes=[pltpu.VMEM((n_pairs, seq, 2 * LANES), BF16),
                        pltpu.VMEM((n_pairs, LANES, LANES), F32),
                        pltpu.VMEM((n_pairs, 2 * LANES, 2 * BLK), BF16),
                        pltpu.VMEM((n_pairs, 1, 2 * BLK), F32),
                        pltpu.VMEM((n_pairs, 1, 2 * BLK), F32),
                        pltpu.VMEM((n_pairs, LANES, 2 * BLK), F32),
                        pltpu.VMEM((2, n_pairs, 2 * BLK, 2 * BLK), F32),
                        pltpu.VMEM((2, n_pairs, 1, 2 * BLK), F32)],
        compiler_params=_cparams("parallel", "parallel", "arbitrary"),
        name="moba_attn",
    )(q_t, k, v_t, bt)


def _dsa_proj_kernel(x_ref, g_ref, win_ref, wkvt_ref, wwt_ref, gq_ref, gkv_ref, gkvt_ref, wuqt_ref, wqit_ref,
                     wuvt_ref, ckv_ref, vt_ref, kk_ref, widx_ref, qnt_ref, qit_ref, *, idx_scale):
    hn = _rms(x_ref[...], g_ref[...]).astype(BF16)
    proj = _dot(hn, win_ref[...])
    c_q = _rms(proj[:, 0:DSA_LORA], gq_ref[...]).astype(BF16)
    ckv_ref[...] = _rms(proj[:, DSA_LORA:2 * DSA_LORA], gkv_ref[...]).astype(BF16)
    kk_ref[...] = proj[:, 2 * DSA_LORA:2 * DSA_LORA + IDX_DIM].astype(BF16)
    ct = _dot_nt(wkvt_ref[...], hn)
    scale = lax.rsqrt(jnp.mean(ct * ct, axis=0, keepdims=True) + EPS)
    gt = jnp.concatenate([gkvt_ref[...]] * (ct.shape[1] // LANES), axis=1)
    v_t = _dot(wuvt_ref[...], (ct * scale * gt).astype(BF16)).astype(BF16)
    for r in range(vt_ref.shape[0]):
        vt_ref[r] = v_t[:, r * BLK:(r + 1) * BLK]
    widx_ref[...] = _dot_nt(wwt_ref[...], hn)[0:IDX_HEADS, :] * idx_scale
    qnt_ref[...] = _dot_nt(wuqt_ref[...], c_q).astype(BF16)
    qit_ref[...] = _dot_nt(wqit_ref[...], c_q).astype(BF16)


def _dsa_proj(x3, g, w_in_row, w_kv_t, w_w_t, g_q, g_kv, w_uq_t, w_qi_t, w_uv_t):
    b, seq, d = x3.shape
    tm = PROJ_TM
    g2, gq2, gkv2 = g.reshape(1, d), g_q.reshape(1, -1), g_kv.reshape(1, -1)
    gkv_t = jnp.broadcast_to(g_kv.reshape(-1, 1), (DSA_LORA, LANES))
    n_q, n_i, n_v = w_uq_t.shape[0], w_qi_t.shape[0], w_uv_t.shape[0]
    consts = (g2, w_in_row, w_kv_t, w_w_t, gq2, gkv2, gkv_t, w_uq_t, w_qi_t, w_uv_t)
    idx_scale = IDX_HEADS ** -0.5 * IDX_DIM ** -0.5
    return pl.pallas_call(
        functools.partial(_dsa_proj_kernel, idx_scale=idx_scale),
        grid=(b, seq // tm),
        in_specs=[pl.BlockSpec((None, tm, d), lambda bi, i: (bi, i, 0))] + [_full(a) for a in consts],
        out_specs=[pl.BlockSpec((None, tm, DSA_LORA), lambda bi, i: (bi, i, 0)),
                   pl.BlockSpec((None, tm // BLK, n_v, BLK), lambda bi, i: (bi, i, 0, 0)),
                   pl.BlockSpec((None, tm, IDX_DIM), lambda bi, i: (bi, i, 0)),
                   pl.BlockSpec((None, IDX_HEADS, tm), lambda bi, i: (bi, 0, i)),
                   pl.BlockSpec((None, n_q, tm), lambda bi, i: (bi, 0, i)),
                   pl.BlockSpec((None, n_i, tm), lambda bi, i: (bi, 0, i))],
        out_shape=[jax.ShapeDtypeStruct((b, seq, DSA_LORA), BF16),
                   jax.ShapeDtypeStruct((b, seq // BLK, n_v, BLK), BF16),
                   jax.ShapeDtypeStruct((b, seq, IDX_DIM), BF16),
                   jax.ShapeDtypeStruct((b, IDX_HEADS, seq), F32),
                   jax.ShapeDtypeStruct((b, n_q, seq), BF16),
                   jax.ShapeDtypeStruct((b, n_i, seq), BF16)],
        compiler_params=_cparams("parallel", "parallel"),
        name="dsa_proj",
    )(x3, *consts)


def _dsa_select_kernel(qit_ref, w_ref, kk_ref, o_ref, keys_ref, half_ref, *, top_k, idx_bits):
    c = pl.program_id(1)
    n_kc, tk, tq = keys_ref.shape
    n_proc = c + 1
    key_row = lax.broadcasted_iota(I32, (tk, tq), 0)
    q_pos = c * tq + lax.broadcasted_iota(I32, (tk, tq), 1)
    w = w_ref[...]

    def score_chunk(kc, _):
        kt = kk_ref[pl.ds(pl.multiple_of(kc * tk, tk), tk), :]
        acc = jnp.zeros((tk, tq), F32)
        for h in range(IDX_HEADS):
            x = _dot(kt, qit_ref[h * IDX_DIM:(h + 1) * IDX_DIM, :])
            acc = acc + w[h:h + 1, :] * jnp.maximum(x, 0.0)
        sc = jnp.where(kc * tk + key_row <= q_pos, acc, NEG)
        bits = pltpu.bitcast(sc, I32)
        key = jnp.where(bits < 0, bits ^ 0x7FFFFFFF, bits)
        keys_ref[kc] = key
        half_ref[kc] = jnp.right_shift(key, 16).astype(I16)
        return 0

    lax.fori_loop(0, n_proc, score_chunk, 0)

    n_pair = (n_proc + 1) // 2

    @pl.when(n_proc % 2 == 1)
    def _():
        half_ref[n_proc] = jnp.full((tk, tq), -32768, I16)

    def count(pred):
        def body(kc, tot):
            hit = jnp.where(pred(keys_ref[kc], kc * tk + key_row), 1, 0)
            return tot + jnp.sum(hit, axis=0, keepdims=True)
        return lax.fori_loop(0, n_proc, body, jnp.zeros((1, tq), I32))

    def count_half(cand, strict=False):
        cand16 = cand.astype(I16)

        def body(i, part):
            tiles = []
            for kc in (2 * i, 2 * i + 1):
                x = half_ref[kc]
                hit = jnp.where((x > cand16) if strict else (x >= cand16), jnp.int16(1), jnp.int16(0))
                tiles += [hit[r:r + BF16_ROWS, :] for r in range(0, tk, BF16_ROWS)]
            while len(tiles) > 1:
                tiles = [a + b for a, b in zip(tiles[::2], tiles[1::2])]
            return part + tiles[0]
        part = lax.fori_loop(0, n_pair, body, jnp.zeros((BF16_ROWS, tq), I16))
        return jnp.sum(part.astype(I32), axis=0, keepdims=True)

    def largest_half(need):
        def bit(i, u):
            cand_u = u | jnp.left_shift(jnp.int32(1), 15 - i)
            return jnp.where(count_half(cand_u - 32768) >= need, cand_u, u)
        return lax.fori_loop(0, 16, bit, jnp.zeros((1, tq), I32)) - 32768

    thr_hi = largest_half(top_k)
    above = count_half(thr_hi, strict=True)

    def low_chunk(kc, _):
        key = keys_ref[kc]
        low = jnp.bitwise_and(key, 0xFFFF) - 32768
        half_ref[kc] = jnp.where(jnp.right_shift(key, 16) == thr_hi, low, -32768).astype(I16)
        return 0

    lax.fori_loop(0, n_proc, low_chunk, 0)
    thr_lo = largest_half(top_k - above)
    cnt_ge = above + count_half(thr_lo)
    thr = jnp.left_shift(thr_hi, 16) + (thr_lo + 32768)

    def tie_cut():
        need = top_k - count(lambda key, pos: key > thr)

        def index_bit(i, x):
            cand = x | jnp.left_shift(jnp.int32(1), idx_bits - 1 - i)
            cnt = count(lambda key, pos: (key == thr) & (pos < cand))
            return jnp.where(cnt < need, cand, x)
        return lax.fori_loop(0, idx_bits, index_bit, jnp.zeros((1, tq), I32))

    def write(chosen):
        def write_chunk(kc, _):
            key = keys_ref[kc]
            pos = kc * tk + key_row
            rows = pl.ds(pl.multiple_of(kc * tk, tk), tk)
            o_ref[rows, :] = jnp.where(chosen(key, pos) & (pos <= q_pos), 0.0, NEG).astype(o_ref.dtype)
            return 0
        lax.fori_loop(0, n_proc, write_chunk, 0)

    tied = jnp.max(cnt_ge) > top_k

    @pl.when(jnp.logical_not(tied))
    def _():
        write(lambda key, pos: key >= thr)

    @pl.when(tied)
    def _():
        last_tie = tie_cut()
        write(lambda key, pos: (key > thr) | ((key == thr) & (pos <= last_tie)))

    def fill_chunk(kc, _):
        o_ref[pl.ds(pl.multiple_of(kc * tk, tk), tk), :] = jnp.full((tk, tq), NEG, o_ref.dtype)
        return 0

    lax.fori_loop(n_proc, n_kc, fill_chunk, 0)


def _dsa_select(qi_t, widx_t, kk):
    b, seq, _ = kk.shape
    t = BLK
    n_kc = seq // t
    top_k = min(DSA_TOPK_MAX, seq // 4)
    assert top_k <= t
    assert n_kc % 2 == 0
    idx_bits = max(1, (seq - 1).bit_length())
    return pl.pallas_call(
        functools.partial(_dsa_select_kernel, top_k=top_k, idx_bits=idx_bits),
        grid=(b, seq // t),
        in_specs=[pl.BlockSpec((None, qi_t.shape[1], t), lambda bi, c: (bi, 0, c)),
                  pl.BlockSpec((None, IDX_HEADS, t), lambda bi, c: (bi, 0, c)),
                  pl.BlockSpec((None, seq, IDX_DIM), lambda bi, c: (bi, 0, 0))],
        out_specs=pl.BlockSpec((None, None, seq, t), lambda bi, c: (bi, c, 0, 0)),
        out_shape=jax.ShapeDtypeStruct((b, seq // t, seq, t), BF16),
        scratch_shapes=[pltpu.VMEM((n_kc, t, t), I32),
                        pltpu.VMEM((n_kc, t, t), I16)],
        compiler_params=_cparams("parallel", "arbitrary"),
        name="dsa_select",
    )(qi_t, widx_t, kk)


def _dsa_attn_kernel(qnt_ref, ckv_ref, vt_ref, mb_ref, bt_ref, wukt_ref, o_ref,
                     qlat_ref, acc_ref, m_ref, l_ref, s_ref, smax_ref, *, n_heads):
    qt = pl.program_id(1)
    t = BLK
    n_grp = n_heads // DSA_GRP
    row = lax.broadcasted_iota(I32, (LANES, t), 0)
    ones_rows = jnp.ones((BF16_ROWS, t), BF16)

    for hp in range(n_heads // 2):
        qp = qnt_ref[hp * LANES:(hp + 1) * LANES, :]
        for hh in range(2):
            qh = jnp.where((row < HEAD_DIM) if hh == 0 else (row >= HEAD_DIM), qp, jnp.zeros_like(qp))
            h = 2 * hp + hh
            qlat_ref[:, h * t:(h + 1) * t] = _dot(wukt_ref[hp], qh).astype(BF16)

    m_ref[...] = jnp.full(m_ref.shape, NEG, F32)
    l_ref[...] = jnp.zeros_like(l_ref)
    acc_ref[...] = jnp.zeros_like(acc_ref)

    def grp_cols(g):
        return slice(g * DSA_GRP * t, (g + 1) * DSA_GRP * t)

    def key_rows(kk):
        return pl.ds(pl.multiple_of(kk * t, t), t)

    def fill(slot, kk, bias_row0, off=None):
        kt = ckv_ref[key_rows(kk), :]
        mb = mb_ref[key_rows(kk), :].astype(F32)
        if off is not None:
            mb = mb + off
        for g in range(n_grp):
            s_grp = _dot(kt, qlat_ref[:, grp_cols(g)])
            parts = []
            for i in range(DSA_GRP):
                s = s_grp[:, i * t:(i + 1) * t] + mb
                if bias_row0 is not None:
                    s = s + bt_ref[g * DSA_GRP + i, bias_row0:bias_row0 + t, :]
                parts.append(s)
            s = jnp.concatenate(parts, axis=1)
            s_ref[slot, :, grp_cols(g)] = s
            smax_ref[slot, :, grp_cols(g)] = jnp.max(s, axis=0, keepdims=True)

    def drain(slot, kk):
        for g in range(n_grp):
            cols = grp_cols(g)
            m_old = m_ref[:, cols]
            m_new = jnp.maximum(m_old, smax_ref[slot, :, cols])
            alpha = jnp.exp2(m_old - m_new)
            pr = jnp.exp2(s_ref[slot, :, cols] - m_new).astype(BF16)
            m_ref[:, cols] = m_new
            for i in range(DSA_GRP):
                h = g * DSA_GRP + i
                feats = slice(h * HEAD_DIM, (h + 1) * HEAD_DIM)
                hc = slice(i * t, (i + 1) * t)
                lc = slice(h * t, (h + 1) * t)
                pv = _dot(jnp.concatenate([vt_ref[kk, feats, :], ones_rows], axis=0), pr[:, hc])
                acc_ref[feats, :] = alpha[:, hc] * acc_ref[feats, :] + pv[0:HEAD_DIM, :]
                l_ref[:, lc] = alpha[:, hc] * l_ref[:, lc] + pv[HEAD_DIM:HEAD_DIM + 1, :]

    n_far = jnp.maximum(qt - 1, 0)
    fill(0, qt, BLK)
    fill(1, jnp.maximum(qt - 1, 0), 0, off=jnp.where(qt >= 1, 0.0, NEG))
    drain(0, qt)
    fill(0, 0, None)
    drain(1, jnp.maximum(qt - 1, 0))

    def far_pair(i, _):
        kk = 2 * i
        fill(1, kk + 1, None)
        drain(0, kk)
        fill(0, kk + 2, None)
        drain(1, kk + 1)
        return 0

    n_trip = jnp.maximum(n_far - 1, 0) // 2
    lax.fori_loop(0, n_trip, far_pair, 0)
    left = n_far - 2 * n_trip
    last = 2 * n_trip

    @pl.when(left == 1)
    def _():
        drain(0, last)

    @pl.when(left == 2)
    def _():
        fill(1, last + 1, None)
        drain(0, last)
        drain(1, last + 1)

    for h in range(n_heads):
        feats = slice(h * HEAD_DIM, (h + 1) * HEAD_DIM)
        acc_ref[feats, :] = acc_ref[feats, :] / l_ref[:, h * t:(h + 1) * t]
    o_ref[...] = acc_ref[...].T.astype(o_ref.dtype)


def _dsa_attention(qn_t, ckv, v_t, mask_t, bt, w_uk_t, n_heads):
    b, seq, _ = ckv.shape
    t = BLK
    d_q = qn_t.shape[1]
    return pl.pallas_call(
        functools.partial(_dsa_attn_kernel, n_heads=n_heads),
        grid=(b, seq // t),
        in_specs=[pl.BlockSpec((None, d_q, t), lambda bi, q: (bi, 0, q)),
                  pl.BlockSpec((None, seq, DSA_LORA), lambda bi, q: (bi, 0, 0)),
                  pl.BlockSpec((None, seq // t, d_q, t), lambda bi, q: (bi, 0, 0, 0)),
                  pl.BlockSpec((None, None, seq, t), lambda bi, q: (bi, q, 0, 0)),
                  _full(bt), _full(w_uk_t)],
        out_specs=pl.BlockSpec((None, t, d_q), lambda bi, q: (bi, q, 0)),
        out_shape=jax.ShapeDtypeStruct((b, seq, d_q), BF16),
        scratch_shapes=[pltpu.VMEM((DSA_LORA, n_heads * t), BF16),
                        pltpu.VMEM((d_q, t), F32),
                        pltpu.VMEM((1, n_heads * t), F32),
                        pltpu.VMEM((1, n_heads * t), F32),
                        pltpu.VMEM((2, t, n_heads * t), F32),
                        pltpu.VMEM((2, 1, n_heads * t), F32)],
        compiler_params=_cparams("parallel", "arbitrary"),
        name="dsa_attn",
    )(qn_t, ckv, v_t, mask_t, bt, w_uk_t)


def _moba_layer(x2, batch, seq, g, w_qkv, bt, n_heads):
    d = x2.shape[1]
    wq_t = (w_qkv[:, :d] * (HEAD_DIM ** -0.5 * LOG2E)).T.astype(BF16)
    wk = w_qkv[:, d:2 * d].astype(BF16)
    wv_t = w_qkv[:, 2 * d:].T.astype(BF16)
    q_t, k, v_t = _moba_proj(x2.reshape(batch, seq, d), g, wq_t, wk, wv_t)
    attn = _moba_attention(q_t, k, v_t, bt, n_heads)
    return attn.reshape(batch * seq, d)


def _dsa_layer(x2, batch, seq, g, w_in, g_q, g_kv, w_uq, w_qi, w_uk, w_uv, bt, n_heads):
    d = x2.shape[1]
    k_lo = 2 * DSA_LORA
    w_in_row = jnp.concatenate([w_in[:, :k_lo + IDX_DIM], jnp.zeros((d, LANES - IDX_DIM), F32)], axis=1).astype(BF16)
    w_kv_t = w_in[:, DSA_LORA:k_lo].T.astype(BF16)
    w_w_t = jnp.concatenate([w_in[:, k_lo + IDX_DIM:].T, jnp.zeros((BF16_ROWS - IDX_HEADS, d), F32)]).astype(BF16)
    w_uq_t = (w_uq * (HEAD_DIM ** -0.5 * LOG2E)).T.astype(BF16)
    w_uv_t = jnp.transpose(w_uv, (0, 2, 1)).reshape(n_heads * HEAD_DIM, DSA_LORA).astype(BF16)
    ckv, v_t, kk, widx_t, qn_t, qi_t = _dsa_proj(x2.reshape(batch, seq, d), g, w_in_row, w_kv_t, w_w_t,
                                                 g_q, g_kv, w_uq_t, w_qi.T.astype(BF16), w_uv_t)
    mask_t = _dsa_select(qi_t, widx_t, kk)
    w_uk_t = jnp.transpose(w_uk.reshape(n_heads // 2, 2 * HEAD_DIM, DSA_LORA), (0, 2, 1)).astype(BF16)
    attn = _dsa_attention(qn_t, ckv, v_t, mask_t, bt[:, :2 * BLK], w_uk_t, n_heads)
    return attn.reshape(batch * seq, -1)


def kernel(x, rel_bias, ln_attn, ln_mlp, moba_w_qkv, moba_w_o, dsa_w_in, dsa_g_q, dsa_g_kv, dsa_w_uq, dsa_w_qi,
           dsa_w_uk, dsa_w_uv, dsa_w_o, mlp_w_up, mlp_w_down, final_norm):
    batch, seq, d = x.shape
    n_heads = rel_bias.shape[1]
    depth = ln_attn.shape[0]
    bt = _bias_tiles(rel_bias)
    x2 = x.reshape(batch * seq, d)
    for i in range(depth):
        j = i // 2
        if i % 2 == 0:
            attn, w_o = _moba_layer(x2, batch, seq, ln_attn[i], moba_w_qkv[j], bt, n_heads), moba_w_o[j]
        else:
            attn, w_o = _dsa_layer(x2, batch, seq, ln_attn[i], dsa_w_in[j], dsa_g_q[j], dsa_g_kv[j], dsa_w_uq[j],
                                   dsa_w_qi[j], dsa_w_uk[j], dsa_w_uv[j], bt, n_heads), dsa_w_o[j]
        x2 = _block_tail(x2, attn, w_o.astype(BF16), ln_mlp[i], mlp_w_up[i].astype(BF16), mlp_w_down[i].astype(BF16),
                         g_final=final_norm if i == depth - 1 else None)
    return x2.reshape(batch, seq, d)
```

```python
import functools
import math

import numpy as np
import jax
import jax.numpy as jnp
from jax import lax
from jax.experimental import pallas as pl
from jax.experimental.pallas import tpu as pltpu

F32 = jnp.float32
BF16 = jnp.bfloat16
I32 = jnp.int32
I16 = jnp.int16

EPS = 1e-6
NEG = -1e30
INT_MIN = -(2 ** 31)
LOG2E = 1.0 / math.log(2.0)

NUM_BUCKETS = 32
MAX_DISTANCE = 128
HEAD_DIM = 64
LANES = 128
BF16_ROWS = 16
BLK = 256
PROJ_TM = 1024
MOBA_TOPK = 3
MOBA_GRP = 8
DSA_TOPK_MAX = 256
DSA_LORA = 256
DSA_GRP = 4
IDX_HEADS = 8
IDX_DIM = 64
VMEM_LIMIT = 56 * 1024 * 1024


def _dot(a, b):
    return jnp.dot(a, b, preferred_element_type=F32)


def _dot_nt(a, b):
    return lax.dot_general(a, b, (((1,), (1,)), ((), ())), preferred_element_type=F32)


def _rms(x, g):
    return x * lax.rsqrt(jnp.mean(x * x, axis=-1, keepdims=True) + EPS) * g


def _cparams(*sem):
    return pltpu.CompilerParams(dimension_semantics=sem, vmem_limit_bytes=VMEM_LIMIT)


def _full(a):
    return pl.BlockSpec(a.shape, lambda *_: (0,) * a.ndim, pipeline_mode=pl.Buffered(1))


def _bucket_map_t():
    jj = np.arange(2 * BLK)[:, None]
    i = np.arange(BLK)[None, :]
    d = i - jj + BLK
    n = np.maximum(d, 0)
    exact = NUM_BUCKETS // 2
    nf = np.maximum(n, 1).astype(np.float32)
    large = exact + (np.log(nf / exact) / math.log(MAX_DISTANCE / exact) * (NUM_BUCKETS - exact)).astype(np.int32)
    large = np.minimum(large, NUM_BUCKETS - 1)
    bucket = np.where(n < exact, n, large)
    bmap = np.where(d >= 0, bucket, -1).astype(np.int32)
    return np.concatenate([bmap, bmap[:BLK]], axis=0)


def _bias_tile_kernel(tab_ref, bmap_ref, o_ref):
    h = pl.program_id(0)
    bm = bmap_ref[...]
    base = tab_ref[NUM_BUCKETS - 1, h]
    acc = jnp.full(bm.shape, NEG, F32)
    for b in range(NUM_BUCKETS):
        acc = jnp.where(bm == b, (tab_ref[b, h] - base) * LOG2E, acc)
    o_ref[...] = acc


def _bias_tiles(rel_bias):
    n_heads = rel_bias.shape[1]
    bmap = jnp.asarray(_bucket_map_t())
    return pl.pallas_call(
        _bias_tile_kernel,
        grid=(n_heads,),
        in_specs=[pl.BlockSpec(memory_space=pltpu.SMEM),
                  pl.BlockSpec((3 * BLK, BLK), lambda h: (0, 0))],
        out_specs=pl.BlockSpec((None, 3 * BLK, BLK), lambda h: (h, 0, 0)),
        out_shape=jax.ShapeDtypeStruct((n_heads, 3 * BLK, BLK), F32),
        name="bias_tiles",
    )(rel_bias, bmap)


def _block_tail_kernel(x_ref, a_ref, wo_ref, g_ref, wu_ref, wd_ref, *rest, final):
    if final:
        gf_ref, o_ref, x1_ref, xn_ref, acc_ref = rest
    else:
        o_ref, x1_ref, xn_ref, acc_ref = rest
    f = pl.program_id(1)

    @pl.when(f == 0)
    def _():
        x1 = x_ref[...] + _dot(a_ref[...], wo_ref[...])
        x1_ref[...] = x1
        xn_ref[...] = _rms(x1, g_ref[...]).astype(BF16)
        acc_ref[...] = jnp.zeros_like(acc_ref)

    h = jnp.square(jnp.maximum(_dot(xn_ref[...], wu_ref[...]), 0.0)).astype(BF16)
    acc_ref[...] += _dot(h, wd_ref[...])

    @pl.when(f == pl.num_programs(1) - 1)
    def _():
        out = x1_ref[...] + acc_ref[...]
        o_ref[...] = _rms(out, gf_ref[...]) if final else out


def _block_tail(x2, attn, w_o, g, w_up, w_down, g_final=None, tm=1024, tf=2048):
    n, d = x2.shape
    k = attn.shape[1]
    d_ff = w_up.shape[1]
    final = g_final is not None
    row = lambda w: pl.BlockSpec((tm, w), lambda i, f: (i, 0))
    vec = pl.BlockSpec((1, d), lambda i, f: (0, 0))
    in_specs = [row(d), row(k), pl.BlockSpec((k, d), lambda i, f: (0, 0), pipeline_mode=pl.Buffered(1)), vec,
                pl.BlockSpec((d, tf), lambda i, f: (0, f)), pl.BlockSpec((tf, d), lambda i, f: (f, 0))]
    args = [x2, attn, w_o, g.reshape(1, d), w_up, w_down]
    if final:
        in_specs.append(vec)
        args.append(g_final.reshape(1, d))
    return pl.pallas_call(
        functools.partial(_block_tail_kernel, final=final),
        grid=(n // tm, d_ff // tf),
        in_specs=in_specs,
        out_specs=row(d),
        out_shape=jax.ShapeDtypeStruct((n, d), F32),
        scratch_shapes=[pltpu.VMEM((tm, d), F32), pltpu.VMEM((tm, d), BF16), pltpu.VMEM((tm, d), F32)],
        compiler_params=_cparams("parallel", "arbitrary"),
        name="block_tail",
    )(*args)


def _moba_proj_kernel(x_ref, g_ref, wqt_ref, wk_ref, wvt_ref, qt_ref, k_ref, vt_ref):
    xn = _rms(x_ref[...], g_ref[...]).astype(BF16)
    qt_ref[...] = _dot_nt(wqt_ref[...], xn).astype(BF16)
    k_ref[...] = _dot(xn, wk_ref[...]).astype(BF16)
    v_t = _dot_nt(wvt_ref[...], xn).astype(BF16)
    for r in range(vt_ref.shape[0]):
        vt_ref[r] = v_t[:, r * BLK:(r + 1) * BLK]


def _moba_proj(x3, g, wq_t, wk, wv_t):
    b, seq, d = x3.shape
    g2 = g.reshape(1, d)
    return pl.pallas_call(
        _moba_proj_kernel,
        grid=(b, seq // PROJ_TM),
        in_specs=[pl.BlockSpec((None, PROJ_TM, d), lambda bi, i: (bi, i, 0)),
                  _full(g2), _full(wq_t), _full(wk), _full(wv_t)],
        out_specs=[pl.BlockSpec((None, d, PROJ_TM), lambda bi, i: (bi, 0, i)),
                   pl.BlockSpec((None, PROJ_TM, d), lambda bi, i: (bi, i, 0)),
                   pl.BlockSpec((None, PROJ_TM // BLK, d, BLK), lambda bi, i: (bi, i, 0, 0))],
        out_shape=[jax.ShapeDtypeStruct((b, d, seq), BF16),
                   jax.ShapeDtypeStruct((b, seq, d), BF16),
                   jax.ShapeDtypeStruct((b, seq // BLK, d, BLK), BF16)],
        compiler_params=_cparams("parallel", "parallel"),
        name="moba_proj",
    )(x3, g2, wq_t, wk, wv_t)


def _moba_kernel(qt_ref, k_ref, vt_ref, bt_ref, o_ref, kaug_ref, kmean_ref, qfar_ref, m_ref, l_ref, acc_ref, s_ref,
                 smax_ref, *, n_blk, k_sel, sel_rows):
    j = pl.program_id(2)
    seq = k_ref.shape[0]
    n_pairs = MOBA_GRP // 2
    t = BLK
    row = lax.broadcasted_iota(I32, (LANES, t), 0)
    blk_id = lax.broadcasted_iota(I32, (sel_rows, t), 0)
    ones_2 = jnp.ones((BF16_ROWS, 2 * t), BF16)

    @pl.when(j == 0)
    def _():
        krow = lax.broadcasted_iota(I32, (seq, LANES), 0)
        kcol = lax.broadcasted_iota(I32, (seq, LANES), 1)
        onehot = jnp.where(krow // BLK == kcol, 1.0, 0.0).astype(BF16)
        for p in range(n_pairs):
            kaug_ref[p, :, 0:LANES] = k_ref[:, p * LANES:(p + 1) * LANES]
            kaug_ref[p, :, LANES:2 * LANES] = onehot
            kmean_ref[p] = jnp.zeros((LANES, LANES), F32)
            for n in range(n_blk):
                kmean_ref[p, n:n + 1, :] = jnp.mean(
                    k_ref[n * BLK:(n + 1) * BLK, p * LANES:(p + 1) * LANES].astype(F32), axis=0, keepdims=True)

    near_blk = jnp.maximum(j - 1, 0)
    near0 = pl.multiple_of(near_blk * BLK, BLK)
    bias0 = pl.multiple_of(jnp.where(j == 0, BLK, 0), BLK)
    pad_rows = jnp.zeros((LANES - sel_rows, t), BF16)

    def store(slot, p, s):
        s_ref[slot, p] = s
        smax_ref[slot, p] = jnp.max(s, axis=0, keepdims=True)

    def drain(slot, blk0):
        for p in range(n_pairs):
            feats = slice(p * LANES, (p + 1) * LANES)
            m_old = m_ref[p]
            m_new = jnp.maximum(m_old, smax_ref[slot, p])
            alpha = jnp.exp2(m_old - m_new)
            pr = jnp.exp2(s_ref[slot, p] - m_new).astype(BF16)
            m_ref[p] = m_new
            v_t = jnp.concatenate([vt_ref[blk0, feats, :], vt_ref[blk0 + 1, feats, :]], axis=1)
            pv = _dot(jnp.concatenate([v_t, ones_2], axis=0), pr)
            acc_ref[p] = alpha * acc_ref[p] + pv[0:LANES, :]
            l_ref[p] = alpha * l_ref[p] + pv[LANES:LANES + 1, :]

    def fill_far(slot, pi):
        n0 = pl.multiple_of(pi * 2 * BLK, 2 * BLK)
        for p in range(n_pairs):
            store(slot, p, _dot(kaug_ref[p, pl.ds(n0, 2 * BLK), :], qfar_ref[p]))

    q_near = []
    for p in range(n_pairs):
        qp = qt_ref[p * LANES:(p + 1) * LANES, :]
        near, far = [], []
        for hh in range(2):
            qh = jnp.where((row < HEAD_DIM) if hh == 0 else (row >= HEAD_DIM), qp, jnp.zeros_like(qp))
            gate = _dot(kmean_ref[p], qh.astype(F32))[0:sel_rows, :]
            g = jnp.where(blk_id < j, gate, NEG)
            g = jnp.where(blk_id < n_blk, g, -jnp.inf)
            sel_bias = jnp.full((sel_rows, t), NEG, F32)
            for it in range(k_sel):
                mx = jnp.max(g, axis=0, keepdims=True)
                first = jnp.min(jnp.where(g == mx, blk_id, LANES), axis=0, keepdims=True)
                pick = blk_id == first
                sel_bias = jnp.where(pick, jnp.where(it < j, 0.0, NEG), sel_bias)
                g = jnp.where(pick, -jnp.inf, g)
            near_bias = jnp.where(blk_id == j, 0.0, sel_bias)
            far_bias = jnp.where(blk_id < j - 1, sel_bias, NEG)
            near.append(jnp.concatenate([qh, near_bias.astype(BF16), pad_rows], axis=0))
            far.append(jnp.concatenate([qh, far_bias.astype(BF16), pad_rows], axis=0))
        q_near.append(jnp.concatenate(near, axis=1))
        qfar_ref[p] = jnp.concatenate(far, axis=1)
        m_ref[p] = jnp.full((1, 2 * t), NEG, F32)
        l_ref[p] = jnp.zeros((1, 2 * t), F32)
        acc_ref[p] = jnp.zeros((LANES, 2 * t), F32)

    for p in range(n_pairs):
        bias = jnp.concatenate([bt_ref[2 * p, pl.ds(bias0, 2 * BLK), :], bt_ref[2 * p + 1, pl.ds(bias0, 2 * BLK), :]],
                               axis=1)
        store(0, p, _dot(kaug_ref[p, pl.ds(near0, 2 * BLK), :], q_near[p]) + bias)

    n_far = jnp.maximum(j, 1) // 2
    fill_far(1, 0)
    drain(0, near_blk)

    def far_two(i, _):
        pi = 2 * i
        fill_far(0, pi + 1)
        drain(1, 2 * pi)
        fill_far(1, pi + 2)
        drain(0, 2 * pi + 2)
        return 0

    n_trip = jnp.maximum(n_far - 1, 0) // 2
    lax.fori_loop(0, n_trip, far_two, 0)
    left = n_far - 2 * n_trip
    last = 2 * n_trip

    @pl.when(left == 1)
    def _():
        drain(1, 2 * last)

    @pl.when(left == 2)
    def _():
        fill_far(0, last + 1)
        drain(1, 2 * last)
        drain(0, 2 * last + 2)

    for p in range(n_pairs):
        acc = acc_ref[p] / l_ref[p]
        out_t = jnp.where(row < HEAD_DIM, acc[:, 0:t], acc[:, t:2 * t])
        o_ref[:, p * LANES:(p + 1) * LANES] = out_t.T.astype(o_ref.dtype)


def _moba_attention(q_t, k, v_t, bt, n_heads):
    b, seq, d = k.shape
    n_blk = seq // BLK
    n_grp = n_heads // MOBA_GRP
    n_pairs = MOBA_GRP // 2
    w = MOBA_GRP * HEAD_DIM
    k_sel = max(1, min(MOBA_TOPK, n_blk - 1))
    sel_rows = BF16_ROWS * pl.cdiv(n_blk, BF16_ROWS)
    assert sel_rows <= LANES and n_blk % 2 == 0
    return pl.pallas_call(
        functools.partial(_moba_kernel, n_blk=n_blk, k_sel=k_sel, sel_rows=sel_rows),
        grid=(b, n_grp, n_blk),
        in_specs=[pl.BlockSpec((None, w, BLK), lambda bi, g, j: (bi, g, j)),
                  pl.BlockSpec((None, seq, w), lambda bi, g, j: (bi, 0, g)),
                  pl.BlockSpec((None, n_blk, w, BLK), lambda bi, g, j: (bi, 0, g, 0)),
                  pl.BlockSpec((MOBA_GRP, 3 * BLK, BLK), lambda bi, g, j: (g, 0, 0), pipeline_mode=pl.Buffered(1))],
        out_specs=pl.BlockSpec((None, BLK, w), lambda bi, g, j: (bi, j, g)),
        out_shape=jax.ShapeDtypeStruct((b, seq, d), BF16),
        scratch_shapes=[pltpu.VMEM((n_pairs, seq, 2 * LANES), BF16),
                        pltpu.VMEM((n_pairs, LANES, LANES), F32),
                        pltpu.VMEM((n_pairs, 2 * LANES, 2 * BLK), BF16),
                        pltpu.VMEM((n_pairs, 1, 2 * BLK), F32),
                        pltpu.VMEM((n_pairs, 1, 2 * BLK), F32),
                        pltpu.VMEM((n_pairs, LANES, 2 * BLK), F32),
                        pltpu.VMEM((2, n_pairs, 2 * BLK, 2 * BLK), F32),
                        pltpu.VMEM((2, n_pairs, 1, 2 * BLK), F32)],
        compiler_params=_cparams("parallel", "parallel", "arbitrary"),
        name="moba_attn",
    )(q_t, k, v_t, bt)


def _dsa_proj_kernel(x_ref, g_ref, win_ref, wkvt_ref, wwt_ref, gq_ref, gkv_ref, gkvt_ref, wuqt_ref, wqit_ref,
                     wuvt_ref, ckv_ref, vt_ref, kk_ref, widx_ref, qnt_ref, qit_ref, *, idx_scale):
    hn = _rms(x_ref[...], g_ref[...]).astype(BF16)
    proj = _dot(hn, win_ref[...])
    c_q = _rms(proj[:, 0:DSA_LORA], gq_ref[...]).astype(BF16)
    ckv_ref[...] = _rms(proj[:, DSA_LORA:2 * DSA_LORA], gkv_ref[...]).astype(BF16)
    kk_ref[...] = proj[:, 2 * DSA_LORA:2 * DSA_LORA + IDX_DIM].astype(BF16)
    ct = _dot_nt(wkvt_ref[...], hn)
    scale = lax.rsqrt(jnp.mean(ct * ct, axis=0, keepdims=True) + EPS)
    gt = jnp.concatenate([gkvt_ref[...]] * (ct.shape[1] // LANES), axis=1)
    v_t = _dot(wuvt_ref[...], (ct * scale * gt).astype(BF16)).astype(BF16)
    for r in range(vt_ref.shape[0]):
        vt_ref[r] = v_t[:, r * BLK:(r + 1) * BLK]
    widx_ref[...] = _dot_nt(wwt_ref[...], hn)[0:IDX_HEADS, :] * idx_scale
    qnt_ref[...] = _dot_nt(wuqt_ref[...], c_q).astype(BF16)
    qit_ref[...] = _dot_nt(wqit_ref[...], c_q).astype(BF16)


def _dsa_proj(x3, g, w_in_row, w_kv_t, w_w_t, g_q, g_kv, w_uq_t, w_qi_t, w_uv_t):
    b, seq, d = x3.shape
    tm = PROJ_TM
    g2, gq2, gkv2 = g.reshape(1, d), g_q.reshape(1, -1), g_kv.reshape(1, -1)
    gkv_t = jnp.broadcast_to(g_kv.reshape(-1, 1), (DSA_LORA, LANES))
    n_q, n_i, n_v = w_uq_t.shape[0], w_qi_t.shape[0], w_uv_t.shape[0]
    consts = (g2, w_in_row, w_kv_t, w_w_t, gq2, gkv2, gkv_t, w_uq_t, w_qi_t, w_uv_t)
    idx_scale = IDX_HEADS ** -0.5 * IDX_DIM ** -0.5
    return pl.pallas_call(
        functools.partial(_dsa_proj_kernel, idx_scale=idx_scale),
        grid=(b, seq // tm),
        in_specs=[pl.BlockSpec((None, tm, d), lambda bi, i: (bi, i, 0))] + [_full(a) for a in consts],
        out_specs=[pl.BlockSpec((None, tm, DSA_LORA), lambda bi, i: (bi, i, 0)),
                   pl.BlockSpec((None, tm // BLK, n_v, BLK), lambda bi, i: (bi, i, 0, 0)),
                   pl.BlockSpec((None, tm, IDX_DIM), lambda bi, i: (bi, i, 0)),
                   pl.BlockSpec((None, IDX_HEADS, tm), lambda bi, i: (bi, 0, i)),
                   pl.BlockSpec((None, n_q, tm), lambda bi, i: (bi, 0, i)),
                   pl.BlockSpec((None, n_i, tm), lambda bi, i: (bi, 0, i))],
        out_shape=[jax.ShapeDtypeStruct((b, seq, DSA_LORA), BF16),
                   jax.ShapeDtypeStruct((b, seq // BLK, n_v, BLK), BF16),
                   jax.ShapeDtypeStruct((b, seq, IDX_DIM), BF16),
                   jax.ShapeDtypeStruct((b, IDX_HEADS, seq), F32),
                   jax.ShapeDtypeStruct((b, n_q, seq), BF16),
                   jax.ShapeDtypeStruct((b, n_i, seq), BF16)],
        compiler_params=_cparams("parallel", "parallel"),
        name="dsa_proj",
    )(x3, *consts)


def _dsa_select_kernel(qit_ref, w_ref, kk_ref, o_ref, keys_ref, half_ref, *, top_k, idx_bits):
    c = pl.program_id(1)
    n_kc, tk, tq = keys_ref.shape
    n_proc = c + 1
    key_row = lax.broadcasted_iota(I32, (tk, tq), 0)
    q_pos = c * tq + lax.broadcasted_iota(I32, (tk, tq), 1)
    w = w_ref[...]

    def score_chunk(kc):
        kt = kk_ref[pl.ds(pl.multiple_of(kc * tk, tk), tk), :]
        acc = jnp.zeros((tk, tq), F32)
        for h in range(IDX_HEADS):
            x = _dot(kt, qit_ref[h * IDX_DIM:(h + 1) * IDX_DIM, :])
            acc = acc + w[h:h + 1, :] * jnp.maximum(x, 0.0)
        sc = jnp.where(kc * tk + key_row <= q_pos, acc, NEG)
        bits = pltpu.bitcast(sc, I32)
        key = jnp.where(bits < 0, bits ^ 0x7FFFFFFF, bits)
        keys_ref[kc] = key
        half_ref[kc] = jnp.right_shift(key, 16).astype(I16)

    def score_two(i, _):
        score_chunk(2 * i)
        score_chunk(2 * i + 1)
        return 0

    lax.fori_loop(0, n_proc // 2, score_two, 0)

    @pl.when(n_proc % 2 == 1)
    def _():
        score_chunk(n_proc - 1)

    n_pair = (n_proc + 1) // 2

    @pl.when(n_proc % 2 == 1)
    def _():
        half_ref[n_proc] = jnp.full((tk, tq), -32768, I16)

    def count(pred):
        def body(kc, tot):
            hit = jnp.where(pred(keys_ref[kc], kc * tk + key_row), 1, 0)
            return tot + jnp.sum(hit, axis=0, keepdims=True)
        return lax.fori_loop(0, n_proc, body, jnp.zeros((1, tq), I32))

    def count_half(cand, strict=False):
        cand16 = cand.astype(I16)

        def body(i, part):
            tiles = []
            for kc in (2 * i, 2 * i + 1):
                x = half_ref[kc]
                hit = jnp.where((x > cand16) if strict else (x >= cand16), jnp.int16(1), jnp.int16(0))
                tiles += [hit[r:r + BF16_ROWS, :] for r in range(0, tk, BF16_ROWS)]
            while len(tiles) > 1:
                tiles = [a + b for a, b in zip(tiles[::2], tiles[1::2])]
            return part + tiles[0]
        part = lax.fori_loop(0, n_pair, body, jnp.zeros((BF16_ROWS, tq), I16))
        return jnp.sum(part.astype(I32), axis=0, keepdims=True)

    def largest_half(need):
        def bit(i, u):
            cand_u = u | jnp.left_shift(jnp.int32(1), 15 - i)
            return jnp.where(count_half(cand_u - 32768) >= need, cand_u, u)
        return lax.fori_loop(0, 16, bit, jnp.zeros((1, tq), I32)) - 32768

    thr_hi = largest_half(top_k)
    above = count_half(thr_hi, strict=True)

    def low_chunk(kc, _):
        key = keys_ref[kc]
        low = jnp.bitwise_and(key, 0xFFFF) - 32768
        half_ref[kc] = jnp.where(jnp.right_shift(key, 16) == thr_hi, low, -32768).astype(I16)
        return 0

    lax.fori_loop(0, n_proc, low_chunk, 0)
    thr_lo = largest_half(top_k - above)
    cnt_ge = above + count_half(thr_lo)
    thr = jnp.left_shift(thr_hi, 16) + (thr_lo + 32768)

    def tie_cut():
        need = top_k - count(lambda key, pos: key > thr)

        def index_bit(i, x):
            cand = x | jnp.left_shift(jnp.int32(1), idx_bits - 1 - i)
            cnt = count(lambda key, pos: (key == thr) & (pos < cand))
            return jnp.where(cnt < need, cand, x)
        return lax.fori_loop(0, idx_bits, index_bit, jnp.zeros((1, tq), I32))

    def write(chosen):
        def write_chunk(kc, _):
            key = keys_ref[kc]
            pos = kc * tk + key_row
            rows = pl.ds(pl.multiple_of(kc * tk, tk), tk)
            o_ref[rows, :] = jnp.where(chosen(key, pos) & (pos <= q_pos), 0.0, NEG).astype(o_ref.dtype)
            return 0
        lax.fori_loop(0, n_proc, write_chunk, 0)

    tied = jnp.max(cnt_ge) > top_k

    @pl.when(jnp.logical_not(tied))
    def _():
        write(lambda key, pos: key >= thr)

    @pl.when(tied)
    def _():
        last_tie = tie_cut()
        write(lambda key, pos: (key > thr) | ((key == thr) & (pos <= last_tie)))

    def fill_chunk(kc, _):
        o_ref[pl.ds(pl.multiple_of(kc * tk, tk), tk), :] = jnp.full((tk, tq), NEG, o_ref.dtype)
        return 0

    lax.fori_loop(n_proc, n_kc, fill_chunk, 0)


def _dsa_select(qi_t, widx_t, kk):
    b, seq, _ = kk.shape
    t = BLK
    n_kc = seq // t
    top_k = min(DSA_TOPK_MAX, seq // 4)
    assert top_k <= t
    assert n_kc % 2 == 0
    idx_bits = max(1, (seq - 1).bit_length())
    return pl.pallas_call(
        functools.partial(_dsa_select_kernel, top_k=top_k, idx_bits=idx_bits),
        grid=(b, seq // t),
        in_specs=[pl.BlockSpec((None, qi_t.shape[1], t), lambda bi, c: (bi, 0, c)),
                  pl.BlockSpec((None, IDX_HEADS, t), lambda bi, c: (bi, 0, c)),
                  pl.BlockSpec((None, seq, IDX_DIM), lambda bi, c: (bi, 0, 0))],
        out_specs=pl.BlockSpec((None, None, seq, t), lambda bi, c: (bi, c, 0, 0)),
        out_shape=jax.ShapeDtypeStruct((b, seq // t, seq, t), BF16),
        scratch_shapes=[pltpu.VMEM((n_kc, t, t), I32),
                        pltpu.VMEM((n_kc, t, t), I16)],
        compiler_params=_cparams("parallel", "arbitrary"),
        name="dsa_select",
    )(qi_t, widx_t, kk)


def _dsa_attn_kernel(qnt_ref, ckv_ref, vt_ref, mb_ref, bt_ref, wukt_ref, o_ref,
                     qlat_ref, acc_ref, m_ref, l_ref, s_ref, smax_ref, *, n_heads):
    qt = pl.program_id(1)
    t = BLK
    n_grp = n_heads // DSA_GRP
    row = lax.broadcasted_iota(I32, (LANES, t), 0)
    ones_rows = jnp.ones((BF16_ROWS, t), BF16)

    for hp in range(n_heads // 2):
        qp = qnt_ref[hp * LANES:(hp + 1) * LANES, :]
        for hh in range(2):
            qh = jnp.where((row < HEAD_DIM) if hh == 0 else (row >= HEAD_DIM), qp, jnp.zeros_like(qp))
            h = 2 * hp + hh
            qlat_ref[:, h * t:(h + 1) * t] = _dot(wukt_ref[hp], qh).astype(BF16)

    m_ref[...] = jnp.full(m_ref.shape, NEG, F32)
    l_ref[...] = jnp.zeros_like(l_ref)
    acc_ref[...] = jnp.zeros_like(acc_ref)

    def grp_cols(g):
        return slice(g * DSA_GRP * t, (g + 1) * DSA_GRP * t)

    def key_rows(kk):
        return pl.ds(pl.multiple_of(kk * t, t), t)

    def fill(slot, kk, bias_row0, off=None):
        kt = ckv_ref[key_rows(kk), :]
        mb = mb_ref[key_rows(kk), :].astype(F32)
        if off is not None:
            mb = mb + off
        for g in range(n_grp):
            s_grp = _dot(kt, qlat_ref[:, grp_cols(g)])
            parts = []
            for i in range(DSA_GRP):
                s = s_grp[:, i * t:(i + 1) * t] + mb
                if bias_row0 is not None:
                    s = s + bt_ref[g * DSA_GRP + i, bias_row0:bias_row0 + t, :]
                parts.append(s)
            s = jnp.concatenate(parts, axis=1)
            s_ref[slot, :, grp_cols(g)] = s
            smax_ref[slot, :, grp_cols(g)] = jnp.max(s, axis=0, keepdims=True)

    def drain(slot, kk):
        for g in range(n_grp):
            cols = grp_cols(g)
            m_old = m_ref[:, cols]
            m_new = jnp.maximum(m_old, smax_ref[slot, :, cols])
            alpha = jnp.exp2(m_old - m_new)
            pr = jnp.exp2(s_ref[slot, :, cols] - m_new).astype(BF16)
            m_ref[:, cols] = m_new
            for i in range(DSA_GRP):
                h = g * DSA_GRP + i
                feats = slice(h * HEAD_DIM, (h + 1) * HEAD_DIM)
                hc = slice(i * t, (i + 1) * t)
                lc = slice(h * t, (h + 1) * t)
                pv = _dot(jnp.concatenate([vt_ref[kk, feats, :], ones_rows], axis=0), pr[:, hc])
                acc_ref[feats, :] = alpha[:, hc] * acc_ref[feats, :] + pv[0:HEAD_DIM, :]
                l_ref[:, lc] = alpha[:, hc] * l_ref[:, lc] + pv[HEAD_DIM:HEAD_DIM + 1, :]

    n_far = jnp.maximum(qt - 1, 0)
    fill(0, qt, BLK)
    fill(1, jnp.maximum(qt - 1, 0), 0, off=jnp.where(qt >= 1, 0.0, NEG))
    drain(0, qt)
    fill(0, 0, None)
    drain(1, jnp.maximum(qt - 1, 0))

    def far_pair(i, _):
        kk = 2 * i
        fill(1, kk + 1, None)
        drain(0, kk)
        fill(0, kk + 2, None)
        drain(1, kk + 1)
        return 0

    n_trip = jnp.maximum(n_far - 1, 0) // 2
    lax.fori_loop(0, n_trip, far_pair, 0)
    left = n_far - 2 * n_trip
    last = 2 * n_trip

    @pl.when(left == 1)
    def _():
        drain(0, last)

    @pl.when(left == 2)
    def _():
        fill(1, last + 1, None)
        drain(0, last)
        drain(1, last + 1)

    for h in range(n_heads):
        feats = slice(h * HEAD_DIM, (h + 1) * HEAD_DIM)
        acc_ref[feats, :] = acc_ref[feats, :] / l_ref[:, h * t:(h + 1) * t]
    o_ref[...] = acc_ref[...].T.astype(o_ref.dtype)


def _dsa_attention(qn_t, ckv, v_t, mask_t, bt, w_uk_t, n_heads):
    b, seq, _ = ckv.shape
    t = BLK
    d_q = qn_t.shape[1]
    return pl.pallas_call(
        functools.partial(_dsa_attn_kernel, n_heads=n_heads),
        grid=(b, seq // t),
        in_specs=[pl.BlockSpec((None, d_q, t), lambda bi, q: (bi, 0, q)),
                  pl.BlockSpec((None, seq, DSA_LORA), lambda bi, q: (bi, 0, 0)),
                  pl.BlockSpec((None, seq // t, d_q, t), lambda bi, q: (bi, 0, 0, 0)),
                  pl.BlockSpec((None, None, seq, t), lambda bi, q: (bi, q, 0, 0)),
                  _full(bt), _full(w_uk_t)],
        out_specs=pl.BlockSpec((None, t, d_q), lambda bi, q: (bi, q, 0)),
        out_shape=jax.ShapeDtypeStruct((b, seq, d_q), BF16),
        scratch_shapes=[pltpu.VMEM((DSA_LORA, n_heads * t), BF16),
                        pltpu.VMEM((d_q, t), F32),
                        pltpu.VMEM((1, n_heads * t), F32),
                        pltpu.VMEM((1, n_heads * t), F32),
                        pltpu.VMEM((2, t, n_heads * t), F32),
                        pltpu.VMEM((2, 1, n_heads * t), F32)],
        compiler_params=_cparams("parallel", "arbitrary"),
        name="dsa_attn",
    )(qn_t, ckv, v_t, mask_t, bt, w_uk_t)


def _moba_layer(x2, batch, seq, g, w_qkv, bt, n_heads):
    d = x2.shape[1]
    wq_t = (w_qkv[:, :d] * (HEAD_DIM ** -0.5 * LOG2E)).T.astype(BF16)
    wk = w_qkv[:, d:2 * d].astype(BF16)
    wv_t = w_qkv[:, 2 * d:].T.astype(BF16)
    q_t, k, v_t = _moba_proj(x2.reshape(batch, seq, d), g, wq_t, wk, wv_t)
    attn = _moba_attention(q_t, k, v_t, bt, n_heads)
    return attn.reshape(batch * seq, d)


def _dsa_layer(x2, batch, seq, g, w_in, g_q, g_kv, w_uq, w_qi, w_uk, w_uv, bt, n_heads):
    d = x2.shape[1]
    k_lo = 2 * DSA_LORA
    w_in_row = jnp.concatenate([w_in[:, :k_lo + IDX_DIM], jnp.zeros((d, LANES - IDX_DIM), F32)], axis=1).astype(BF16)
    w_kv_t = w_in[:, DSA_LORA:k_lo].T.astype(BF16)
    w_w_t = jnp.concatenate([w_in[:, k_lo + IDX_DIM:].T, jnp.zeros((BF16_ROWS - IDX_HEADS, d), F32)]).astype(BF16)
    w_uq_t = (w_uq * (HEAD_DIM ** -0.5 * LOG2E)).T.astype(BF16)
    w_uv_t = jnp.transpose(w_uv, (0, 2, 1)).reshape(n_heads * HEAD_DIM, DSA_LORA).astype(BF16)
    ckv, v_t, kk, widx_t, qn_t, qi_t = _dsa_proj(x2.reshape(batch, seq, d), g, w_in_row, w_kv_t, w_w_t,
                                                 g_q, g_kv, w_uq_t, w_qi.T.astype(BF16), w_uv_t)
    mask_t = _dsa_select(qi_t, widx_t, kk)
    w_uk_t = jnp.transpose(w_uk.reshape(n_heads // 2, 2 * HEAD_DIM, DSA_LORA), (0, 2, 1)).astype(BF16)
    attn = _dsa_attention(qn_t, ckv, v_t, mask_t, bt[:, :2 * BLK], w_uk_t, n_heads)
    return attn.reshape(batch * seq, -1)


def kernel(x, rel_bias, ln_attn, ln_mlp, moba_w_qkv, moba_w_o, dsa_w_in, dsa_g_q, dsa_g_kv, dsa_w_uq, dsa_w_qi,
           dsa_w_uk, dsa_w_uv, dsa_w_o, mlp_w_up, mlp_w_down, final_norm):
    batch, seq, d = x.shape
    n_heads = rel_bias.shape[1]
    depth = ln_attn.shape[0]
    bt = _bias_tiles(rel_bias)
    x2 = x.reshape(batch * seq, d)
    for i in range(depth):
        j = i // 2
        if i % 2 == 0:
            attn, w_o = _moba_layer(x2, batch, seq, ln_attn[i], moba_w_qkv[j], bt, n_heads), moba_w_o[j]
        else:
            attn, w_o = _dsa_layer(x2, batch, seq, ln_attn[i], dsa_w_in[j], dsa_g_q[j], dsa_g_kv[j], dsa_w_uq[j],
                                   dsa_w_qi[j], dsa_w_uk[j], dsa_w_uv[j], bt, n_heads), dsa_w_o[j]
        x2 = _block_tail(x2, attn, w_o.astype(BF16), ln_mlp[i], mlp_w_up[i].astype(BF16), mlp_w_down[i].astype(BF16),
                         g_final=final_norm if i == depth - 1 else None)
    return x2.reshape(batch, seq, d)
```

```python
import functools
import math

import numpy as np
import jax
import jax.numpy as jnp
from jax import lax
from jax.experimental import pallas as pl
from jax.experimental.pallas import tpu as pltpu

F32 = jnp.float32
BF16 = jnp.bfloat16
I32 = jnp.int32
I16 = jnp.int16

EPS = 1e-6
NEG = -1e30
INT_MIN = -(2 ** 31)
LOG2E = 1.0 / math.log(2.0)

NUM_BUCKETS = 32
MAX_DISTANCE = 128
HEAD_DIM = 64
LANES = 128
BF16_ROWS = 16
BLK = 256
PROJ_TM = 1024
MOBA_TOPK = 3
MOBA_GRP = 8
DSA_TOPK_MAX = 256
DSA_LORA = 256
DSA_GRP = 4
IDX_HEADS = 8
IDX_DIM = 64
SCORE_UNROLL = 4
VMEM_LIMIT = 56 * 1024 * 1024


def _dot(a, b):
    return jnp.dot(a, b, preferred_element_type=F32)


def _dot_nt(a, b):
    return lax.dot_general(a, b, (((1,), (1,)), ((), ())), preferred_element_type=F32)


def _rms(x, g):
    return x * lax.rsqrt(jnp.mean(x * x, axis=-1, keepdims=True) + EPS) * g


def _cparams(*sem):
    return pltpu.CompilerParams(dimension_semantics=sem, vmem_limit_bytes=VMEM_LIMIT)


def _full(a):
    return pl.BlockSpec(a.shape, lambda *_: (0,) * a.ndim, pipeline_mode=pl.Buffered(1))


def _bucket_map_t():
    jj = np.arange(2 * BLK)[:, None]
    i = np.arange(BLK)[None, :]
    d = i - jj + BLK
    n = np.maximum(d, 0)
    exact = NUM_BUCKETS // 2
    nf = np.maximum(n, 1).astype(np.float32)
    large = exact + (np.log(nf / exact) / math.log(MAX_DISTANCE / exact) * (NUM_BUCKETS - exact)).astype(np.int32)
    large = np.minimum(large, NUM_BUCKETS - 1)
    bucket = np.where(n < exact, n, large)
    bmap = np.where(d >= 0, bucket, -1).astype(np.int32)
    return np.concatenate([bmap, bmap[:BLK]], axis=0)


def _bias_tile_kernel(tab_ref, bmap_ref, o_ref):
    h = pl.program_id(0)
    bm = bmap_ref[...]
    base = tab_ref[NUM_BUCKETS - 1, h]
    acc = jnp.full(bm.shape, NEG, F32)
    for b in range(NUM_BUCKETS):
        acc = jnp.where(bm == b, (tab_ref[b, h] - base) * LOG2E, acc)
    o_ref[...] = acc


def _bias_tiles(rel_bias):
    n_heads = rel_bias.shape[1]
    bmap = jnp.asarray(_bucket_map_t())
    return pl.pallas_call(
        _bias_tile_kernel,
        grid=(n_heads,),
        in_specs=[pl.BlockSpec(memory_space=pltpu.SMEM),
                  pl.BlockSpec((3 * BLK, BLK), lambda h: (0, 0))],
        out_specs=pl.BlockSpec((None, 3 * BLK, BLK), lambda h: (h, 0, 0)),
        out_shape=jax.ShapeDtypeStruct((n_heads, 3 * BLK, BLK), F32),
        name="bias_tiles",
    )(rel_bias, bmap)


def _block_tail_kernel(x_ref, a_ref, wo_ref, g_ref, wu_ref, wd_ref, *rest, final):
    if final:
        gf_ref, o_ref, x1_ref, xn_ref, acc_ref = rest
    else:
        o_ref, x1_ref, xn_ref, acc_ref = rest
    f = pl.program_id(1)

    @pl.when(f == 0)
    def _():
        x1 = x_ref[...] + _dot(a_ref[...], wo_ref[...])
        x1_ref[...] = x1
        xn_ref[...] = _rms(x1, g_ref[...]).astype(BF16)
        acc_ref[...] = jnp.zeros_like(acc_ref)

    h = jnp.square(jnp.maximum(_dot(xn_ref[...], wu_ref[...]), 0.0)).astype(BF16)
    acc_ref[...] += _dot(h, wd_ref[...])

    @pl.when(f == pl.num_programs(1) - 1)
    def _():
        out = x1_ref[...] + acc_ref[...]
        o_ref[...] = _rms(out, gf_ref[...]) if final else out


def _block_tail(x2, attn, w_o, g, w_up, w_down, g_final=None, tm=1024, tf=2048):
    n, d = x2.shape
    k = attn.shape[1]
    d_ff = w_up.shape[1]
    final = g_final is not None
    row = lambda w: pl.BlockSpec((tm, w), lambda i, f: (i, 0))
    vec = pl.BlockSpec((1, d), lambda i, f: (0, 0))
    in_specs = [row(d), row(k), pl.BlockSpec((k, d), lambda i, f: (0, 0), pipeline_mode=pl.Buffered(1)), vec,
                pl.BlockSpec((d, tf), lambda i, f: (0, f)), pl.BlockSpec((tf, d), lambda i, f: (f, 0))]
    args = [x2, attn, w_o, g.reshape(1, d), w_up, w_down]
    if final:
        in_specs.append(vec)
        args.append(g_final.reshape(1, d))
    return pl.pallas_call(
        functools.partial(_block_tail_kernel, final=final),
        grid=(n // tm, d_ff // tf),
        in_specs=in_specs,
        out_specs=row(d),
        out_shape=jax.ShapeDtypeStruct((n, d), F32),
        scratch_shapes=[pltpu.VMEM((tm, d), F32), pltpu.VMEM((tm, d), BF16), pltpu.VMEM((tm, d), F32)],
        compiler_params=_cparams("parallel", "arbitrary"),
        name="block_tail",
    )(*args)


def _moba_proj_kernel(x_ref, g_ref, wqt_ref, wk_ref, wvt_ref, qt_ref, k_ref, vt_ref):
    xn = _rms(x_ref[...], g_ref[...]).astype(BF16)
    qt_ref[...] = _dot_nt(wqt_ref[...], xn).astype(BF16)
    k_ref[...] = _dot(xn, wk_ref[...]).astype(BF16)
    v_t = _dot_nt(wvt_ref[...], xn).astype(BF16)
    for r in range(vt_ref.shape[0]):
        vt_ref[r] = v_t[:, r * BLK:(r + 1) * BLK]


def _moba_proj(x3, g, wq_t, wk, wv_t):
    b, seq, d = x3.shape
    g2 = g.reshape(1, d)
    return pl.pallas_call(
        _moba_proj_kernel,
        grid=(b, seq // PROJ_TM),
        in_specs=[pl.BlockSpec((None, PROJ_TM, d), lambda bi, i: (bi, i, 0)),
                  _full(g2), _full(wq_t), _full(wk), _full(wv_t)],
        out_specs=[pl.BlockSpec((None, d, PROJ_TM), lambda bi, i: (bi, 0, i)),
                   pl.BlockSpec((None, PROJ_TM, d), lambda bi, i: (bi, i, 0)),
                   pl.BlockSpec((None, PROJ_TM // BLK, d, BLK), lambda bi, i: (bi, i, 0, 0))],
        out_shape=[jax.ShapeDtypeStruct((b, d, seq), BF16),
                   jax.ShapeDtypeStruct((b, seq, d), BF16),
                   jax.ShapeDtypeStruct((b, seq // BLK, d, BLK), BF16)],
        compiler_params=_cparams("parallel", "parallel"),
        name="moba_proj",
    )(x3, g2, wq_t, wk, wv_t)


def _moba_kernel(qt_ref, k_ref, vt_ref, bt_ref, o_ref, kaug_ref, kmean_ref, qfar_ref, m_ref, l_ref, acc_ref, s_ref,
                 smax_ref, *, n_blk, k_sel, sel_rows):
    j = pl.program_id(2)
    seq = k_ref.shape[0]
    n_pairs = MOBA_GRP // 2
    t = BLK
    row = lax.broadcasted_iota(I32, (LANES, t), 0)
    blk_id = lax.broadcasted_iota(I32, (sel_rows, t), 0)
    ones_2 = jnp.ones((BF16_ROWS, 2 * t), BF16)

    @pl.when(j == 0)
    def _():
        krow = lax.broadcasted_iota(I32, (seq, LANES), 0)
        kcol = lax.broadcasted_iota(I32, (seq, LANES), 1)
        onehot = jnp.where(krow // BLK == kcol, 1.0, 0.0).astype(BF16)
        for p in range(n_pairs):
            kaug_ref[p, :, 0:LANES] = k_ref[:, p * LANES:(p + 1) * LANES]
            kaug_ref[p, :, LANES:2 * LANES] = onehot
            kmean_ref[p] = jnp.zeros((LANES, LANES), F32)
            for n in range(n_blk):
                kmean_ref[p, n:n + 1, :] = jnp.mean(
                    k_ref[n * BLK:(n + 1) * BLK, p * LANES:(p + 1) * LANES].astype(F32), axis=0, keepdims=True)

    near_blk = jnp.maximum(j - 1, 0)
    near0 = pl.multiple_of(near_blk * BLK, BLK)
    bias0 = pl.multiple_of(jnp.where(j == 0, BLK, 0), BLK)
    pad_rows = jnp.zeros((LANES - sel_rows, t), BF16)

    def store(slot, p, s):
        s_ref[slot, p] = s
        smax_ref[slot, p] = jnp.max(s, axis=0, keepdims=True)

    def drain(slot, blk0):
        for p in range(n_pairs):
            feats = slice(p * LANES, (p + 1) * LANES)
            m_old = m_ref[p]
            m_new = jnp.maximum(m_old, smax_ref[slot, p])
            alpha = jnp.exp2(m_old - m_new)
            pr = jnp.exp2(s_ref[slot, p] - m_new).astype(BF16)
            m_ref[p] = m_new
            v_t = jnp.concatenate([vt_ref[blk0, feats, :], vt_ref[blk0 + 1, feats, :]], axis=1)
            pv = _dot(jnp.concatenate([v_t, ones_2], axis=0), pr)
            acc_ref[p] = alpha * acc_ref[p] + pv[0:LANES, :]
            l_ref[p] = alpha * l_ref[p] + pv[LANES:LANES + 1, :]

    def fill_far(slot, pi):
        n0 = pl.multiple_of(pi * 2 * BLK, 2 * BLK)
        for p in range(n_pairs):
            store(slot, p, _dot(kaug_ref[p, pl.ds(n0, 2 * BLK), :], qfar_ref[p]))

    q_near = []
    for p in range(n_pairs):
        qp = qt_ref[p * LANES:(p + 1) * LANES, :]
        near, far = [], []
        for hh in range(2):
            qh = jnp.where((row < HEAD_DIM) if hh == 0 else (row >= HEAD_DIM), qp, jnp.zeros_like(qp))
            gate = _dot(kmean_ref[p], qh.astype(F32))[0:sel_rows, :]
            g = jnp.where(blk_id < j, gate, NEG)
            g = jnp.where(blk_id < n_blk, g, -jnp.inf)
            sel_bias = jnp.full((sel_rows, t), NEG, F32)
            for it in range(k_sel):
                mx = jnp.max(g, axis=0, keepdims=True)
                first = jnp.min(jnp.where(g == mx, blk_id, LANES), axis=0, keepdims=True)
                pick = blk_id == first
                sel_bias = jnp.where(pick, jnp.where(it < j, 0.0, NEG), sel_bias)
                g = jnp.where(pick, -jnp.inf, g)
            near_bias = jnp.where(blk_id == j, 0.0, sel_bias)
            far_bias = jnp.where(blk_id < j - 1, sel_bias, NEG)
            near.append(jnp.concatenate([qh, near_bias.astype(BF16), pad_rows], axis=0))
            far.append(jnp.concatenate([qh, far_bias.astype(BF16), pad_rows], axis=0))
        q_near.append(jnp.concatenate(near, axis=1))
        qfar_ref[p] = jnp.concatenate(far, axis=1)
        m_ref[p] = jnp.full((1, 2 * t), NEG, F32)
        l_ref[p] = jnp.zeros((1, 2 * t), F32)
        acc_ref[p] = jnp.zeros((LANES, 2 * t), F32)

    for p in range(n_pairs):
        bias = jnp.concatenate([bt_ref[2 * p, pl.ds(bias0, 2 * BLK), :], bt_ref[2 * p + 1, pl.ds(bias0, 2 * BLK), :]],
                               axis=1)
        store(0, p, _dot(kaug_ref[p, pl.ds(near0, 2 * BLK), :], q_near[p]) + bias)

    n_far = jnp.maximum(j, 1) // 2
    fill_far(1, 0)
    drain(0, near_blk)

    def far_two(i, _):
        pi = 2 * i
        fill_far(0, pi + 1)
        drain(1, 2 * pi)
        fill_far(1, pi + 2)
        drain(0, 2 * pi + 2)
        return 0

    n_trip = jnp.maximum(n_far - 1, 0) // 2
    lax.fori_loop(0, n_trip, far_two, 0)
    left = n_far - 2 * n_trip
    last = 2 * n_trip

    @pl.when(left == 1)
    def _():
        drain(1, 2 * last)

    @pl.when(left == 2)
    def _():
        fill_far(0, last + 1)
        drain(1, 2 * last)
        drain(0, 2 * last + 2)

    for p in range(n_pairs):
        acc = acc_ref[p] / l_ref[p]
        out_t = jnp.where(row < HEAD_DIM, acc[:, 0:t], acc[:, t:2 * t])
        o_ref[:, p * LANES:(p + 1) * LANES] = out_t.T.astype(o_ref.dtype)


def _moba_attention(q_t, k, v_t, bt, n_heads):
    b, seq, d = k.shape
    n_blk = seq // BLK
    n_grp = n_heads // MOBA_GRP
    n_pairs = MOBA_GRP // 2
    w = MOBA_GRP * HEAD_DIM
    k_sel = max(1, min(MOBA_TOPK, n_blk - 1))
    sel_rows = BF16_ROWS * pl.cdiv(n_blk, BF16_ROWS)
    assert sel_rows <= LANES and n_blk % 2 == 0
    return pl.pallas_call(
        functools.partial(_moba_kernel, n_blk=n_blk, k_sel=k_sel, sel_rows=sel_rows),
        grid=(b, n_grp, n_blk),
        in_specs=[pl.BlockSpec((None, w, BLK), lambda bi, g, j: (bi, g, j)),
                  pl.BlockSpec((None, seq, w), lambda bi, g, j: (bi, 0, g)),
                  pl.BlockSpec((None, n_blk, w, BLK), lambda bi, g, j: (bi, 0, g, 0)),
                  pl.BlockSpec((MOBA_GRP, 3 * BLK, BLK), lambda bi, g, j: (g, 0, 0), pipeline_mode=pl.Buffered(1))],
        out_specs=pl.BlockSpec((None, BLK, w), lambda bi, g, j: (bi, j, g)),
        out_shape=jax.ShapeDtypeStruct((b, seq, d), BF16),
        scratch_shapes=[pltpu.VMEM((n_pairs, seq, 2 * LANES), BF16),
                        pltpu.VMEM((n_pairs, LANES, LANES), F32),
                        pltpu.VMEM((n_pairs, 2 * LANES, 2 * BLK), BF16),
                        pltpu.VMEM((n_pairs, 1, 2 * BLK), F32),
                        pltpu.VMEM((n_pairs, 1, 2 * BLK), F32),
                        pltpu.VMEM((n_pairs, LANES, 2 * BLK), F32),
                        pltpu.VMEM((2, n_pairs, 2 * BLK, 2 * BLK), F32),
                        pltpu.VMEM((2, n_pairs, 1, 2 * BLK), F32)],
        compiler_params=_cparams("parallel", "parallel", "arbitrary"),
        name="moba_attn",
    )(q_t, k, v_t, bt)


def _dsa_proj_kernel(x_ref, g_ref, win_ref, wkvt_ref, wwt_ref, gq_ref, gkv_ref, gkvt_ref, wuqt_ref, wqit_ref,
                     wuvt_ref, ckv_ref, vt_ref, kk_ref, widx_ref, qnt_ref, qit_ref, *, idx_scale):
    hn = _rms(x_ref[...], g_ref[...]).astype(BF16)
    proj = _dot(hn, win_ref[...])
    c_q = _rms(proj[:, 0:DSA_LORA], gq_ref[...]).astype(BF16)
    ckv_ref[...] = _rms(proj[:, DSA_LORA:2 * DSA_LORA], gkv_ref[...]).astype(BF16)
    kk_ref[...] = proj[:, 2 * DSA_LORA:2 * DSA_LORA + IDX_DIM].astype(BF16)
    ct = _dot_nt(wkvt_ref[...], hn)
    scale = lax.rsqrt(jnp.mean(ct * ct, axis=0, keepdims=True) + EPS)
    gt = jnp.concatenate([gkvt_ref[...]] * (ct.shape[1] // LANES), axis=1)
    v_t = _dot(wuvt_ref[...], (ct * scale * gt).astype(BF16)).astype(BF16)
    for r in range(vt_ref.shape[0]):
        vt_ref[r] = v_t[:, r * BLK:(r + 1) * BLK]
    widx_ref[...] = _dot_nt(wwt_ref[...], hn)[0:IDX_HEADS, :] * idx_scale
    qnt_ref[...] = _dot_nt(wuqt_ref[...], c_q).astype(BF16)
    qit_ref[...] = _dot_nt(wqit_ref[...], c_q).astype(BF16)


def _dsa_proj(x3, g, w_in_row, w_kv_t, w_w_t, g_q, g_kv, w_uq_t, w_qi_t, w_uv_t):
    b, seq, d = x3.shape
    tm = PROJ_TM
    g2, gq2, gkv2 = g.reshape(1, d), g_q.reshape(1, -1), g_kv.reshape(1, -1)
    gkv_t = jnp.broadcast_to(g_kv.reshape(-1, 1), (DSA_LORA, LANES))
    n_q, n_i, n_v = w_uq_t.shape[0], w_qi_t.shape[0], w_uv_t.shape[0]
    consts = (g2, w_in_row, w_kv_t, w_w_t, gq2, gkv2, gkv_t, w_uq_t, w_qi_t, w_uv_t)
    idx_scale = IDX_HEADS ** -0.5 * IDX_DIM ** -0.5
    return pl.pallas_call(
        functools.partial(_dsa_proj_kernel, idx_scale=idx_scale),
        grid=(b, seq // tm),
        in_specs=[pl.BlockSpec((None, tm, d), lambda bi, i: (bi, i, 0))] + [_full(a) for a in consts],
        out_specs=[pl.BlockSpec((None, tm, DSA_LORA), lambda bi, i: (bi, i, 0)),
                   pl.BlockSpec((None, tm // BLK, n_v, BLK), lambda bi, i: (bi, i, 0, 0)),
                   pl.BlockSpec((None, tm, IDX_DIM), lambda bi, i: (bi, i, 0)),
                   pl.BlockSpec((None, IDX_HEADS, tm), lambda bi, i: (bi, 0, i)),
                   pl.BlockSpec((None, n_q, tm), lambda bi, i: (bi, 0, i)),
                   pl.BlockSpec((None, n_i, tm), lambda bi, i: (bi, 0, i))],
        out_shape=[jax.ShapeDtypeStruct((b, seq, DSA_LORA), BF16),
                   jax.ShapeDtypeStruct((b, seq // BLK, n_v, BLK), BF16),
                   jax.ShapeDtypeStruct((b, seq, IDX_DIM), BF16),
                   jax.ShapeDtypeStruct((b, IDX_HEADS, seq), F32),
                   jax.ShapeDtypeStruct((b, n_q, seq), BF16),
                   jax.ShapeDtypeStruct((b, n_i, seq), BF16)],
        compiler_params=_cparams("parallel", "parallel"),
        name="dsa_proj",
    )(x3, *consts)


def _dsa_select_kernel(qit_ref, w_ref, kk_ref, o_ref, keys_ref, half_ref, *, top_k, idx_bits):
    c = pl.program_id(1)
    n_kc, tk, tq = keys_ref.shape
    n_proc = c + 1
    key_row = lax.broadcasted_iota(I32, (tk, tq), 0)
    q_pos = c * tq + lax.broadcasted_iota(I32, (tk, tq), 1)
    w = w_ref[...]

    def score_chunk(kc):
        kt = kk_ref[pl.ds(pl.multiple_of(kc * tk, tk), tk), :]
        acc = jnp.zeros((tk, tq), F32)
        for h in range(IDX_HEADS):
            x = _dot(kt, qit_ref[h * IDX_DIM:(h + 1) * IDX_DIM, :])
            acc = acc + w[h:h + 1, :] * jnp.maximum(x, 0.0)
        sc = jnp.where(kc * tk + key_row <= q_pos, acc, NEG)
        bits = pltpu.bitcast(sc, I32)
        key = jnp.where(bits < 0, bits ^ 0x7FFFFFFF, bits)
        keys_ref[kc] = key
        half_ref[kc] = jnp.right_shift(key, 16).astype(I16)

    def score_group(i, _):
        for r in range(SCORE_UNROLL):
            score_chunk(SCORE_UNROLL * i + r)
        return 0

    n_grouped = n_proc // SCORE_UNROLL
    lax.fori_loop(0, n_grouped, score_group, 0)

    def score_one(kc, _):
        score_chunk(kc)
        return 0

    lax.fori_loop(SCORE_UNROLL * n_grouped, n_proc, score_one, 0)

    n_pair = (n_proc + 1) // 2

    @pl.when(n_proc % 2 == 1)
    def _():
        half_ref[n_proc] = jnp.full((tk, tq), -32768, I16)

    def count(pred):
        def body(kc, tot):
            hit = jnp.where(pred(keys_ref[kc], kc * tk + key_row), 1, 0)
            return tot + jnp.sum(hit, axis=0, keepdims=True)
        return lax.fori_loop(0, n_proc, body, jnp.zeros((1, tq), I32))

    def count_half(cand, strict=False):
        cand16 = cand.astype(I16)

        def body(i, part):
            tiles = []
            for kc in (2 * i, 2 * i + 1):
                x = half_ref[kc]
                hit = jnp.where((x > cand16) if strict else (x >= cand16), jnp.int16(1), jnp.int16(0))
                tiles += [hit[r:r + BF16_ROWS, :] for r in range(0, tk, BF16_ROWS)]
            while len(tiles) > 1:
                tiles = [a + b for a, b in zip(tiles[::2], tiles[1::2])]
            return part + tiles[0]
        part = lax.fori_loop(0, n_pair, body, jnp.zeros((BF16_ROWS, tq), I16))
        return jnp.sum(part.astype(I32), axis=0, keepdims=True)

    def largest_half(need):
        def bit(i, u):
            cand_u = u | jnp.left_shift(jnp.int32(1), 15 - i)
            return jnp.where(count_half(cand_u - 32768) >= need, cand_u, u)
        return lax.fori_loop(0, 16, bit, jnp.zeros((1, tq), I32)) - 32768

    thr_hi = largest_half(top_k)
    above = count_half(thr_hi, strict=True)

    def low_chunk(kc, _):
        key = keys_ref[kc]
        low = jnp.bitwise_and(key, 0xFFFF) - 32768
        half_ref[kc] = jnp.where(jnp.right_shift(key, 16) == thr_hi, low, -32768).astype(I16)
        return 0

    lax.fori_loop(0, n_proc, low_chunk, 0)
    thr_lo = largest_half(top_k - above)
    cnt_ge = above + count_half(thr_lo)
    thr = jnp.left_shift(thr_hi, 16) + (thr_lo + 32768)

    def tie_cut():
        need = top_k - count(lambda key, pos: key > thr)

        def index_bit(i, x):
            cand = x | jnp.left_shift(jnp.int32(1), idx_bits - 1 - i)
            cnt = count(lambda key, pos: (key == thr) & (pos < cand))
            return jnp.where(cnt < need, cand, x)
        return lax.fori_loop(0, idx_bits, index_bit, jnp.zeros((1, tq), I32))

    def write(chosen):
        def write_chunk(kc, _):
            key = keys_ref[kc]
            pos = kc * tk + key_row
            rows = pl.ds(pl.multiple_of(kc * tk, tk), tk)
            o_ref[rows, :] = jnp.where(chosen(key, pos) & (pos <= q_pos), 0.0, NEG).astype(o_ref.dtype)
            return 0
        lax.fori_loop(0, n_proc, write_chunk, 0)

    tied = jnp.max(cnt_ge) > top_k

    @pl.when(jnp.logical_not(tied))
    def _():
        write(lambda key, pos: key >= thr)

    @pl.when(tied)
    def _():
        last_tie = tie_cut()
        write(lambda key, pos: (key > thr) | ((key == thr) & (pos <= last_tie)))

    def fill_chunk(kc, _):
        o_ref[pl.ds(pl.multiple_of(kc * tk, tk), tk), :] = jnp.full((tk, tq), NEG, o_ref.dtype)
        return 0

    lax.fori_loop(n_proc, n_kc, fill_chunk, 0)


def _dsa_select(qi_t, widx_t, kk):
    b, seq, _ = kk.shape
    t = BLK
    n_kc = seq // t
    top_k = min(DSA_TOPK_MAX, seq // 4)
    assert top_k <= t
    assert n_kc % 2 == 0
    idx_bits = max(1, (seq - 1).bit_length())
    return pl.pallas_call(
        functools.partial(_dsa_select_kernel, top_k=top_k, idx_bits=idx_bits),
        grid=(b, seq // t),
        in_specs=[pl.BlockSpec((None, qi_t.shape[1], t), lambda bi, c: (bi, 0, c)),
                  pl.BlockSpec((None, IDX_HEADS, t), lambda bi, c: (bi, 0, c)),
                  pl.BlockSpec((None, seq, IDX_DIM), lambda bi, c: (bi, 0, 0))],
        out_specs=pl.BlockSpec((None, None, seq, t), lambda bi, c: (bi, c, 0, 0)),
        out_shape=jax.ShapeDtypeStruct((b, seq // t, seq, t), BF16),
        scratch_shapes=[pltpu.VMEM((n_kc, t, t), I32),
                        pltpu.VMEM((n_kc, t, t), I16)],
        compiler_params=_cparams("parallel", "arbitrary"),
        name="dsa_select",
    )(qi_t, widx_t, kk)


def _dsa_attn_kernel(qnt_ref, ckv_ref, vt_ref, mb_ref, bt_ref, wukt_ref, o_ref,
                     qlat_ref, acc_ref, m_ref, l_ref, s_ref, smax_ref, *, n_heads):
    qt = pl.program_id(1)
    t = BLK
    n_grp = n_heads // DSA_GRP
    row = lax.broadcasted_iota(I32, (LANES, t), 0)
    ones_rows = jnp.ones((BF16_ROWS, t), BF16)

    for hp in range(n_heads // 2):
        qp = qnt_ref[hp * LANES:(hp + 1) * LANES, :]
        for hh in range(2):
            qh = jnp.where((row < HEAD_DIM) if hh == 0 else (row >= HEAD_DIM), qp, jnp.zeros_like(qp))
            h = 2 * hp + hh
            qlat_ref[:, h * t:(h + 1) * t] = _dot(wukt_ref[hp], qh).astype(BF16)

    m_ref[...] = jnp.full(m_ref.shape, NEG, F32)
    l_ref[...] = jnp.zeros_like(l_ref)
    acc_ref[...] = jnp.zeros_like(acc_ref)

    def grp_cols(g):
        return slice(g * DSA_GRP * t, (g + 1) * DSA_GRP * t)

    def key_rows(kk):
        return pl.ds(pl.multiple_of(kk * t, t), t)

    def fill(slot, kk, bias_row0, off=None):
        kt = ckv_ref[key_rows(kk), :]
        mb = mb_ref[key_rows(kk), :].astype(F32)
        if off is not None:
            mb = mb + off
        for g in range(n_grp):
            s_grp = _dot(kt, qlat_ref[:, grp_cols(g)])
            parts = []
            for i in range(DSA_GRP):
                s = s_grp[:, i * t:(i + 1) * t] + mb
                if bias_row0 is not None:
                    s = s + bt_ref[g * DSA_GRP + i, bias_row0:bias_row0 + t, :]
                parts.append(s)
            s = jnp.concatenate(parts, axis=1)
            s_ref[slot, :, grp_cols(g)] = s
            smax_ref[slot, :, grp_cols(g)] = jnp.max(s, axis=0, keepdims=True)

    def drain(slot, kk):
        for g in range(n_grp):
            cols = grp_cols(g)
            m_old = m_ref[:, cols]
            m_new = jnp.maximum(m_old, smax_ref[slot, :, cols])
            alpha = jnp.exp2(m_old - m_new)
            pr = jnp.exp2(s_ref[slot, :, cols] - m_new).astype(BF16)
            m_ref[:, cols] = m_new
            for i in range(DSA_GRP):
                h = g * DSA_GRP + i
                feats = slice(h * HEAD_DIM, (h + 1) * HEAD_DIM)
                hc = slice(i * t, (i + 1) * t)
                lc = slice(h * t, (h + 1) * t)
                pv = _dot(jnp.concatenate([vt_ref[kk, feats, :], ones_rows], axis=0), pr[:, hc])
                acc_ref[feats, :] = alpha[:, hc] * acc_ref[feats, :] + pv[0:HEAD_DIM, :]
                l_ref[:, lc] = alpha[:, hc] * l_ref[:, lc] + pv[HEAD_DIM:HEAD_DIM + 1, :]

    n_far = jnp.maximum(qt - 1, 0)
    fill(0, qt, BLK)
    fill(1, jnp.maximum(qt - 1, 0), 0, off=jnp.where(qt >= 1, 0.0, NEG))
    drain(0, qt)
    fill(0, 0, None)
    drain(1, jnp.maximum(qt - 1, 0))

    def far_pair(i, _):
        kk = 2 * i
        fill(1, kk + 1, None)
        drain(0, kk)
        fill(0, kk + 2, None)
        drain(1, kk + 1)
        return 0

    n_trip = jnp.maximum(n_far - 1, 0) // 2
    lax.fori_loop(0, n_trip, far_pair, 0)
    left = n_far - 2 * n_trip
    last = 2 * n_trip

    @pl.when(left == 1)
    def _():
        drain(0, last)

    @pl.when(left == 2)
    def _():
        fill(1, last + 1, None)
        drain(0, last)
        drain(1, last + 1)

    for h in range(n_heads):
        feats = slice(h * HEAD_DIM, (h + 1) * HEAD_DIM)
        acc_ref[feats, :] = acc_ref[feats, :] / l_ref[:, h * t:(h + 1) * t]
    o_ref[...] = acc_ref[...].T.astype(o_ref.dtype)


def _dsa_attention(qn_t, ckv, v_t, mask_t, bt, w_uk_t, n_heads):
    b, seq, _ = ckv.shape
    t = BLK
    d_q = qn_t.shape[1]
    return pl.pallas_call(
        functools.partial(_dsa_attn_kernel, n_heads=n_heads),
        grid=(b, seq // t),
        in_specs=[pl.BlockSpec((None, d_q, t), lambda bi, q: (bi, 0, q)),
                  pl.BlockSpec((None, seq, DSA_LORA), lambda bi, q: (bi, 0, 0)),
                  pl.BlockSpec((None, seq // t, d_q, t), lambda bi, q: (bi, 0, 0, 0)),
                  pl.BlockSpec((None, None, seq, t), lambda bi, q: (bi, q, 0, 0)),
                  _full(bt), _full(w_uk_t)],
        out_specs=pl.BlockSpec((None, t, d_q), lambda bi, q: (bi, q, 0)),
        out_shape=jax.ShapeDtypeStruct((b, seq, d_q), BF16),
        scratch_shapes=[pltpu.VMEM((DSA_LORA, n_heads * t), BF16),
                        pltpu.VMEM((d_q, t), F32),
                        pltpu.VMEM((1, n_heads * t), F32),
                        pltpu.VMEM((1, n_heads * t), F32),
                        pltpu.VMEM((2, t, n_heads * t), F32),
                        pltpu.VMEM((2, 1, n_heads * t), F32)],
        compiler_params=_cparams("parallel", "arbitrary"),
        name="dsa_attn",
    )(qn_t, ckv, v_t, mask_t, bt, w_uk_t)


def _moba_layer(x2, batch, seq, g, w_qkv, bt, n_heads):
    d = x2.shape[1]
    wq_t = (w_qkv[:, :d] * (HEAD_DIM ** -0.5 * LOG2E)).T.astype(BF16)
    wk = w_qkv[:, d:2 * d].astype(BF16)
    wv_t = w_qkv[:, 2 * d:].T.astype(BF16)
    q_t, k, v_t = _moba_proj(x2.reshape(batch, seq, d), g, wq_t, wk, wv_t)
    attn = _moba_attention(q_t, k, v_t, bt, n_heads)
    return attn.reshape(batch * seq, d)


def _dsa_layer(x2, batch, seq, g, w_in, g_q, g_kv, w_uq, w_qi, w_uk, w_uv, bt, n_heads):
    d = x2.shape[1]
    k_lo = 2 * DSA_LORA
    w_in_row = jnp.concatenate([w_in[:, :k_lo + IDX_DIM], jnp.zeros((d, LANES - IDX_DIM), F32)], axis=1).astype(BF16)
    w_kv_t = w_in[:, DSA_LORA:k_lo].T.astype(BF16)
    w_w_t = jnp.concatenate([w_in[:, k_lo + IDX_DIM:].T, jnp.zeros((BF16_ROWS - IDX_HEADS, d), F32)]).astype(BF16)
    w_uq_t = (w_uq * (HEAD_DIM ** -0.5 * LOG2E)).T.astype(BF16)
    w_uv_t = jnp.transpose(w_uv, (0, 2, 1)).reshape(n_heads * HEAD_DIM, DSA_LORA).astype(BF16)
    ckv, v_t, kk, widx_t, qn_t, qi_t = _dsa_proj(x2.reshape(batch, seq, d), g, w_in_row, w_kv_t, w_w_t,
                                                 g_q, g_kv, w_uq_t, w_qi.T.astype(BF16), w_uv_t)
    mask_t = _dsa_select(qi_t, widx_t, kk)
    w_uk_t = jnp.transpose(w_uk.reshape(n_heads // 2, 2 * HEAD_DIM, DSA_LORA), (0, 2, 1)).astype(BF16)
    attn = _dsa_attention(qn_t, ckv, v_t, mask_t, bt[:, :2 * BLK], w_uk_t, n_heads)
    return attn.reshape(batch * seq, -1)


def kernel(x, rel_bias, ln_attn, ln_mlp, moba_w_qkv, moba_w_o, dsa_w_in, dsa_g_q, dsa_g_kv, dsa_w_uq, dsa_w_qi,
           dsa_w_uk, dsa_w_uv, dsa_w_o, mlp_w_up, mlp_w_down, final_norm):
    batch, seq, d = x.shape
    n_heads = rel_bias.shape[1]
    depth = ln_attn.shape[0]
    bt = _bias_tiles(rel_bias)
    x2 = x.reshape(batch * seq, d)
    for i in range(depth):
        j = i // 2
        if i % 2 == 0:
            attn, w_o = _moba_layer(x2, batch, seq, ln_attn[i], moba_w_qkv[j], bt, n_heads), moba_w_o[j]
        else:
            attn, w_o = _dsa_layer(x2, batch, seq, ln_attn[i], dsa_w_in[j], dsa_g_q[j], dsa_g_kv[j], dsa_w_uq[j],
                                   dsa_w_qi[j], dsa_w_uk[j], dsa_w_uv[j], bt, n_heads), dsa_w_o[j]
        x2 = _block_tail(x2, attn, w_o.astype(BF16), ln_mlp[i], mlp_w_up[i].astype(BF16), mlp_w_down[i].astype(BF16),
                         g_final=final_norm if i == depth - 1 else None)
    return x2.reshape(batch, seq, d)
```

```python
import functools
import math

import numpy as np
import jax
import jax.numpy as jnp
from jax import lax
from jax.experimental import pallas as pl
from jax.experimental.pallas import tpu as pltpu

F32 = jnp.float32
BF16 = jnp.bfloat16
I32 = jnp.int32
I16 = jnp.int16

EPS = 1e-6
NEG = -1e30
INT_MIN = -(2 ** 31)
LOG2E = 1.0 / math.log(2.0)

NUM_BUCKETS = 32
MAX_DISTANCE = 128
HEAD_DIM = 64
LANES = 128
BF16_ROWS = 16
BLK = 256
PROJ_TM = 1024
MOBA_TOPK = 3
MOBA_GRP = 8
DSA_TOPK_MAX = 256
DSA_LORA = 256
DSA_GRP = 4
IDX_HEADS = 8
IDX_DIM = 64
SCORE_UNROLL = 4
VMEM_LIMIT = 56 * 1024 * 1024


def _dot(a, b):
    return jnp.dot(a, b, preferred_element_type=F32)


def _dot_nt(a, b):
    return lax.dot_general(a, b, (((1,), (1,)), ((), ())), preferred_element_type=F32)


def _rms(x, g):
    return x * lax.rsqrt(jnp.mean(x * x, axis=-1, keepdims=True) + EPS) * g


def _cparams(*sem):
    return pltpu.CompilerParams(dimension_semantics=sem, vmem_limit_bytes=VMEM_LIMIT)


def _full(a):
    return pl.BlockSpec(a.shape, lambda *_: (0,) * a.ndim, pipeline_mode=pl.Buffered(1))


def _bucket_map_t():
    jj = np.arange(2 * BLK)[:, None]
    i = np.arange(BLK)[None, :]
    d = i - jj + BLK
    n = np.maximum(d, 0)
    exact = NUM_BUCKETS // 2
    nf = np.maximum(n, 1).astype(np.float32)
    large = exact + (np.log(nf / exact) / math.log(MAX_DISTANCE / exact) * (NUM_BUCKETS - exact)).astype(np.int32)
    large = np.minimum(large, NUM_BUCKETS - 1)
    bucket = np.where(n < exact, n, large)
    bmap = np.where(d >= 0, bucket, -1).astype(np.int32)
    return np.concatenate([bmap, bmap[:BLK]], axis=0)


def _bias_tile_kernel(tab_ref, bmap_ref, o_ref):
    h = pl.program_id(0)
    bm = bmap_ref[...]
    base = tab_ref[NUM_BUCKETS - 1, h]
    acc = jnp.full(bm.shape, NEG, F32)
    for b in range(NUM_BUCKETS):
        acc = jnp.where(bm == b, (tab_ref[b, h] - base) * LOG2E, acc)
    o_ref[...] = acc


def _bias_tiles(rel_bias):
    n_heads = rel_bias.shape[1]
    bmap = jnp.asarray(_bucket_map_t())
    return pl.pallas_call(
        _bias_tile_kernel,
        grid=(n_heads,),
        in_specs=[pl.BlockSpec(memory_space=pltpu.SMEM),
                  pl.BlockSpec((3 * BLK, BLK), lambda h: (0, 0))],
        out_specs=pl.BlockSpec((None, 3 * BLK, BLK), lambda h: (h, 0, 0)),
        out_shape=jax.ShapeDtypeStruct((n_heads, 3 * BLK, BLK), F32),
        name="bias_tiles",
    )(rel_bias, bmap)


def _block_tail_kernel(x_ref, a_ref, wo_ref, g_ref, wu_ref, wd_ref, *rest, final):
    if final:
        gf_ref, o_ref, x1_ref, xn_ref, acc_ref = rest
    else:
        o_ref, x1_ref, xn_ref, acc_ref = rest
    f = pl.program_id(1)

    @pl.when(f == 0)
    def _():
        x1 = x_ref[...] + _dot(a_ref[...], wo_ref[...])
        x1_ref[...] = x1
        xn_ref[...] = _rms(x1, g_ref[...]).astype(BF16)
        acc_ref[...] = jnp.zeros_like(acc_ref)

    h = jnp.square(jnp.maximum(_dot(xn_ref[...], wu_ref[...]), 0.0)).astype(BF16)
    acc_ref[...] += _dot(h, wd_ref[...])

    @pl.when(f == pl.num_programs(1) - 1)
    def _():
        out = x1_ref[...] + acc_ref[...]
        o_ref[...] = _rms(out, gf_ref[...]) if final else out


def _block_tail(x2, attn, w_o, g, w_up, w_down, g_final=None, tm=1024, tf=2048):
    n, d = x2.shape
    k = attn.shape[1]
    d_ff = w_up.shape[1]
    final = g_final is not None
    row = lambda w: pl.BlockSpec((tm, w), lambda i, f: (i, 0))
    vec = pl.BlockSpec((1, d), lambda i, f: (0, 0))
    in_specs = [row(d), row(k), pl.BlockSpec((k, d), lambda i, f: (0, 0), pipeline_mode=pl.Buffered(1)), vec,
                pl.BlockSpec((d, tf), lambda i, f: (0, f)), pl.BlockSpec((tf, d), lambda i, f: (f, 0))]
    args = [x2, attn, w_o, g.reshape(1, d), w_up, w_down]
    if final:
        in_specs.append(vec)
        args.append(g_final.reshape(1, d))
    return pl.pallas_call(
        functools.partial(_block_tail_kernel, final=final),
        grid=(n // tm, d_ff // tf),
        in_specs=in_specs,
        out_specs=row(d),
        out_shape=jax.ShapeDtypeStruct((n, d), F32),
        scratch_shapes=[pltpu.VMEM((tm, d), F32), pltpu.VMEM((tm, d), BF16), pltpu.VMEM((tm, d), F32)],
        compiler_params=_cparams("parallel", "arbitrary"),
        name="block_tail",
    )(*args)


def _moba_proj_kernel(x_ref, g_ref, wqt_ref, wk_ref, wvt_ref, qt_ref, k_ref, vt_ref):
    xn = _rms(x_ref[...], g_ref[...]).astype(BF16)
    qt_ref[...] = _dot_nt(wqt_ref[...], xn).astype(BF16)
    k_ref[...] = _dot(xn, wk_ref[...]).astype(BF16)
    v_t = _dot_nt(wvt_ref[...], xn).astype(BF16)
    for r in range(vt_ref.shape[0]):
        vt_ref[r] = v_t[:, r * BLK:(r + 1) * BLK]


def _moba_proj(x3, g, wq_t, wk, wv_t):
    b, seq, d = x3.shape
    g2 = g.reshape(1, d)
    return pl.pallas_call(
        _moba_proj_kernel,
        grid=(b, seq // PROJ_TM),
        in_specs=[pl.BlockSpec((None, PROJ_TM, d), lambda bi, i: (bi, i, 0)),
                  _full(g2), _full(wq_t), _full(wk), _full(wv_t)],
        out_specs=[pl.BlockSpec((None, d, PROJ_TM), lambda bi, i: (bi, 0, i)),
                   pl.BlockSpec((None, PROJ_TM, d), lambda bi, i: (bi, i, 0)),
                   pl.BlockSpec((None, PROJ_TM // BLK, d, BLK), lambda bi, i: (bi, i, 0, 0))],
        out_shape=[jax.ShapeDtypeStruct((b, d, seq), BF16),
                   jax.ShapeDtypeStruct((b, seq, d), BF16),
                   jax.ShapeDtypeStruct((b, seq // BLK, d, BLK), BF16)],
        compiler_params=_cparams("parallel", "parallel"),
        name="moba_proj",
    )(x3, g2, wq_t, wk, wv_t)


def _moba_kernel(qt_ref, k_ref, vt_ref, bt_ref, o_ref, kaug_ref, kmean_ref, qfar_ref, m_ref, l_ref, acc_ref, s_ref,
                 smax_ref, *, n_blk, k_sel, sel_rows):
    j = pl.program_id(2)
    seq = k_ref.shape[0]
    n_pairs = MOBA_GRP // 2
    t = BLK
    row = lax.broadcasted_iota(I32, (LANES, t), 0)
    blk_id = lax.broadcasted_iota(I32, (sel_rows, t), 0)
    ones_2 = jnp.ones((BF16_ROWS, 2 * t), BF16)

    @pl.when(j == 0)
    def _():
        krow = lax.broadcasted_iota(I32, (seq, LANES), 0)
        kcol = lax.broadcasted_iota(I32, (seq, LANES), 1)
        onehot = jnp.where(krow // BLK == kcol, 1.0, 0.0).astype(BF16)
        for p in range(n_pairs):
            kaug_ref[p, :, 0:LANES] = k_ref[:, p * LANES:(p + 1) * LANES]
            kaug_ref[p, :, LANES:2 * LANES] = onehot
            kmean_ref[p] = jnp.zeros((LANES, LANES), F32)
            for n in range(n_blk):
                kmean_ref[p, n:n + 1, :] = jnp.mean(
                    k_ref[n * BLK:(n + 1) * BLK, p * LANES:(p + 1) * LANES].astype(F32), axis=0, keepdims=True)

    near_blk = jnp.maximum(j - 1, 0)
    near0 = pl.multiple_of(near_blk * BLK, BLK)
    bias0 = pl.multiple_of(jnp.where(j == 0, BLK, 0), BLK)
    pad_rows = jnp.zeros((LANES - sel_rows, t), BF16)

    def store(slot, p, s):
        s_ref[slot, p] = s
        smax_ref[slot, p] = jnp.max(s, axis=0, keepdims=True)

    def drain(slot, blk0):
        for p in range(n_pairs):
            feats = slice(p * LANES, (p + 1) * LANES)
            m_old = m_ref[p]
            m_new = jnp.maximum(m_old, smax_ref[slot, p])
            alpha = jnp.exp2(m_old - m_new)
            pr = jnp.exp2(s_ref[slot, p] - m_new).astype(BF16)
            m_ref[p] = m_new
            v_t = jnp.concatenate([vt_ref[blk0, feats, :], vt_ref[blk0 + 1, feats, :]], axis=1)
            pv = _dot(jnp.concatenate([v_t, ones_2], axis=0), pr)
            acc_ref[p] = alpha * acc_ref[p] + pv[0:LANES, :]
            l_ref[p] = alpha * l_ref[p] + pv[LANES:LANES + 1, :]

    def fill_far(slot, pi):
        n0 = pl.multiple_of(pi * 2 * BLK, 2 * BLK)
        for p in range(n_pairs):
            store(slot, p, _dot(kaug_ref[p, pl.ds(n0, 2 * BLK), :], qfar_ref[p]))

    q_near = []
    for p in range(n_pairs):
        qp = qt_ref[p * LANES:(p + 1) * LANES, :]
        near, far = [], []
        for hh in range(2):
            qh = jnp.where((row < HEAD_DIM) if hh == 0 else (row >= HEAD_DIM), qp, jnp.zeros_like(qp))
            gate = _dot(kmean_ref[p], qh.astype(F32))[0:sel_rows, :]
            g = jnp.where(blk_id < j, gate, NEG)
            g = jnp.where(blk_id < n_blk, g, -jnp.inf)
            sel_bias = jnp.full((sel_rows, t), NEG, F32)
            for it in range(k_sel):
                mx = jnp.max(g, axis=0, keepdims=True)
                first = jnp.min(jnp.where(g == mx, blk_id, LANES), axis=0, keepdims=True)
                pick = blk_id == first
                sel_bias = jnp.where(pick, jnp.where(it < j, 0.0, NEG), sel_bias)
                g = jnp.where(pick, -jnp.inf, g)
            near_bias = jnp.where(blk_id == j, 0.0, sel_bias)
            far_bias = jnp.where(blk_id < j - 1, sel_bias, NEG)
            near.append(jnp.concatenate([qh, near_bias.astype(BF16), pad_rows], axis=0))
            far.append(jnp.concatenate([qh, far_bias.astype(BF16), pad_rows], axis=0))
        q_near.append(jnp.concatenate(near, axis=1))
        qfar_ref[p] = jnp.concatenate(far, axis=1)
        m_ref[p] = jnp.full((1, 2 * t), NEG, F32)
        l_ref[p] = jnp.zeros((1, 2 * t), F32)
        acc_ref[p] = jnp.zeros((LANES, 2 * t), F32)

    for p in range(n_pairs):
        bias = jnp.concatenate([bt_ref[2 * p, pl.ds(bias0, 2 * BLK), :], bt_ref[2 * p + 1, pl.ds(bias0, 2 * BLK), :]],
                               axis=1)
        store(0, p, _dot(kaug_ref[p, pl.ds(near0, 2 * BLK), :], q_near[p]) + bias)

    n_far = jnp.maximum(j, 1) // 2
    fill_far(1, 0)
    drain(0, near_blk)

    def far_two(i, _):
        pi = 2 * i
        fill_far(0, pi + 1)
        drain(1, 2 * pi)
        fill_far(1, pi + 2)
        drain(0, 2 * pi + 2)
        return 0

    n_trip = jnp.maximum(n_far - 1, 0) // 2
    lax.fori_loop(0, n_trip, far_two, 0)
    left = n_far - 2 * n_trip
    last = 2 * n_trip

    @pl.when(left == 1)
    def _():
        drain(1, 2 * last)

    @pl.when(left == 2)
    def _():
        fill_far(0, last + 1)
        drain(1, 2 * last)
        drain(0, 2 * last + 2)

    for p in range(n_pairs):
        acc = acc_ref[p] / l_ref[p]
        out_t = jnp.where(row < HEAD_DIM, acc[:, 0:t], acc[:, t:2 * t])
        o_ref[:, p * LANES:(p + 1) * LANES] = out_t.T.astype(o_ref.dtype)


def _moba_attention(q_t, k, v_t, bt, n_heads):
    b, seq, d = k.shape
    n_blk = seq // BLK
    n_grp = n_heads // MOBA_GRP
    n_pairs = MOBA_GRP // 2
    w = MOBA_GRP * HEAD_DIM
    k_sel = max(1, min(MOBA_TOPK, n_blk - 1))
    sel_rows = BF16_ROWS * pl.cdiv(n_blk, BF16_ROWS)
    assert sel_rows <= LANES and n_blk % 2 == 0
    return pl.pallas_call(
        functools.partial(_moba_kernel, n_blk=n_blk, k_sel=k_sel, sel_rows=sel_rows),
        grid=(b, n_grp, n_blk),
        in_specs=[pl.BlockSpec((None, w, BLK), lambda bi, g, j: (bi, g, j)),
                  pl.BlockSpec((None, seq, w), lambda bi, g, j: (bi, 0, g)),
                  pl.BlockSpec((None, n_blk, w, BLK), lambda bi, g, j: (bi, 0, g, 0)),
                  pl.BlockSpec((MOBA_GRP, 3 * BLK, BLK), lambda bi, g, j: (g, 0, 0), pipeline_mode=pl.Buffered(1))],
        out_specs=pl.BlockSpec((None, BLK, w), lambda bi, g, j: (bi, j, g)),
        out_shape=jax.ShapeDtypeStruct((b, seq, d), BF16),
        scratch_shapes=[pltpu.VMEM((n_pairs, seq, 2 * LANES), BF16),
                        pltpu.VMEM((n_pairs, LANES, LANES), F32),
                        pltpu.VMEM((n_pairs, 2 * LANES, 2 * BLK), BF16),
                        pltpu.VMEM((n_pairs, 1, 2 * BLK), F32),
                        pltpu.VMEM((n_pairs, 1, 2 * BLK), F32),
                        pltpu.VMEM((n_pairs, LANES, 2 * BLK), F32),
                        pltpu.VMEM((2, n_pairs, 2 * BLK, 2 * BLK), F32),
                        pltpu.VMEM((2, n_pairs, 1, 2 * BLK), F32)],
        compiler_params=_cparams("parallel", "parallel", "arbitrary"),
        name="moba_attn",
    )(q_t, k, v_t, bt)


def _dsa_proj_kernel(x_ref, g_ref, win_ref, wkvt_ref, wwt_ref, gq_ref, gkv_ref, gkvt_ref, wuqt_ref, wqit_ref,
                     wuvt_ref, wukt_ref, ckv_ref, vt_ref, kk_ref, widx_ref, qlat_ref, qit_ref, *, idx_scale):
    hn = _rms(x_ref[...], g_ref[...]).astype(BF16)
    proj = _dot(hn, win_ref[...])
    c_q = _rms(proj[:, 0:DSA_LORA], gq_ref[...]).astype(BF16)
    ckv_ref[...] = _rms(proj[:, DSA_LORA:2 * DSA_LORA], gkv_ref[...]).astype(BF16)
    kk_ref[...] = proj[:, 2 * DSA_LORA:2 * DSA_LORA + IDX_DIM].astype(BF16)
    ct = _dot_nt(wkvt_ref[...], hn)
    scale = lax.rsqrt(jnp.mean(ct * ct, axis=0, keepdims=True) + EPS)
    gt = jnp.concatenate([gkvt_ref[...]] * (ct.shape[1] // LANES), axis=1)
    v_t = _dot(wuvt_ref[...], (ct * scale * gt).astype(BF16)).astype(BF16)
    for r in range(vt_ref.shape[0]):
        vt_ref[r] = v_t[:, r * BLK:(r + 1) * BLK]
    widx_ref[...] = _dot_nt(wwt_ref[...], hn)[0:IDX_HEADS, :] * idx_scale
    qn_t = _dot_nt(wuqt_ref[...], c_q).astype(BF16)
    row = lax.broadcasted_iota(I32, (LANES, BLK), 0)
    for r in range(qlat_ref.shape[0]):
        for hp in range(wukt_ref.shape[0]):
            qp = qn_t[hp * LANES:(hp + 1) * LANES, r * BLK:(r + 1) * BLK]
            for hh in range(2):
                qh = jnp.where((row < HEAD_DIM) if hh == 0 else (row >= HEAD_DIM), qp, jnp.zeros_like(qp))
                h = 2 * hp + hh
                qlat_ref[r, :, h * BLK:(h + 1) * BLK] = _dot(wukt_ref[hp], qh).astype(BF16)
    qit_ref[...] = _dot_nt(wqit_ref[...], c_q).astype(BF16)


def _dsa_proj(x3, g, w_in_row, w_kv_t, w_w_t, g_q, g_kv, w_uq_t, w_qi_t, w_uv_t, w_uk_t):
    b, seq, d = x3.shape
    tm = PROJ_TM
    g2, gq2, gkv2 = g.reshape(1, d), g_q.reshape(1, -1), g_kv.reshape(1, -1)
    gkv_t = jnp.broadcast_to(g_kv.reshape(-1, 1), (DSA_LORA, LANES))
    n_q, n_i, n_v = w_uq_t.shape[0], w_qi_t.shape[0], w_uv_t.shape[0]
    consts = (g2, w_in_row, w_kv_t, w_w_t, gq2, gkv2, gkv_t, w_uq_t, w_qi_t, w_uv_t, w_uk_t)
    n_lat = n_q // HEAD_DIM * BLK
    idx_scale = IDX_HEADS ** -0.5 * IDX_DIM ** -0.5
    return pl.pallas_call(
        functools.partial(_dsa_proj_kernel, idx_scale=idx_scale),
        grid=(b, seq // tm),
        in_specs=[pl.BlockSpec((None, tm, d), lambda bi, i: (bi, i, 0))] + [_full(a) for a in consts],
        out_specs=[pl.BlockSpec((None, tm, DSA_LORA), lambda bi, i: (bi, i, 0)),
                   pl.BlockSpec((None, tm // BLK, n_v, BLK), lambda bi, i: (bi, i, 0, 0)),
                   pl.BlockSpec((None, tm, IDX_DIM), lambda bi, i: (bi, i, 0)),
                   pl.BlockSpec((None, IDX_HEADS, tm), lambda bi, i: (bi, 0, i)),
                   pl.BlockSpec((None, tm // BLK, DSA_LORA, n_lat), lambda bi, i: (bi, i, 0, 0)),
                   pl.BlockSpec((None, n_i, tm), lambda bi, i: (bi, 0, i))],
        out_shape=[jax.ShapeDtypeStruct((b, seq, DSA_LORA), BF16),
                   jax.ShapeDtypeStruct((b, seq // BLK, n_v, BLK), BF16),
                   jax.ShapeDtypeStruct((b, seq, IDX_DIM), BF16),
                   jax.ShapeDtypeStruct((b, IDX_HEADS, seq), F32),
                   jax.ShapeDtypeStruct((b, seq // BLK, DSA_LORA, n_lat), BF16),
                   jax.ShapeDtypeStruct((b, n_i, seq), BF16)],
        compiler_params=_cparams("parallel", "parallel"),
        name="dsa_proj",
    )(x3, *consts)


def _dsa_select_kernel(qit_ref, w_ref, kk_ref, o_ref, keys_ref, half_ref, *, top_k, idx_bits):
    c = pl.program_id(1)
    n_kc, tk, tq = keys_ref.shape
    n_proc = c + 1
    key_row = lax.broadcasted_iota(I32, (tk, tq), 0)
    q_pos = c * tq + lax.broadcasted_iota(I32, (tk, tq), 1)
    w = w_ref[...]

    def score_chunk(kc):
        kt = kk_ref[pl.ds(pl.multiple_of(kc * tk, tk), tk), :]
        acc = jnp.zeros((tk, tq), F32)
        for h in range(IDX_HEADS):
            x = _dot(kt, qit_ref[h * IDX_DIM:(h + 1) * IDX_DIM, :])
            acc = acc + w[h:h + 1, :] * jnp.maximum(x, 0.0)
        sc = jnp.where(kc * tk + key_row <= q_pos, acc, NEG)
        bits = pltpu.bitcast(sc, I32)
        key = jnp.where(bits < 0, bits ^ 0x7FFFFFFF, bits)
        keys_ref[kc] = key
        half_ref[kc] = jnp.right_shift(key, 16).astype(I16)

    def score_group(i, _):
        for r in range(SCORE_UNROLL):
            score_chunk(SCORE_UNROLL * i + r)
        return 0

    n_grouped = n_proc // SCORE_UNROLL
    lax.fori_loop(0, n_grouped, score_group, 0)

    def score_one(kc, _):
        score_chunk(kc)
        return 0

    lax.fori_loop(SCORE_UNROLL * n_grouped, n_proc, score_one, 0)

    n_pair = (n_proc + 1) // 2

    @pl.when(n_proc % 2 == 1)
    def _():
        half_ref[n_proc] = jnp.full((tk, tq), -32768, I16)

    def count(pred):
        def body(kc, tot):
            hit = jnp.where(pred(keys_ref[kc], kc * tk + key_row), 1, 0)
            return tot + jnp.sum(hit, axis=0, keepdims=True)
        return lax.fori_loop(0, n_proc, body, jnp.zeros((1, tq), I32))

    def count_half(cand, strict=False):
        cand16 = cand.astype(I16)

        def body(i, part):
            tiles = []
            for kc in (2 * i, 2 * i + 1):
                x = half_ref[kc]
                hit = jnp.where((x > cand16) if strict else (x >= cand16), jnp.int16(1), jnp.int16(0))
                tiles += [hit[r:r + BF16_ROWS, :] for r in range(0, tk, BF16_ROWS)]
            while len(tiles) > 1:
                tiles = [a + b for a, b in zip(tiles[::2], tiles[1::2])]
            return part + tiles[0]
        part = lax.fori_loop(0, n_pair, body, jnp.zeros((BF16_ROWS, tq), I16))
        return jnp.sum(part.astype(I32), axis=0, keepdims=True)

    def largest_half(need):
        def bit(i, u):
            cand_u = u | jnp.left_shift(jnp.int32(1), 15 - i)
            return jnp.where(count_half(cand_u - 32768) >= need, cand_u, u)
        return lax.fori_loop(0, 16, bit, jnp.zeros((1, tq), I32)) - 32768

    thr_hi = largest_half(top_k)
    above = count_half(thr_hi, strict=True)

    def low_chunk(kc, _):
        key = keys_ref[kc]
        low = jnp.bitwise_and(key, 0xFFFF) - 32768
        half_ref[kc] = jnp.where(jnp.right_shift(key, 16) == thr_hi, low, -32768).astype(I16)
        return 0

    lax.fori_loop(0, n_proc, low_chunk, 0)
    thr_lo = largest_half(top_k - above)
    cnt_ge = above + count_half(thr_lo)
    thr = jnp.left_shift(thr_hi, 16) + (thr_lo + 32768)

    def tie_cut():
        need = top_k - count(lambda key, pos: key > thr)

        def index_bit(i, x):
            cand = x | jnp.left_shift(jnp.int32(1), idx_bits - 1 - i)
            cnt = count(lambda key, pos: (key == thr) & (pos < cand))
            return jnp.where(cnt < need, cand, x)
        return lax.fori_loop(0, idx_bits, index_bit, jnp.zeros((1, tq), I32))

    def write(chosen):
        def write_chunk(kc, _):
            key = keys_ref[kc]
            pos = kc * tk + key_row
            rows = pl.ds(pl.multiple_of(kc * tk, tk), tk)
            o_ref[rows, :] = jnp.where(chosen(key, pos) & (pos <= q_pos), 0.0, NEG).astype(o_ref.dtype)
            return 0
        lax.fori_loop(0, n_proc, write_chunk, 0)

    tied = jnp.max(cnt_ge) > top_k

    @pl.when(jnp.logical_not(tied))
    def _():
        write(lambda key, pos: key >= thr)

    @pl.when(tied)
    def _():
        last_tie = tie_cut()
        write(lambda key, pos: (key > thr) | ((key == thr) & (pos <= last_tie)))

    def fill_chunk(kc, _):
        o_ref[pl.ds(pl.multiple_of(kc * tk, tk), tk), :] = jnp.full((tk, tq), NEG, o_ref.dtype)
        return 0

    lax.fori_loop(n_proc, n_kc, fill_chunk, 0)


def _dsa_select(qi_t, widx_t, kk):
    b, seq, _ = kk.shape
    t = BLK
    n_kc = seq // t
    top_k = min(DSA_TOPK_MAX, seq // 4)
    assert top_k <= t
    assert n_kc % 2 == 0
    idx_bits = max(1, (seq - 1).bit_length())
    return pl.pallas_call(
        functools.partial(_dsa_select_kernel, top_k=top_k, idx_bits=idx_bits),
        grid=(b, seq // t),
        in_specs=[pl.BlockSpec((None, qi_t.shape[1], t), lambda bi, c: (bi, 0, c)),
                  pl.BlockSpec((None, IDX_HEADS, t), lambda bi, c: (bi, 0, c)),
                  pl.BlockSpec((None, seq, IDX_DIM), lambda bi, c: (bi, 0, 0))],
        out_specs=pl.BlockSpec((None, None, seq, t), lambda bi, c: (bi, c, 0, 0)),
        out_shape=jax.ShapeDtypeStruct((b, seq // t, seq, t), BF16),
        scratch_shapes=[pltpu.VMEM((n_kc, t, t), I32),
                        pltpu.VMEM((n_kc, t, t), I16)],
        compiler_params=_cparams("parallel", "arbitrary"),
        name="dsa_select",
    )(qi_t, widx_t, kk)


def _dsa_attn_kernel(qlat_ref, ckv_ref, vt_ref, mb_ref, bt_ref, o_ref,
                     acc_ref, m_ref, l_ref, s_ref, smax_ref, *, n_heads):
    qt = pl.program_id(1)
    t = BLK
    n_grp = n_heads // DSA_GRP
    ones_rows = jnp.ones((BF16_ROWS, t), BF16)

    m_ref[...] = jnp.full(m_ref.shape, NEG, F32)
    l_ref[...] = jnp.zeros_like(l_ref)
    acc_ref[...] = jnp.zeros_like(acc_ref)

    def grp_cols(g):
        return slice(g * DSA_GRP * t, (g + 1) * DSA_GRP * t)

    def key_rows(kk):
        return pl.ds(pl.multiple_of(kk * t, t), t)

    def fill(slot, kk, bias_row0, off=None):
        kt = ckv_ref[key_rows(kk), :]
        mb = mb_ref[key_rows(kk), :].astype(F32)
        if off is not None:
            mb = mb + off
        for g in range(n_grp):
            s_grp = _dot(kt, qlat_ref[:, grp_cols(g)])
            parts = []
            for i in range(DSA_GRP):
                s = s_grp[:, i * t:(i + 1) * t] + mb
                if bias_row0 is not None:
                    s = s + bt_ref[g * DSA_GRP + i, bias_row0:bias_row0 + t, :]
                parts.append(s)
            s = jnp.concatenate(parts, axis=1)
            s_ref[slot, :, grp_cols(g)] = s
            smax_ref[slot, :, grp_cols(g)] = jnp.max(s, axis=0, keepdims=True)

    def drain(slot, kk):
        for g in range(n_grp):
            cols = grp_cols(g)
            m_old = m_ref[:, cols]
            m_new = jnp.maximum(m_old, smax_ref[slot, :, cols])
            alpha = jnp.exp2(m_old - m_new)
            pr = jnp.exp2(s_ref[slot, :, cols] - m_new).astype(BF16)
            m_ref[:, cols] = m_new
            for i in range(DSA_GRP):
                h = g * DSA_GRP + i
                feats = slice(h * HEAD_DIM, (h + 1) * HEAD_DIM)
                hc = slice(i * t, (i + 1) * t)
                lc = slice(h * t, (h + 1) * t)
                pv = _dot(jnp.concatenate([vt_ref[kk, feats, :], ones_rows], axis=0), pr[:, hc])
                acc_ref[feats, :] = alpha[:, hc] * acc_ref[feats, :] + pv[0:HEAD_DIM, :]
                l_ref[:, lc] = alpha[:, hc] * l_ref[:, lc] + pv[HEAD_DIM:HEAD_DIM + 1, :]

    n_far = jnp.maximum(qt - 1, 0)
    fill(0, qt, BLK)
    fill(1, jnp.maximum(qt - 1, 0), 0, off=jnp.where(qt >= 1, 0.0, NEG))
    drain(0, qt)
    fill(0, 0, None)
    drain(1, jnp.maximum(qt - 1, 0))

    def far_pair(i, _):
        kk = 2 * i
        fill(1, kk + 1, None)
        drain(0, kk)
        fill(0, kk + 2, None)
        drain(1, kk + 1)
        return 0

    n_trip = jnp.maximum(n_far - 1, 0) // 2
    lax.fori_loop(0, n_trip, far_pair, 0)
    left = n_far - 2 * n_trip
    last = 2 * n_trip

    @pl.when(left == 1)
    def _():
        drain(0, last)

    @pl.when(left == 2)
    def _():
        fill(1, last + 1, None)
        drain(0, last)
        drain(1, last + 1)

    for h in range(n_heads):
        feats = slice(h * HEAD_DIM, (h + 1) * HEAD_DIM)
        acc_ref[feats, :] = acc_ref[feats, :] / l_ref[:, h * t:(h + 1) * t]
    o_ref[...] = acc_ref[...].T.astype(o_ref.dtype)


def _dsa_attention(qlat_t, ckv, v_t, mask_t, bt, n_heads):
    b, seq, _ = ckv.shape
    t = BLK
    d_q = v_t.shape[2]
    return pl.pallas_call(
        functools.partial(_dsa_attn_kernel, n_heads=n_heads),
        grid=(b, seq // t),
        in_specs=[pl.BlockSpec((None, None, DSA_LORA, n_heads * t), lambda bi, q: (bi, q, 0, 0)),
                  pl.BlockSpec((None, seq, DSA_LORA), lambda bi, q: (bi, 0, 0)),
                  pl.BlockSpec((None, seq // t, d_q, t), lambda bi, q: (bi, 0, 0, 0)),
                  pl.BlockSpec((None, None, seq, t), lambda bi, q: (bi, q, 0, 0)),
                  _full(bt)],
        out_specs=pl.BlockSpec((None, t, d_q), lambda bi, q: (bi, q, 0)),
        out_shape=jax.ShapeDtypeStruct((b, seq, d_q), BF16),
        scratch_shapes=[pltpu.VMEM((d_q, t), F32),
                        pltpu.VMEM((1, n_heads * t), F32),
                        pltpu.VMEM((1, n_heads * t), F32),
                        pltpu.VMEM((2, t, n_heads * t), F32),
                        pltpu.VMEM((2, 1, n_heads * t), F32)],
        compiler_params=_cparams("parallel", "arbitrary"),
        name="dsa_attn",
    )(qlat_t, ckv, v_t, mask_t, bt)


def _moba_layer(x2, batch, seq, g, w_qkv, bt, n_heads):
    d = x2.shape[1]
    wq_t = (w_qkv[:, :d] * (HEAD_DIM ** -0.5 * LOG2E)).T.astype(BF16)
    wk = w_qkv[:, d:2 * d].astype(BF16)
    wv_t = w_qkv[:, 2 * d:].T.astype(BF16)
    q_t, k, v_t = _moba_proj(x2.reshape(batch, seq, d), g, wq_t, wk, wv_t)
    attn = _moba_attention(q_t, k, v_t, bt, n_heads)
    return attn.reshape(batch * seq, d)


def _dsa_layer(x2, batch, seq, g, w_in, g_q, g_kv, w_uq, w_qi, w_uk, w_uv, bt, n_heads):
    d = x2.shape[1]
    k_lo = 2 * DSA_LORA
    w_in_row = jnp.concatenate([w_in[:, :k_lo + IDX_DIM], jnp.zeros((d, LANES - IDX_DIM), F32)], axis=1).astype(BF16)
    w_kv_t = w_in[:, DSA_LORA:k_lo].T.astype(BF16)
    w_w_t = jnp.concatenate([w_in[:, k_lo + IDX_DIM:].T, jnp.zeros((BF16_ROWS - IDX_HEADS, d), F32)]).astype(BF16)
    w_uq_t = (w_uq * (HEAD_DIM ** -0.5 * LOG2E)).T.astype(BF16)
    w_uv_t = jnp.transpose(w_uv, (0, 2, 1)).reshape(n_heads * HEAD_DIM, DSA_LORA).astype(BF16)
    w_uk_t = jnp.transpose(w_uk.reshape(n_heads // 2, 2 * HEAD_DIM, DSA_LORA), (0, 2, 1)).astype(BF16)
    ckv, v_t, kk, widx_t, qlat_t, qi_t = _dsa_proj(x2.reshape(batch, seq, d), g, w_in_row, w_kv_t, w_w_t,
                                                   g_q, g_kv, w_uq_t, w_qi.T.astype(BF16), w_uv_t, w_uk_t)
    mask_t = _dsa_select(qi_t, widx_t, kk)
    attn = _dsa_attention(qlat_t, ckv, v_t, mask_t, bt[:, :2 * BLK], n_heads)
    return attn.reshape(batch * seq, -1)


def kernel(x, rel_bias, ln_attn, ln_mlp, moba_w_qkv, moba_w_o, dsa_w_in, dsa_g_q, dsa_g_kv, dsa_w_uq, dsa_w_qi,
           dsa_w_uk, dsa_w_uv, dsa_w_o, mlp_w_up, mlp_w_down, final_norm):
    batch, seq, d = x.shape
    n_heads = rel_bias.shape[1]
    depth = ln_attn.shape[0]
    bt = _bias_tiles(rel_bias)
    x2 = x.reshape(batch * seq, d)
    for i in range(depth):
        j = i // 2
        if i % 2 == 0:
            attn, w_o = _moba_layer(x2, batch, seq, ln_attn[i], moba_w_qkv[j], bt, n_heads), moba_w_o[j]
        else:
            attn, w_o = _dsa_layer(x2, batch, seq, ln_attn[i], dsa_w_in[j], dsa_g_q[j], dsa_g_kv[j], dsa_w_uq[j],
                                   dsa_w_qi[j], dsa_w_uk[j], dsa_w_uv[j], bt, n_heads), dsa_w_o[j]
        x2 = _block_tail(x2, attn, w_o.astype(BF16), ln_mlp[i], mlp_w_up[i].astype(BF16), mlp_w_down[i].astype(BF16),
                         g_final=final_norm if i == depth - 1 else None)
    return x2.reshape(batch, seq, d)
```
